```python
import math
import jax, jax.numpy as jnp
from jax import lax
import numpy as np

D_MODEL = 1024
BATCH = 4
SEQ = 4096
DEPTH = 2

D_MIX = D_MODEL
N_DIR = 2
RG_WIDTH = 384
RG_HEADS = 6
RG_HEAD_DIM = RG_WIDTH // RG_HEADS
RG_CONV = 4
RG_C = 8.0
S5_WIDTH = 384
S5_GROUP = 16
S5_GROUPS = S5_WIDTH // S5_GROUP
S5_STATE = 64
HY_WIDTH = D_MIX - RG_WIDTH - S5_WIDTH
HY_CONV = 3
HY_BANDS = 16
HY_FEAT = 1 + 2 * HY_BANDS
HY_FILT_HIDDEN = 64
HY_FAST_DECAY = 0.3
HY_SLOW_DECAY = 1.5
HY_DECAY_TARGET = 1e-2
D_FF = 2816
FFN_CONV = 3
RMS_EPS = 1e-6
IN_COLS = 2 * RG_WIDTH + S5_WIDTH + 3 * HY_WIDTH

kernel_name = "hymba_style_rglru_s5_hyena_encoder"


def rmsnorm(x, g):
    xf = x.astype(jnp.float32)
    inv = lax.rsqrt(jnp.mean(xf * xf, axis=-1, keepdims=True) + RMS_EPS)
    return xf * inv * g.astype(jnp.float32)


def rms_nogain(x):
    xf = x.astype(jnp.float32)
    return xf * lax.rsqrt(jnp.mean(xf * xf, axis=-1, keepdims=True) + RMS_EPS)


def dwconv_centred(u, w, b):
    K = w.shape[0]
    L = u.shape[1]
    left = (K - 1) // 2
    up = jnp.pad(u, ((0, 0), (left, K - 1 - left), (0, 0)))
    out = b
    for k in range(K):
        out = out + up[:, k:k + L] * w[k]
    return out


def _lin_combine(e1, e2):
    a1, b1 = e1
    a2, b2 = e2
    return a1 * a2, a2 * b1 + b2


def _clin_combine(e1, e2):
    ar1, ai1, br1, bi1 = e1
    ar2, ai2, br2, bi2 = e2
    return (ar2 * ar1 - ai2 * ai1,
            ar2 * ai1 + ai2 * ar1,
            ar2 * br1 - ai2 * bi1 + br2,
            ar2 * bi1 + ai2 * br1 + bi2)


def rglru_mixer(xb, gb, conv_w, conv_b, wa, ba, wx, bx, lam):
    Bsz, L, _ = xb.shape
    u = dwconv_centred(xb, conv_w, conv_b)
    uh = u.reshape(Bsz, L, RG_HEADS, RG_HEAD_DIM)
    h_sum = jnp.zeros_like(u)
    for d in range(N_DIR):
        r = jax.nn.sigmoid(jnp.einsum('blhi,hij->blhj', uh, wa[d]).reshape(Bsz, L, RG_WIDTH) + ba[d])
        i = jax.nn.sigmoid(jnp.einsum('blhi,hij->blhj', uh, wx[d]).reshape(Bsz, L, RG_WIDTH) + bx[d])
        log_a = -RG_C * r * jax.nn.softplus(-lam[d].astype(jnp.float32))
        a = jnp.exp(log_a)
        mult = jnp.sqrt(-jnp.expm1(2.0 * log_a))
        _, h = lax.associative_scan(_lin_combine, (a, mult * (i * u)), reverse=(d == 1), axis=1)
        h_sum = h_sum + h
    return h_sum * jax.nn.gelu(gb)


def s5_mixer(u, a_re, a_im, log_dt, b_re, b_im, c_re, c_im, d_skip, glu_w, glu_b):
    Bsz, L, _ = u.shape
    ug = u.reshape(Bsz, L, S5_GROUPS, S5_GROUP)
    y = ug * d_skip.reshape(S5_GROUPS, S5_GROUP)
    for d in range(N_DIR):
        lr = a_re[d].astype(jnp.float32)
        li = a_im[d].astype(jnp.float32)
        dt = jnp.exp(log_dt[d].astype(jnp.float32))[:, None]
        mag = jnp.exp(lr * dt)
        abar_r = mag * jnp.cos(li * dt)
        abar_i = mag * jnp.sin(li * dt)
        den = lr * lr + li * li
        nr = abar_r - 1.0
        ni = abar_i
        coef_r = (nr * lr + ni * li) / den
        coef_i = (ni * lr - nr * li) / den
        bbar_r = coef_r[..., None] * b_re[d] - coef_i[..., None] * b_im[d]
        bbar_i = coef_r[..., None] * b_im[d] + coef_i[..., None] * b_re[d]
        bu_r = jnp.einsum('blgc,gpc->blgp', ug, bbar_r)
        bu_i = jnp.einsum('blgc,gpc->blgp', ug, bbar_i)
        ar = jnp.broadcast_to(abar_r, bu_r.shape)
        ai = jnp.broadcast_to(abar_i, bu_r.shape)
        _, _, hr, hi = lax.associative_scan(_clin_combine, (ar, ai, bu_r, bu_i), reverse=(d == 1), axis=1)
        y = y + jnp.einsum('blgp,gcp->blgc', hr, c_re[d]) - jnp.einsum('blgp,gcp->blgc', hi, c_im[d])
    y = jax.nn.gelu(y.reshape(Bsz, L, S5_WIDTH))
    return y * jax.nn.sigmoid(y @ glu_w + glu_b)


def hyena_filters(L, w1, b1, freq1, w2, b2, freq2, w3):
    pos = jnp.arange(L, dtype=jnp.float32)
    t = pos / max(L - 1, 1)
    w = (2.0 * math.pi / L) * pos
    bands = jnp.linspace(1e-4, HY_BANDS - 1, HY_BANDS, dtype=jnp.float32)
    ang = w[:, None] * bands
    feats = jnp.concatenate([t[:, None], jnp.cos(ang), -jnp.sin(ang)], axis=-1)
    hid = jnp.sin(freq1 * (feats @ w1 + b1))
    hid = jnp.sin(freq2 * (hid @ w2 + b2))
    k = hid @ w3
    max_decay = math.log(HY_DECAY_TARGET) / HY_FAST_DECAY
    min_decay = math.log(HY_DECAY_TARGET) / HY_SLOW_DECAY
    deltas = jnp.abs(jnp.linspace(min_decay, max_decay, HY_WIDTH, dtype=jnp.float32))
    k = k * jnp.exp(-t[:, None] * jnp.tile(deltas, N_DIR))
    return k[:, :HY_WIDTH], k[:, HY_WIDTH:]


def bidir_fft_conv(z, k_fwd, k_bwd):
    L = z.shape[1]
    zero = jnp.zeros((1, HY_WIDTH), jnp.float32)
    kern = jnp.concatenate([k_fwd[:1] + k_bwd[:1], k_fwd[1:], zero, k_bwd[:0:-1]], axis=0)
    zf = jnp.fft.rfft(z.astype(jnp.float32), n=2 * L, axis=1)
    kf = jnp.fft.rfft(kern, axis=0)
    return jnp.fft.irfft(zf * kf, n=2 * L, axis=1)[:, :L]


def hyena_mixer(pc, conv_w, conv_b, w1, b1, freq1, w2, b2, freq2, w3, bias):
    L = pc.shape[1]
    q = dwconv_centred(pc, conv_w, conv_b)
    x0, x1, v = jnp.split(q, 3, axis=-1)
    k_fwd, k_bwd = hyena_filters(L, w1, b1, freq1, w2, b2, freq2, w3)
    z = v * x1
    z = bidir_fft_conv(z, k_fwd, k_bwd) + z * bias
    return z * x0


def conv_ffn(h, w_up, conv_w, conv_b, w_down):
    u = dwconv_centred(h @ w_up, conv_w, conv_b)
    a, v = jnp.split(u, 2, axis=-1)
    return (jax.nn.gelu(a) * v) @ w_down


def setup_inputs(seed: int = 0) -> dict:
    key = jax.random.key(seed)
    ks = jax.random.split(key, 40)
    f32 = jnp.float32

    def nrm(k, shape, scale):
        return jax.random.normal(k, shape, f32) * scale

    def gain(k, shape):
        return 1.0 + 0.02 * jax.random.normal(k, shape, f32)

    a_target = jax.random.uniform(ks[9], (DEPTH, N_DIR, RG_WIDTH), f32, 0.9, 0.999)
    s = a_target ** (1.0 / RG_C)
    rg_lambda = jnp.log(s) - jnp.log1p(-s)
    n_idx = jnp.arange(S5_STATE, dtype=f32)
    return {
        "x": nrm(ks[0], (BATCH, SEQ, D_MODEL), 1.0),
        "norm1_g": gain(ks[1], (DEPTH, D_MODEL)),
        "w_in": nrm(ks[2], (DEPTH, D_MODEL, IN_COLS), D_MODEL ** -0.5),
        "rg_conv_w": nrm(ks[3], (DEPTH, RG_CONV, RG_WIDTH), RG_CONV ** -0.5),
        "rg_conv_b": nrm(ks[4], (DEPTH, RG_WIDTH), 0.01),
        "rg_wa": nrm(ks[5], (DEPTH, N_DIR, RG_HEADS, RG_HEAD_DIM, RG_HEAD_DIM), RG_HEAD_DIM ** -0.5),
        "rg_ba": nrm(ks[6], (DEPTH, N_DIR, RG_WIDTH), 0.01),
        "rg_wx": nrm(ks[7], (DEPTH, N_DIR, RG_HEADS, RG_HEAD_DIM, RG_HEAD_DIM), RG_HEAD_DIM ** -0.5),
        "rg_bx": nrm(ks[8], (DEPTH, N_DIR, RG_WIDTH), 0.01),
        "rg_lambda": rg_lambda,
        "s5_a_re": -0.5 + nrm(ks[10], (DEPTH, N_DIR, S5_GROUPS, S5_STATE), 0.01),
        "s5_a_im": math.pi * n_idx + nrm(ks[11], (DEPTH, N_DIR, S5_GROUPS, S5_STATE), 0.01),
        "s5_log_dt": jax.random.uniform(ks[12], (DEPTH, N_DIR, S5_GROUPS), f32, math.log(1e-3), math.log(1e-1)),
        "s5_b_re": nrm(ks[13], (DEPTH, N_DIR, S5_GROUPS, S5_STATE, S5_GROUP), (2 * S5_GROUP) ** -0.5),
        "s5_b_im": nrm(ks[14], (DEPTH, N_DIR, S5_GROUPS, S5_STATE, S5_GROUP), (2 * S5_GROUP) ** -0.5),
        "s5_c_re": nrm(ks[15], (DEPTH, N_DIR, S5_GROUPS, S5_GROUP, S5_STATE), (2 * S5_STATE) ** -0.5),
        "s5_c_im": nrm(ks[16], (DEPTH, N_DIR, S5_GROUPS, S5_GROUP, S5_STATE), (2 * S5_STATE) ** -0.5),
        "s5_d": nrm(ks[17], (DEPTH, S5_WIDTH), 1.0),
        "s5_glu_w": nrm(ks[18], (DEPTH, S5_WIDTH, S5_WIDTH), S5_WIDTH ** -0.5),
        "s5_glu_b": nrm(ks[19], (DEPTH, S5_WIDTH), 0.01),
        "hy_conv_w": nrm(ks[20], (DEPTH, HY_CONV, 3 * HY_WIDTH), HY_CONV ** -0.5),
        "hy_conv_b": nrm(ks[21], (DEPTH, 3 * HY_WIDTH), 0.01),
        "hy_filt_w1": nrm(ks[22], (DEPTH, HY_FEAT, HY_FILT_HIDDEN), HY_FEAT ** -0.5),
        "hy_filt_b1": nrm(ks[23], (DEPTH, HY_FILT_HIDDEN), 0.1),
        "hy_filt_freq1": gain(ks[24], (DEPTH, HY_FILT_HIDDEN)),
        "hy_filt_w2": nrm(ks[25], (DEPTH, HY_FILT_HIDDEN, HY_FILT_HIDDEN), HY_FILT_HIDDEN ** -0.5),
        "hy_filt_b2": nrm(ks[26], (DEPTH, HY_FILT_HIDDEN), 0.1),
        "hy_filt_freq2": gain(ks[27], (DEPTH, HY_FILT_HIDDEN)),
        "hy_filt_w3": nrm(ks[28], (DEPTH, HY_FILT_HIDDEN, N_DIR * HY_WIDTH), 0.1 * HY_FILT_HIDDEN ** -0.5),
        "hy_bias": nrm(ks[29], (DEPTH, HY_WIDTH), 1.0),
        "mix_norm_g": gain(ks[30], (DEPTH, D_MIX)),
        "w_out": nrm(ks[31], (DEPTH, D_MIX, D_MODEL), D_MIX ** -0.5),
        "norm2_g": gain(ks[32], (DEPTH, D_MODEL)),
        "w_up": nrm(ks[33], (DEPTH, D_MODEL, 2 * D_FF), D_MODEL ** -0.5),
        "ffn_conv_w": nrm(ks[34], (DEPTH, FFN_CONV, 2 * D_FF), FFN_CONV ** -0.5),
        "ffn_conv_b": nrm(ks[35], (DEPTH, 2 * D_FF), 0.01),
        "w_down": nrm(ks[36], (DEPTH, D_FF, D_MODEL), D_FF ** -0.5),
        "final_norm_g": gain(ks[37], (D_MODEL,)),
    }


def reference(x, norm1_g, w_in, rg_conv_w, rg_conv_b, rg_wa, rg_ba, rg_wx, rg_bx, rg_lambda,
              s5_a_re, s5_a_im, s5_log_dt, s5_b_re, s5_b_im, s5_c_re, s5_c_im, s5_d, s5_glu_w, s5_glu_b,
              hy_conv_w, hy_conv_b, hy_filt_w1, hy_filt_b1, hy_filt_freq1, hy_filt_w2, hy_filt_b2,
              hy_filt_freq2, hy_filt_w3, hy_bias, mix_norm_g, w_out,
              norm2_g, w_up, ffn_conv_w, ffn_conv_b, w_down, final_norm_g):
    h = x.astype(jnp.float32)
    split_idx = [RG_WIDTH, 2 * RG_WIDTH, 2 * RG_WIDTH + S5_WIDTH]
    for l in range(DEPTH):
        n = rmsnorm(h, norm1_g[l])
        proj = n @ w_in[l]
        xa, ga, ub, pc = jnp.split(proj, split_idx, axis=-1)
        ya = rglru_mixer(xa, ga, rg_conv_w[l], rg_conv_b[l], rg_wa[l], rg_ba[l],
                         rg_wx[l], rg_bx[l], rg_lambda[l])
        yb = s5_mixer(ub, s5_a_re[l], s5_a_im[l], s5_log_dt[l], s5_b_re[l], s5_b_im[l],
                      s5_c_re[l], s5_c_im[l], s5_d[l], s5_glu_w[l], s5_glu_b[l])
        yc = hyena_mixer(pc, hy_conv_w[l], hy_conv_b[l], hy_filt_w1[l], hy_filt_b1[l], hy_filt_freq1[l],
                         hy_filt_w2[l], hy_filt_b2[l], hy_filt_freq2[l], hy_filt_w3[l], hy_bias[l])
        ymix = jnp.concatenate([rms_nogain(ya), rms_nogain(yb), rms_nogain(yc)], axis=-1) * mix_norm_g[l]
        h = h + ymix @ w_out[l]
        h = h + conv_ffn(rmsnorm(h, norm2_g[l]), w_up[l], ffn_conv_w[l], ffn_conv_b[l], w_down[l])
    return rmsnorm(h, final_norm_g).astype(x.dtype)
```

```python
import functools
import math

import numpy as np
import jax
import jax.numpy as jnp
from jax import lax
from jax.experimental import pallas as pl
from jax.experimental.pallas import tpu as pltpu

F32 = jnp.float32
BF16 = jnp.bfloat16

RMS_EPS = 1e-6
RG_WIDTH = 384
RG_HEADS = 6
RG_C = 8.0
S5_WIDTH = 384
S5_GROUP = 16
S5_GROUPS = 24
S5_STATE = 64
S5_NSTATE = S5_GROUPS * S5_STATE
HY_WIDTH = 256
HY_BANDS = 16
HY_FAST_DECAY = 0.3
HY_SLOW_DECAY = 1.5
HY_DECAY_TARGET = 1e-2
D_FF = 2816

LANES = 128
SUBLANES = 8
HALO = SUBLANES
VMEM_LIMIT = 56 * 1024 * 1024

HIGHEST = lax.Precision.HIGHEST


def _dot(a, b, precision=None):
    return jnp.dot(a, b, preferred_element_type=F32, precision=precision)


def _gelu(x):
    c = math.sqrt(2.0 / math.pi)
    return 0.5 * x * (1.0 + jnp.tanh(c * (x + 0.044715 * (x * x * x))))


def _sigmoid(x):
    return 1.0 / (1.0 + jnp.exp(-x))


def _rms_nogain(x):
    return x * lax.rsqrt(jnp.mean(x * x, axis=-1, keepdims=True) + RMS_EPS)


def _params(sem):
    return pltpu.CompilerParams(dimension_semantics=sem, vmem_limit_bytes=VMEM_LIMIT)


def _full(shape):
    nd = len(shape)
    return pl.BlockSpec(shape, lambda *_: (0,) * nd)


def _halo_specs(tm, d):
    blocks_per_tile = tm // HALO

    def main(i, *_):
        return (i, 0)

    def prev(i, *_):
        return (jnp.maximum(i * blocks_per_tile - 1, 0), 0)

    def nxt_factory(nblocks):
        def nxt(i, *_):
            return (jnp.minimum((i + 1) * blocks_per_tile, nblocks - 1), 0)
        return nxt

    return main, prev, nxt_factory


def _fill_normed(x_scr, h_ref, hp_ref, hn_ref, g, first, last, tm):
    def norm(x):
        return x * lax.rsqrt(jnp.mean(x * x, axis=-1, keepdims=True) + RMS_EPS) * g

    x_scr[0:HALO, :] = norm(jnp.where(first, 0.0, hp_ref[...])).astype(x_scr.dtype)
    x_scr[HALO:HALO + tm, :] = norm(h_ref[...]).astype(x_scr.dtype)
    x_scr[HALO + tm:2 * HALO + tm, :] = norm(jnp.where(last, 0.0, hn_ref[...])).astype(x_scr.dtype)


def _conv_taps(u_scr, lo, hi, w_ref, b_ref, left, tm):
    k = w_ref.shape[0]
    out = b_ref[:, lo:hi]
    for j in range(k):
        out = out + u_scr[pl.ds(HALO + j - left, tm), lo:hi] * w_ref[j:j + 1, lo:hi]
    return out


def _proj_kernel(h_ref, hp_ref, hn_ref, g_ref, w_ref, rcw_ref, rcb_ref, hcw_ref, hcb_ref,
                 urg_ref, ga_ref, ub_ref, x0_ref, z_ref, x_scr, u_scr, *, tiles_per_seq, tm):
    i = pl.program_id(0)
    first = (i % tiles_per_seq) == 0
    last = (i % tiles_per_seq) == tiles_per_seq - 1
    _fill_normed(x_scr, h_ref, hp_ref, hn_ref, g_ref[...], first, last, tm)
    u_scr[...] = _dot(x_scr[...].astype(BF16), w_ref[...])
    a0, a1, a2, a3 = 0, RG_WIDTH, 2 * RG_WIDTH, 2 * RG_WIDTH + S5_WIDTH
    k = rcw_ref.shape[0]
    acc = rcb_ref[...]
    for j in range(k):
        acc = acc + u_scr[pl.ds(HALO + j - 1, tm), a0:a1] * rcw_ref[j:j + 1, :]
    urg_ref[...] = acc
    ga_ref[...] = u_scr[pl.ds(HALO, tm), a1:a2]
    ub_ref[...] = u_scr[pl.ds(HALO, tm), a2:a3]
    q = []
    for part in range(3):
        lo = part * HY_WIDTH
        acc = hcb_ref[:, lo:lo + HY_WIDTH]
        for j in range(hcw_ref.shape[0]):
            acc = acc + (u_scr[pl.ds(HALO + j - 1, tm), a3 + lo:a3 + lo + HY_WIDTH]
                         * hcw_ref[j:j + 1, lo:lo + HY_WIDTH])
        q.append(acc)
    x0_ref[...] = q[0]
    z_ref[...] = q[2] * q[1]


def _proj(h, g, w_in, rcw, rcb, hcw, hcb, seq, tm):
    n, d = h.shape
    cols = w_in.shape[1]
    main, prev, nxt_f = _halo_specs(tm, d)
    nxt = nxt_f(n // HALO)
    row = lambda c: pl.BlockSpec((tm, c), lambda i: (i, 0))
    outs = [jax.ShapeDtypeStruct((n, c), F32) for c in (RG_WIDTH, RG_WIDTH, S5_WIDTH, HY_WIDTH, HY_WIDTH)]
    return pl.pallas_call(
        functools.partial(_proj_kernel, tiles_per_seq=seq // tm, tm=tm),
        grid=(n // tm,),
        in_specs=[pl.BlockSpec((tm, d), main), pl.BlockSpec((HALO, d), prev), pl.BlockSpec((HALO, d), nxt),
                  _full((1, d)), _full((d, cols)), _full(rcw.shape), _full(rcb.shape),
                  _full(hcw.shape), _full(hcb.shape)],
        out_specs=[row(RG_WIDTH), row(RG_WIDTH), row(S5_WIDTH), row(HY_WIDTH), row(HY_WIDTH)],
        out_shape=outs,
        scratch_shapes=[pltpu.VMEM((tm + 2 * HALO, d), F32), pltpu.VMEM((tm + 2 * HALO, cols), F32)],
        compiler_params=_params(("parallel",)),
        name="proj",
    )(h, h, h, g, w_in, rcw, rcb, hcw, hcb)


def _lo_mask():
    return lax.broadcasted_iota(jnp.int32, (SUBLANES, LANES), 0) < (SUBLANES // 2)


def _swap_halves(x):
    return pltpu.roll(x, SUBLANES // 2, 0)


def _rglru_kernel(u_ref, w_ref, bias_ref, sp_ref, o_ref, a_scr, b_scr, carry_scr, *, t, nb, reverse):
    i = pl.program_id(0)
    nm = RG_WIDTH // LANES

    @pl.when(i == 0)
    def _():
        carry_scr[...] = jnp.zeros_like(carry_scr)

    u = u_ref[...].reshape(nb * t, RG_WIDTH)
    gates = _dot(u.astype(BF16), w_ref[...]) + bias_ref[...]
    r = _sigmoid(gates[:, :RG_WIDTH])
    gi = _sigmoid(gates[:, RG_WIDTH:])
    log_a = (-RG_C) * r * sp_ref[...]
    a = jnp.exp(log_a)
    mult = jnp.sqrt(-jnp.tanh(log_a) * (a * a + 1.0))
    bb = mult * (gi * u)
    for b in range(nb):
        for m in range(nm):
            a_scr[m, pl.ds(b, t, stride=nb), :] = a[b * t:(b + 1) * t, m * LANES:(m + 1) * LANES]
            b_scr[m, pl.ds(b, t, stride=nb), :] = bb[b * t:(b + 1) * t, m * LANES:(m + 1) * LANES]

    lo = _lo_mask()
    ntile = t // 2

    def body(s, carry):
        jj = (ntile - 1 - s) if reverse else s
        row = pl.multiple_of(jj * SUBLANES, SUBLANES)
        new = []
        for m in range(nm):
            at = a_scr[m, pl.ds(row, SUBLANES), :]
            bt = b_scr[m, pl.ds(row, SUBLANES), :]
            h1 = at * _swap_halves(carry[m]) + bt
            h2 = at * _swap_halves(h1) + bt
            out = jnp.where(lo, h2, h1) if reverse else jnp.where(lo, h1, h2)
            b_scr[m, pl.ds(row, SUBLANES), :] = out
            new.append(out)
        return tuple(new)

    carry = lax.fori_loop(0, ntile, body, tuple(carry_scr[m] for m in range(nm)))
    for m in range(nm):
        carry_scr[m] = carry[m]
    for b in range(nb):
        for m in range(nm):
            o_ref[b, :, m * LANES:(m + 1) * LANES] = b_scr[m, pl.ds(b, t, stride=nb), :]


def _rglru_dir(u3, w, bias, sp, t, reverse):
    nb, seq, c = u3.shape
    nchunk = seq // t
    idx = (lambda i: (0, nchunk - 1 - i, 0)) if reverse else (lambda i: (0, i, 0))
    nm = c // LANES
    return pl.pallas_call(
        functools.partial(_rglru_kernel, t=t, nb=nb, reverse=reverse),
        grid=(nchunk,),
        in_specs=[pl.BlockSpec((nb, t, c), idx), _full(w.shape), _full(bias.shape), _full(sp.shape)],
        out_specs=pl.BlockSpec((nb, t, c), idx),
        out_shape=jax.ShapeDtypeStruct((nb, seq, c), F32),
        scratch_shapes=[pltpu.VMEM((nm, nb * t, LANES), F32), pltpu.VMEM((nm, nb * t, LANES), F32),
                        pltpu.VMEM((nm, SUBLANES, LANES), F32)],
        compiler_params=_params(("arbitrary",)),
        name="rglru_bwd" if reverse else "rglru_fwd",
    )(u3, w, bias, sp)


S5_NBLK = S5_NSTATE // LANES
S5_GRP = 4


def _s5_kernel(u_ref, wb_ref, wcr_ref, wci_ref, ar_ref, ai_ref, o_ref,
               u_scr, bu_scr, y_scr, carry_scr, *, t, nb, reverse):
    i = pl.program_id(0)
    nm = S5_WIDTH // LANES
    per = S5_NBLK // nm

    @pl.when(i == 0)
    def _():
        carry_scr[...] = jnp.zeros_like(carry_scr)

    for b in range(nb):
        for m in range(nm):
            u_scr[m, pl.ds(b, t, stride=nb), :] = u_ref[b, :, m * LANES:(m + 1) * LANES]
    for n in range(S5_NBLK):
        res = _dot(u_scr[n // per].astype(BF16), wb_ref[n])
        bu_scr[2 * n] = res[:, :LANES]
        bu_scr[2 * n + 1] = res[:, LANES:]

    lo = _lo_mask()
    ntile = t // 2
    for g in range(S5_NBLK // S5_GRP):
        blocks = list(range(g * S5_GRP, (g + 1) * S5_GRP))
        ars = [ar_ref[n] for n in blocks]
        ais = [ai_ref[n] for n in blocks]

        def body(s, carry, blocks=blocks, ars=ars, ais=ais):
            jj = (ntile - 1 - s) if reverse else s
            row = pl.multiple_of(jj * SUBLANES, SUBLANES)
            new = []
            for k, n in enumerate(blocks):
                ar, ai = ars[k], ais[k]
                tr = bu_scr[2 * n, pl.ds(row, SUBLANES), :]
                ti = bu_scr[2 * n + 1, pl.ds(row, SUBLANES), :]
                pr = _swap_halves(carry[2 * k])
                pi_ = _swap_halves(carry[2 * k + 1])
                h1r = ar * pr - ai * pi_ + tr
                h1i = ar * pi_ + ai * pr + ti
                qr = _swap_halves(h1r)
                qi = _swap_halves(h1i)
                h2r = ar * qr - ai * qi + tr
                h2i = ar * qi + ai * qr + ti
                if reverse:
                    outr = jnp.where(lo, h2r, h1r)
                    outi = jnp.where(lo, h2i, h1i)
                else:
                    outr = jnp.where(lo, h1r, h2r)
                    outi = jnp.where(lo, h1i, h2i)
                bu_scr[2 * n, pl.ds(row, SUBLANES), :] = outr
                bu_scr[2 * n + 1, pl.ds(row, SUBLANES), :] = outi
                new += [outr, outi]
            return tuple(new)

        init = []
        for n in blocks:
            init += [carry_scr[2 * n], carry_scr[2 * n + 1]]
        carry = lax.fori_loop(0, ntile, body, tuple(init))
        for k, n in enumerate(blocks):
            carry_scr[2 * n] = carry[2 * k]
            carry_scr[2 * n + 1] = carry[2 * k + 1]

    for m in range(nm):
        acc = None
        for n in range(m * per, (m + 1) * per):
            term = (_dot(bu_scr[2 * n].astype(BF16), wcr_ref[n])
                    + _dot(bu_scr[2 * n + 1].astype(BF16), wci_ref[n]))
            acc = term if acc is None else acc + term
        y_scr[m] = acc
    for b in range(nb):
        for m in range(nm):
            o_ref[b, :, m * LANES:(m + 1) * LANES] = y_scr[m, pl.ds(b, t, stride=nb), :]


def _s5_dir(u3, wb, wcr, wci, ar, ai, t, reverse):
    nb, seq, c = u3.shape
    nchunk = seq // t
    idx = (lambda i: (0, nchunk - 1 - i, 0)) if reverse else (lambda i: (0, i, 0))
    nm = c // LANES
    return pl.pallas_call(
        functools.partial(_s5_kernel, t=t, nb=nb, reverse=reverse),
        grid=(nchunk,),
        in_specs=[pl.BlockSpec((nb, t, c), idx), _full(wb.shape), _full(wcr.shape), _full(wci.shape),
                  _full(ar.shape), _full(ai.shape)],
        out_specs=pl.BlockSpec((nb, t, c), idx),
        out_shape=jax.ShapeDtypeStruct((nb, seq, c), F32),
        scratch_shapes=[pltpu.VMEM((nm, nb * t, LANES), F32),
                        pltpu.VMEM((2 * S5_NBLK, nb * t, LANES), F32),
                        pltpu.VMEM((nm, nb * t, LANES), F32),
                        pltpu.VMEM((2 * S5_NBLK, SUBLANES, LANES), F32)],
        compiler_params=_params(("arbitrary",)),
        name="s5_bwd" if reverse else "s5_fwd",
    )(u3, wb, wcr, wci, ar, ai)


def _s5_weights(a_re, a_im, log_dt, b_re, b_im, c_re, c_im):
    lr = a_re.astype(F32)
    li = a_im.astype(F32)
    dt = jnp.exp(log_dt.astype(F32))[:, None]
    mag = jnp.exp(lr * dt)
    abar_r = mag * jnp.cos(li * dt)
    abar_i = mag * jnp.sin(li * dt)
    den = lr * lr + li * li
    nr = abar_r - 1.0
    ni = abar_i
    coef_r = (nr * lr + ni * li) / den
    coef_i = (ni * lr - nr * li) / den
    bbar_r = coef_r[..., None] * b_re - coef_i[..., None] * b_im
    bbar_i = coef_r[..., None] * b_im + coef_i[..., None] * b_re
    eye = jnp.eye(S5_GROUPS, dtype=F32)
    bd_r = jnp.einsum('gpc,gh->gchp', bbar_r, eye).reshape(S5_WIDTH, S5_NSTATE)
    bd_i = jnp.einsum('gpc,gh->gchp', bbar_i, eye).reshape(S5_WIDTH, S5_NSTATE)
    cd_r = jnp.einsum('gcp,gh->gphc', c_re, eye).reshape(S5_NSTATE, S5_WIDTH)
    cd_i = jnp.einsum('gcp,gh->gphc', c_im, eye).reshape(S5_NSTATE, S5_WIDTH)
    per = S5_NBLK // (S5_WIDTH // LANES)
    wb, wcr, wci = [], [], []
    for n in range(S5_NBLK):
        m = n // per
        rows = slice(m * LANES, (m + 1) * LANES)
        cols = slice(n * LANES, (n + 1) * LANES)
        wb.append(jnp.concatenate([bd_r[rows, cols], bd_i[rows, cols]], axis=1))
        wcr.append(cd_r[cols, rows])
        wci.append(-cd_i[cols, rows])
    wb = jnp.stack(wb).astype(BF16)
    wcr = jnp.stack(wcr).astype(BF16)
    wci = jnp.stack(wci).astype(BF16)
    ar = jnp.broadcast_to(abar_r.reshape(S5_NBLK, 1, LANES), (S5_NBLK, SUBLANES, LANES))
    ai = jnp.broadcast_to(abar_i.reshape(S5_NBLK, 1, LANES), (S5_NBLK, SUBLANES, LANES))
    return wb, wcr, wci, ar, ai


def _odd8(n):
    p = -(-n // SUBLANES)
    if p % 2 == 0:
        p += 1
    return p * SUBLANES


class _FftPlan:
    def __init__(self, seq):
        self.seq = seq
        r = 1
        while r * r < seq:
            r *= 2
        self.r = r
        self.nq = seq // r
        self.q = 2 * seq // r
        self.k1n = self.q // 2 + 1
        self.kp = -(-self.k1n // SUBLANES) * SUBLANES
        self.zpitch = _odd8(r)
        self.apitch = _odd8(2 * self.kp)
        self.cpitch = _odd8(2 * r)
        p = 2 * seq
        n1 = np.arange(self.nq)[None, :]
        k1 = np.arange(self.k1n)[:, None]
        ang = 2.0 * np.pi * n1 * k1 / self.q
        f1 = np.zeros((2 * self.kp, self.nq))
        f1[:self.k1n] = np.cos(ang)
        f1[self.kp:self.kp + self.k1n] = -np.sin(ang)
        self.f1 = f1
        w = np.full((self.k1n,), 2.0)
        w[0] = 1.0
        w[-1] = 1.0
        g1 = np.zeros((self.nq, 2 * self.kp))
        g1[:, :self.k1n] = (np.cos(ang) * w[:, None] / p).T
        g1[:, self.kp:self.kp + self.k1n] = (-np.sin(ang) * w[:, None] / p).T
        self.g1 = g1
        kk = np.arange(self.k1n)[:, None, None]
        k2 = np.arange(r)[None, :, None]
        n2 = np.arange(r)[None, None, :]
        ph = 2.0 * np.pi * (n2 * k2 / r + n2 * kk / p)
        tr, ti = np.cos(ph), -np.sin(ph)
        self.m2 = np.concatenate([np.concatenate([tr, -ti], axis=2),
                                  np.concatenate([ti, tr], axis=2)], axis=1)
        ur, ui = np.transpose(tr, (0, 2, 1)), -np.transpose(ti, (0, 2, 1))
        self.m2i = np.concatenate([np.concatenate([ur, -ui], axis=2),
                                   np.concatenate([ui, ur], axis=2)], axis=1)


def _fft_forward(plan, src_ref, f1_ref, m2_ref, zp, as_, emit):
    r, nq, kp = plan.r, plan.nq, plan.kp
    for n1 in range(nq):
        zp[n1 * plan.zpitch:n1 * plan.zpitch + r, :] = src_ref[n1 * r:(n1 + 1) * r, :]

    def stage1(n2, c):
        slab = zp[pl.ds(n2, nq, stride=plan.zpitch), :]
        a = _dot(f1_ref[...].astype(BF16), slab.astype(BF16))
        as_[pl.ds(pl.multiple_of(n2 * plan.apitch, SUBLANES), 2 * kp), :] = a
        return c

    lax.fori_loop(0, r, stage1, 0)

    def stage2(k1, c):
        sr = as_[pl.ds(k1, r, stride=plan.apitch), :]
        si = as_[pl.ds(kp + k1, r, stride=plan.apitch), :]
        s = jnp.concatenate([sr, si], axis=0).astype(BF16)
        emit(k1, _dot(m2_ref[k1].astype(BF16), s))
        return c

    lax.fori_loop(0, plan.k1n, stage2, 0)


def _hy_spec_kernel(kf_ref, kb_ref, f1_ref, m2_ref, o_ref, zp, as_, *, plan):
    r = plan.r

    def emit_f(k1, x):
        o_ref[0, k1] = x

    def emit_b(k1, x):
        sign = jnp.where(lax.broadcasted_iota(jnp.int32, (2 * r, LANES), 0) < r, 1.0, -1.0)
        o_ref[0, k1] = o_ref[0, k1] + sign * x

    _fft_forward(plan, kf_ref, f1_ref, m2_ref, zp, as_, emit_f)
    _fft_forward(plan, kb_ref, f1_ref, m2_ref, zp, as_, emit_b)


def _hy_conv_kernel(z_ref, spec_ref, f1_ref, m2_ref, m2i_ref, g1_ref, o_ref, zp, as_, cs, *, plan):
    r, nq, kp, k1n = plan.r, plan.nq, plan.kp, plan.k1n
    if kp > k1n:
        cs[k1n * plan.cpitch:kp * plan.cpitch, :] = jnp.zeros(((kp - k1n) * plan.cpitch, LANES), F32)

    def emit(k1, x):
        kf = spec_ref[0, k1]
        xr, xi = x[:r], x[r:]
        kr, ki = kf[:r], kf[r:]
        prod = jnp.concatenate([xr * kr - xi * ki, xr * ki + xi * kr], axis=0).astype(BF16)
        c = _dot(m2i_ref[k1].astype(BF16), prod)
        cs[pl.ds(pl.multiple_of(k1 * plan.cpitch, SUBLANES), 2 * r), :] = c

    _fft_forward(plan, z_ref.at[0], f1_ref, m2_ref, zp, as_, emit)

    def stage3(n2, c):
        cr = cs[pl.ds(n2, kp, stride=plan.cpitch), :]
        ci = cs[pl.ds(r + n2, kp, stride=plan.cpitch), :]
        y = _dot(g1_ref[...].astype(BF16), jnp.concatenate([cr, ci], axis=0).astype(BF16))
        zp[pl.ds(n2, nq, stride=plan.zpitch), :] = y
        return c

    lax.fori_loop(0, r, stage3, 0)
    for n1 in range(nq):
        o_ref[0, n1 * r:(n1 + 1) * r, :] = zp[n1 * plan.zpitch:n1 * plan.zpitch + r, :]


def _hy_filter_kernel(feat_ref, w1_ref, b1_ref, f1_ref, w2_ref, b2_ref, f2_ref, w3_ref, dl_ref, o_ref):
    feats = feat_ref[...]
    hid = jnp.sin(f1_ref[...] * (_dot(feats, w1_ref[...], HIGHEST) + b1_ref[...]))
    hid = jnp.sin(f2_ref[...] * (_dot(hid, w2_ref[...], HIGHEST) + b2_ref[...]))
    k = _dot(hid, w3_ref[...], HIGHEST)
    t = feats[:, 0:1]
    o_ref[...] = k * jnp.exp(-(t * dl_ref[...]))


def _hyena_features(seq):
    pos = np.arange(seq, dtype=np.float64)
    t = pos / max(seq - 1, 1)
    w = (2.0 * math.pi / seq) * pos
    bands = np.linspace(1e-4, HY_BANDS - 1, HY_BANDS, dtype=np.float64)
    ang = w[:, None] * bands
    feats = np.concatenate([t[:, None], np.cos(ang), -np.sin(ang)], axis=-1).astype(np.float32)
    out = np.zeros((seq, LANES), np.float32)
    out[:, :feats.shape[1]] = feats
    return out


def _hyena_deltas():
    max_decay = math.log(HY_DECAY_TARGET) / HY_FAST_DECAY
    min_decay = math.log(HY_DECAY_TARGET) / HY_SLOW_DECAY
    deltas = np.abs(np.linspace(min_decay, max_decay, HY_WIDTH, dtype=np.float64))
    return np.tile(deltas, 2)[None, :].astype(np.float32)


def _pad2(x, rows, cols):
    return jnp.zeros((rows, cols), F32).at[:x.shape[0], :x.shape[1]].set(x.astype(F32))


def _hyena_filters(seq, w1, b1, fr1, w2, b2, fr2, w3, tf):
    feats = jnp.asarray(_hyena_features(seq))
    hid = LANES
    args = (feats, _pad2(w1, LANES, hid), _pad2(b1[None], 1, hid), _pad2(fr1[None], 1, hid),
            _pad2(w2, hid, hid), _pad2(b2[None], 1, hid), _pad2(fr2[None], 1, hid),
            _pad2(w3, hid, 2 * HY_WIDTH), jnp.asarray(_hyena_deltas()))
    return pl.pallas_call(
        _hy_filter_kernel,
        grid=(seq // tf,),
        in_specs=[pl.BlockSpec((tf, LANES), lambda i: (i, 0))] + [_full(a.shape) for a in args[1:]],
        out_specs=pl.BlockSpec((tf, 2 * HY_WIDTH), lambda i: (i, 0)),
        out_shape=jax.ShapeDtypeStruct((seq, 2 * HY_WIDTH), F32),
        compiler_params=_params(("parallel",)),
        name="hy_filter",
    )(*args)


def _hyena_conv(z3, filt, plan):
    nb, seq, c = z3.shape
    nh = c // LANES
    f1 = jnp.asarray(plan.f1, F32)
    m2 = jnp.asarray(plan.m2, F32)
    m2i = jnp.asarray(plan.m2i, F32)
    g1 = jnp.asarray(plan.g1, F32)
    r, kp, k1n = plan.r, plan.kp, plan.k1n
    zp_shape = (plan.nq * plan.zpitch, LANES)
    as_shape = (r * plan.apitch, LANES)
    cs_shape = (kp * plan.cpitch, LANES)
    spec = pl.pallas_call(
        functools.partial(_hy_spec_kernel, plan=plan),
        grid=(nh,),
        in_specs=[pl.BlockSpec((seq, LANES), lambda j: (0, j)),
                  pl.BlockSpec((seq, LANES), lambda j: (0, nh + j)),
                  _full(f1.shape), _full(m2.shape)],
        out_specs=pl.BlockSpec((1, k1n, 2 * r, LANES), lambda j: (j, 0, 0, 0)),
        out_shape=jax.ShapeDtypeStruct((nh, k1n, 2 * r, LANES), F32),
        scratch_shapes=[pltpu.VMEM(zp_shape, F32), pltpu.VMEM(as_shape, F32)],
        compiler_params=_params(("parallel",)),
        name="hy_spec",
    )(filt, filt, f1, m2)
    return pl.pallas_call(
        functools.partial(_hy_conv_kernel, plan=plan),
        grid=(nb, nh),
        in_specs=[pl.BlockSpec((1, seq, LANES), lambda b, j: (b, 0, j)),
                  pl.BlockSpec((1, k1n, 2 * r, LANES), lambda b, j: (j, 0, 0, 0)),
                  _full(f1.shape), _full(m2.shape), _full(m2i.shape), _full(g1.shape)],
        out_specs=pl.BlockSpec((1, seq, LANES), lambda b, j: (b, 0, j)),
        out_shape=jax.ShapeDtypeStruct((nb, seq, c), F32),
        scratch_shapes=[pltpu.VMEM(zp_shape, F32), pltpu.VMEM(as_shape, F32), pltpu.VMEM(cs_shape, F32)],
        compiler_params=_params(("parallel", "parallel")),
        name="hy_conv",
    )(z3, spec, f1, m2, m2i, g1)


def _mix_kernel(h_ref, hf_ref, hb_ref, ga_ref, yf_ref, yb_ref, ub_ref, yc_ref, z_ref, x0_ref,
                d_ref, gw_ref, gb_ref, hbias_ref, mg_ref, wo_ref, o_ref):
    ya = (hf_ref[...] + hb_ref[...]) * _gelu(ga_ref[...])
    yb = _gelu(ub_ref[...] * d_ref[...] + yf_ref[...] + yb_ref[...])
    yb = yb * _sigmoid(_dot(yb.astype(BF16), gw_ref[...]) + gb_ref[...])
    z = z_ref[...]
    yc = (yc_ref[...] + z * hbias_ref[...]) * x0_ref[...]
    a1, a2 = RG_WIDTH, RG_WIDTH + S5_WIDTH
    na = (_rms_nogain(ya) * mg_ref[:, :a1]).astype(BF16)
    nb = (_rms_nogain(yb) * mg_ref[:, a1:a2]).astype(BF16)
    nc = (_rms_nogain(yc) * mg_ref[:, a2:]).astype(BF16)
    out = _dot(na, wo_ref[:a1, :]) + _dot(nb, wo_ref[a1:a2, :]) + _dot(nc, wo_ref[a2:, :])
    o_ref[...] = h_ref[...] + out


def _mix(h, hf, hb, ga, yf, yb, ub, yc, z, x0, d, gw, gb, hbias, mg, wo, tm):
    n, dm = h.shape
    rows = [h, hf, hb, ga, yf, yb, ub, yc, z, x0]
    consts = [d, gw, gb, hbias, mg, wo]
    return pl.pallas_call(
        _mix_kernel,
        grid=(n // tm,),
        in_specs=[pl.BlockSpec((tm, a.shape[1]), lambda i: (i, 0)) for a in rows]
                 + [_full(a.shape) for a in consts],
        out_specs=pl.BlockSpec((tm, dm), lambda i: (i, 0)),
        out_shape=jax.ShapeDtypeStruct((n, dm), F32),
        compiler_params=_params(("parallel",)),
        name="mix",
    )(*rows, *consts)


def _ffn_kernel(h_ref, hp_ref, hn_ref, g_ref, wa_ref, wv_ref, cwa_ref, cwv_ref, cba_ref, cbv_ref,
                wd_ref, fg_ref, o_ref, x_scr, ua_scr, uv_scr, acc_scr, *, tiles_per_seq, tm, final):
    i = pl.program_id(0)
    c = pl.program_id(1)
    nc = pl.num_programs(1)

    @pl.when(c == 0)
    def _():
        first = (i % tiles_per_seq) == 0
        last = (i % tiles_per_seq) == tiles_per_seq - 1
        _fill_normed(x_scr, h_ref, hp_ref, hn_ref, g_ref[...], first, last, tm)
        acc_scr[...] = jnp.zeros_like(acc_scr)

    x = x_scr[...].astype(BF16)
    ua_scr[...] = _dot(x, wa_ref[...])
    uv_scr[...] = _dot(x, wv_ref[...])
    a = cba_ref[...]
    v = cbv_ref[...]
    for j in range(cwa_ref.shape[0]):
        a = a + ua_scr[pl.ds(HALO + j - 1, tm), :] * cwa_ref[j:j + 1, :]
        v = v + uv_scr[pl.ds(HALO + j - 1, tm), :] * cwv_ref[j:j + 1, :]
    gated = (_gelu(a) * v).astype(BF16)
    acc_scr[...] += _dot(gated, wd_ref[...])

    @pl.when(c == nc - 1)
    def _():
        out = h_ref[...] + acc_scr[...]
        if final:
            out = out * lax.rsqrt(jnp.mean(out * out, axis=-1, keepdims=True) + RMS_EPS) * fg_ref[...]
        o_ref[...] = out


def _ffn(h, g, w_up, conv_w, conv_b, w_down, final_g, seq, tm, nchunk, final):
    n, d = h.shape
    fc = D_FF // nchunk
    main, prev, nxt_f = _halo_specs(tm, d)
    nxt = nxt_f(n // HALO)
    return pl.pallas_call(
        functools.partial(_ffn_kernel, tiles_per_seq=seq // tm, tm=tm, final=final),
        grid=(n // tm, nchunk),
        in_specs=[pl.BlockSpec((tm, d), main), pl.BlockSpec((HALO, d), prev), pl.BlockSpec((HALO, d), nxt),
                  _full((1, d)),
                  pl.BlockSpec((d, fc), lambda i, c: (0, c)),
                  pl.BlockSpec((d, fc), lambda i, c: (0, nchunk + c)),
                  pl.BlockSpec((conv_w.shape[0], fc), lambda i, c: (0, c)),
                  pl.BlockSpec((conv_w.shape[0], fc), lambda i, c: (0, nchunk + c)),
                  pl.BlockSpec((1, fc), lambda i, c: (0, c)),
                  pl.BlockSpec((1, fc), lambda i, c: (0, nchunk + c)),
                  pl.BlockSpec((fc, d), lambda i, c: (c, 0)),
                  _full((1, d))],
        out_specs=pl.BlockSpec((tm, d), lambda i, c: (i, 0)),
        out_shape=jax.ShapeDtypeStruct((n, d), F32),
        scratch_shapes=[pltpu.VMEM((tm + 2 * HALO, d), F32), pltpu.VMEM((tm + 2 * HALO, fc), F32),
                        pltpu.VMEM((tm + 2 * HALO, fc), F32), pltpu.VMEM((tm, d), F32)],
        compiler_params=_params(("parallel", "arbitrary")),
        name="ffn_final" if final else "ffn",
    )(h, h, h, g, w_up, w_up, conv_w, conv_w, conv_b, conv_b, w_down, final_g)


def _softplus(x):
    return jnp.maximum(x, 0.0) + jnp.log1p(jnp.exp(-jnp.abs(x)))


def _block_diag_heads(w):
    heads, hd, _ = w.shape
    eye = jnp.eye(heads, dtype=F32)
    return jnp.einsum('hij,hk->hikj', w.astype(F32), eye).reshape(heads * hd, heads * hd)


def _trunk(x, p, *, tm, t_scan, ffn_chunks, tf):
    nb, seq, d = x.shape
    n = nb * seq
    depth = p["w_in"].shape[0]
    plan = _FftPlan(seq)
    h = x.astype(F32).reshape(n, d)
    for l in range(depth):
        urg, ga, ub, x0, z = _proj(h, p["norm1_g"][l][None], p["w_in"][l].astype(BF16),
                                   p["rg_conv_w"][l], p["rg_conv_b"][l][None],
                                   p["hy_conv_w"][l], p["hy_conv_b"][l][None], seq, tm)
        urg3 = urg.reshape(nb, seq, RG_WIDTH)
        hs = []
        for dr in range(2):
            w = jnp.concatenate([_block_diag_heads(p["rg_wa"][l, dr]), _block_diag_heads(p["rg_wx"][l, dr])],
                                axis=1).astype(BF16)
            bias = jnp.concatenate([p["rg_ba"][l, dr], p["rg_bx"][l, dr]])[None]
            sp = _softplus(-p["rg_lambda"][l, dr].astype(F32))[None]
            hs.append(_rglru_dir(urg3, w, bias, sp, t_scan, reverse=(dr == 1)).reshape(n, RG_WIDTH))
        ub3 = ub.reshape(nb, seq, S5_WIDTH)
        ys = []
        for dr in range(2):
            wts = _s5_weights(p["s5_a_re"][l, dr], p["s5_a_im"][l, dr], p["s5_log_dt"][l, dr],
                              p["s5_b_re"][l, dr], p["s5_b_im"][l, dr], p["s5_c_re"][l, dr], p["s5_c_im"][l, dr])
            ys.append(_s5_dir(ub3, *wts, t_scan, reverse=(dr == 1)).reshape(n, S5_WIDTH))
        filt = _hyena_filters(seq, p["hy_filt_w1"][l], p["hy_filt_b1"][l], p["hy_filt_freq1"][l],
                              p["hy_filt_w2"][l], p["hy_filt_b2"][l], p["hy_filt_freq2"][l],
                              p["hy_filt_w3"][l], tf)
        yc = _hyena_conv(z.reshape(nb, seq, HY_WIDTH), filt, plan).reshape(n, HY_WIDTH)
        h = _mix(h, hs[0], hs[1], ga, ys[0], ys[1], ub, yc, z, x0,
                 p["s5_d"][l][None], p["s5_glu_w"][l].astype(BF16), p["s5_glu_b"][l][None],
                 p["hy_bias"][l][None], p["mix_norm_g"][l][None], p["w_out"][l].astype(BF16), tm)
        h = _ffn(h, p["norm2_g"][l][None], p["w_up"][l].astype(BF16), p["ffn_conv_w"][l],
                 p["ffn_conv_b"][l][None], p["w_down"][l].astype(BF16), p["final_norm_g"][None],
                 seq, tm, ffn_chunks, final=(l == depth - 1))
    return h.reshape(nb, seq, d).astype(x.dtype)


def kernel(x, norm1_g, w_in, rg_conv_w, rg_conv_b, rg_wa, rg_ba, rg_wx, rg_bx, rg_lambda, s5_a_re, s5_a_im, s5_log_dt, s5_b_re, s5_b_im, s5_c_re, s5_c_im, s5_d, s5_glu_w, s5_glu_b, hy_conv_w, hy_conv_b, hy_filt_w1, hy_filt_b1, hy_filt_freq1, hy_filt_w2, hy_filt_b2, hy_filt_freq2, hy_filt_w3, hy_bias, mix_norm_g, w_out, norm2_g, w_up, ffn_conv_w, ffn_conv_b, w_down, final_norm_g):
    p = dict(norm1_g=norm1_g, w_in=w_in, rg_conv_w=rg_conv_w, rg_conv_b=rg_conv_b, rg_wa=rg_wa, rg_ba=rg_ba,
             rg_wx=rg_wx, rg_bx=rg_bx, rg_lambda=rg_lambda, s5_a_re=s5_a_re, s5_a_im=s5_a_im,
             s5_log_dt=s5_log_dt, s5_b_re=s5_b_re, s5_b_im=s5_b_im, s5_c_re=s5_c_re, s5_c_im=s5_c_im,
             s5_d=s5_d, s5_glu_w=s5_glu_w, s5_glu_b=s5_glu_b, hy_conv_w=hy_conv_w, hy_conv_b=hy_conv_b,
             hy_filt_w1=hy_filt_w1, hy_filt_b1=hy_filt_b1, hy_filt_freq1=hy_filt_freq1, hy_filt_w2=hy_filt_w2,
             hy_filt_b2=hy_filt_b2, hy_filt_freq2=hy_filt_freq2, hy_filt_w3=hy_filt_w3, hy_bias=hy_bias,
             mix_norm_g=mix_norm_g, w_out=w_out, norm2_g=norm2_g, w_up=w_up, ffn_conv_w=ffn_conv_w,
             ffn_conv_b=ffn_conv_b, w_down=w_down, final_norm_g=final_norm_g)
    return _trunk(x, p, tm=512, t_scan=256, ffn_chunks=2, tf=512)
```

```python
import functools
import math

import numpy as np
import jax
import jax.numpy as jnp
from jax import lax
from jax.experimental import pallas as pl
from jax.experimental.pallas import tpu as pltpu

F32 = jnp.float32
BF16 = jnp.bfloat16

RMS_EPS = 1e-6
RG_WIDTH = 384
RG_HEADS = 6
RG_C = 8.0
S5_WIDTH = 384
S5_GROUP = 16
S5_GROUPS = 24
S5_STATE = 64
S5_NSTATE = S5_GROUPS * S5_STATE
HY_WIDTH = 256
HY_BANDS = 16
HY_FAST_DECAY = 0.3
HY_SLOW_DECAY = 1.5
HY_DECAY_TARGET = 1e-2
D_FF = 2816

LANES = 128
SUBLANES = 8
HALO = SUBLANES
VMEM_LIMIT = 56 * 1024 * 1024

HIGHEST = lax.Precision.HIGHEST


def _dot(a, b, precision=None):
    return jnp.dot(a, b, preferred_element_type=F32, precision=precision)


def _gelu(x):
    c = math.sqrt(2.0 / math.pi)
    return 0.5 * x * (1.0 + jnp.tanh(c * (x + 0.044715 * (x * x * x))))


def _sigmoid(x):
    return 1.0 / (1.0 + jnp.exp(-x))


def _rms_nogain(x):
    return x * lax.rsqrt(jnp.mean(x * x, axis=-1, keepdims=True) + RMS_EPS)


def _params(sem):
    return pltpu.CompilerParams(dimension_semantics=sem, vmem_limit_bytes=VMEM_LIMIT)


def _full(shape):
    nd = len(shape)
    return pl.BlockSpec(shape, lambda *_: (0,) * nd)


def _halo_specs(tm, d):
    blocks_per_tile = tm // HALO

    def main(i, *_):
        return (i, 0)

    def prev(i, *_):
        return (jnp.maximum(i * blocks_per_tile - 1, 0), 0)

    def nxt_factory(nblocks):
        def nxt(i, *_):
            return (jnp.minimum((i + 1) * blocks_per_tile, nblocks - 1), 0)
        return nxt

    return main, prev, nxt_factory


def _fill_normed(x_scr, h_ref, hp_ref, hn_ref, g, first, last, tm):
    def norm(x):
        return x * lax.rsqrt(jnp.mean(x * x, axis=-1, keepdims=True) + RMS_EPS) * g

    x_scr[0:HALO, :] = norm(jnp.where(first, 0.0, hp_ref[...])).astype(x_scr.dtype)
    x_scr[HALO:HALO + tm, :] = norm(h_ref[...]).astype(x_scr.dtype)
    x_scr[HALO + tm:2 * HALO + tm, :] = norm(jnp.where(last, 0.0, hn_ref[...])).astype(x_scr.dtype)


def _conv_taps(u_scr, lo, hi, w_ref, b_ref, left, tm):
    k = w_ref.shape[0]
    out = b_ref[:, lo:hi]
    for j in range(k):
        out = out + u_scr[pl.ds(HALO + j - left, tm), lo:hi] * w_ref[j:j + 1, lo:hi]
    return out


def _proj_kernel(h_ref, hp_ref, hn_ref, g_ref, w_ref, rcw_ref, rcb_ref, hcw_ref, hcb_ref,
                 urg_ref, ga_ref, ub_ref, x0_ref, z_ref, x_scr, u_scr, *, tiles_per_seq, tm):
    i = pl.program_id(0)
    first = (i % tiles_per_seq) == 0
    last = (i % tiles_per_seq) == tiles_per_seq - 1
    _fill_normed(x_scr, h_ref, hp_ref, hn_ref, g_ref[...], first, last, tm)
    u_scr[...] = _dot(x_scr[...].astype(BF16), w_ref[...])
    a0, a1, a2, a3 = 0, RG_WIDTH, 2 * RG_WIDTH, 2 * RG_WIDTH + S5_WIDTH
    k = rcw_ref.shape[0]
    acc = rcb_ref[...]
    for j in range(k):
        acc = acc + u_scr[pl.ds(HALO + j - 1, tm), a0:a1] * rcw_ref[j:j + 1, :]
    urg_ref[...] = acc
    ga_ref[...] = u_scr[pl.ds(HALO, tm), a1:a2]
    ub_ref[...] = u_scr[pl.ds(HALO, tm), a2:a3]
    q = []
    for part in range(3):
        lo = part * HY_WIDTH
        acc = hcb_ref[:, lo:lo + HY_WIDTH]
        for j in range(hcw_ref.shape[0]):
            acc = acc + (u_scr[pl.ds(HALO + j - 1, tm), a3 + lo:a3 + lo + HY_WIDTH]
                         * hcw_ref[j:j + 1, lo:lo + HY_WIDTH])
        q.append(acc)
    x0_ref[...] = q[0]
    z_ref[...] = q[2] * q[1]


def _proj(h, g, w_in, rcw, rcb, hcw, hcb, seq, tm):
    n, d = h.shape
    cols = w_in.shape[1]
    main, prev, nxt_f = _halo_specs(tm, d)
    nxt = nxt_f(n // HALO)
    row = lambda c: pl.BlockSpec((tm, c), lambda i: (i, 0))
    outs = [jax.ShapeDtypeStruct((n, c), F32) for c in (RG_WIDTH, RG_WIDTH, S5_WIDTH, HY_WIDTH, HY_WIDTH)]
    return pl.pallas_call(
        functools.partial(_proj_kernel, tiles_per_seq=seq // tm, tm=tm),
        grid=(n // tm,),
        in_specs=[pl.BlockSpec((tm, d), main), pl.BlockSpec((HALO, d), prev), pl.BlockSpec((HALO, d), nxt),
                  _full((1, d)), _full((d, cols)), _full(rcw.shape), _full(rcb.shape),
                  _full(hcw.shape), _full(hcb.shape)],
        out_specs=[row(RG_WIDTH), row(RG_WIDTH), row(S5_WIDTH), row(HY_WIDTH), row(HY_WIDTH)],
        out_shape=outs,
        scratch_shapes=[pltpu.VMEM((tm + 2 * HALO, d), F32), pltpu.VMEM((tm + 2 * HALO, cols), F32)],
        compiler_params=_params(("parallel",)),
        name="proj",
    )(h, h, h, g, w_in, rcw, rcb, hcw, hcb)


def _lo_mask():
    return lax.broadcasted_iota(jnp.int32, (SUBLANES, LANES), 0) < (SUBLANES // 2)


def _swap_halves(x):
    return pltpu.roll(x, SUBLANES // 2, 0)


def _rglru_kernel(u_ref, w_ref, bias_ref, sp_ref, o_ref, a_scr, b_scr, carry_scr, *, t, nb, reverse):
    i = pl.program_id(0)
    nm = RG_WIDTH // LANES

    @pl.when(i == 0)
    def _():
        carry_scr[...] = jnp.zeros_like(carry_scr)

    u = u_ref[...].reshape(nb * t, RG_WIDTH)
    gates = _dot(u.astype(BF16), w_ref[...]) + bias_ref[...]
    r = _sigmoid(gates[:, :RG_WIDTH])
    gi = _sigmoid(gates[:, RG_WIDTH:])
    log_a = (-RG_C) * r * sp_ref[...]
    a = jnp.exp(log_a)
    mult = jnp.sqrt(-jnp.tanh(log_a) * (a * a + 1.0))
    bb = mult * (gi * u)
    for b in range(nb):
        for m in range(nm):
            a_scr[m, pl.ds(b, t, stride=nb), :] = a[b * t:(b + 1) * t, m * LANES:(m + 1) * LANES]
            b_scr[m, pl.ds(b, t, stride=nb), :] = bb[b * t:(b + 1) * t, m * LANES:(m + 1) * LANES]

    lo = _lo_mask()
    ntile = t // 2

    def body(s, carry):
        jj = (ntile - 1 - s) if reverse else s
        row = pl.multiple_of(jj * SUBLANES, SUBLANES)
        new = []
        for m in range(nm):
            at = a_scr[m, pl.ds(row, SUBLANES), :]
            bt = b_scr[m, pl.ds(row, SUBLANES), :]
            h1 = at * _swap_halves(carry[m]) + bt
            h2 = at * _swap_halves(h1) + bt
            out = jnp.where(lo, h2, h1) if reverse else jnp.where(lo, h1, h2)
            b_scr[m, pl.ds(row, SUBLANES), :] = out
            new.append(out)
        return tuple(new)

    carry = lax.fori_loop(0, ntile, body, tuple(carry_scr[m] for m in range(nm)))
    for m in range(nm):
        carry_scr[m] = carry[m]
    for b in range(nb):
        for m in range(nm):
            o_ref[b, :, m * LANES:(m + 1) * LANES] = b_scr[m, pl.ds(b, t, stride=nb), :]


def _rglru_dir(u3, w, bias, sp, t, reverse):
    nb, seq, c = u3.shape
    nchunk = seq // t
    idx = (lambda i: (0, nchunk - 1 - i, 0)) if reverse else (lambda i: (0, i, 0))
    nm = c // LANES
    return pl.pallas_call(
        functools.partial(_rglru_kernel, t=t, nb=nb, reverse=reverse),
        grid=(nchunk,),
        in_specs=[pl.BlockSpec((nb, t, c), idx), _full(w.shape), _full(bias.shape), _full(sp.shape)],
        out_specs=pl.BlockSpec((nb, t, c), idx),
        out_shape=jax.ShapeDtypeStruct((nb, seq, c), F32),
        scratch_shapes=[pltpu.VMEM((nm, nb * t, LANES), F32), pltpu.VMEM((nm, nb * t, LANES), F32),
                        pltpu.VMEM((nm, SUBLANES, LANES), F32)],
        compiler_params=_params(("arbitrary",)),
        name="rglru_bwd" if reverse else "rglru_fwd",
    )(u3, w, bias, sp)


S5_NBLK = S5_NSTATE // LANES
S5_GRP = 4


def _s5_kernel(u_ref, wb_ref, wcr_ref, wci_ref, ar_ref, ai_ref, o_ref,
               u_scr, bu_scr, y_scr, carry_scr, *, t, nb, reverse):
    i = pl.program_id(0)
    nm = S5_WIDTH // LANES
    per = S5_NBLK // nm

    @pl.when(i == 0)
    def _():
        carry_scr[...] = jnp.zeros_like(carry_scr)

    for b in range(nb):
        for m in range(nm):
            u_scr[m, pl.ds(b, t, stride=nb), :] = u_ref[b, :, m * LANES:(m + 1) * LANES]
    for n in range(S5_NBLK):
        res = _dot(u_scr[n // per].astype(BF16), wb_ref[n])
        bu_scr[2 * n] = res[:, :LANES]
        bu_scr[2 * n + 1] = res[:, LANES:]

    lo = _lo_mask()
    ntile = t // 2
    for g in range(S5_NBLK // S5_GRP):
        blocks = list(range(g * S5_GRP, (g + 1) * S5_GRP))
        ars = [ar_ref[n] for n in blocks]
        ais = [ai_ref[n] for n in blocks]

        def body(s, carry, blocks=blocks, ars=ars, ais=ais):
            jj = (ntile - 1 - s) if reverse else s
            row = pl.multiple_of(jj * SUBLANES, SUBLANES)
            new = []
            for k, n in enumerate(blocks):
                ar, ai = ars[k], ais[k]
                tr = bu_scr[2 * n, pl.ds(row, SUBLANES), :]
                ti = bu_scr[2 * n + 1, pl.ds(row, SUBLANES), :]
                pr = _swap_halves(carry[2 * k])
                pi_ = _swap_halves(carry[2 * k + 1])
                h1r = ar * pr - ai * pi_ + tr
                h1i = ar * pi_ + ai * pr + ti
                qr = _swap_halves(h1r)
                qi = _swap_halves(h1i)
                h2r = ar * qr - ai * qi + tr
                h2i = ar * qi + ai * qr + ti
                if reverse:
                    outr = jnp.where(lo, h2r, h1r)
                    outi = jnp.where(lo, h2i, h1i)
                else:
                    outr = jnp.where(lo, h1r, h2r)
                    outi = jnp.where(lo, h1i, h2i)
                bu_scr[2 * n, pl.ds(row, SUBLANES), :] = outr
                bu_scr[2 * n + 1, pl.ds(row, SUBLANES), :] = outi
                new += [outr, outi]
            return tuple(new)

        init = []
        for n in blocks:
            init += [carry_scr[2 * n], carry_scr[2 * n + 1]]
        carry = lax.fori_loop(0, ntile, body, tuple(init))
        for k, n in enumerate(blocks):
            carry_scr[2 * n] = carry[2 * k]
            carry_scr[2 * n + 1] = carry[2 * k + 1]

    for m in range(nm):
        acc = None
        for n in range(m * per, (m + 1) * per):
            term = (_dot(bu_scr[2 * n].astype(BF16), wcr_ref[n])
                    + _dot(bu_scr[2 * n + 1].astype(BF16), wci_ref[n]))
            acc = term if acc is None else acc + term
        y_scr[m] = acc
    for b in range(nb):
        for m in range(nm):
            o_ref[b, :, m * LANES:(m + 1) * LANES] = y_scr[m, pl.ds(b, t, stride=nb), :]


def _s5_dir(u3, wb, wcr, wci, ar, ai, t, reverse):
    nb, seq, c = u3.shape
    nchunk = seq // t
    idx = (lambda i: (0, nchunk - 1 - i, 0)) if reverse else (lambda i: (0, i, 0))
    nm = c // LANES
    return pl.pallas_call(
        functools.partial(_s5_kernel, t=t, nb=nb, reverse=reverse),
        grid=(nchunk,),
        in_specs=[pl.BlockSpec((nb, t, c), idx), _full(wb.shape), _full(wcr.shape), _full(wci.shape),
                  _full(ar.shape), _full(ai.shape)],
        out_specs=pl.BlockSpec((nb, t, c), idx),
        out_shape=jax.ShapeDtypeStruct((nb, seq, c), F32),
        scratch_shapes=[pltpu.VMEM((nm, nb * t, LANES), F32),
                        pltpu.VMEM((2 * S5_NBLK, nb * t, LANES), F32),
                        pltpu.VMEM((nm, nb * t, LANES), F32),
                        pltpu.VMEM((2 * S5_NBLK, SUBLANES, LANES), F32)],
        compiler_params=_params(("arbitrary",)),
        name="s5_bwd" if reverse else "s5_fwd",
    )(u3, wb, wcr, wci, ar, ai)


def _s5_weights(a_re, a_im, log_dt, b_re, b_im, c_re, c_im):
    lr = a_re.astype(F32)
    li = a_im.astype(F32)
    dt = jnp.exp(log_dt.astype(F32))[:, None]
    mag = jnp.exp(lr * dt)
    abar_r = mag * jnp.cos(li * dt)
    abar_i = mag * jnp.sin(li * dt)
    den = lr * lr + li * li
    nr = abar_r - 1.0
    ni = abar_i
    coef_r = (nr * lr + ni * li) / den
    coef_i = (ni * lr - nr * li) / den
    bbar_r = coef_r[..., None] * b_re - coef_i[..., None] * b_im
    bbar_i = coef_r[..., None] * b_im + coef_i[..., None] * b_re
    eye = jnp.eye(S5_GROUPS, dtype=F32)
    bd_r = jnp.einsum('gpc,gh->gchp', bbar_r, eye).reshape(S5_WIDTH, S5_NSTATE)
    bd_i = jnp.einsum('gpc,gh->gchp', bbar_i, eye).reshape(S5_WIDTH, S5_NSTATE)
    cd_r = jnp.einsum('gcp,gh->gphc', c_re, eye).reshape(S5_NSTATE, S5_WIDTH)
    cd_i = jnp.einsum('gcp,gh->gphc', c_im, eye).reshape(S5_NSTATE, S5_WIDTH)
    per = S5_NBLK // (S5_WIDTH // LANES)
    wb, wcr, wci = [], [], []
    for n in range(S5_NBLK):
        m = n // per
        rows = slice(m * LANES, (m + 1) * LANES)
        cols = slice(n * LANES, (n + 1) * LANES)
        wb.append(jnp.concatenate([bd_r[rows, cols], bd_i[rows, cols]], axis=1))
        wcr.append(cd_r[cols, rows])
        wci.append(-cd_i[cols, rows])
    wb = jnp.stack(wb).astype(BF16)
    wcr = jnp.stack(wcr).astype(BF16)
    wci = jnp.stack(wci).astype(BF16)
    ar = jnp.broadcast_to(abar_r.reshape(S5_NBLK, 1, LANES), (S5_NBLK, SUBLANES, LANES))
    ai = jnp.broadcast_to(abar_i.reshape(S5_NBLK, 1, LANES), (S5_NBLK, SUBLANES, LANES))
    return wb, wcr, wci, ar, ai


def _odd8(n):
    p = -(-n // SUBLANES)
    if p % 2 == 0:
        p += 1
    return p * SUBLANES


class _FftPlan:
    def __init__(self, seq):
        self.seq = seq
        r = 1
        while r * r < seq:
            r *= 2
        self.r = r
        self.nq = seq // r
        self.q = 2 * seq // r
        self.k1n = self.q // 2 + 1
        self.kp = -(-self.k1n // SUBLANES) * SUBLANES
        self.zpitch = _odd8(r)
        self.apitch = _odd8(2 * self.kp)
        self.cpitch = _odd8(2 * r)
        self.unroll = min(8, r)
        self.unroll2 = next(u for u in (13, 5, 4, 3, 2, 1) if self.k1n % u == 0)
        p = 2 * seq
        n1 = np.arange(self.nq)[None, :]
        k1 = np.arange(self.k1n)[:, None]
        ang = 2.0 * np.pi * n1 * k1 / self.q
        f1 = np.zeros((2 * self.kp, self.nq))
        f1[:self.k1n] = np.cos(ang)
        f1[self.kp:self.kp + self.k1n] = -np.sin(ang)
        self.f1 = f1
        w = np.full((self.k1n,), 2.0)
        w[0] = 1.0
        w[-1] = 1.0
        g1 = np.zeros((self.nq, 2 * self.kp))
        g1[:, :self.k1n] = (np.cos(ang) * w[:, None] / p).T
        g1[:, self.kp:self.kp + self.k1n] = (-np.sin(ang) * w[:, None] / p).T
        self.g1 = g1
        kk = np.arange(self.k1n)[:, None, None]
        k2 = np.arange(r)[None, :, None]
        n2 = np.arange(r)[None, None, :]
        ph = 2.0 * np.pi * (n2 * k2 / r + n2 * kk / p)
        tr, ti = np.cos(ph), -np.sin(ph)
        self.m2 = np.concatenate([np.concatenate([tr, -ti], axis=2),
                                  np.concatenate([ti, tr], axis=2)], axis=1)
        ur, ui = np.transpose(tr, (0, 2, 1)), -np.transpose(ti, (0, 2, 1))
        self.m2i = np.concatenate([np.concatenate([ur, -ui], axis=2),
                                   np.concatenate([ui, ur], axis=2)], axis=1)


def _fft_forward(plan, src_ref, f1_ref, m2_ref, zp, as_, emit):
    r, nq, kp = plan.r, plan.nq, plan.kp
    for n1 in range(nq):
        zp[n1 * plan.zpitch:n1 * plan.zpitch + r, :] = src_ref[n1 * r:(n1 + 1) * r, :]

    def stage1(n2, c):
        slab = zp[pl.ds(n2, nq, stride=plan.zpitch), :]
        a = _dot(f1_ref[...].astype(BF16), slab.astype(BF16))
        as_[pl.ds(pl.multiple_of(n2 * plan.apitch, SUBLANES), 2 * kp), :] = a
        return c

    lax.fori_loop(0, r, stage1, 0, unroll=plan.unroll)

    def stage2(k1, c):
        sr = as_[pl.ds(k1, r, stride=plan.apitch), :]
        si = as_[pl.ds(kp + k1, r, stride=plan.apitch), :]
        s = jnp.concatenate([sr, si], axis=0).astype(BF16)
        emit(k1, _dot(m2_ref[k1].astype(BF16), s))
        return c

    lax.fori_loop(0, plan.k1n, stage2, 0, unroll=plan.unroll2)


def _hy_spec_kernel(kf_ref, kb_ref, f1_ref, m2_ref, o_ref, zp, as_, *, plan):
    r = plan.r

    def emit_f(k1, x):
        o_ref[0, k1] = x

    def emit_b(k1, x):
        sign = jnp.where(lax.broadcasted_iota(jnp.int32, (2 * r, LANES), 0) < r, 1.0, -1.0)
        o_ref[0, k1] = o_ref[0, k1] + sign * x

    _fft_forward(plan, kf_ref, f1_ref, m2_ref, zp, as_, emit_f)
    _fft_forward(plan, kb_ref, f1_ref, m2_ref, zp, as_, emit_b)


def _hy_conv_kernel(z_ref, spec_ref, f1_ref, m2_ref, m2i_ref, g1_ref, o_ref, zp, as_, cs, *, plan):
    r, nq, kp, k1n = plan.r, plan.nq, plan.kp, plan.k1n
    if kp > k1n:
        cs[k1n * plan.cpitch:kp * plan.cpitch, :] = jnp.zeros(((kp - k1n) * plan.cpitch, LANES), F32)

    def emit(k1, x):
        kf = spec_ref[0, k1]
        xr, xi = x[:r], x[r:]
        kr, ki = kf[:r], kf[r:]
        prod = jnp.concatenate([xr * kr - xi * ki, xr * ki + xi * kr], axis=0).astype(BF16)
        c = _dot(m2i_ref[k1].astype(BF16), prod)
        cs[pl.ds(pl.multiple_of(k1 * plan.cpitch, SUBLANES), 2 * r), :] = c

    _fft_forward(plan, z_ref.at[0], f1_ref, m2_ref, zp, as_, emit)

    def stage3(n2, c):
        cr = cs[pl.ds(n2, kp, stride=plan.cpitch), :]
        ci = cs[pl.ds(r + n2, kp, stride=plan.cpitch), :]
        y = _dot(g1_ref[...].astype(BF16), jnp.concatenate([cr, ci], axis=0).astype(BF16))
        zp[pl.ds(n2, nq, stride=plan.zpitch), :] = y
        return c

    lax.fori_loop(0, r, stage3, 0, unroll=plan.unroll)
    for n1 in range(nq):
        o_ref[0, n1 * r:(n1 + 1) * r, :] = zp[n1 * plan.zpitch:n1 * plan.zpitch + r, :]


def _hy_filter_kernel(feat_ref, w1_ref, b1_ref, f1_ref, w2_ref, b2_ref, f2_ref, w3_ref, dl_ref, o_ref):
    feats = feat_ref[...]
    hid = jnp.sin(f1_ref[...] * (_dot(feats, w1_ref[...], HIGHEST) + b1_ref[...]))
    hid = jnp.sin(f2_ref[...] * (_dot(hid, w2_ref[...], HIGHEST) + b2_ref[...]))
    k = _dot(hid, w3_ref[...], HIGHEST)
    t = feats[:, 0:1]
    o_ref[...] = k * jnp.exp(-(t * dl_ref[...]))


def _hyena_features(seq):
    pos = np.arange(seq, dtype=np.float64)
    t = pos / max(seq - 1, 1)
    w = (2.0 * math.pi / seq) * pos
    bands = np.linspace(1e-4, HY_BANDS - 1, HY_BANDS, dtype=np.float64)
    ang = w[:, None] * bands
    feats = np.concatenate([t[:, None], np.cos(ang), -np.sin(ang)], axis=-1).astype(np.float32)
    out = np.zeros((seq, LANES), np.float32)
    out[:, :feats.shape[1]] = feats
    return out


def _hyena_deltas():
    max_decay = math.log(HY_DECAY_TARGET) / HY_FAST_DECAY
    min_decay = math.log(HY_DECAY_TARGET) / HY_SLOW_DECAY
    deltas = np.abs(np.linspace(min_decay, max_decay, HY_WIDTH, dtype=np.float64))
    return np.tile(deltas, 2)[None, :].astype(np.float32)


def _pad2(x, rows, cols):
    return jnp.zeros((rows, cols), F32).at[:x.shape[0], :x.shape[1]].set(x.astype(F32))


def _hyena_filters(seq, w1, b1, fr1, w2, b2, fr2, w3, tf):
    feats = jnp.asarray(_hyena_features(seq))
    hid = LANES
    args = (feats, _pad2(w1, LANES, hid), _pad2(b1[None], 1, hid), _pad2(fr1[None], 1, hid),
            _pad2(w2, hid, hid), _pad2(b2[None], 1, hid), _pad2(fr2[None], 1, hid),
            _pad2(w3, hid, 2 * HY_WIDTH), jnp.asarray(_hyena_deltas()))
    return pl.pallas_call(
        _hy_filter_kernel,
        grid=(seq // tf,),
        in_specs=[pl.BlockSpec((tf, LANES), lambda i: (i, 0))] + [_full(a.shape) for a in args[1:]],
        out_specs=pl.BlockSpec((tf, 2 * HY_WIDTH), lambda i: (i, 0)),
        out_shape=jax.ShapeDtypeStruct((seq, 2 * HY_WIDTH), F32),
        compiler_params=_params(("parallel",)),
        name="hy_filter",
    )(*args)


def _hyena_conv(z3, filt, plan):
    nb, seq, c = z3.shape
    nh = c // LANES
    f1 = jnp.asarray(plan.f1, F32)
    m2 = jnp.asarray(plan.m2, F32)
    m2i = jnp.asarray(plan.m2i, F32)
    g1 = jnp.asarray(plan.g1, F32)
    r, kp, k1n = plan.r, plan.kp, plan.k1n
    zp_shape = (plan.nq * plan.zpitch, LANES)
    as_shape = (r * plan.apitch, LANES)
    cs_shape = (kp * plan.cpitch, LANES)
    spec = pl.pallas_call(
        functools.partial(_hy_spec_kernel, plan=plan),
        grid=(nh,),
        in_specs=[pl.BlockSpec((seq, LANES), lambda j: (0, j)),
                  pl.BlockSpec((seq, LANES), lambda j: (0, nh + j)),
                  _full(f1.shape), _full(m2.shape)],
        out_specs=pl.BlockSpec((1, k1n, 2 * r, LANES), lambda j: (j, 0, 0, 0)),
        out_shape=jax.ShapeDtypeStruct((nh, k1n, 2 * r, LANES), F32),
        scratch_shapes=[pltpu.VMEM(zp_shape, F32), pltpu.VMEM(as_shape, F32)],
        compiler_params=_params(("parallel",)),
        name="hy_spec",
    )(filt, filt, f1, m2)
    return pl.pallas_call(
        functools.partial(_hy_conv_kernel, plan=plan),
        grid=(nb, nh),
        in_specs=[pl.BlockSpec((1, seq, LANES), lambda b, j: (b, 0, j)),
                  pl.BlockSpec((1, k1n, 2 * r, LANES), lambda b, j: (j, 0, 0, 0)),
                  _full(f1.shape), _full(m2.shape), _full(m2i.shape), _full(g1.shape)],
        out_specs=pl.BlockSpec((1, seq, LANES), lambda b, j: (b, 0, j)),
        out_shape=jax.ShapeDtypeStruct((nb, seq, c), F32),
        scratch_shapes=[pltpu.VMEM(zp_shape, F32), pltpu.VMEM(as_shape, F32), pltpu.VMEM(cs_shape, F32)],
        compiler_params=_params(("parallel", "parallel")),
        name="hy_conv",
    )(z3, spec, f1, m2, m2i, g1)


def _mix_kernel(h_ref, hf_ref, hb_ref, ga_ref, yf_ref, yb_ref, ub_ref, yc_ref, z_ref, x0_ref,
                d_ref, gw_ref, gb_ref, hbias_ref, mg_ref, wo_ref, o_ref):
    ya = (hf_ref[...] + hb_ref[...]) * _gelu(ga_ref[...])
    yb = _gelu(ub_ref[...] * d_ref[...] + yf_ref[...] + yb_ref[...])
    yb = yb * _sigmoid(_dot(yb.astype(BF16), gw_ref[...]) + gb_ref[...])
    z = z_ref[...]
    yc = (yc_ref[...] + z * hbias_ref[...]) * x0_ref[...]
    a1, a2 = RG_WIDTH, RG_WIDTH + S5_WIDTH
    na = (_rms_nogain(ya) * mg_ref[:, :a1]).astype(BF16)
    nb = (_rms_nogain(yb) * mg_ref[:, a1:a2]).astype(BF16)
    nc = (_rms_nogain(yc) * mg_ref[:, a2:]).astype(BF16)
    out = _dot(na, wo_ref[:a1, :]) + _dot(nb, wo_ref[a1:a2, :]) + _dot(nc, wo_ref[a2:, :])
    o_ref[...] = h_ref[...] + out


def _mix(h, hf, hb, ga, yf, yb, ub, yc, z, x0, d, gw, gb, hbias, mg, wo, tm):
    n, dm = h.shape
    rows = [h, hf, hb, ga, yf, yb, ub, yc, z, x0]
    consts = [d, gw, gb, hbias, mg, wo]
    return pl.pallas_call(
        _mix_kernel,
        grid=(n // tm,),
        in_specs=[pl.BlockSpec((tm, a.shape[1]), lambda i: (i, 0)) for a in rows]
                 + [_full(a.shape) for a in consts],
        out_specs=pl.BlockSpec((tm, dm), lambda i: (i, 0)),
        out_shape=jax.ShapeDtypeStruct((n, dm), F32),
        compiler_params=_params(("parallel",)),
        name="mix",
    )(*rows, *consts)


def _ffn_kernel(h_ref, hp_ref, hn_ref, g_ref, wa_ref, wv_ref, cwa_ref, cwv_ref, cba_ref, cbv_ref,
                wd_ref, fg_ref, o_ref, x_scr, ua_scr, uv_scr, acc_scr, *, tiles_per_seq, tm, final):
    i = pl.program_id(0)
    c = pl.program_id(1)
    nc = pl.num_programs(1)

    @pl.when(c == 0)
    def _():
        first = (i % tiles_per_seq) == 0
        last = (i % tiles_per_seq) == tiles_per_seq - 1
        _fill_normed(x_scr, h_ref, hp_ref, hn_ref, g_ref[...], first, last, tm)
        acc_scr[...] = jnp.zeros_like(acc_scr)

    x = x_scr[...].astype(BF16)
    ua_scr[...] = _dot(x, wa_ref[...])
    uv_scr[...] = _dot(x, wv_ref[...])
    a = cba_ref[...]
    v = cbv_ref[...]
    for j in range(cwa_ref.shape[0]):
        a = a + ua_scr[pl.ds(HALO + j - 1, tm), :] * cwa_ref[j:j + 1, :]
        v = v + uv_scr[pl.ds(HALO + j - 1, tm), :] * cwv_ref[j:j + 1, :]
    gated = (_gelu(a) * v).astype(BF16)
    acc_scr[...] += _dot(gated, wd_ref[...])

    @pl.when(c == nc - 1)
    def _():
        out = h_ref[...] + acc_scr[...]
        if final:
            out = out * lax.rsqrt(jnp.mean(out * out, axis=-1, keepdims=True) + RMS_EPS) * fg_ref[...]
        o_ref[...] = out


def _ffn(h, g, w_up, conv_w, conv_b, w_down, final_g, seq, tm, nchunk, final):
    n, d = h.shape
    fc = D_FF // nchunk
    main, prev, nxt_f = _halo_specs(tm, d)
    nxt = nxt_f(n // HALO)
    return pl.pallas_call(
        functools.partial(_ffn_kernel, tiles_per_seq=seq // tm, tm=tm, final=final),
        grid=(n // tm, nchunk),
        in_specs=[pl.BlockSpec((tm, d), main), pl.BlockSpec((HALO, d), prev), pl.BlockSpec((HALO, d), nxt),
                  _full((1, d)),
                  pl.BlockSpec((d, fc), lambda i, c: (0, c)),
                  pl.BlockSpec((d, fc), lambda i, c: (0, nchunk + c)),
                  pl.BlockSpec((conv_w.shape[0], fc), lambda i, c: (0, c)),
                  pl.BlockSpec((conv_w.shape[0], fc), lambda i, c: (0, nchunk + c)),
                  pl.BlockSpec((1, fc), lambda i, c: (0, c)),
                  pl.BlockSpec((1, fc), lambda i, c: (0, nchunk + c)),
                  pl.BlockSpec((fc, d), lambda i, c: (c, 0)),
                  _full((1, d))],
        out_specs=pl.BlockSpec((tm, d), lambda i, c: (i, 0)),
        out_shape=jax.ShapeDtypeStruct((n, d), F32),
        scratch_shapes=[pltpu.VMEM((tm + 2 * HALO, d), F32), pltpu.VMEM((tm + 2 * HALO, fc), F32),
                        pltpu.VMEM((tm + 2 * HALO, fc), F32), pltpu.VMEM((tm, d), F32)],
        compiler_params=_params(("parallel", "arbitrary")),
        name="ffn_final" if final else "ffn",
    )(h, h, h, g, w_up, w_up, conv_w, conv_w, conv_b, conv_b, w_down, final_g)


def _softplus(x):
    return jnp.maximum(x, 0.0) + jnp.log1p(jnp.exp(-jnp.abs(x)))


def _block_diag_heads(w):
    heads, hd, _ = w.shape
    eye = jnp.eye(heads, dtype=F32)
    return jnp.einsum('hij,hk->hikj', w.astype(F32), eye).reshape(heads * hd, heads * hd)


def _trunk(x, p, *, tm, t_scan, ffn_chunks, tf):
    nb, seq, d = x.shape
    n = nb * seq
    depth = p["w_in"].shape[0]
    plan = _FftPlan(seq)
    h = x.astype(F32).reshape(n, d)
    for l in range(depth):
        urg, ga, ub, x0, z = _proj(h, p["norm1_g"][l][None], p["w_in"][l].astype(BF16),
                                   p["rg_conv_w"][l], p["rg_conv_b"][l][None],
                                   p["hy_conv_w"][l], p["hy_conv_b"][l][None], seq, tm)
        urg3 = urg.reshape(nb, seq, RG_WIDTH)
        hs = []
        for dr in range(2):
            w = jnp.concatenate([_block_diag_heads(p["rg_wa"][l, dr]), _block_diag_heads(p["rg_wx"][l, dr])],
                                axis=1).astype(BF16)
            bias = jnp.concatenate([p["rg_ba"][l, dr], p["rg_bx"][l, dr]])[None]
            sp = _softplus(-p["rg_lambda"][l, dr].astype(F32))[None]
            hs.append(_rglru_dir(urg3, w, bias, sp, t_scan, reverse=(dr == 1)).reshape(n, RG_WIDTH))
        ub3 = ub.reshape(nb, seq, S5_WIDTH)
        ys = []
        for dr in range(2):
            wts = _s5_weights(p["s5_a_re"][l, dr], p["s5_a_im"][l, dr], p["s5_log_dt"][l, dr],
                              p["s5_b_re"][l, dr], p["s5_b_im"][l, dr], p["s5_c_re"][l, dr], p["s5_c_im"][l, dr])
            ys.append(_s5_dir(ub3, *wts, t_scan, reverse=(dr == 1)).reshape(n, S5_WIDTH))
        filt = _hyena_filters(seq, p["hy_filt_w1"][l], p["hy_filt_b1"][l], p["hy_filt_freq1"][l],
                              p["hy_filt_w2"][l], p["hy_filt_b2"][l], p["hy_filt_freq2"][l],
                              p["hy_filt_w3"][l], tf)
        yc = _hyena_conv(z.reshape(nb, seq, HY_WIDTH), filt, plan).reshape(n, HY_WIDTH)
        h = _mix(h, hs[0], hs[1], ga, ys[0], ys[1], ub, yc, z, x0,
                 p["s5_d"][l][None], p["s5_glu_w"][l].astype(BF16), p["s5_glu_b"][l][None],
                 p["hy_bias"][l][None], p["mix_norm_g"][l][None], p["w_out"][l].astype(BF16), tm)
        h = _ffn(h, p["norm2_g"][l][None], p["w_up"][l].astype(BF16), p["ffn_conv_w"][l],
                 p["ffn_conv_b"][l][None], p["w_down"][l].astype(BF16), p["final_norm_g"][None],
                 seq, tm, ffn_chunks, final=(l == depth - 1))
    return h.reshape(nb, seq, d).astype(x.dtype)


def kernel(x, norm1_g, w_in, rg_conv_w, rg_conv_b, rg_wa, rg_ba, rg_wx, rg_bx, rg_lambda, s5_a_re, s5_a_im, s5_log_dt, s5_b_re, s5_b_im, s5_c_re, s5_c_im, s5_d, s5_glu_w, s5_glu_b, hy_conv_w, hy_conv_b, hy_filt_w1, hy_filt_b1, hy_filt_freq1, hy_filt_w2, hy_filt_b2, hy_filt_freq2, hy_filt_w3, hy_bias, mix_norm_g, w_out, norm2_g, w_up, ffn_conv_w, ffn_conv_b, w_down, final_norm_g):
    p = dict(norm1_g=norm1_g, w_in=w_in, rg_conv_w=rg_conv_w, rg_conv_b=rg_conv_b, rg_wa=rg_wa, rg_ba=rg_ba,
             rg_wx=rg_wx, rg_bx=rg_bx, rg_lambda=rg_lambda, s5_a_re=s5_a_re, s5_a_im=s5_a_im,
             s5_log_dt=s5_log_dt, s5_b_re=s5_b_re, s5_b_im=s5_b_im, s5_c_re=s5_c_re, s5_c_im=s5_c_im,
             s5_d=s5_d, s5_glu_w=s5_glu_w, s5_glu_b=s5_glu_b, hy_conv_w=hy_conv_w, hy_conv_b=hy_conv_b,
             hy_filt_w1=hy_filt_w1, hy_filt_b1=hy_filt_b1, hy_filt_freq1=hy_filt_freq1, hy_filt_w2=hy_filt_w2,
             hy_filt_b2=hy_filt_b2, hy_filt_freq2=hy_filt_freq2, hy_filt_w3=hy_filt_w3, hy_bias=hy_bias,
             mix_norm_g=mix_norm_g, w_out=w_out, norm2_g=norm2_g, w_up=w_up, ffn_conv_w=ffn_conv_w,
             ffn_conv_b=ffn_conv_b, w_down=w_down, final_norm_g=final_norm_g)
    return _trunk(x, p, tm=512, t_scan=256, ffn_chunks=2, tf=512)
```

```python
import functools
import math

import numpy as np
import jax
import jax.numpy as jnp
from jax import lax
from jax.experimental import pallas as pl
from jax.experimental.pallas import tpu as pltpu

F32 = jnp.float32
BF16 = jnp.bfloat16

RMS_EPS = 1e-6
RG_WIDTH = 384
RG_HEADS = 6
RG_C = 8.0
S5_WIDTH = 384
S5_GROUP = 16
S5_GROUPS = 24
S5_STATE = 64
S5_NSTATE = S5_GROUPS * S5_STATE
HY_WIDTH = 256
HY_BANDS = 16
HY_FAST_DECAY = 0.3
HY_SLOW_DECAY = 1.5
HY_DECAY_TARGET = 1e-2
D_FF = 2816
N_DIR = 2

LANES = 128
SUBLANES = 8
HALO = SUBLANES
VMEM_LIMIT = 56 * 1024 * 1024

HIGHEST = lax.Precision.HIGHEST


def _dot(a, b, precision=None):
    return jnp.dot(a, b, preferred_element_type=F32, precision=precision)


def _gelu(x):
    c = math.sqrt(2.0 / math.pi)
    return 0.5 * x * (1.0 + jnp.tanh(c * (x + 0.044715 * (x * x * x))))


def _sigmoid(x):
    return 1.0 / (1.0 + jnp.exp(-x))


def _rms_nogain(x):
    return x * lax.rsqrt(jnp.mean(x * x, axis=-1, keepdims=True) + RMS_EPS)


def _params(sem):
    return pltpu.CompilerParams(dimension_semantics=sem, vmem_limit_bytes=VMEM_LIMIT)


def _full(shape):
    nd = len(shape)
    return pl.BlockSpec(shape, lambda *_: (0,) * nd)


def _layer_spec(arr, *lead):
    rest = arr.shape[len(lead):]
    zeros = (0,) * len(rest)
    return pl.BlockSpec((None,) * len(lead) + rest, lambda *_: tuple(lead) + zeros)


def _halo_specs(tm, n):
    blocks_per_tile = tm // HALO
    nblocks = n // HALO

    def main(i, *_):
        return (i, 0)

    def prev(i, *_):
        return (jnp.maximum(i * blocks_per_tile - 1, 0), 0)

    def nxt(i, *_):
        return (jnp.minimum((i + 1) * blocks_per_tile, nblocks - 1), 0)

    return main, prev, nxt


def _fill_normed(x_scr, h_ref, hp_ref, hn_ref, g, first, last, tm):
    def norm(x):
        return x * lax.rsqrt(jnp.mean(x * x, axis=-1, keepdims=True) + RMS_EPS) * g

    x_scr[0:HALO, :] = norm(jnp.where(first, 0.0, hp_ref[...]))
    x_scr[HALO:HALO + tm, :] = norm(h_ref[...])
    x_scr[HALO + tm:2 * HALO + tm, :] = norm(jnp.where(last, 0.0, hn_ref[...]))


def _proj_kernel(h_ref, hp_ref, hn_ref, g_ref, w_ref, rcw_ref, rcb_ref, hcw_ref, hcb_ref,
                 urg_ref, ga_ref, ub_ref, x0_ref, z_ref, x_scr, u_scr, *, tiles_per_seq, tm):
    i = pl.program_id(0)
    first = (i % tiles_per_seq) == 0
    last = (i % tiles_per_seq) == tiles_per_seq - 1
    _fill_normed(x_scr, h_ref, hp_ref, hn_ref, g_ref[...], first, last, tm)
    u_scr[...] = _dot(x_scr[...].astype(BF16), w_ref[...])
    a0, a1, a2, a3 = 0, RG_WIDTH, 2 * RG_WIDTH, 2 * RG_WIDTH + S5_WIDTH
    acc = rcb_ref[...]
    for j in range(rcw_ref.shape[0]):
        acc = acc + u_scr[pl.ds(HALO + j - 1, tm), a0:a1] * rcw_ref[j:j + 1, :]
    urg_ref[...] = acc
    ga_ref[...] = u_scr[pl.ds(HALO, tm), a1:a2]
    ub_ref[...] = u_scr[pl.ds(HALO, tm), a2:a3]
    q = []
    for part in range(3):
        lo = part * HY_WIDTH
        acc = hcb_ref[:, lo:lo + HY_WIDTH]
        for j in range(hcw_ref.shape[0]):
            acc = acc + (u_scr[pl.ds(HALO + j - 1, tm), a3 + lo:a3 + lo + HY_WIDTH]
                         * hcw_ref[j:j + 1, lo:lo + HY_WIDTH])
        q.append(acc)
    x0_ref[...] = q[0]
    z_ref[...] = q[2] * q[1]


def _proj(h, g, w_in, rcw, rcb, hcw, hcb, l, seq, tm):
    n, d = h.shape
    cols = w_in.shape[-1]
    main, prev, nxt = _halo_specs(tm, n)
    row = lambda c: pl.BlockSpec((tm, c), lambda i: (i, 0))
    outs = [jax.ShapeDtypeStruct((n, c), F32) for c in (RG_WIDTH, RG_WIDTH, S5_WIDTH, HY_WIDTH, HY_WIDTH)]
    consts = [g, w_in, rcw, rcb, hcw, hcb]
    return pl.pallas_call(
        functools.partial(_proj_kernel, tiles_per_seq=seq // tm, tm=tm),
        grid=(n // tm,),
        in_specs=[pl.BlockSpec((tm, d), main), pl.BlockSpec((HALO, d), prev), pl.BlockSpec((HALO, d), nxt)]
                 + [_layer_spec(a, l) for a in consts],
        out_specs=[row(RG_WIDTH), row(RG_WIDTH), row(S5_WIDTH), row(HY_WIDTH), row(HY_WIDTH)],
        out_shape=outs,
        scratch_shapes=[pltpu.VMEM((tm + 2 * HALO, d), F32), pltpu.VMEM((tm + 2 * HALO, cols), F32)],
        compiler_params=_params(("parallel",)),
        name="proj",
    )(h, h, h, *consts)


def _lo_mask():
    return lax.broadcasted_iota(jnp.int32, (SUBLANES, LANES), 0) < (SUBLANES // 2)


def _swap_halves(x):
    return pltpu.roll(x, SUBLANES // 2, 0)


def _rglru_kernel(u_ref, w_ref, bias_ref, sp_ref, o_ref, a_scr, b_scr, carry_scr, *, t, nb, reverse):
    i = pl.program_id(0)
    nm = RG_WIDTH // LANES

    @pl.when(i == 0)
    def _():
        carry_scr[...] = jnp.zeros_like(carry_scr)

    u = u_ref[...].reshape(nb * t, RG_WIDTH)
    gates = _dot(u.astype(BF16), w_ref[...]) + bias_ref[...]
    r = _sigmoid(gates[:, :RG_WIDTH])
    gi = _sigmoid(gates[:, RG_WIDTH:])
    log_a = (-RG_C) * r * sp_ref[...]
    a = jnp.exp(log_a)
    mult = jnp.sqrt(-jnp.tanh(log_a) * (a * a + 1.0))
    bb = mult * (gi * u)
    for b in range(nb):
        for m in range(nm):
            a_scr[m, pl.ds(b, t, stride=nb), :] = a[b * t:(b + 1) * t, m * LANES:(m + 1) * LANES]
            b_scr[m, pl.ds(b, t, stride=nb), :] = bb[b * t:(b + 1) * t, m * LANES:(m + 1) * LANES]

    lo = _lo_mask()
    ntile = t // 2

    def body(s, carry):
        jj = (ntile - 1 - s) if reverse else s
        row = pl.multiple_of(jj * SUBLANES, SUBLANES)
        new = []
        for m in range(nm):
            at = a_scr[m, pl.ds(row, SUBLANES), :]
            bt = b_scr[m, pl.ds(row, SUBLANES), :]
            h1 = at * _swap_halves(carry[m]) + bt
            h2 = at * _swap_halves(h1) + bt
            out = jnp.where(lo, h2, h1) if reverse else jnp.where(lo, h1, h2)
            b_scr[m, pl.ds(row, SUBLANES), :] = out
            new.append(out)
        return tuple(new)

    carry = lax.fori_loop(0, ntile, body, tuple(carry_scr[m] for m in range(nm)))
    for m in range(nm):
        carry_scr[m] = carry[m]
    for b in range(nb):
        for m in range(nm):
            o_ref[b, :, m * LANES:(m + 1) * LANES] = b_scr[m, pl.ds(b, t, stride=nb), :]


def _rglru_dir(u3, w, bias, sp, l, dr, t):
    nb, seq, c = u3.shape
    assert 2 * nb == SUBLANES
    nchunk = seq // t
    reverse = dr == 1
    idx = (lambda i: (0, nchunk - 1 - i, 0)) if reverse else (lambda i: (0, i, 0))
    nm = c // LANES
    return pl.pallas_call(
        functools.partial(_rglru_kernel, t=t, nb=nb, reverse=reverse),
        grid=(nchunk,),
        in_specs=[pl.BlockSpec((nb, t, c), idx), _layer_spec(w, l, dr), _layer_spec(bias, l, dr),
                  _layer_spec(sp, l, dr)],
        out_specs=pl.BlockSpec((nb, t, c), idx),
        out_shape=jax.ShapeDtypeStruct((nb, seq, c), F32),
        scratch_shapes=[pltpu.VMEM((nm, nb * t, LANES), F32), pltpu.VMEM((nm, nb * t, LANES), F32),
                        pltpu.VMEM((nm, SUBLANES, LANES), F32)],
        compiler_params=_params(("arbitrary",)),
        name="rglru_bwd" if reverse else "rglru_fwd",
    )(u3, w, bias, sp)


def _rglru_weights(p):
    eye = jnp.eye(RG_HEADS, dtype=F32)
    both = jnp.stack([p["rg_wa"], p["rg_wx"]], axis=2).astype(F32)
    w = jnp.einsum('ldqhij,hk->ldhiqkj', both, eye)
    depth = w.shape[0]
    w = w.reshape(depth, N_DIR, RG_WIDTH, 2 * RG_WIDTH).astype(BF16)
    bias = jnp.concatenate([p["rg_ba"], p["rg_bx"]], axis=-1).astype(F32)[:, :, None, :]
    x = -p["rg_lambda"].astype(F32)
    sp = (jnp.maximum(x, 0.0) + jnp.log1p(jnp.exp(-jnp.abs(x))))[:, :, None, :]
    return w, bias, sp


S5_NBLK = S5_NSTATE // LANES
S5_GRP = 4
S5_PER = S5_NBLK // (S5_WIDTH // LANES)


def _reverse_tiles(src, dst, nm, ntile):
    def body(j, c):
        s = pl.multiple_of((ntile - 1 - j) * SUBLANES, SUBLANES)
        d = pl.multiple_of(j * SUBLANES, SUBLANES)
        for m in range(nm):
            dst[m, pl.ds(d, SUBLANES), :] = src[m, pl.ds(s, SUBLANES), :]
        return c

    lax.fori_loop(0, ntile, body, 0, unroll=8)


def _s5_kernel(uf_ref, ub_ref, wb_ref, wc_ref, ar_ref, ai_ref, yf_ref, yb_ref,
               uf8, ub8, ubr, bu, ym, ymr, carry_scr, *, t, nb):
    i = pl.program_id(0)
    nm = S5_WIDTH // LANES

    @pl.when(i == 0)
    def _():
        carry_scr[...] = jnp.zeros_like(carry_scr)
        uf8[...] = jnp.zeros_like(uf8)
        ub8[...] = jnp.zeros_like(ub8)

    for b in range(nb):
        for m in range(nm):
            uf8[m, pl.ds(b, t, stride=SUBLANES), :] = uf_ref[b, :, m * LANES:(m + 1) * LANES]
            ub8[m, pl.ds(nb + b, t, stride=SUBLANES), :] = ub_ref[b, :, m * LANES:(m + 1) * LANES]
    _reverse_tiles(ub8, ubr, nm, t)

    for m in range(nm):
        lhs = jnp.concatenate([uf8[m], ubr[m]], axis=1).astype(BF16)
        for n in range(m * S5_PER, (m + 1) * S5_PER):
            res = _dot(lhs, wb_ref[n])
            bu[2 * n] = res[:, :LANES]
            bu[2 * n + 1] = res[:, LANES:]

    for g in range(S5_NBLK // S5_GRP):
        blocks = list(range(g * S5_GRP, (g + 1) * S5_GRP))
        ars = [ar_ref[n] for n in blocks]
        ais = [ai_ref[n] for n in blocks]

        def body(j, carry, blocks=blocks, ars=ars, ais=ais):
            row = pl.multiple_of(j * SUBLANES, SUBLANES)
            new = []
            for k, n in enumerate(blocks):
                hr, hi = carry[2 * k], carry[2 * k + 1]
                nr = ars[k] * hr - ais[k] * hi + bu[2 * n, pl.ds(row, SUBLANES), :]
                ni = ars[k] * hi + ais[k] * hr + bu[2 * n + 1, pl.ds(row, SUBLANES), :]
                bu[2 * n, pl.ds(row, SUBLANES), :] = nr
                bu[2 * n + 1, pl.ds(row, SUBLANES), :] = ni
                new += [nr, ni]
            return tuple(new)

        init = []
        for n in blocks:
            init += [carry_scr[2 * n], carry_scr[2 * n + 1]]
        carry = lax.fori_loop(0, t, body, tuple(init), unroll=2)
        for k, n in enumerate(blocks):
            carry_scr[2 * n] = carry[2 * k]
            carry_scr[2 * n + 1] = carry[2 * k + 1]

    rows = lax.broadcasted_iota(jnp.int32, (SUBLANES * t, LANES), 0)
    fwd_row = (rows & (SUBLANES - 1)) < nb
    for m in range(nm):
        acc = None
        for n in range(m * S5_PER, (m + 1) * S5_PER):
            hcat = jnp.concatenate([bu[2 * n], bu[2 * n + 1]], axis=1).astype(BF16)
            term = _dot(hcat, wc_ref[n])
            acc = term if acc is None else acc + term
        ym[m] = jnp.where(fwd_row, acc[:, :LANES], acc[:, LANES:])
    _reverse_tiles(ym, ymr, nm, t)
    for b in range(nb):
        for m in range(nm):
            yf_ref[b, :, m * LANES:(m + 1) * LANES] = ym[m, pl.ds(b, t, stride=SUBLANES), :]
            yb_ref[b, :, m * LANES:(m + 1) * LANES] = ymr[m, pl.ds(nb + b, t, stride=SUBLANES), :]


def _s5(u3, wb, wc, ar, ai, l, t):
    nb, seq, c = u3.shape
    assert 2 * nb == SUBLANES
    nchunk = seq // t
    fwd = lambda i: (0, i, 0)
    bwd = lambda i: (0, nchunk - 1 - i, 0)
    nm = c // LANES
    rows = SUBLANES * t
    out = jax.ShapeDtypeStruct((nb, seq, c), F32)
    return pl.pallas_call(
        functools.partial(_s5_kernel, t=t, nb=nb),
        grid=(nchunk,),
        in_specs=[pl.BlockSpec((nb, t, c), fwd), pl.BlockSpec((nb, t, c), bwd),
                  _layer_spec(wb, l), _layer_spec(wc, l), _layer_spec(ar, l), _layer_spec(ai, l)],
        out_specs=[pl.BlockSpec((nb, t, c), fwd), pl.BlockSpec((nb, t, c), bwd)],
        out_shape=[out, out],
        scratch_shapes=[pltpu.VMEM((nm, rows, LANES), F32), pltpu.VMEM((nm, rows, LANES), F32),
                        pltpu.VMEM((nm, rows, LANES), F32),
                        pltpu.VMEM((2 * S5_NBLK, rows, LANES), F32),
                        pltpu.VMEM((nm, rows, LANES), F32), pltpu.VMEM((nm, rows, LANES), F32),
                        pltpu.VMEM((2 * S5_NBLK, SUBLANES, LANES), F32)],
        compiler_params=_params(("arbitrary",)),
        name="s5",
    )(u3, u3, wb, wc, ar, ai)


def _s5_group_mask():
    mask = np.zeros((S5_NBLK, LANES // S5_GROUP, LANES // S5_STATE), np.float32)
    for n in range(S5_NBLK):
        for s in range(LANES // S5_STATE):
            mask[n, (LANES // S5_STATE) * (n % S5_PER) + s, s] = 1.0
    return mask


def _s5_weights(p):
    lr = p["s5_a_re"].astype(F32)
    li = p["s5_a_im"].astype(F32)
    dt = jnp.exp(p["s5_log_dt"].astype(F32))[..., None]
    mag = jnp.exp(lr * dt)
    abar_r = mag * jnp.cos(li * dt)
    abar_i = mag * jnp.sin(li * dt)
    den = lr * lr + li * li
    nr = abar_r - 1.0
    ni = abar_i
    coef_r = ((nr * lr + ni * li) / den)[..., None]
    coef_i = ((ni * lr - nr * li) / den)[..., None]
    b_re = p["s5_b_re"].astype(F32)
    b_im = p["s5_b_im"].astype(F32)
    bbar = jnp.stack([coef_r * b_re - coef_i * b_im, coef_r * b_im + coef_i * b_re], axis=2)
    depth = lr.shape[0]
    half = LANES // S5_STATE
    mask = jnp.asarray(_s5_group_mask())
    bb = bbar.reshape(depth, N_DIR, 2, S5_NBLK, half, S5_STATE, S5_GROUP)
    wb = jnp.einsum('ldrnspc,nks->lndkcrsp', bb, mask).reshape(depth, S5_NBLK, 2 * LANES, 2 * LANES)
    cc = jnp.stack([p["s5_c_re"].astype(F32), -p["s5_c_im"].astype(F32)], axis=2)
    cc = cc.reshape(depth, N_DIR, 2, S5_NBLK, half, S5_GROUP, S5_STATE)
    wc = jnp.einsum('ldrnscp,nks->lnrspdkc', cc, mask).reshape(depth, S5_NBLK, 2 * LANES, 2 * LANES)

    def tile_rows(a):
        a = a.reshape(depth, N_DIR, S5_NBLK, LANES).transpose(0, 2, 1, 3)
        return jnp.repeat(a, SUBLANES // N_DIR, axis=2)

    return wb.astype(BF16), wc.astype(BF16), tile_rows(abar_r), tile_rows(abar_i)


def _odd8(n):
    p = -(-n // SUBLANES)
    if p % 2 == 0:
        p += 1
    return p * SUBLANES


class _FftPlan:
    def __init__(self, seq):
        self.seq = seq
        r = 1
        while r * r < seq:
            r *= 2
        self.r = r
        self.nq = seq // r
        self.q = 2 * seq // r
        self.k1n = self.q // 2 + 1
        self.kp = -(-self.k1n // SUBLANES) * SUBLANES
        self.zpitch = _odd8(r)
        self.apitch = _odd8(2 * self.kp)
        self.cpitch = _odd8(2 * r)
        self.unroll = min(8, r)
        self.unroll2 = next(u for u in (13, 5, 4, 3, 2, 1) if self.k1n % u == 0)
        p = 2 * seq
        n1 = np.arange(self.nq)[None, :]
        k1 = np.arange(self.k1n)[:, None]
        ang = 2.0 * np.pi * n1 * k1 / self.q
        f1 = np.zeros((2 * self.kp, self.nq))
        f1[:self.k1n] = np.cos(ang)
        f1[self.kp:self.kp + self.k1n] = -np.sin(ang)
        self.f1 = f1
        w = np.full((self.k1n,), 2.0)
        w[0] = 1.0
        w[-1] = 1.0
        g1 = np.zeros((self.nq, 2 * self.kp))
        g1[:, :self.k1n] = (np.cos(ang) * w[:, None] / p).T
        g1[:, self.kp:self.kp + self.k1n] = (-np.sin(ang) * w[:, None] / p).T
        self.g1 = g1
        kk = np.arange(self.k1n)[:, None, None]
        k2 = np.arange(r)[None, :, None]
        n2 = np.arange(r)[None, None, :]
        ph = 2.0 * np.pi * (n2 * k2 / r + n2 * kk / p)
        tr, ti = np.cos(ph), -np.sin(ph)
        self.m2 = np.concatenate([np.concatenate([tr, -ti], axis=2),
                                  np.concatenate([ti, tr], axis=2)], axis=1)
        ur, ui = np.transpose(tr, (0, 2, 1)), -np.transpose(ti, (0, 2, 1))
        self.m2i = np.concatenate([np.concatenate([ur, -ui], axis=2),
                                   np.concatenate([ui, ur], axis=2)], axis=1)


def _fft_forward(plan, src_ref, f1_ref, m2_ref, zp, as_, emit):
    r, nq, kp = plan.r, plan.nq, plan.kp
    for n1 in range(nq):
        zp[n1 * plan.zpitch:n1 * plan.zpitch + r, :] = src_ref[n1 * r:(n1 + 1) * r, :]

    def stage1(n2, c):
        slab = zp[pl.ds(n2, nq, stride=plan.zpitch), :]
        a = _dot(f1_ref[...].astype(BF16), slab.astype(BF16))
        as_[pl.ds(pl.multiple_of(n2 * plan.apitch, SUBLANES), 2 * kp), :] = a
        return c

    lax.fori_loop(0, r, stage1, 0, unroll=plan.unroll)

    def stage2(k1, c):
        sr = as_[pl.ds(k1, r, stride=plan.apitch), :]
        si = as_[pl.ds(kp + k1, r, stride=plan.apitch), :]
        s = jnp.concatenate([sr, si], axis=0).astype(BF16)
        emit(k1, _dot(m2_ref[k1].astype(BF16), s))
        return c

    lax.fori_loop(0, plan.k1n, stage2, 0, unroll=plan.unroll2)


def _hy_spec_kernel(kf_ref, kb_ref, f1_ref, m2_ref, o_ref, zp, as_, *, plan):
    r = plan.r

    def emit_f(k1, x):
        o_ref[0, k1] = x

    def emit_b(k1, x):
        sign = jnp.where(lax.broadcasted_iota(jnp.int32, (2 * r, LANES), 0) < r, 1.0, -1.0)
        o_ref[0, k1] = o_ref[0, k1] + sign * x

    _fft_forward(plan, kf_ref, f1_ref, m2_ref, zp, as_, emit_f)
    _fft_forward(plan, kb_ref, f1_ref, m2_ref, zp, as_, emit_b)


def _hy_conv_kernel(z_ref, spec_ref, f1_ref, m2_ref, m2i_ref, g1_ref, o_ref, zp, as_, cs, *, plan):
    r, nq, kp, k1n = plan.r, plan.nq, plan.kp, plan.k1n
    if kp > k1n:
        cs[k1n * plan.cpitch:kp * plan.cpitch, :] = jnp.zeros(((kp - k1n) * plan.cpitch, LANES), F32)

    def emit(k1, x):
        kf = spec_ref[0, k1]
        xr, xi = x[:r], x[r:]
        kr, ki = kf[:r], kf[r:]
        prod = jnp.concatenate([xr * kr - xi * ki, xr * ki + xi * kr], axis=0).astype(BF16)
        c = _dot(m2i_ref[k1].astype(BF16), prod)
        cs[pl.ds(pl.multiple_of(k1 * plan.cpitch, SUBLANES), 2 * r), :] = c

    _fft_forward(plan, z_ref.at[0], f1_ref, m2_ref, zp, as_, emit)

    def stage3(n2, c):
        cr = cs[pl.ds(n2, kp, stride=plan.cpitch), :]
        ci = cs[pl.ds(r + n2, kp, stride=plan.cpitch), :]
        y = _dot(g1_ref[...].astype(BF16), jnp.concatenate([cr, ci], axis=0).astype(BF16))
        zp[pl.ds(n2, nq, stride=plan.zpitch), :] = y
        return c

    lax.fori_loop(0, r, stage3, 0, unroll=plan.unroll)
    for n1 in range(nq):
        o_ref[0, n1 * r:(n1 + 1) * r, :] = zp[n1 * plan.zpitch:n1 * plan.zpitch + r, :]


def _hy_filter_kernel(feat_ref, w1_ref, b1_ref, f1_ref, w2_ref, b2_ref, f2_ref, w3_ref, dl_ref, o_ref):
    feats = feat_ref[...]
    hid = jnp.sin(f1_ref[...] * (_dot(feats, w1_ref[...], HIGHEST) + b1_ref[...]))
    hid = jnp.sin(f2_ref[...] * (_dot(hid, w2_ref[...], HIGHEST) + b2_ref[...]))
    k = _dot(hid, w3_ref[...], HIGHEST)
    t = feats[:, 0:1]
    o_ref[...] = k * jnp.exp(-(t * dl_ref[...]))


def _hyena_features(seq):
    pos = np.arange(seq, dtype=np.float64)
    t = pos / max(seq - 1, 1)
    w = (2.0 * math.pi / seq) * pos
    bands = np.linspace(1e-4, HY_BANDS - 1, HY_BANDS, dtype=np.float64)
    ang = w[:, None] * bands
    feats = np.concatenate([t[:, None], np.cos(ang), -np.sin(ang)], axis=-1).astype(np.float32)
    out = np.zeros((seq, LANES), np.float32)
    out[:, :feats.shape[1]] = feats
    return out


def _hyena_deltas():
    max_decay = math.log(HY_DECAY_TARGET) / HY_FAST_DECAY
    min_decay = math.log(HY_DECAY_TARGET) / HY_SLOW_DECAY
    deltas = np.abs(np.linspace(min_decay, max_decay, HY_WIDTH, dtype=np.float64))
    return np.tile(deltas, 2)[None, :].astype(np.float32)


def _pad_to(x, rows, cols):
    x = x.astype(F32)
    return jnp.pad(x, ((0, 0), (0, rows - x.shape[1]), (0, cols - x.shape[2])))


def _hyena_filter_weights(p):
    hid = LANES
    vec = lambda a: _pad_to(a[:, None, :], 1, hid)
    return (_pad_to(p["hy_filt_w1"], LANES, hid), vec(p["hy_filt_b1"]), vec(p["hy_filt_freq1"]),
            _pad_to(p["hy_filt_w2"], hid, hid), vec(p["hy_filt_b2"]), vec(p["hy_filt_freq2"]),
            _pad_to(p["hy_filt_w3"], hid, 2 * HY_WIDTH))


def _hyena_filters(seq, fw, l, tf):
    feats = jnp.asarray(_hyena_features(seq))
    deltas = jnp.asarray(_hyena_deltas())
    return pl.pallas_call(
        _hy_filter_kernel,
        grid=(seq // tf,),
        in_specs=[pl.BlockSpec((tf, LANES), lambda i: (i, 0))] + [_layer_spec(a, l) for a in fw]
                 + [_full(deltas.shape)],
        out_specs=pl.BlockSpec((tf, 2 * HY_WIDTH), lambda i: (i, 0)),
        out_shape=jax.ShapeDtypeStruct((seq, 2 * HY_WIDTH), F32),
        compiler_params=_params(("parallel",)),
        name="hy_filter",
    )(feats, *fw, deltas)


def _hyena_conv(z3, filt, plan):
    nb, seq, c = z3.shape
    nh = c // LANES
    f1 = jnp.asarray(plan.f1, F32)
    m2 = jnp.asarray(plan.m2, F32)
    m2i = jnp.asarray(plan.m2i, F32)
    g1 = jnp.asarray(plan.g1, F32)
    r, kp, k1n = plan.r, plan.kp, plan.k1n
    zp_shape = (plan.nq * plan.zpitch, LANES)
    as_shape = (r * plan.apitch, LANES)
    cs_shape = (kp * plan.cpitch, LANES)
    spec = pl.pallas_call(
        functools.partial(_hy_spec_kernel, plan=plan),
        grid=(nh,),
        in_specs=[pl.BlockSpec((seq, LANES), lambda j: (0, j)),
                  pl.BlockSpec((seq, LANES), lambda j: (0, nh + j)),
                  _full(f1.shape), _full(m2.shape)],
        out_specs=pl.BlockSpec((1, k1n, 2 * r, LANES), lambda j: (j, 0, 0, 0)),
        out_shape=jax.ShapeDtypeStruct((nh, k1n, 2 * r, LANES), F32),
        scratch_shapes=[pltpu.VMEM(zp_shape, F32), pltpu.VMEM(as_shape, F32)],
        compiler_params=_params(("parallel",)),
        name="hy_spec",
    )(filt, filt, f1, m2)
    return pl.pallas_call(
        functools.partial(_hy_conv_kernel, plan=plan),
        grid=(nb, nh),
        in_specs=[pl.BlockSpec((1, seq, LANES), lambda b, j: (b, 0, j)),
                  pl.BlockSpec((1, k1n, 2 * r, LANES), lambda b, j: (j, 0, 0, 0)),
                  _full(f1.shape), _full(m2.shape), _full(m2i.shape), _full(g1.shape)],
        out_specs=pl.BlockSpec((1, seq, LANES), lambda b, j: (b, 0, j)),
        out_shape=jax.ShapeDtypeStruct((nb, seq, c), F32),
        scratch_shapes=[pltpu.VMEM(zp_shape, F32), pltpu.VMEM(as_shape, F32), pltpu.VMEM(cs_shape, F32)],
        compiler_params=_params(("parallel", "parallel")),
        name="hy_conv",
    )(z3, spec, f1, m2, m2i, g1)


def _mix_kernel(h_ref, hf_ref, hb_ref, ga_ref, yf_ref, yb_ref, ub_ref, yc_ref, z_ref, x0_ref,
                d_ref, gw_ref, gb_ref, hbias_ref, mg_ref, wo_ref, o_ref):
    ya = (hf_ref[...] + hb_ref[...]) * _gelu(ga_ref[...])
    yb = _gelu(ub_ref[...] * d_ref[...] + yf_ref[...] + yb_ref[...])
    yb = yb * _sigmoid(_dot(yb.astype(BF16), gw_ref[...]) + gb_ref[...])
    z = z_ref[...]
    yc = (yc_ref[...] + z * hbias_ref[...]) * x0_ref[...]
    a1, a2 = RG_WIDTH, RG_WIDTH + S5_WIDTH
    na = (_rms_nogain(ya) * mg_ref[:, :a1]).astype(BF16)
    nb = (_rms_nogain(yb) * mg_ref[:, a1:a2]).astype(BF16)
    nc = (_rms_nogain(yc) * mg_ref[:, a2:]).astype(BF16)
    out = _dot(na, wo_ref[:a1, :]) + _dot(nb, wo_ref[a1:a2, :]) + _dot(nc, wo_ref[a2:, :])
    o_ref[...] = h_ref[...] + out


def _mix(rows, consts, l, tm):
    n, dm = rows[0].shape
    return pl.pallas_call(
        _mix_kernel,
        grid=(n // tm,),
        in_specs=[pl.BlockSpec((tm, a.shape[1]), lambda i: (i, 0)) for a in rows]
                 + [_layer_spec(a, l) for a in consts],
        out_specs=pl.BlockSpec((tm, dm), lambda i: (i, 0)),
        out_shape=jax.ShapeDtypeStruct((n, dm), F32),
        compiler_params=_params(("parallel",)),
        name="mix",
    )(*rows, *consts)


def _ffn_kernel(h_ref, hp_ref, hn_ref, g_ref, wa_ref, wv_ref, cwa_ref, cwv_ref, cba_ref, cbv_ref,
                wd_ref, fg_ref, o_ref, x_scr, ua_scr, uv_scr, acc_scr, *, tiles_per_seq, tm, final):
    i = pl.program_id(0)
    c = pl.program_id(1)
    nc = pl.num_programs(1)

    @pl.when(c == 0)
    def _():
        first = (i % tiles_per_seq) == 0
        last = (i % tiles_per_seq) == tiles_per_seq - 1
        _fill_normed(x_scr, h_ref, hp_ref, hn_ref, g_ref[...], first, last, tm)
        acc_scr[...] = jnp.zeros_like(acc_scr)

    x = x_scr[...].astype(BF16)
    ua_scr[...] = _dot(x, wa_ref[...])
    uv_scr[...] = _dot(x, wv_ref[...])
    a = cba_ref[...]
    v = cbv_ref[...]
    for j in range(cwa_ref.shape[0]):
        a = a + ua_scr[pl.ds(HALO + j - 1, tm), :] * cwa_ref[j:j + 1, :]
        v = v + uv_scr[pl.ds(HALO + j - 1, tm), :] * cwv_ref[j:j + 1, :]
    gated = (_gelu(a) * v).astype(BF16)
    acc_scr[...] += _dot(gated, wd_ref[...])

    @pl.when(c == nc - 1)
    def _():
        out = h_ref[...] + acc_scr[...]
        if final:
            out = out * lax.rsqrt(jnp.mean(out * out, axis=-1, keepdims=True) + RMS_EPS) * fg_ref[...]
        o_ref[...] = out


def _ffn(h, g, w_up, conv_w, conv_b, w_down, final_g, l, seq, tm, nchunk, final):
    n, d = h.shape
    fc = D_FF // nchunk
    taps = conv_w.shape[1]
    main, prev, nxt = _halo_specs(tm, n)
    return pl.pallas_call(
        functools.partial(_ffn_kernel, tiles_per_seq=seq // tm, tm=tm, final=final),
        grid=(n // tm, nchunk),
        in_specs=[pl.BlockSpec((tm, d), main), pl.BlockSpec((HALO, d), prev), pl.BlockSpec((HALO, d), nxt),
                  _layer_spec(g, l),
                  pl.BlockSpec((None, d, fc), lambda i, c: (l, 0, c)),
                  pl.BlockSpec((None, d, fc), lambda i, c: (l, 0, nchunk + c)),
                  pl.BlockSpec((None, taps, fc), lambda i, c: (l, 0, c)),
                  pl.BlockSpec((None, taps, fc), lambda i, c: (l, 0, nchunk + c)),
                  pl.BlockSpec((None, 1, fc), lambda i, c: (l, 0, c)),
                  pl.BlockSpec((None, 1, fc), lambda i, c: (l, 0, nchunk + c)),
                  pl.BlockSpec((None, fc, d), lambda i, c: (l, c, 0)),
                  _full(final_g.shape)],
        out_specs=pl.BlockSpec((tm, d), lambda i, c: (i, 0)),
        out_shape=jax.ShapeDtypeStruct((n, d), F32),
        scratch_shapes=[pltpu.VMEM((tm + 2 * HALO, d), F32), pltpu.VMEM((tm + 2 * HALO, fc), F32),
                        pltpu.VMEM((tm + 2 * HALO, fc), F32), pltpu.VMEM((tm, d), F32)],
        compiler_params=_params(("parallel", "arbitrary")),
        name="ffn_final" if final else "ffn",
    )(h, h, h, g, w_up, w_up, conv_w, conv_w, conv_b, conv_b, w_down, final_g)


def _row_vec(a):
    return a.astype(F32)[:, None, :]


def _trunk(x, p, *, tm, t_scan, t_s5, ffn_chunks, tf):
    nb, seq, d = x.shape
    n = nb * seq
    depth = p["w_in"].shape[0]
    plan = _FftPlan(seq)

    proj_c = (_row_vec(p["norm1_g"]), p["w_in"].astype(BF16), p["rg_conv_w"].astype(F32),
              _row_vec(p["rg_conv_b"]), p["hy_conv_w"].astype(F32), _row_vec(p["hy_conv_b"]))
    rg_w, rg_bias, rg_sp = _rglru_weights(p)
    s5_w = _s5_weights(p)
    hy_fw = _hyena_filter_weights(p)
    mix_c = (_row_vec(p["s5_d"]), p["s5_glu_w"].astype(BF16), _row_vec(p["s5_glu_b"]),
             _row_vec(p["hy_bias"]), _row_vec(p["mix_norm_g"]), p["w_out"].astype(BF16))
    ffn_c = (_row_vec(p["norm2_g"]), p["w_up"].astype(BF16), p["ffn_conv_w"].astype(F32),
             _row_vec(p["ffn_conv_b"]), p["w_down"].astype(BF16), p["final_norm_g"].astype(F32)[None])

    h = x.astype(F32).reshape(n, d)
    for l in range(depth):
        urg, ga, ub, x0, z = _proj(h, *proj_c, l, seq, tm)
        urg3 = urg.reshape(nb, seq, RG_WIDTH)
        hs = [_rglru_dir(urg3, rg_w, rg_bias, rg_sp, l, dr, t_scan).reshape(n, RG_WIDTH)
              for dr in range(N_DIR)]
        yf, yb = _s5(ub.reshape(nb, seq, S5_WIDTH), *s5_w, l, t_s5)
        filt = _hyena_filters(seq, hy_fw, l, tf)
        yc = _hyena_conv(z.reshape(nb, seq, HY_WIDTH), filt, plan).reshape(n, HY_WIDTH)
        rows = [h, hs[0], hs[1], ga, yf.reshape(n, S5_WIDTH), yb.reshape(n, S5_WIDTH), ub, yc, z, x0]
        h = _mix(rows, mix_c, l, tm)
        h = _ffn(h, *ffn_c, l, seq, tm, ffn_chunks, final=(l == depth - 1))
    return h.reshape(nb, seq, d).astype(x.dtype)


def kernel(x, norm1_g, w_in, rg_conv_w, rg_conv_b, rg_wa, rg_ba, rg_wx, rg_bx, rg_lambda, s5_a_re, s5_a_im, s5_log_dt, s5_b_re, s5_b_im, s5_c_re, s5_c_im, s5_d, s5_glu_w, s5_glu_b, hy_conv_w, hy_conv_b, hy_filt_w1, hy_filt_b1, hy_filt_freq1, hy_filt_w2, hy_filt_b2, hy_filt_freq2, hy_filt_w3, hy_bias, mix_norm_g, w_out, norm2_g, w_up, ffn_conv_w, ffn_conv_b, w_down, final_norm_g):
    p = dict(norm1_g=norm1_g, w_in=w_in, rg_conv_w=rg_conv_w, rg_conv_b=rg_conv_b, rg_wa=rg_wa, rg_ba=rg_ba,
             rg_wx=rg_wx, rg_bx=rg_bx, rg_lambda=rg_lambda, s5_a_re=s5_a_re, s5_a_im=s5_a_im,
             s5_log_dt=s5_log_dt, s5_b_re=s5_b_re, s5_b_im=s5_b_im, s5_c_re=s5_c_re, s5_c_im=s5_c_im,
             s5_d=s5_d, s5_glu_w=s5_glu_w, s5_glu_b=s5_glu_b, hy_conv_w=hy_conv_w, hy_conv_b=hy_conv_b,
             hy_filt_w1=hy_filt_w1, hy_filt_b1=hy_filt_b1, hy_filt_freq1=hy_filt_freq1, hy_filt_w2=hy_filt_w2,
             hy_filt_b2=hy_filt_b2, hy_filt_freq2=hy_filt_freq2, hy_filt_w3=hy_filt_w3, hy_bias=hy_bias,
             mix_norm_g=mix_norm_g, w_out=w_out, norm2_g=norm2_g, w_up=w_up, ffn_conv_w=ffn_conv_w,
             ffn_conv_b=ffn_conv_b, w_down=w_down, final_norm_g=final_norm_g)
    return _trunk(x, p, tm=512, t_scan=256, t_s5=128, ffn_chunks=2, tf=512)
```

```python
import functools
import math

import numpy as np
import jax
import jax.numpy as jnp
from jax import lax
from jax.experimental import pallas as pl
from jax.experimental.pallas import tpu as pltpu

F32 = jnp.float32
BF16 = jnp.bfloat16

RMS_EPS = 1e-6
RG_WIDTH = 384
RG_HEADS = 6
RG_C = 8.0
S5_WIDTH = 384
S5_GROUP = 16
S5_GROUPS = 24
S5_STATE = 64
S5_NSTATE = S5_GROUPS * S5_STATE
HY_WIDTH = 256
HY_BANDS = 16
HY_FAST_DECAY = 0.3
HY_SLOW_DECAY = 1.5
HY_DECAY_TARGET = 1e-2
D_FF = 2816
N_DIR = 2

LANES = 128
SUBLANES = 8
HALO = SUBLANES
VMEM_LIMIT = 56 * 1024 * 1024

HIGHEST = lax.Precision.HIGHEST


def _dot(a, b, precision=None):
    return jnp.dot(a, b, preferred_element_type=F32, precision=precision)


def _gelu(x):
    c = math.sqrt(2.0 / math.pi)
    return 0.5 * x * (1.0 + jnp.tanh(c * (x + 0.044715 * (x * x * x))))


def _sigmoid(x):
    return 1.0 / (1.0 + jnp.exp(-x))


def _rms_nogain(x):
    return x * lax.rsqrt(jnp.mean(x * x, axis=-1, keepdims=True) + RMS_EPS)


def _params(sem):
    return pltpu.CompilerParams(dimension_semantics=sem, vmem_limit_bytes=VMEM_LIMIT)


def _full(shape):
    nd = len(shape)
    return pl.BlockSpec(shape, lambda *_: (0,) * nd)


def _layer_spec(arr, *lead):
    rest = arr.shape[len(lead):]
    zeros = (0,) * len(rest)
    return pl.BlockSpec((None,) * len(lead) + rest, lambda *_: tuple(lead) + zeros)


def _halo_specs(tm, n):
    blocks_per_tile = tm // HALO
    nblocks = n // HALO

    def main(i, *_):
        return (i, 0)

    def prev(i, *_):
        return (jnp.maximum(i * blocks_per_tile - 1, 0), 0)

    def nxt(i, *_):
        return (jnp.minimum((i + 1) * blocks_per_tile, nblocks - 1), 0)

    return main, prev, nxt


def _fill_normed(x_scr, h_ref, hp_ref, hn_ref, g, first, last, tm):
    def norm(x):
        return x * lax.rsqrt(jnp.mean(x * x, axis=-1, keepdims=True) + RMS_EPS) * g

    x_scr[0:HALO, :] = norm(jnp.where(first, 0.0, hp_ref[...]))
    x_scr[HALO:HALO + tm, :] = norm(h_ref[...])
    x_scr[HALO + tm:2 * HALO + tm, :] = norm(jnp.where(last, 0.0, hn_ref[...]))


def _proj_kernel(h_ref, hp_ref, hn_ref, g_ref, w_ref, rcw_ref, rcb_ref, hcw_ref, hcb_ref,
                 urg_ref, ga_ref, ub_ref, x0_ref, z_ref, x_scr, u_scr, *, tiles_per_seq, tm):
    i = pl.program_id(0)
    first = (i % tiles_per_seq) == 0
    last = (i % tiles_per_seq) == tiles_per_seq - 1
    _fill_normed(x_scr, h_ref, hp_ref, hn_ref, g_ref[...], first, last, tm)
    u_scr[...] = _dot(x_scr[...].astype(BF16), w_ref[...])
    a0, a1, a2, a3 = 0, RG_WIDTH, 2 * RG_WIDTH, 2 * RG_WIDTH + S5_WIDTH
    acc = rcb_ref[...]
    for j in range(rcw_ref.shape[0]):
        acc = acc + u_scr[pl.ds(HALO + j - 1, tm), a0:a1] * rcw_ref[j:j + 1, :]
    urg_ref[...] = acc
    ga_ref[...] = u_scr[pl.ds(HALO, tm), a1:a2]
    ub_ref[...] = u_scr[pl.ds(HALO, tm), a2:a3]
    q = []
    for part in range(3):
        lo = part * HY_WIDTH
        acc = hcb_ref[:, lo:lo + HY_WIDTH]
        for j in range(hcw_ref.shape[0]):
            acc = acc + (u_scr[pl.ds(HALO + j - 1, tm), a3 + lo:a3 + lo + HY_WIDTH]
                         * hcw_ref[j:j + 1, lo:lo + HY_WIDTH])
        q.append(acc)
    x0_ref[...] = q[0]
    z_ref[...] = q[2] * q[1]


def _proj(h, g, w_in, rcw, rcb, hcw, hcb, l, seq, tm):
    n, d = h.shape
    cols = w_in.shape[-1]
    main, prev, nxt = _halo_specs(tm, n)
    row = lambda c: pl.BlockSpec((tm, c), lambda i: (i, 0))
    outs = [jax.ShapeDtypeStruct((n, c), F32) for c in (RG_WIDTH, RG_WIDTH, S5_WIDTH, HY_WIDTH, HY_WIDTH)]
    consts = [g, w_in, rcw, rcb, hcw, hcb]
    return pl.pallas_call(
        functools.partial(_proj_kernel, tiles_per_seq=seq // tm, tm=tm),
        grid=(n // tm,),
        in_specs=[pl.BlockSpec((tm, d), main), pl.BlockSpec((HALO, d), prev), pl.BlockSpec((HALO, d), nxt)]
                 + [_layer_spec(a, l) for a in consts],
        out_specs=[row(RG_WIDTH), row(RG_WIDTH), row(S5_WIDTH), row(HY_WIDTH), row(HY_WIDTH)],
        out_shape=outs,
        scratch_shapes=[pltpu.VMEM((tm + 2 * HALO, d), F32), pltpu.VMEM((tm + 2 * HALO, cols), F32)],
        compiler_params=_params(("parallel",)),
        name="proj",
    )(h, h, h, *consts)


def _lo_mask():
    return lax.broadcasted_iota(jnp.int32, (SUBLANES, LANES), 0) < (SUBLANES // 2)


def _rglru_kernel(uf_ref, ub_ref, w_ref, bias_ref, sp_ref, of_ref, ob_ref,
                  af, ab, bf, bb, carry_scr, *, t, nb):
    i = pl.program_id(0)
    nm = RG_WIDTH // LANES

    @pl.when(i == 0)
    def _():
        carry_scr[...] = jnp.zeros_like(carry_scr)
        for scr in (af, ab, bf, bb):
            scr[...] = jnp.zeros_like(scr)

    for dr, (u_ref, a_scr, b_scr) in enumerate(((uf_ref, af, bf), (ub_ref, ab, bb))):
        u = u_ref[...].reshape(nb * t, RG_WIDTH)
        gates = _dot(u.astype(BF16), w_ref[dr]) + bias_ref[dr]
        r = _sigmoid(gates[:, :RG_WIDTH])
        gi = _sigmoid(gates[:, RG_WIDTH:])
        log_a = (-RG_C) * r * sp_ref[dr]
        a = jnp.exp(log_a)
        mult = jnp.sqrt(-jnp.tanh(log_a) * (a * a + 1.0))
        bin_ = mult * (gi * u)
        for b in range(nb):
            for m in range(nm):
                rows = pl.ds(dr * nb + b, t, stride=SUBLANES)
                a_scr[m, rows, :] = a[b * t:(b + 1) * t, m * LANES:(m + 1) * LANES]
                b_scr[m, rows, :] = bin_[b * t:(b + 1) * t, m * LANES:(m + 1) * LANES]

    lo = _lo_mask()

    def body(j, carry):
        row = pl.multiple_of(j * SUBLANES, SUBLANES)
        mrow = pl.multiple_of((t - 1 - j) * SUBLANES, SUBLANES)
        new = []
        for m in range(nm):
            at = jnp.where(lo, af[m, pl.ds(row, SUBLANES), :], ab[m, pl.ds(mrow, SUBLANES), :])
            bt = jnp.where(lo, bf[m, pl.ds(row, SUBLANES), :], bb[m, pl.ds(mrow, SUBLANES), :])
            h = at * carry[m] + bt
            bf[m, pl.ds(row, SUBLANES), :] = h
            bb[m, pl.ds(mrow, SUBLANES), :] = h
            new.append(h)
        return tuple(new)

    carry = lax.fori_loop(0, t, body, tuple(carry_scr[m] for m in range(nm)), unroll=4)
    for m in range(nm):
        carry_scr[m] = carry[m]
    for b in range(nb):
        for m in range(nm):
            of_ref[b, :, m * LANES:(m + 1) * LANES] = bf[m, pl.ds(b, t, stride=SUBLANES), :]
            ob_ref[b, :, m * LANES:(m + 1) * LANES] = bb[m, pl.ds(nb + b, t, stride=SUBLANES), :]


def _rglru(u3, w, bias, sp, l, t):
    nb, seq, c = u3.shape
    assert 2 * nb == SUBLANES
    nchunk = seq // t
    fwd = lambda i: (0, i, 0)
    bwd = lambda i: (0, nchunk - 1 - i, 0)
    nm = c // LANES
    out = jax.ShapeDtypeStruct((nb, seq, c), F32)
    scr = pltpu.VMEM((nm, SUBLANES * t, LANES), F32)
    return pl.pallas_call(
        functools.partial(_rglru_kernel, t=t, nb=nb),
        grid=(nchunk,),
        in_specs=[pl.BlockSpec((nb, t, c), fwd), pl.BlockSpec((nb, t, c), bwd),
                  _layer_spec(w, l), _layer_spec(bias, l), _layer_spec(sp, l)],
        out_specs=[pl.BlockSpec((nb, t, c), fwd), pl.BlockSpec((nb, t, c), bwd)],
        out_shape=[out, out],
        scratch_shapes=[scr, scr, scr, scr, pltpu.VMEM((nm, SUBLANES, LANES), F32)],
        compiler_params=_params(("arbitrary",)),
        name="rglru",
    )(u3, u3, w, bias, sp)


def _rglru_weights(p):
    eye = jnp.eye(RG_HEADS, dtype=F32)
    both = jnp.stack([p["rg_wa"], p["rg_wx"]], axis=2).astype(F32)
    w = jnp.einsum('ldqhij,hk->ldhiqkj', both, eye)
    depth = w.shape[0]
    w = w.reshape(depth, N_DIR, RG_WIDTH, 2 * RG_WIDTH).astype(BF16)
    bias = jnp.concatenate([p["rg_ba"], p["rg_bx"]], axis=-1).astype(F32)[:, :, None, :]
    x = -p["rg_lambda"].astype(F32)
    sp = (jnp.maximum(x, 0.0) + jnp.log1p(jnp.exp(-jnp.abs(x))))[:, :, None, :]
    return w, bias, sp


S5_NBLK = S5_NSTATE // LANES
S5_GRP = 4
S5_PER = S5_NBLK // (S5_WIDTH // LANES)


def _reverse_tiles(src, dst, nm, ntile):
    def body(j, c):
        s = pl.multiple_of((ntile - 1 - j) * SUBLANES, SUBLANES)
        d = pl.multiple_of(j * SUBLANES, SUBLANES)
        for m in range(nm):
            dst[m, pl.ds(d, SUBLANES), :] = src[m, pl.ds(s, SUBLANES), :]
        return c

    lax.fori_loop(0, ntile, body, 0, unroll=8)


def _s5_kernel(uf_ref, ub_ref, wb_ref, wc_ref, ar_ref, ai_ref, yf_ref, yb_ref,
               uf8, ub8, ubr, bu, ym, ymr, carry_scr, *, t, nb):
    i = pl.program_id(0)
    nm = S5_WIDTH // LANES

    @pl.when(i == 0)
    def _():
        carry_scr[...] = jnp.zeros_like(carry_scr)
        uf8[...] = jnp.zeros_like(uf8)
        ub8[...] = jnp.zeros_like(ub8)

    for b in range(nb):
        for m in range(nm):
            uf8[m, pl.ds(b, t, stride=SUBLANES), :] = uf_ref[b, :, m * LANES:(m + 1) * LANES]
            ub8[m, pl.ds(nb + b, t, stride=SUBLANES), :] = ub_ref[b, :, m * LANES:(m + 1) * LANES]
    _reverse_tiles(ub8, ubr, nm, t)

    for m in range(nm):
        lhs = jnp.concatenate([uf8[m], ubr[m]], axis=1).astype(BF16)
        res = _dot(lhs, wb_ref[m])
        for q in range(2 * S5_PER):
            bu[2 * S5_PER * m + q] = res[:, q * LANES:(q + 1) * LANES]

    for g in range(S5_NBLK // S5_GRP):
        blocks = list(range(g * S5_GRP, (g + 1) * S5_GRP))
        ars = [ar_ref[n] for n in blocks]
        ais = [ai_ref[n] for n in blocks]

        def body(j, carry, blocks=blocks, ars=ars, ais=ais):
            row = pl.multiple_of(j * SUBLANES, SUBLANES)
            new = []
            for k, n in enumerate(blocks):
                hr, hi = carry[2 * k], carry[2 * k + 1]
                nr = ars[k] * hr - ais[k] * hi + bu[2 * n, pl.ds(row, SUBLANES), :]
                ni = ars[k] * hi + ais[k] * hr + bu[2 * n + 1, pl.ds(row, SUBLANES), :]
                bu[2 * n, pl.ds(row, SUBLANES), :] = nr
                bu[2 * n + 1, pl.ds(row, SUBLANES), :] = ni
                new += [nr, ni]
            return tuple(new)

        init = []
        for n in blocks:
            init += [carry_scr[2 * n], carry_scr[2 * n + 1]]
        carry = lax.fori_loop(0, t, body, tuple(init), unroll=4)
        for k, n in enumerate(blocks):
            carry_scr[2 * n] = carry[2 * k]
            carry_scr[2 * n + 1] = carry[2 * k + 1]

    rows = lax.broadcasted_iota(jnp.int32, (SUBLANES * t, LANES), 0)
    fwd_row = (rows & (SUBLANES - 1)) < nb
    for m in range(nm):
        hcat = jnp.concatenate([bu[2 * S5_PER * m + q] for q in range(2 * S5_PER)], axis=1).astype(BF16)
        acc = _dot(hcat, wc_ref[m])
        ym[m] = jnp.where(fwd_row, acc[:, :LANES], acc[:, LANES:])
    _reverse_tiles(ym, ymr, nm, t)
    for b in range(nb):
        for m in range(nm):
            yf_ref[b, :, m * LANES:(m + 1) * LANES] = ym[m, pl.ds(b, t, stride=SUBLANES), :]
            yb_ref[b, :, m * LANES:(m + 1) * LANES] = ymr[m, pl.ds(nb + b, t, stride=SUBLANES), :]


def _s5(u3, wb, wc, ar, ai, l, t):
    nb, seq, c = u3.shape
    assert 2 * nb == SUBLANES
    nchunk = seq // t
    fwd = lambda i: (0, i, 0)
    bwd = lambda i: (0, nchunk - 1 - i, 0)
    nm = c // LANES
    rows = SUBLANES * t
    out = jax.ShapeDtypeStruct((nb, seq, c), F32)
    return pl.pallas_call(
        functools.partial(_s5_kernel, t=t, nb=nb),
        grid=(nchunk,),
        in_specs=[pl.BlockSpec((nb, t, c), fwd), pl.BlockSpec((nb, t, c), bwd),
                  _layer_spec(wb, l), _layer_spec(wc, l), _layer_spec(ar, l), _layer_spec(ai, l)],
        out_specs=[pl.BlockSpec((nb, t, c), fwd), pl.BlockSpec((nb, t, c), bwd)],
        out_shape=[out, out],
        scratch_shapes=[pltpu.VMEM((nm, rows, LANES), F32), pltpu.VMEM((nm, rows, LANES), F32),
                        pltpu.VMEM((nm, rows, LANES), F32),
                        pltpu.VMEM((2 * S5_NBLK, rows, LANES), F32),
                        pltpu.VMEM((nm, rows, LANES), F32), pltpu.VMEM((nm, rows, LANES), F32),
                        pltpu.VMEM((2 * S5_NBLK, SUBLANES, LANES), F32)],
        compiler_params=_params(("arbitrary",)),
        name="s5",
    )(u3, u3, wb, wc, ar, ai)


def _s5_group_mask():
    mask = np.zeros((S5_NBLK, LANES // S5_GROUP, LANES // S5_STATE), np.float32)
    for n in range(S5_NBLK):
        for s in range(LANES // S5_STATE):
            mask[n, (LANES // S5_STATE) * (n % S5_PER) + s, s] = 1.0
    return mask


def _s5_weights(p):
    lr = p["s5_a_re"].astype(F32)
    li = p["s5_a_im"].astype(F32)
    dt = jnp.exp(p["s5_log_dt"].astype(F32))[..., None]
    mag = jnp.exp(lr * dt)
    abar_r = mag * jnp.cos(li * dt)
    abar_i = mag * jnp.sin(li * dt)
    den = lr * lr + li * li
    nr = abar_r - 1.0
    ni = abar_i
    coef_r = ((nr * lr + ni * li) / den)[..., None]
    coef_i = ((ni * lr - nr * li) / den)[..., None]
    b_re = p["s5_b_re"].astype(F32)
    b_im = p["s5_b_im"].astype(F32)
    bbar = jnp.stack([coef_r * b_re - coef_i * b_im, coef_r * b_im + coef_i * b_re], axis=2)
    depth = lr.shape[0]
    half = LANES // S5_STATE
    mask = jnp.asarray(_s5_group_mask())
    bb = bbar.reshape(depth, N_DIR, 2, S5_NBLK, half, S5_STATE, S5_GROUP)
    nm = S5_WIDTH // LANES
    wb = jnp.einsum('ldrnspc,nks->lndkcrsp', bb, mask).reshape(depth, nm, S5_PER, 2 * LANES, 2 * LANES)
    wb = wb.transpose(0, 1, 3, 2, 4).reshape(depth, nm, 2 * LANES, S5_PER * 2 * LANES)
    cc = jnp.stack([p["s5_c_re"].astype(F32), -p["s5_c_im"].astype(F32)], axis=2)
    cc = cc.reshape(depth, N_DIR, 2, S5_NBLK, half, S5_GROUP, S5_STATE)
    wc = jnp.einsum('ldrnscp,nks->lnrspdkc', cc, mask).reshape(depth, nm, S5_PER * 2 * LANES, 2 * LANES)

    def tile_rows(a):
        a = a.reshape(depth, N_DIR, S5_NBLK, LANES).transpose(0, 2, 1, 3)
        return jnp.repeat(a, SUBLANES // N_DIR, axis=2)

    return wb.astype(BF16), wc.astype(BF16), tile_rows(abar_r), tile_rows(abar_i)


def _odd8(n):
    p = -(-n // SUBLANES)
    if p % 2 == 0:
        p += 1
    return p * SUBLANES


class _FftPlan:
    def __init__(self, seq):
        self.seq = seq
        r = 1
        while r * r < seq:
            r *= 2
        self.r = r
        self.nq = seq // r
        self.q = 2 * seq // r
        self.k1n = self.q // 2 + 1
        self.kp = -(-self.k1n // SUBLANES) * SUBLANES
        self.zpitch = _odd8(r)
        self.apitch = _odd8(2 * self.kp)
        self.cpitch = _odd8(2 * r)
        self.unroll = min(8, r)
        self.unroll2 = next(u for u in (13, 5, 4, 3, 2, 1) if self.k1n % u == 0)
        p = 2 * seq
        n1 = np.arange(self.nq)[None, :]
        k1 = np.arange(self.k1n)[:, None]
        ang = 2.0 * np.pi * n1 * k1 / self.q
        f1 = np.zeros((2 * self.kp, self.nq))
        f1[:self.k1n] = np.cos(ang)
        f1[self.kp:self.kp + self.k1n] = -np.sin(ang)
        self.f1 = f1
        w = np.full((self.k1n,), 2.0)
        w[0] = 1.0
        w[-1] = 1.0
        g1 = np.zeros((self.nq, 2 * self.kp))
        g1[:, :self.k1n] = (np.cos(ang) * w[:, None] / p).T
        g1[:, self.kp:self.kp + self.k1n] = (-np.sin(ang) * w[:, None] / p).T
        self.g1 = g1
        kk = np.arange(self.k1n)[:, None, None]
        k2 = np.arange(r)[None, :, None]
        n2 = np.arange(r)[None, None, :]
        ph = 2.0 * np.pi * (n2 * k2 / r + n2 * kk / p)
        tr, ti = np.cos(ph), -np.sin(ph)
        self.m2 = np.concatenate([np.concatenate([tr, -ti], axis=2),
                                  np.concatenate([ti, tr], axis=2)], axis=1)
        ur, ui = np.transpose(tr, (0, 2, 1)), -np.transpose(ti, (0, 2, 1))
        self.m2i = np.concatenate([np.concatenate([ur, -ui], axis=2),
                                   np.concatenate([ui, ur], axis=2)], axis=1)


def _fft_forward(plan, src_ref, f1_ref, m2_ref, zp, as_, emit):
    r, nq, kp = plan.r, plan.nq, plan.kp
    for n1 in range(nq):
        zp[n1 * plan.zpitch:n1 * plan.zpitch + r, :] = src_ref[n1 * r:(n1 + 1) * r, :]

    def stage1(n2, c):
        slab = zp[pl.ds(n2, nq, stride=plan.zpitch), :]
        a = _dot(f1_ref[...].astype(BF16), slab.astype(BF16))
        as_[pl.ds(pl.multiple_of(n2 * plan.apitch, SUBLANES), 2 * kp), :] = a
        return c

    lax.fori_loop(0, r, stage1, 0, unroll=plan.unroll)

    def stage2(k1, c):
        sr = as_[pl.ds(k1, r, stride=plan.apitch), :]
        si = as_[pl.ds(kp + k1, r, stride=plan.apitch), :]
        s = jnp.concatenate([sr, si], axis=0).astype(BF16)
        emit(k1, _dot(m2_ref[k1].astype(BF16), s))
        return c

    lax.fori_loop(0, plan.k1n, stage2, 0, unroll=plan.unroll2)


def _hy_spec_kernel(kf_ref, kb_ref, f1_ref, m2_ref, o_ref, zp, as_, *, plan):
    r = plan.r

    def emit_f(k1, x):
        o_ref[0, k1] = x

    def emit_b(k1, x):
        sign = jnp.where(lax.broadcasted_iota(jnp.int32, (2 * r, LANES), 0) < r, 1.0, -1.0)
        o_ref[0, k1] = o_ref[0, k1] + sign * x

    _fft_forward(plan, kf_ref, f1_ref, m2_ref, zp, as_, emit_f)
    _fft_forward(plan, kb_ref, f1_ref, m2_ref, zp, as_, emit_b)


def _hy_conv_kernel(z_ref, spec_ref, f1_ref, m2_ref, m2i_ref, g1_ref, o_ref, zp, as_, cs, *, plan):
    r, nq, kp, k1n = plan.r, plan.nq, plan.kp, plan.k1n
    if kp > k1n:
        cs[k1n * plan.cpitch:kp * plan.cpitch, :] = jnp.zeros(((kp - k1n) * plan.cpitch, LANES), F32)

    def emit(k1, x):
        kf = spec_ref[0, k1]
        xr, xi = x[:r], x[r:]
        kr, ki = kf[:r], kf[r:]
        prod = jnp.concatenate([xr * kr - xi * ki, xr * ki + xi * kr], axis=0).astype(BF16)
        c = _dot(m2i_ref[k1].astype(BF16), prod)
        cs[pl.ds(pl.multiple_of(k1 * plan.cpitch, SUBLANES), 2 * r), :] = c

    _fft_forward(plan, z_ref.at[0], f1_ref, m2_ref, zp, as_, emit)

    def stage3(n2, c):
        cr = cs[pl.ds(n2, kp, stride=plan.cpitch), :]
        ci = cs[pl.ds(r + n2, kp, stride=plan.cpitch), :]
        y = _dot(g1_ref[...].astype(BF16), jnp.concatenate([cr, ci], axis=0).astype(BF16))
        zp[pl.ds(n2, nq, stride=plan.zpitch), :] = y
        return c

    lax.fori_loop(0, r, stage3, 0, unroll=plan.unroll)
    for n1 in range(nq):
        o_ref[0, n1 * r:(n1 + 1) * r, :] = zp[n1 * plan.zpitch:n1 * plan.zpitch + r, :]


def _hy_filter_kernel(feat_ref, w1_ref, b1_ref, f1_ref, w2_ref, b2_ref, f2_ref, w3_ref, dl_ref, o_ref):
    feats = feat_ref[...]
    hid = jnp.sin(f1_ref[...] * (_dot(feats, w1_ref[...], HIGHEST) + b1_ref[...]))
    hid = jnp.sin(f2_ref[...] * (_dot(hid, w2_ref[...], HIGHEST) + b2_ref[...]))
    k = _dot(hid, w3_ref[...], HIGHEST)
    t = feats[:, 0:1]
    o_ref[...] = k * jnp.exp(-(t * dl_ref[...]))


def _hyena_features(seq):
    pos = np.arange(seq, dtype=np.float64)
    t = pos / max(seq - 1, 1)
    w = (2.0 * math.pi / seq) * pos
    bands = np.linspace(1e-4, HY_BANDS - 1, HY_BANDS, dtype=np.float64)
    ang = w[:, None] * bands
    feats = np.concatenate([t[:, None], np.cos(ang), -np.sin(ang)], axis=-1).astype(np.float32)
    out = np.zeros((seq, LANES), np.float32)
    out[:, :feats.shape[1]] = feats
    return out


def _hyena_deltas():
    max_decay = math.log(HY_DECAY_TARGET) / HY_FAST_DECAY
    min_decay = math.log(HY_DECAY_TARGET) / HY_SLOW_DECAY
    deltas = np.abs(np.linspace(min_decay, max_decay, HY_WIDTH, dtype=np.float64))
    return np.tile(deltas, 2)[None, :].astype(np.float32)


def _pad_to(x, rows, cols):
    x = x.astype(F32)
    return jnp.pad(x, ((0, 0), (0, rows - x.shape[1]), (0, cols - x.shape[2])))


def _hyena_filter_weights(p):
    hid = LANES
    vec = lambda a: _pad_to(a[:, None, :], 1, hid)
    return (_pad_to(p["hy_filt_w1"], LANES, hid), vec(p["hy_filt_b1"]), vec(p["hy_filt_freq1"]),
            _pad_to(p["hy_filt_w2"], hid, hid), vec(p["hy_filt_b2"]), vec(p["hy_filt_freq2"]),
            _pad_to(p["hy_filt_w3"], hid, 2 * HY_WIDTH))


def _hyena_filters(seq, fw, l, tf):
    feats = jnp.asarray(_hyena_features(seq))
    deltas = jnp.asarray(_hyena_deltas())
    return pl.pallas_call(
        _hy_filter_kernel,
        grid=(seq // tf,),
        in_specs=[pl.BlockSpec((tf, LANES), lambda i: (i, 0))] + [_layer_spec(a, l) for a in fw]
                 + [_full(deltas.shape)],
        out_specs=pl.BlockSpec((tf, 2 * HY_WIDTH), lambda i: (i, 0)),
        out_shape=jax.ShapeDtypeStruct((seq, 2 * HY_WIDTH), F32),
        compiler_params=_params(("parallel",)),
        name="hy_filter",
    )(feats, *fw, deltas)


def _hyena_conv(z3, filt, plan):
    nb, seq, c = z3.shape
    nh = c // LANES
    f1 = jnp.asarray(plan.f1, F32)
    m2 = jnp.asarray(plan.m2, F32)
    m2i = jnp.asarray(plan.m2i, F32)
    g1 = jnp.asarray(plan.g1, F32)
    r, kp, k1n = plan.r, plan.kp, plan.k1n
    zp_shape = (plan.nq * plan.zpitch, LANES)
    as_shape = (r * plan.apitch, LANES)
    cs_shape = (kp * plan.cpitch, LANES)
    spec = pl.pallas_call(
        functools.partial(_hy_spec_kernel, plan=plan),
        grid=(nh,),
        in_specs=[pl.BlockSpec((seq, LANES), lambda j: (0, j)),
                  pl.BlockSpec((seq, LANES), lambda j: (0, nh + j)),
                  _full(f1.shape), _full(m2.shape)],
        out_specs=pl.BlockSpec((1, k1n, 2 * r, LANES), lambda j: (j, 0, 0, 0)),
        out_shape=jax.ShapeDtypeStruct((nh, k1n, 2 * r, LANES), F32),
        scratch_shapes=[pltpu.VMEM(zp_shape, F32), pltpu.VMEM(as_shape, F32)],
        compiler_params=_params(("parallel",)),
        name="hy_spec",
    )(filt, filt, f1, m2)
    return pl.pallas_call(
        functools.partial(_hy_conv_kernel, plan=plan),
        grid=(nb, nh),
        in_specs=[pl.BlockSpec((1, seq, LANES), lambda b, j: (b, 0, j)),
                  pl.BlockSpec((1, k1n, 2 * r, LANES), lambda b, j: (j, 0, 0, 0)),
                  _full(f1.shape), _full(m2.shape), _full(m2i.shape), _full(g1.shape)],
        out_specs=pl.BlockSpec((1, seq, LANES), lambda b, j: (b, 0, j)),
        out_shape=jax.ShapeDtypeStruct((nb, seq, c), F32),
        scratch_shapes=[pltpu.VMEM(zp_shape, F32), pltpu.VMEM(as_shape, F32), pltpu.VMEM(cs_shape, F32)],
        compiler_params=_params(("parallel", "parallel")),
        name="hy_conv",
    )(z3, spec, f1, m2, m2i, g1)


def _mix_kernel(h_ref, hf_ref, hb_ref, ga_ref, yf_ref, yb_ref, ub_ref, yc_ref, z_ref, x0_ref,
                d_ref, gw_ref, gb_ref, hbias_ref, mg_ref, wo_ref, o_ref):
    ya = (hf_ref[...] + hb_ref[...]) * _gelu(ga_ref[...])
    yb = _gelu(ub_ref[...] * d_ref[...] + yf_ref[...] + yb_ref[...])
    yb = yb * _sigmoid(_dot(yb.astype(BF16), gw_ref[...]) + gb_ref[...])
    z = z_ref[...]
    yc = (yc_ref[...] + z * hbias_ref[...]) * x0_ref[...]
    a1, a2 = RG_WIDTH, RG_WIDTH + S5_WIDTH
    na = (_rms_nogain(ya) * mg_ref[:, :a1]).astype(BF16)
    nb = (_rms_nogain(yb) * mg_ref[:, a1:a2]).astype(BF16)
    nc = (_rms_nogain(yc) * mg_ref[:, a2:]).astype(BF16)
    out = _dot(na, wo_ref[:a1, :]) + _dot(nb, wo_ref[a1:a2, :]) + _dot(nc, wo_ref[a2:, :])
    o_ref[...] = h_ref[...] + out


def _mix(rows, consts, l, tm):
    n, dm = rows[0].shape
    return pl.pallas_call(
        _mix_kernel,
        grid=(n // tm,),
        in_specs=[pl.BlockSpec((tm, a.shape[1]), lambda i: (i, 0)) for a in rows]
                 + [_layer_spec(a, l) for a in consts],
        out_specs=pl.BlockSpec((tm, dm), lambda i: (i, 0)),
        out_shape=jax.ShapeDtypeStruct((n, dm), F32),
        compiler_params=_params(("parallel",)),
        name="mix",
    )(*rows, *consts)


FFN_TILE = 256


def _ffn_kernel(h_ref, hp_ref, hn_ref, g_ref, wu_ref, cw_ref, cb_ref, wd_ref, fg_ref, o_ref,
                x_scr, u_scr, gated_scr, y_scr, *, tiles_per_seq, tm, final):
    i = pl.program_id(0)
    first = (i % tiles_per_seq) == 0
    last = (i % tiles_per_seq) == tiles_per_seq - 1
    nslab = h_ref.shape[1] // LANES
    ph = tm // SUBLANES
    g = g_ref[...]

    def norm(v):
        return v * lax.rsqrt(jnp.mean(v * v, axis=-1, keepdims=True) + RMS_EPS) * g

    xn = norm(h_ref[...])
    xp = norm(jnp.where(first, 0.0, hp_ref[...]))
    xq = norm(jnp.where(last, 0.0, hn_ref[...]))
    for c in range(nslab):
        lanes = slice(c * LANES, (c + 1) * LANES)
        for s in range(SUBLANES):
            x_scr[c, pl.ds(s, ph, stride=SUBLANES), :] = xn[s * ph:(s + 1) * ph, lanes]
        x_scr[c, tm:tm + HALO, :] = xp[:, lanes]
        x_scr[c, tm + HALO:tm + 2 * HALO, :] = xq[:, lanes]
    x = jnp.concatenate([x_scr[c] for c in range(nslab)], axis=1).astype(BF16)
    sub = lax.broadcasted_iota(jnp.int32, (SUBLANES, FFN_TILE), 0)
    assert cw_ref.shape[0] == 3
    for k in range(D_FF // FFN_TILE):
        halves = []
        for part in range(2):
            lo = part * D_FF + k * FFN_TILE
            slot = 2 * k + part
            u_scr[slot] = _dot(x, wu_ref[:, lo:lo + FFN_TILE])
            head = jnp.where(sub == 0, pltpu.roll(u_scr[slot, tm:tm + HALO, :], 1, 0),
                             pltpu.roll(u_scr[slot, tm - SUBLANES:tm, :], 1, 0))
            tail = jnp.where(sub == SUBLANES - 1,
                             pltpu.roll(u_scr[slot, tm + HALO:tm + 2 * HALO, :], SUBLANES - 1, 0),
                             pltpu.roll(u_scr[slot, 0:SUBLANES, :], SUBLANES - 1, 0))
            prv = jnp.concatenate([head, u_scr[slot, 0:tm - SUBLANES, :]], axis=0)
            nxt = jnp.concatenate([u_scr[slot, SUBLANES:tm, :], tail], axis=0)
            cols = slice(lo, lo + FFN_TILE)
            halves.append(cb_ref[:, cols] + prv * cw_ref[0:1, cols] + u_scr[slot, 0:tm, :] * cw_ref[1:2, cols]
                          + nxt * cw_ref[2:3, cols])
        gated_scr[:, k * FFN_TILE:(k + 1) * FFN_TILE] = (_gelu(halves[0]) * halves[1]).astype(BF16)
    y = _dot(gated_scr[...], wd_ref[...])
    for c in range(nslab):
        y_scr[c] = y[:, c * LANES:(c + 1) * LANES]
    for s in range(SUBLANES):
        for c in range(nslab):
            rows = slice(s * ph, (s + 1) * ph)
            lanes = slice(c * LANES, (c + 1) * LANES)
            o_ref[rows, lanes] = h_ref[rows, lanes] + y_scr[c, pl.ds(s, ph, stride=SUBLANES), :]
    if final:
        out = o_ref[...]
        o_ref[...] = out * lax.rsqrt(jnp.mean(out * out, axis=-1, keepdims=True) + RMS_EPS) * fg_ref[...]


def _resident(arr, *lead):
    rest = arr.shape[len(lead):]
    zeros = (0,) * len(rest)
    return pl.BlockSpec((None,) * len(lead) + rest, lambda *_: tuple(lead) + zeros,
                        pipeline_mode=pl.Buffered(1))


def _ffn(h, g, w_up, conv_w, conv_b, w_down, final_g, l, seq, tm, final):
    n, d = h.shape
    main, prev, nxt = _halo_specs(tm, n)
    return pl.pallas_call(
        functools.partial(_ffn_kernel, tiles_per_seq=seq // tm, tm=tm, final=final),
        grid=(n // tm,),
        in_specs=[pl.BlockSpec((tm, d), main), pl.BlockSpec((HALO, d), prev), pl.BlockSpec((HALO, d), nxt),
                  _layer_spec(g, l), _resident(w_up, l), _layer_spec(conv_w, l), _layer_spec(conv_b, l),
                  _resident(w_down, l), _full(final_g.shape)],
        out_specs=pl.BlockSpec((tm, d), lambda i: (i, 0)),
        out_shape=jax.ShapeDtypeStruct((n, d), F32),
        scratch_shapes=[pltpu.VMEM((d // LANES, tm + 2 * HALO, LANES), F32),
                        pltpu.VMEM((2 * (D_FF // FFN_TILE), tm + 2 * HALO, FFN_TILE), F32),
                        pltpu.VMEM((tm, D_FF), BF16),
                        pltpu.VMEM((d // LANES, tm, LANES), F32)],
        compiler_params=_params(("parallel",)),
        name="ffn_final" if final else "ffn",
    )(h, h, h, g, w_up, conv_w, conv_b, w_down, final_g)


def _row_vec(a):
    return a.astype(F32)[:, None, :]


def _trunk(x, p, *, tm, t_scan, t_s5, tf):
    nb, seq, d = x.shape
    n = nb * seq
    depth = p["w_in"].shape[0]
    plan = _FftPlan(seq)

    proj_c = (_row_vec(p["norm1_g"]), p["w_in"].astype(BF16), p["rg_conv_w"].astype(F32),
              _row_vec(p["rg_conv_b"]), p["hy_conv_w"].astype(F32), _row_vec(p["hy_conv_b"]))
    rg_w, rg_bias, rg_sp = _rglru_weights(p)
    s5_w = _s5_weights(p)
    hy_fw = _hyena_filter_weights(p)
    mix_c = (_row_vec(p["s5_d"]), p["s5_glu_w"].astype(BF16), _row_vec(p["s5_glu_b"]),
             _row_vec(p["hy_bias"]), _row_vec(p["mix_norm_g"]), p["w_out"].astype(BF16))
    ffn_c = (_row_vec(p["norm2_g"]), p["w_up"].astype(BF16), p["ffn_conv_w"].astype(F32),
             _row_vec(p["ffn_conv_b"]), p["w_down"].astype(BF16), p["final_norm_g"].astype(F32)[None])

    h = x.astype(F32).reshape(n, d)
    for l in range(depth):
        urg, ga, ub, x0, z = _proj(h, *proj_c, l, seq, tm)
        hs = [a.reshape(n, RG_WIDTH)
              for a in _rglru(urg.reshape(nb, seq, RG_WIDTH), rg_w, rg_bias, rg_sp, l, t_scan)]
        yf, yb = _s5(ub.reshape(nb, seq, S5_WIDTH), *s5_w, l, t_s5)
        filt = _hyena_filters(seq, hy_fw, l, tf)
        yc = _hyena_conv(z.reshape(nb, seq, HY_WIDTH), filt, plan).reshape(n, HY_WIDTH)
        rows = [h, hs[0], hs[1], ga, yf.reshape(n, S5_WIDTH), yb.reshape(n, S5_WIDTH), ub, yc, z, x0]
        h = _mix(rows, mix_c, l, tm)
        h = _ffn(h, *ffn_c, l, seq, tm, final=(l == depth - 1))
    return h.reshape(nb, seq, d).astype(x.dtype)


def kernel(x, norm1_g, w_in, rg_conv_w, rg_conv_b, rg_wa, rg_ba, rg_wx, rg_bx, rg_lambda, s5_a_re, s5_a_im, s5_log_dt, s5_b_re, s5_b_im, s5_c_re, s5_c_im, s5_d, s5_glu_w, s5_glu_b, hy_conv_w, hy_conv_b, hy_filt_w1, hy_filt_b1, hy_filt_freq1, hy_filt_w2, hy_filt_b2, hy_filt_freq2, hy_filt_w3, hy_bias, mix_norm_g, w_out, norm2_g, w_up, ffn_conv_w, ffn_conv_b, w_down, final_norm_g):
    p = dict(norm1_g=norm1_g, w_in=w_in, rg_conv_w=rg_conv_w, rg_conv_b=rg_conv_b, rg_wa=rg_wa, rg_ba=rg_ba,
             rg_wx=rg_wx, rg_bx=rg_bx, rg_lambda=rg_lambda, s5_a_re=s5_a_re, s5_a_im=s5_a_im,
             s5_log_dt=s5_log_dt, s5_b_re=s5_b_re, s5_b_im=s5_b_im, s5_c_re=s5_c_re, s5_c_im=s5_c_im,
             s5_d=s5_d, s5_glu_w=s5_glu_w, s5_glu_b=s5_glu_b, hy_conv_w=hy_conv_w, hy_conv_b=hy_conv_b,
             hy_filt_w1=hy_filt_w1, hy_filt_b1=hy_filt_b1, hy_filt_freq1=hy_filt_freq1, hy_filt_w2=hy_filt_w2,
             hy_filt_b2=hy_filt_b2, hy_filt_freq2=hy_filt_freq2, hy_filt_w3=hy_filt_w3, hy_bias=hy_bias,
             mix_norm_g=mix_norm_g, w_out=w_out, norm2_g=norm2_g, w_up=w_up, ffn_conv_w=ffn_conv_w,
             ffn_conv_b=ffn_conv_b, w_down=w_down, final_norm_g=final_norm_g)
    return _trunk(x, p, tm=512, t_scan=256, t_s5=128, tf=512)
```

```python
import functools
import math

import numpy as np
import jax
import jax.numpy as jnp
from jax import lax
from jax.experimental import pallas as pl
from jax.experimental.pallas import tpu as pltpu

F32 = jnp.float32
BF16 = jnp.bfloat16
STREAM = jnp.bfloat16

RMS_EPS = 1e-6
RG_WIDTH = 384
RG_HEADS = 6
RG_C = 8.0
S5_WIDTH = 384
S5_GROUP = 16
S5_GROUPS = 24
S5_STATE = 64
S5_NSTATE = S5_GROUPS * S5_STATE
HY_WIDTH = 256
HY_BANDS = 16
HY_FAST_DECAY = 0.3
HY_SLOW_DECAY = 1.5
HY_DECAY_TARGET = 1e-2
D_FF = 2816
N_DIR = 2

LANES = 128
SUBLANES = 8
HALO = SUBLANES
VMEM_LIMIT = 56 * 1024 * 1024

HIGHEST = lax.Precision.HIGHEST


def _dot(a, b, precision=None):
    return jnp.dot(a, b, preferred_element_type=F32, precision=precision)


def _gelu(x):
    c = math.sqrt(2.0 / math.pi)
    return 0.5 * x * (1.0 + jnp.tanh(c * (x + 0.044715 * (x * x * x))))


def _sigmoid(x):
    return 1.0 / (1.0 + jnp.exp(-x))


def _rms_nogain(x):
    return x * lax.rsqrt(jnp.mean(x * x, axis=-1, keepdims=True) + RMS_EPS)


def _params(sem):
    return pltpu.CompilerParams(dimension_semantics=sem, vmem_limit_bytes=VMEM_LIMIT)


def _full(shape):
    nd = len(shape)
    return pl.BlockSpec(shape, lambda *_: (0,) * nd)


def _layer_spec(arr, *lead):
    rest = arr.shape[len(lead):]
    zeros = (0,) * len(rest)
    return pl.BlockSpec((None,) * len(lead) + rest, lambda *_: tuple(lead) + zeros)


def _halo_specs(tm, n):
    blocks_per_tile = tm // HALO
    nblocks = n // HALO

    def main(i, *_):
        return (i, 0)

    def prev(i, *_):
        return (jnp.maximum(i * blocks_per_tile - 1, 0), 0)

    def nxt(i, *_):
        return (jnp.minimum((i + 1) * blocks_per_tile, nblocks - 1), 0)

    return main, prev, nxt


def _fill_normed(x_scr, h_ref, hp_ref, hn_ref, g, first, last, tm):
    def norm(x):
        return x * lax.rsqrt(jnp.mean(x * x, axis=-1, keepdims=True) + RMS_EPS) * g

    x_scr[0:HALO, :] = norm(jnp.where(first, 0.0, hp_ref[...]))
    x_scr[HALO:HALO + tm, :] = norm(h_ref[...])
    x_scr[HALO + tm:2 * HALO + tm, :] = norm(jnp.where(last, 0.0, hn_ref[...]))


def _proj_kernel(h_ref, hp_ref, hn_ref, g_ref, w_ref, rcw_ref, rcb_ref, hcw_ref, hcb_ref,
                 urg_ref, ga_ref, ub_ref, x0_ref, z_ref, x_scr, u_scr, *, tiles_per_seq, tm):
    i = pl.program_id(0)
    first = (i % tiles_per_seq) == 0
    last = (i % tiles_per_seq) == tiles_per_seq - 1
    _fill_normed(x_scr, h_ref, hp_ref, hn_ref, g_ref[...], first, last, tm)
    u_scr[...] = _dot(x_scr[...].astype(BF16), w_ref[...])
    a0, a1, a2, a3 = 0, RG_WIDTH, 2 * RG_WIDTH, 2 * RG_WIDTH + S5_WIDTH
    acc = rcb_ref[...]
    for j in range(rcw_ref.shape[0]):
        acc = acc + u_scr[pl.ds(HALO + j - 1, tm), a0:a1] * rcw_ref[j:j + 1, :]
    urg_ref[...] = acc
    ga_ref[...] = u_scr[pl.ds(HALO, tm), a1:a2].astype(ga_ref.dtype)
    ub_ref[...] = u_scr[pl.ds(HALO, tm), a2:a3].astype(ub_ref.dtype)
    q = []
    for part in range(3):
        lo = part * HY_WIDTH
        acc = hcb_ref[:, lo:lo + HY_WIDTH]
        for j in range(hcw_ref.shape[0]):
            acc = acc + (u_scr[pl.ds(HALO + j - 1, tm), a3 + lo:a3 + lo + HY_WIDTH]
                         * hcw_ref[j:j + 1, lo:lo + HY_WIDTH])
        q.append(acc)
    x0_ref[...] = q[0].astype(x0_ref.dtype)
    z_ref[...] = (q[2] * q[1]).astype(z_ref.dtype)


def _proj(h, g, w_in, rcw, rcb, hcw, hcb, l, seq, tm):
    n, d = h.shape
    cols = w_in.shape[-1]
    main, prev, nxt = _halo_specs(tm, n)
    row = lambda c: pl.BlockSpec((tm, c), lambda i: (i, 0))
    outs = [jax.ShapeDtypeStruct((n, c), dt) for c, dt in
            ((RG_WIDTH, F32), (RG_WIDTH, STREAM), (S5_WIDTH, STREAM), (HY_WIDTH, STREAM), (HY_WIDTH, STREAM))]
    consts = [g, w_in, rcw, rcb, hcw, hcb]
    return pl.pallas_call(
        functools.partial(_proj_kernel, tiles_per_seq=seq // tm, tm=tm),
        grid=(n // tm,),
        in_specs=[pl.BlockSpec((tm, d), main), pl.BlockSpec((HALO, d), prev), pl.BlockSpec((HALO, d), nxt)]
                 + [_layer_spec(a, l) for a in consts],
        out_specs=[row(RG_WIDTH), row(RG_WIDTH), row(S5_WIDTH), row(HY_WIDTH), row(HY_WIDTH)],
        out_shape=outs,
        scratch_shapes=[pltpu.VMEM((tm + 2 * HALO, d), F32), pltpu.VMEM((tm + 2 * HALO, cols), F32)],
        compiler_params=_params(("parallel",)),
        name="proj",
    )(h, h, h, *consts)


def _lo_mask():
    return lax.broadcasted_iota(jnp.int32, (SUBLANES, LANES), 0) < (SUBLANES // 2)


def _rglru_kernel(uf_ref, ub_ref, w_ref, bias_ref, sp_ref, of_ref, ob_ref,
                  af, ab, bf, bb, carry_scr, *, t, nb):
    i = pl.program_id(0)
    nm = RG_WIDTH // LANES

    @pl.when(i == 0)
    def _():
        carry_scr[...] = jnp.zeros_like(carry_scr)
        for scr in (af, ab, bf, bb):
            scr[...] = jnp.zeros_like(scr)

    for dr, (u_ref, a_scr, b_scr) in enumerate(((uf_ref, af, bf), (ub_ref, ab, bb))):
        u = u_ref[...].reshape(nb * t, RG_WIDTH)
        gates = _dot(u.astype(BF16), w_ref[dr]) + bias_ref[dr]
        r = _sigmoid(gates[:, :RG_WIDTH])
        gi = _sigmoid(gates[:, RG_WIDTH:])
        log_a = (-RG_C) * r * sp_ref[dr]
        a = jnp.exp(log_a)
        mult = jnp.sqrt(-jnp.tanh(log_a) * (a * a + 1.0))
        bin_ = mult * (gi * u)
        for b in range(nb):
            for m in range(nm):
                rows = pl.ds(dr * nb + b, t, stride=SUBLANES)
                a_scr[m, rows, :] = a[b * t:(b + 1) * t, m * LANES:(m + 1) * LANES]
                b_scr[m, rows, :] = bin_[b * t:(b + 1) * t, m * LANES:(m + 1) * LANES]

    lo = _lo_mask()

    def body(j, carry):
        row = pl.multiple_of(j * SUBLANES, SUBLANES)
        mrow = pl.multiple_of((t - 1 - j) * SUBLANES, SUBLANES)
        new = []
        for m in range(nm):
            at = jnp.where(lo, af[m, pl.ds(row, SUBLANES), :], ab[m, pl.ds(mrow, SUBLANES), :])
            bt = jnp.where(lo, bf[m, pl.ds(row, SUBLANES), :], bb[m, pl.ds(mrow, SUBLANES), :])
            h = at * carry[m] + bt
            bf[m, pl.ds(row, SUBLANES), :] = h
            bb[m, pl.ds(mrow, SUBLANES), :] = h
            new.append(h)
        return tuple(new)

    carry = lax.fori_loop(0, t, body, tuple(carry_scr[m] for m in range(nm)), unroll=4)
    for m in range(nm):
        carry_scr[m] = carry[m]
    for b in range(nb):
        for m in range(nm):
            of_ref[b, :, m * LANES:(m + 1) * LANES] = bf[m, pl.ds(b, t, stride=SUBLANES), :].astype(STREAM)
            ob_ref[b, :, m * LANES:(m + 1) * LANES] = bb[m, pl.ds(nb + b, t, stride=SUBLANES), :].astype(STREAM)


def _rglru(u3, w, bias, sp, l, t):
    nb, seq, c = u3.shape
    assert 2 * nb == SUBLANES
    nchunk = seq // t
    fwd = lambda i: (0, i, 0)
    bwd = lambda i: (0, nchunk - 1 - i, 0)
    nm = c // LANES
    out = jax.ShapeDtypeStruct((nb, seq, c), STREAM)
    scr = pltpu.VMEM((nm, SUBLANES * t, LANES), F32)
    return pl.pallas_call(
        functools.partial(_rglru_kernel, t=t, nb=nb),
        grid=(nchunk,),
        in_specs=[pl.BlockSpec((nb, t, c), fwd), pl.BlockSpec((nb, t, c), bwd),
                  _layer_spec(w, l), _layer_spec(bias, l), _layer_spec(sp, l)],
        out_specs=[pl.BlockSpec((nb, t, c), fwd), pl.BlockSpec((nb, t, c), bwd)],
        out_shape=[out, out],
        scratch_shapes=[scr, scr, scr, scr, pltpu.VMEM((nm, SUBLANES, LANES), F32)],
        compiler_params=_params(("arbitrary",)),
        name="rglru",
    )(u3, u3, w, bias, sp)


def _rglru_weights(p):
    eye = jnp.eye(RG_HEADS, dtype=F32)
    both = jnp.stack([p["rg_wa"], p["rg_wx"]], axis=2).astype(F32)
    w = jnp.einsum('ldqhij,hk->ldhiqkj', both, eye)
    depth = w.shape[0]
    w = w.reshape(depth, N_DIR, RG_WIDTH, 2 * RG_WIDTH).astype(BF16)
    bias = jnp.concatenate([p["rg_ba"], p["rg_bx"]], axis=-1).astype(F32)[:, :, None, :]
    x = -p["rg_lambda"].astype(F32)
    sp = (jnp.maximum(x, 0.0) + jnp.log1p(jnp.exp(-jnp.abs(x))))[:, :, None, :]
    return w, bias, sp


S5_NBLK = S5_NSTATE // LANES
S5_GRP = 4
S5_PER = S5_NBLK // (S5_WIDTH // LANES)


def _reverse_tiles(src, dst, nm, ntile):
    for j in range(ntile):
        s = (ntile - 1 - j) * SUBLANES
        for m in range(nm):
            dst[m, j * SUBLANES:(j + 1) * SUBLANES, :] = src[m, s:s + SUBLANES, :]


def _s5_kernel(uf_ref, ub_ref, wb_ref, wc_ref, ar_ref, ai_ref, yf_ref, yb_ref,
               uf8, ub8, ubr, bu, ym, ymr, carry_scr, *, t, nb):
    i = pl.program_id(0)
    nm = S5_WIDTH // LANES

    @pl.when(i == 0)
    def _():
        carry_scr[...] = jnp.zeros_like(carry_scr)
        uf8[...] = jnp.zeros_like(uf8)
        ub8[...] = jnp.zeros_like(ub8)

    for b in range(nb):
        for m in range(nm):
            uf8[m, pl.ds(b, t, stride=SUBLANES), :] = uf_ref[b, :, m * LANES:(m + 1) * LANES].astype(F32)
            ub8[m, pl.ds(nb + b, t, stride=SUBLANES), :] = ub_ref[b, :, m * LANES:(m + 1) * LANES].astype(F32)
    _reverse_tiles(ub8, ubr, nm, t)

    for m in range(nm):
        lhs = jnp.concatenate([uf8[m], ubr[m]], axis=1).astype(BF16)
        res = _dot(lhs, wb_ref[m])
        for q in range(2 * S5_PER):
            bu[2 * S5_PER * m + q] = res[:, q * LANES:(q + 1) * LANES]

    for g in range(S5_NBLK // S5_GRP):
        blocks = list(range(g * S5_GRP, (g + 1) * S5_GRP))
        ars = [ar_ref[n] for n in blocks]
        ais = [ai_ref[n] for n in blocks]

        def body(j, carry, blocks=blocks, ars=ars, ais=ais):
            row = j * SUBLANES
            new = []
            for k, n in enumerate(blocks):
                hr, hi = carry[2 * k], carry[2 * k + 1]
                nr = ars[k] * hr - ais[k] * hi + bu[2 * n, pl.ds(row, SUBLANES), :]
                ni = ars[k] * hi + ais[k] * hr + bu[2 * n + 1, pl.ds(row, SUBLANES), :]
                bu[2 * n, pl.ds(row, SUBLANES), :] = nr
                bu[2 * n + 1, pl.ds(row, SUBLANES), :] = ni
                new += [nr, ni]
            return tuple(new)

        init = []
        for n in blocks:
            init += [carry_scr[2 * n], carry_scr[2 * n + 1]]
        carry = tuple(init)
        for j in range(t):
            carry = body(j, carry)
        for k, n in enumerate(blocks):
            carry_scr[2 * n] = carry[2 * k]
            carry_scr[2 * n + 1] = carry[2 * k + 1]

    rows = lax.broadcasted_iota(jnp.int32, (SUBLANES * t, LANES), 0)
    fwd_row = (rows & (SUBLANES - 1)) < nb
    for m in range(nm):
        hcat = jnp.concatenate([bu[2 * S5_PER * m + q] for q in range(2 * S5_PER)], axis=1).astype(BF16)
        acc = _dot(hcat, wc_ref[m])
        ym[m] = jnp.where(fwd_row, acc[:, :LANES], acc[:, LANES:])
    _reverse_tiles(ym, ymr, nm, t)
    for b in range(nb):
        for m in range(nm):
            yf_ref[b, :, m * LANES:(m + 1) * LANES] = ym[m, pl.ds(b, t, stride=SUBLANES), :].astype(STREAM)
            yb_ref[b, :, m * LANES:(m + 1) * LANES] = ymr[m, pl.ds(nb + b, t, stride=SUBLANES), :].astype(STREAM)


def _s5(u3, wb, wc, ar, ai, l, t):
    nb, seq, c = u3.shape
    assert 2 * nb == SUBLANES
    nchunk = seq // t
    fwd = lambda i: (0, i, 0)
    bwd = lambda i: (0, nchunk - 1 - i, 0)
    nm = c // LANES
    rows = SUBLANES * t
    out = jax.ShapeDtypeStruct((nb, seq, c), STREAM)
    return pl.pallas_call(
        functools.partial(_s5_kernel, t=t, nb=nb),
        grid=(nchunk,),
        in_specs=[pl.BlockSpec((nb, t, c), fwd), pl.BlockSpec((nb, t, c), bwd),
                  _layer_spec(wb, l), _layer_spec(wc, l), _layer_spec(ar, l), _layer_spec(ai, l)],
        out_specs=[pl.BlockSpec((nb, t, c), fwd), pl.BlockSpec((nb, t, c), bwd)],
        out_shape=[out, out],
        scratch_shapes=[pltpu.VMEM((nm, rows, LANES), F32), pltpu.VMEM((nm, rows, LANES), F32),
                        pltpu.VMEM((nm, rows, LANES), F32),
                        pltpu.VMEM((2 * S5_NBLK, rows, LANES), F32),
                        pltpu.VMEM((nm, rows, LANES), F32), pltpu.VMEM((nm, rows, LANES), F32),
                        pltpu.VMEM((2 * S5_NBLK, SUBLANES, LANES), F32)],
        compiler_params=_params(("arbitrary",)),
        name="s5",
    )(u3, u3, wb, wc, ar, ai)


def _s5_group_mask():
    mask = np.zeros((S5_NBLK, LANES // S5_GROUP, LANES // S5_STATE), np.float32)
    for n in range(S5_NBLK):
        for s in range(LANES // S5_STATE):
            mask[n, (LANES // S5_STATE) * (n % S5_PER) + s, s] = 1.0
    return mask


def _s5_weights(p):
    lr = p["s5_a_re"].astype(F32)
    li = p["s5_a_im"].astype(F32)
    dt = jnp.exp(p["s5_log_dt"].astype(F32))[..., None]
    mag = jnp.exp(lr * dt)
    abar_r = mag * jnp.cos(li * dt)
    abar_i = mag * jnp.sin(li * dt)
    den = lr * lr + li * li
    nr = abar_r - 1.0
    ni = abar_i
    coef_r = ((nr * lr + ni * li) / den)[..., None]
    coef_i = ((ni * lr - nr * li) / den)[..., None]
    b_re = p["s5_b_re"].astype(F32)
    b_im = p["s5_b_im"].astype(F32)
    bbar = jnp.stack([coef_r * b_re - coef_i * b_im, coef_r * b_im + coef_i * b_re], axis=2)
    depth = lr.shape[0]
    half = LANES // S5_STATE
    mask = jnp.asarray(_s5_group_mask())
    bb = bbar.reshape(depth, N_DIR, 2, S5_NBLK, half, S5_STATE, S5_GROUP)
    nm = S5_WIDTH // LANES
    wb = jnp.einsum('ldrnspc,nks->lndkcrsp', bb, mask).reshape(depth, nm, S5_PER, 2 * LANES, 2 * LANES)
    wb = wb.transpose(0, 1, 3, 2, 4).reshape(depth, nm, 2 * LANES, S5_PER * 2 * LANES)
    cc = jnp.stack([p["s5_c_re"].astype(F32), -p["s5_c_im"].astype(F32)], axis=2)
    cc = cc.reshape(depth, N_DIR, 2, S5_NBLK, half, S5_GROUP, S5_STATE)
    wc = jnp.einsum('ldrnscp,nks->lnrspdkc', cc, mask).reshape(depth, nm, S5_PER * 2 * LANES, 2 * LANES)

    def tile_rows(a):
        a = a.reshape(depth, N_DIR, S5_NBLK, LANES).transpose(0, 2, 1, 3)
        return jnp.repeat(a, SUBLANES // N_DIR, axis=2)

    return wb.astype(BF16), wc.astype(BF16), tile_rows(abar_r), tile_rows(abar_i)


def _odd8(n):
    p = -(-n // SUBLANES)
    if p % 2 == 0:
        p += 1
    return p * SUBLANES


class _FftPlan:
    def __init__(self, seq):
        self.seq = seq
        r = 1
        while r * r < seq:
            r *= 2
        self.r = r
        self.nq = seq // r
        self.q = 2 * seq // r
        self.k1n = self.q // 2 + 1
        self.kp = -(-self.k1n // SUBLANES) * SUBLANES
        self.zpitch = _odd8(r)
        self.apitch = _odd8(2 * self.kp)
        self.cpitch = _odd8(2 * r)
        self.unroll = min(8, r)
        self.unroll2 = next(u for u in (13, 5, 4, 3, 2, 1) if self.k1n % u == 0)
        p = 2 * seq
        n1 = np.arange(self.nq)[None, :]
        k1 = np.arange(self.k1n)[:, None]
        ang = 2.0 * np.pi * n1 * k1 / self.q
        f1 = np.zeros((2 * self.kp, self.nq))
        f1[:self.k1n] = np.cos(ang)
        f1[self.kp:self.kp + self.k1n] = -np.sin(ang)
        self.f1 = f1
        w = np.full((self.k1n,), 2.0)
        w[0] = 1.0
        w[-1] = 1.0
        g1 = np.zeros((self.nq, 2 * self.kp))
        g1[:, :self.k1n] = (np.cos(ang) * w[:, None] / p).T
        g1[:, self.kp:self.kp + self.k1n] = (-np.sin(ang) * w[:, None] / p).T
        self.g1 = g1
        kk = np.arange(self.k1n)[:, None, None]
        k2 = np.arange(r)[None, :, None]
        n2 = np.arange(r)[None, None, :]
        ph = 2.0 * np.pi * (n2 * k2 / r + n2 * kk / p)
        tr, ti = np.cos(ph), -np.sin(ph)
        self.m2 = np.concatenate([np.concatenate([tr, -ti], axis=2),
                                  np.concatenate([ti, tr], axis=2)], axis=1)
        ur, ui = np.transpose(tr, (0, 2, 1)), -np.transpose(ti, (0, 2, 1))
        self.m2i = np.concatenate([np.concatenate([ur, -ui], axis=2),
                                   np.concatenate([ui, ur], axis=2)], axis=1)


def _fft_forward(plan, src_ref, f1_ref, m2_ref, zp, as_, emit):
    r, nq, kp = plan.r, plan.nq, plan.kp
    for n1 in range(nq):
        zp[n1 * plan.zpitch:n1 * plan.zpitch + r, :] = src_ref[n1 * r:(n1 + 1) * r, :].astype(F32)

    def stage1(n2, c):
        slab = zp[pl.ds(n2, nq, stride=plan.zpitch), :]
        a = _dot(f1_ref[...].astype(BF16), slab.astype(BF16))
        as_[pl.ds(pl.multiple_of(n2 * plan.apitch, SUBLANES), 2 * kp), :] = a
        return c

    lax.fori_loop(0, r, stage1, 0, unroll=plan.unroll)

    def stage2(k1, c):
        sr = as_[pl.ds(k1, r, stride=plan.apitch), :]
        si = as_[pl.ds(kp + k1, r, stride=plan.apitch), :]
        s = jnp.concatenate([sr, si], axis=0).astype(BF16)
        emit(k1, _dot(m2_ref[k1].astype(BF16), s))
        return c

    lax.fori_loop(0, plan.k1n, stage2, 0, unroll=plan.unroll2)


def _hy_spec_kernel(kf_ref, kb_ref, f1_ref, m2_ref, o_ref, zp, as_, *, plan):
    r = plan.r

    def emit_f(k1, x):
        o_ref[0, k1] = x

    def emit_b(k1, x):
        sign = jnp.where(lax.broadcasted_iota(jnp.int32, (2 * r, LANES), 0) < r, 1.0, -1.0)
        o_ref[0, k1] = o_ref[0, k1] + sign * x

    _fft_forward(plan, kf_ref, f1_ref, m2_ref, zp, as_, emit_f)
    _fft_forward(plan, kb_ref, f1_ref, m2_ref, zp, as_, emit_b)


def _hy_conv_kernel(z_ref, spec_ref, f1_ref, m2_ref, m2i_ref, g1_ref, o_ref, zp, as_, cs, *, plan):
    r, nq, kp, k1n = plan.r, plan.nq, plan.kp, plan.k1n
    if kp > k1n:
        cs[k1n * plan.cpitch:kp * plan.cpitch, :] = jnp.zeros(((kp - k1n) * plan.cpitch, LANES), F32)

    def emit(k1, x):
        kf = spec_ref[0, k1]
        xr, xi = x[:r], x[r:]
        kr, ki = kf[:r], kf[r:]
        prod = jnp.concatenate([xr * kr - xi * ki, xr * ki + xi * kr], axis=0).astype(BF16)
        c = _dot(m2i_ref[k1].astype(BF16), prod)
        cs[pl.ds(pl.multiple_of(k1 * plan.cpitch, SUBLANES), 2 * r), :] = c

    _fft_forward(plan, z_ref.at[0], f1_ref, m2_ref, zp, as_, emit)

    def stage3(n2, c):
        cr = cs[pl.ds(n2, kp, stride=plan.cpitch), :]
        ci = cs[pl.ds(r + n2, kp, stride=plan.cpitch), :]
        y = _dot(g1_ref[...].astype(BF16), jnp.concatenate([cr, ci], axis=0).astype(BF16))
        zp[pl.ds(n2, nq, stride=plan.zpitch), :] = y
        return c

    lax.fori_loop(0, r, stage3, 0, unroll=plan.unroll)
    for n1 in range(nq):
        o_ref[0, n1 * r:(n1 + 1) * r, :] = zp[n1 * plan.zpitch:n1 * plan.zpitch + r, :].astype(o_ref.dtype)


def _hy_filter_kernel(feat_ref, w1_ref, b1_ref, f1_ref, w2_ref, b2_ref, f2_ref, w3_ref, dl_ref, o_ref):
    feats = feat_ref[...]
    hid = jnp.sin(f1_ref[...] * (_dot(feats, w1_ref[...], HIGHEST) + b1_ref[...]))
    hid = jnp.sin(f2_ref[...] * (_dot(hid, w2_ref[...], HIGHEST) + b2_ref[...]))
    k = _dot(hid, w3_ref[...], HIGHEST)
    t = feats[:, 0:1]
    o_ref[...] = k * jnp.exp(-(t * dl_ref[...]))


def _hyena_features(seq):
    pos = np.arange(seq, dtype=np.float64)
    t = pos / max(seq - 1, 1)
    w = (2.0 * math.pi / seq) * pos
    bands = np.linspace(1e-4, HY_BANDS - 1, HY_BANDS, dtype=np.float64)
    ang = w[:, None] * bands
    feats = np.concatenate([t[:, None], np.cos(ang), -np.sin(ang)], axis=-1).astype(np.float32)
    out = np.zeros((seq, LANES), np.float32)
    out[:, :feats.shape[1]] = feats
    return out


def _hyena_deltas():
    max_decay = math.log(HY_DECAY_TARGET) / HY_FAST_DECAY
    min_decay = math.log(HY_DECAY_TARGET) / HY_SLOW_DECAY
    deltas = np.abs(np.linspace(min_decay, max_decay, HY_WIDTH, dtype=np.float64))
    return np.tile(deltas, 2)[None, :].astype(np.float32)


def _pad_to(x, rows, cols):
    x = x.astype(F32)
    return jnp.pad(x, ((0, 0), (0, rows - x.shape[1]), (0, cols - x.shape[2])))


def _hyena_filter_weights(p):
    hid = LANES
    vec = lambda a: _pad_to(a[:, None, :], 1, hid)
    return (_pad_to(p["hy_filt_w1"], LANES, hid), vec(p["hy_filt_b1"]), vec(p["hy_filt_freq1"]),
            _pad_to(p["hy_filt_w2"], hid, hid), vec(p["hy_filt_b2"]), vec(p["hy_filt_freq2"]),
            _pad_to(p["hy_filt_w3"], hid, 2 * HY_WIDTH))


def _hyena_filters(seq, fw, l, tf):
    feats = jnp.asarray(_hyena_features(seq))
    deltas = jnp.asarray(_hyena_deltas())
    return pl.pallas_call(
        _hy_filter_kernel,
        grid=(seq // tf,),
        in_specs=[pl.BlockSpec((tf, LANES), lambda i: (i, 0))] + [_layer_spec(a, l) for a in fw]
                 + [_full(deltas.shape)],
        out_specs=pl.BlockSpec((tf, 2 * HY_WIDTH), lambda i: (i, 0)),
        out_shape=jax.ShapeDtypeStruct((seq, 2 * HY_WIDTH), F32),
        compiler_params=_params(("parallel",)),
        name="hy_filter",
    )(feats, *fw, deltas)


def _hyena_conv(z3, filt, plan):
    nb, seq, c = z3.shape
    nh = c // LANES
    f1 = jnp.asarray(plan.f1, F32)
    m2 = jnp.asarray(plan.m2, F32)
    m2i = jnp.asarray(plan.m2i, F32)
    g1 = jnp.asarray(plan.g1, F32)
    r, kp, k1n = plan.r, plan.kp, plan.k1n
    zp_shape = (plan.nq * plan.zpitch, LANES)
    as_shape = (r * plan.apitch, LANES)
    cs_shape = (kp * plan.cpitch, LANES)
    spec = pl.pallas_call(
        functools.partial(_hy_spec_kernel, plan=plan),
        grid=(nh,),
        in_specs=[pl.BlockSpec((seq, LANES), lambda j: (0, j)),
                  pl.BlockSpec((seq, LANES), lambda j: (0, nh + j)),
                  _full(f1.shape), _full(m2.shape)],
        out_specs=pl.BlockSpec((1, k1n, 2 * r, LANES), lambda j: (j, 0, 0, 0)),
        out_shape=jax.ShapeDtypeStruct((nh, k1n, 2 * r, LANES), F32),
        scratch_shapes=[pltpu.VMEM(zp_shape, F32), pltpu.VMEM(as_shape, F32)],
        compiler_params=_params(("parallel",)),
        name="hy_spec",
    )(filt, filt, f1, m2)
    return pl.pallas_call(
        functools.partial(_hy_conv_kernel, plan=plan),
        grid=(nb, nh),
        in_specs=[pl.BlockSpec((1, seq, LANES), lambda b, j: (b, 0, j)),
                  pl.BlockSpec((1, k1n, 2 * r, LANES), lambda b, j: (j, 0, 0, 0)),
                  _full(f1.shape), _full(m2.shape), _full(m2i.shape), _full(g1.shape)],
        out_specs=pl.BlockSpec((1, seq, LANES), lambda b, j: (b, 0, j)),
        out_shape=jax.ShapeDtypeStruct((nb, seq, c), STREAM),
        scratch_shapes=[pltpu.VMEM(zp_shape, F32), pltpu.VMEM(as_shape, F32), pltpu.VMEM(cs_shape, F32)],
        compiler_params=_params(("parallel", "parallel")),
        name="hy_conv",
    )(z3, spec, f1, m2, m2i, g1)


def _mix_kernel(h_ref, hf_ref, hb_ref, ga_ref, yf_ref, yb_ref, ub_ref, yc_ref, z_ref, x0_ref,
                d_ref, gw_ref, gb_ref, hbias_ref, mg_ref, wo_ref, o_ref):
    f32 = lambda ref: ref[...].astype(F32)
    ya = (f32(hf_ref) + f32(hb_ref)) * _gelu(f32(ga_ref))
    yb = _gelu(f32(ub_ref) * d_ref[...] + f32(yf_ref) + f32(yb_ref))
    yb = yb * _sigmoid(_dot(yb.astype(BF16), gw_ref[...]) + gb_ref[...])
    yc = (f32(yc_ref) + f32(z_ref) * hbias_ref[...]) * f32(x0_ref)
    a1, a2 = RG_WIDTH, RG_WIDTH + S5_WIDTH
    na = (_rms_nogain(ya) * mg_ref[:, :a1]).astype(BF16)
    nb = (_rms_nogain(yb) * mg_ref[:, a1:a2]).astype(BF16)
    nc = (_rms_nogain(yc) * mg_ref[:, a2:]).astype(BF16)
    out = _dot(na, wo_ref[:a1, :]) + _dot(nb, wo_ref[a1:a2, :]) + _dot(nc, wo_ref[a2:, :])
    o_ref[...] = h_ref[...] + out


def _mix(rows, consts, l, tm):
    n, dm = rows[0].shape
    return pl.pallas_call(
        _mix_kernel,
        grid=(n // tm,),
        in_specs=[pl.BlockSpec((tm, a.shape[1]), lambda i: (i, 0)) for a in rows]
                 + [_layer_spec(a, l) for a in consts],
        out_specs=pl.BlockSpec((tm, dm), lambda i: (i, 0)),
        out_shape=jax.ShapeDtypeStruct((n, dm), F32),
        compiler_params=_params(("parallel",)),
        name="mix",
    )(*rows, *consts)


FFN_TILE = 256


def _ffn_kernel(h_ref, hp_ref, hn_ref, g_ref, wu_ref, cw_ref, cb_ref, wd_ref, fg_ref, o_ref,
                x_scr, u_scr, gated_scr, y_scr, *, tiles_per_seq, tm, final):
    i = pl.program_id(0)
    first = (i % tiles_per_seq) == 0
    last = (i % tiles_per_seq) == tiles_per_seq - 1
    nslab = h_ref.shape[1] // LANES
    ph = tm // SUBLANES
    g = g_ref[...]

    def norm(v):
        return v * lax.rsqrt(jnp.mean(v * v, axis=-1, keepdims=True) + RMS_EPS) * g

    xn = norm(h_ref[...])
    xp = norm(jnp.where(first, 0.0, hp_ref[...]))
    xq = norm(jnp.where(last, 0.0, hn_ref[...]))
    for c in range(nslab):
        lanes = slice(c * LANES, (c + 1) * LANES)
        for s in range(SUBLANES):
            x_scr[c, pl.ds(s, ph, stride=SUBLANES), :] = xn[s * ph:(s + 1) * ph, lanes]
        x_scr[c, tm:tm + HALO, :] = xp[:, lanes]
        x_scr[c, tm + HALO:tm + 2 * HALO, :] = xq[:, lanes]
    x = jnp.concatenate([x_scr[c] for c in range(nslab)], axis=1).astype(BF16)
    sub = lax.broadcasted_iota(jnp.int32, (SUBLANES, FFN_TILE), 0)
    assert cw_ref.shape[0] == 3
    for k in range(D_FF // FFN_TILE):
        halves = []
        for part in range(2):
            lo = part * D_FF + k * FFN_TILE
            slot = 2 * k + part
            u_scr[slot] = _dot(x, wu_ref[:, lo:lo + FFN_TILE])
            head = jnp.where(sub == 0, pltpu.roll(u_scr[slot, tm:tm + HALO, :], 1, 0),
                             pltpu.roll(u_scr[slot, tm - SUBLANES:tm, :], 1, 0))
            tail = jnp.where(sub == SUBLANES - 1,
                             pltpu.roll(u_scr[slot, tm + HALO:tm + 2 * HALO, :], SUBLANES - 1, 0),
                             pltpu.roll(u_scr[slot, 0:SUBLANES, :], SUBLANES - 1, 0))
            prv = jnp.concatenate([head, u_scr[slot, 0:tm - SUBLANES, :]], axis=0)
            nxt = jnp.concatenate([u_scr[slot, SUBLANES:tm, :], tail], axis=0)
            cols = slice(lo, lo + FFN_TILE)
            halves.append(cb_ref[:, cols] + prv * cw_ref[0:1, cols] + u_scr[slot, 0:tm, :] * cw_ref[1:2, cols]
                          + nxt * cw_ref[2:3, cols])
        gated_scr[:, k * FFN_TILE:(k + 1) * FFN_TILE] = (_gelu(halves[0]) * halves[1]).astype(BF16)
    y = _dot(gated_scr[...], wd_ref[...])
    for c in range(nslab):
        y_scr[c] = y[:, c * LANES:(c + 1) * LANES]
    for s in range(SUBLANES):
        for c in range(nslab):
            rows = slice(s * ph, (s + 1) * ph)
            lanes = slice(c * LANES, (c + 1) * LANES)
            o_ref[rows, lanes] = h_ref[rows, lanes] + y_scr[c, pl.ds(s, ph, stride=SUBLANES), :]
    if final:
        out = o_ref[...]
        o_ref[...] = out * lax.rsqrt(jnp.mean(out * out, axis=-1, keepdims=True) + RMS_EPS) * fg_ref[...]


def _resident(arr, *lead):
    rest = arr.shape[len(lead):]
    zeros = (0,) * len(rest)
    return pl.BlockSpec((None,) * len(lead) + rest, lambda *_: tuple(lead) + zeros,
                        pipeline_mode=pl.Buffered(1))


def _ffn(h, g, w_up, conv_w, conv_b, w_down, final_g, l, seq, tm, final):
    n, d = h.shape
    main, prev, nxt = _halo_specs(tm, n)
    return pl.pallas_call(
        functools.partial(_ffn_kernel, tiles_per_seq=seq // tm, tm=tm, final=final),
        grid=(n // tm,),
        in_specs=[pl.BlockSpec((tm, d), main), pl.BlockSpec((HALO, d), prev), pl.BlockSpec((HALO, d), nxt),
                  _layer_spec(g, l), _resident(w_up, l), _layer_spec(conv_w, l), _layer_spec(conv_b, l),
                  _resident(w_down, l), _full(final_g.shape)],
        out_specs=pl.BlockSpec((tm, d), lambda i: (i, 0)),
        out_shape=jax.ShapeDtypeStruct((n, d), F32),
        scratch_shapes=[pltpu.VMEM((d // LANES, tm + 2 * HALO, LANES), F32),
                        pltpu.VMEM((2 * (D_FF // FFN_TILE), tm + 2 * HALO, FFN_TILE), F32),
                        pltpu.VMEM((tm, D_FF), BF16),
                        pltpu.VMEM((d // LANES, tm, LANES), F32)],
        compiler_params=_params(("parallel",)),
        name="ffn_final" if final else "ffn",
    )(h, h, h, g, w_up, conv_w, conv_b, w_down, final_g)


def _row_vec(a):
    return a.astype(F32)[:, None, :]


def _trunk(x, p, *, tm, t_scan, t_s5, tf):
    nb, seq, d = x.shape
    n = nb * seq
    depth = p["w_in"].shape[0]
    plan = _FftPlan(seq)

    proj_c = (_row_vec(p["norm1_g"]), p["w_in"].astype(BF16), p["rg_conv_w"].astype(F32),
              _row_vec(p["rg_conv_b"]), p["hy_conv_w"].astype(F32), _row_vec(p["hy_conv_b"]))
    rg_w, rg_bias, rg_sp = _rglru_weights(p)
    s5_w = _s5_weights(p)
    hy_fw = _hyena_filter_weights(p)
    mix_c = (_row_vec(p["s5_d"]), p["s5_glu_w"].astype(BF16), _row_vec(p["s5_glu_b"]),
             _row_vec(p["hy_bias"]), _row_vec(p["mix_norm_g"]), p["w_out"].astype(BF16))
    ffn_c = (_row_vec(p["norm2_g"]), p["w_up"].astype(BF16), p["ffn_conv_w"].astype(F32),
             _row_vec(p["ffn_conv_b"]), p["w_down"].astype(BF16), p["final_norm_g"].astype(F32)[None])

    h = x.astype(F32).reshape(n, d)
    for l in range(depth):
        urg, ga, ub, x0, z = _proj(h, *proj_c, l, seq, tm)
        hs = [a.reshape(n, RG_WIDTH)
              for a in _rglru(urg.reshape(nb, seq, RG_WIDTH), rg_w, rg_bias, rg_sp, l, t_scan)]
        yf, yb = _s5(ub.reshape(nb, seq, S5_WIDTH), *s5_w, l, t_s5)
        filt = _hyena_filters(seq, hy_fw, l, tf)
        yc = _hyena_conv(z.reshape(nb, seq, HY_WIDTH), filt, plan).reshape(n, HY_WIDTH)
        rows = [h, hs[0], hs[1], ga, yf.reshape(n, S5_WIDTH), yb.reshape(n, S5_WIDTH), ub, yc, z, x0]
        h = _mix(rows, mix_c, l, tm)
        h = _ffn(h, *ffn_c, l, seq, tm, final=(l == depth - 1))
    return h.reshape(nb, seq, d).astype(x.dtype)


def kernel(x, norm1_g, w_in, rg_conv_w, rg_conv_b, rg_wa, rg_ba, rg_wx, rg_bx, rg_lambda, s5_a_re, s5_a_im, s5_log_dt, s5_b_re, s5_b_im, s5_c_re, s5_c_im, s5_d, s5_glu_w, s5_glu_b, hy_conv_w, hy_conv_b, hy_filt_w1, hy_filt_b1, hy_filt_freq1, hy_filt_w2, hy_filt_b2, hy_filt_freq2, hy_filt_w3, hy_bias, mix_norm_g, w_out, norm2_g, w_up, ffn_conv_w, ffn_conv_b, w_down, final_norm_g):
    p = dict(norm1_g=norm1_g, w_in=w_in, rg_conv_w=rg_conv_w, rg_conv_b=rg_conv_b, rg_wa=rg_wa, rg_ba=rg_ba,
             rg_wx=rg_wx, rg_bx=rg_bx, rg_lambda=rg_lambda, s5_a_re=s5_a_re, s5_a_im=s5_a_im,
             s5_log_dt=s5_log_dt, s5_b_re=s5_b_re, s5_b_im=s5_b_im, s5_c_re=s5_c_re, s5_c_im=s5_c_im,
             s5_d=s5_d, s5_glu_w=s5_glu_w, s5_glu_b=s5_glu_b, hy_conv_w=hy_conv_w, hy_conv_b=hy_conv_b,
             hy_filt_w1=hy_filt_w1, hy_filt_b1=hy_filt_b1, hy_filt_freq1=hy_filt_freq1, hy_filt_w2=hy_filt_w2,
             hy_filt_b2=hy_filt_b2, hy_filt_freq2=hy_filt_freq2, hy_filt_w3=hy_filt_w3, hy_bias=hy_bias,
             mix_norm_g=mix_norm_g, w_out=w_out, norm2_g=norm2_g, w_up=w_up, ffn_conv_w=ffn_conv_w,
             ffn_conv_b=ffn_conv_b, w_down=w_down, final_norm_g=final_norm_g)
    return _trunk(x, p, tm=512, t_scan=256, t_s5=128, tf=512)
```

```python
import functools
import math

import numpy as np
import jax
import jax.numpy as jnp
from jax import lax
from jax.experimental import pallas as pl
from jax.experimental.pallas import tpu as pltpu

F32 = jnp.float32
BF16 = jnp.bfloat16
STREAM = jnp.bfloat16

RMS_EPS = 1e-6
RG_WIDTH = 384
RG_HEADS = 6
RG_C = 8.0
S5_WIDTH = 384
S5_GROUP = 16
S5_GROUPS = 24
S5_STATE = 64
S5_NSTATE = S5_GROUPS * S5_STATE
HY_WIDTH = 256
HY_BANDS = 16
HY_FAST_DECAY = 0.3
HY_SLOW_DECAY = 1.5
HY_DECAY_TARGET = 1e-2
D_FF = 2816
N_DIR = 2

LANES = 128
SUBLANES = 8
HALO = SUBLANES
VMEM_LIMIT = 56 * 1024 * 1024

HIGHEST = lax.Precision.HIGHEST


def _dot(a, b, precision=None):
    return jnp.dot(a, b, preferred_element_type=F32, precision=precision)


def _gelu(x):
    c = math.sqrt(2.0 / math.pi)
    return 0.5 * x * (1.0 + jnp.tanh(c * (x + 0.044715 * (x * x * x))))


def _sigmoid(x):
    return 0.5 * jnp.tanh(0.5 * x) + 0.5


def _rms_nogain(x):
    return x * lax.rsqrt(jnp.mean(x * x, axis=-1, keepdims=True) + RMS_EPS)


def _params(sem):
    return pltpu.CompilerParams(dimension_semantics=sem, vmem_limit_bytes=VMEM_LIMIT)


def _full(shape):
    nd = len(shape)
    return pl.BlockSpec(shape, lambda *_: (0,) * nd)


def _layer_spec(arr, *lead):
    rest = arr.shape[len(lead):]
    zeros = (0,) * len(rest)
    return pl.BlockSpec((None,) * len(lead) + rest, lambda *_: tuple(lead) + zeros)


def _halo_specs(tm, n):
    blocks_per_tile = tm // HALO
    nblocks = n // HALO

    def main(i, *_):
        return (i, 0)

    def prev(i, *_):
        return (jnp.maximum(i * blocks_per_tile - 1, 0), 0)

    def nxt(i, *_):
        return (jnp.minimum((i + 1) * blocks_per_tile, nblocks - 1), 0)

    return main, prev, nxt


def _fill_normed(x_scr, h_ref, hp_ref, hn_ref, g, first, last, tm):
    def norm(x):
        return x * lax.rsqrt(jnp.mean(x * x, axis=-1, keepdims=True) + RMS_EPS) * g

    x_scr[0:HALO, :] = norm(jnp.where(first, 0.0, hp_ref[...]))
    x_scr[HALO:HALO + tm, :] = norm(h_ref[...])
    x_scr[HALO + tm:2 * HALO + tm, :] = norm(jnp.where(last, 0.0, hn_ref[...]))


def _proj_kernel(h_ref, hp_ref, hn_ref, g_ref, w_ref, rcw_ref, rcb_ref, hcw_ref, hcb_ref,
                 urg_ref, ga_ref, ub_ref, x0_ref, z_ref, x_scr, u_scr, *, tiles_per_seq, tm):
    i = pl.program_id(0)
    first = (i % tiles_per_seq) == 0
    last = (i % tiles_per_seq) == tiles_per_seq - 1
    _fill_normed(x_scr, h_ref, hp_ref, hn_ref, g_ref[...], first, last, tm)
    u_scr[...] = _dot(x_scr[...].astype(BF16), w_ref[...])
    a0, a1, a2, a3 = 0, RG_WIDTH, 2 * RG_WIDTH, 2 * RG_WIDTH + S5_WIDTH
    acc = rcb_ref[...]
    for j in range(rcw_ref.shape[0]):
        acc = acc + u_scr[pl.ds(HALO + j - 1, tm), a0:a1] * rcw_ref[j:j + 1, :]
    urg_ref[...] = acc
    ga_ref[...] = u_scr[pl.ds(HALO, tm), a1:a2].astype(ga_ref.dtype)
    ub_ref[...] = u_scr[pl.ds(HALO, tm), a2:a3].astype(ub_ref.dtype)
    q = []
    for part in range(3):
        lo = part * HY_WIDTH
        acc = hcb_ref[:, lo:lo + HY_WIDTH]
        for j in range(hcw_ref.shape[0]):
            acc = acc + (u_scr[pl.ds(HALO + j - 1, tm), a3 + lo:a3 + lo + HY_WIDTH]
                         * hcw_ref[j:j + 1, lo:lo + HY_WIDTH])
        q.append(acc)
    x0_ref[...] = q[0].astype(x0_ref.dtype)
    z_ref[...] = (q[2] * q[1]).astype(z_ref.dtype)


def _proj(h, g, w_in, rcw, rcb, hcw, hcb, l, seq, tm):
    n, d = h.shape
    cols = w_in.shape[-1]
    main, prev, nxt = _halo_specs(tm, n)
    row = lambda c: pl.BlockSpec((tm, c), lambda i: (i, 0))
    outs = [jax.ShapeDtypeStruct((n, c), dt) for c, dt in
            ((RG_WIDTH, F32), (RG_WIDTH, STREAM), (S5_WIDTH, STREAM), (HY_WIDTH, STREAM), (HY_WIDTH, STREAM))]
    consts = [g, w_in, rcw, rcb, hcw, hcb]
    return pl.pallas_call(
        functools.partial(_proj_kernel, tiles_per_seq=seq // tm, tm=tm),
        grid=(n // tm,),
        in_specs=[pl.BlockSpec((tm, d), main), pl.BlockSpec((HALO, d), prev), pl.BlockSpec((HALO, d), nxt)]
                 + [_layer_spec(a, l) for a in consts],
        out_specs=[row(RG_WIDTH), row(RG_WIDTH), row(S5_WIDTH), row(HY_WIDTH), row(HY_WIDTH)],
        out_shape=outs,
        scratch_shapes=[pltpu.VMEM((tm + 2 * HALO, d), F32), pltpu.VMEM((tm + 2 * HALO, cols), F32)],
        compiler_params=_params(("parallel",)),
        name="proj",
    )(h, h, h, *consts)


RG_STAGES = 4


def _lo_mask():
    return lax.broadcasted_iota(jnp.int32, (SUBLANES, LANES), 0) < (SUBLANES // 2)


def _rglru_kernel(uf_ref, ub_ref, w_ref, bias_ref, sp_ref, of_ref, ob_ref,
                  af, ab, bf, bb, carry_scr, *, t, nb):
    i = pl.program_id(0)
    nm = RG_WIDTH // LANES

    @pl.when(i == 0)
    def _():
        carry_scr[...] = jnp.zeros_like(carry_scr)
        for scr in (af, ab, bf, bb):
            scr[...] = jnp.zeros_like(scr)

    tq = t // RG_STAGES
    for q in range(RG_STAGES):
        for dr, (u_ref, a_scr, b_scr) in enumerate(((uf_ref, af, bf), (ub_ref, ab, bb))):
            t0 = q * tq if dr == 0 else t - (q + 1) * tq
            u = u_ref[:, t0:t0 + tq, :].reshape(nb * tq, RG_WIDTH)
            gates = _dot(u.astype(BF16), w_ref[dr]) + bias_ref[dr]
            r = _sigmoid(gates[:, :RG_WIDTH])
            gi = _sigmoid(gates[:, RG_WIDTH:])
            log_a = (-RG_C) * r * sp_ref[dr]
            a = jnp.exp(log_a)
            m2 = -jnp.tanh(log_a) * (a * a + 1.0)
            mult = jnp.where(m2 > 0.0, m2 * lax.rsqrt(m2), 0.0)
            bin_ = mult * (gi * u)
            for b in range(nb):
                for m in range(nm):
                    rows = pl.ds(t0 * SUBLANES + dr * nb + b, tq, stride=SUBLANES)
                    a_scr[m, rows, :] = a[b * tq:(b + 1) * tq, m * LANES:(m + 1) * LANES]
                    b_scr[m, rows, :] = bin_[b * tq:(b + 1) * tq, m * LANES:(m + 1) * LANES]

    lo = _lo_mask()
    carry = [carry_scr[m] for m in range(nm)]
    for j in range(t):
        row = j * SUBLANES
        mrow = (t - 1 - j) * SUBLANES
        for m in range(nm):
            at = jnp.where(lo, af[m, row:row + SUBLANES, :], ab[m, mrow:mrow + SUBLANES, :])
            bt = jnp.where(lo, bf[m, row:row + SUBLANES, :], bb[m, mrow:mrow + SUBLANES, :])
            h = at * carry[m] + bt
            bf[m, row:row + SUBLANES, :] = h
            bb[m, mrow:mrow + SUBLANES, :] = h
            carry[m] = h
    for m in range(nm):
        carry_scr[m] = carry[m]
    for b in range(nb):
        for m in range(nm):
            of_ref[b, :, m * LANES:(m + 1) * LANES] = bf[m, pl.ds(b, t, stride=SUBLANES), :].astype(STREAM)
            ob_ref[b, :, m * LANES:(m + 1) * LANES] = bb[m, pl.ds(nb + b, t, stride=SUBLANES), :].astype(STREAM)


def _rglru(u3, w, bias, sp, l, t):
    nb, seq, c = u3.shape
    assert 2 * nb == SUBLANES
    nchunk = seq // t
    fwd = lambda i: (0, i, 0)
    bwd = lambda i: (0, nchunk - 1 - i, 0)
    nm = c // LANES
    out = jax.ShapeDtypeStruct((nb, seq, c), STREAM)
    scr = pltpu.VMEM((nm, SUBLANES * t, LANES), F32)
    return pl.pallas_call(
        functools.partial(_rglru_kernel, t=t, nb=nb),
        grid=(nchunk,),
        in_specs=[pl.BlockSpec((nb, t, c), fwd), pl.BlockSpec((nb, t, c), bwd),
                  _layer_spec(w, l), _layer_spec(bias, l), _layer_spec(sp, l)],
        out_specs=[pl.BlockSpec((nb, t, c), fwd), pl.BlockSpec((nb, t, c), bwd)],
        out_shape=[out, out],
        scratch_shapes=[scr, scr, scr, scr, pltpu.VMEM((nm, SUBLANES, LANES), F32)],
        compiler_params=_params(("arbitrary",)),
        name="rglru",
    )(u3, u3, w, bias, sp)


def _rglru_weights(p):
    eye = jnp.eye(RG_HEADS, dtype=F32)
    both = jnp.stack([p["rg_wa"], p["rg_wx"]], axis=2).astype(F32)
    w = jnp.einsum('ldqhij,hk->ldhiqkj', both, eye)
    depth = w.shape[0]
    w = w.reshape(depth, N_DIR, RG_WIDTH, 2 * RG_WIDTH).astype(BF16)
    bias = jnp.concatenate([p["rg_ba"], p["rg_bx"]], axis=-1).astype(F32)[:, :, None, :]
    x = -p["rg_lambda"].astype(F32)
    sp = (jnp.maximum(x, 0.0) + jnp.log1p(jnp.exp(-jnp.abs(x))))[:, :, None, :]
    return w, bias, sp


S5_NBLK = S5_NSTATE // LANES
S5_GRP = 4
S5_PER = S5_NBLK // (S5_WIDTH // LANES)


def _reverse_tiles(src, dst, nm, ntile):
    for j in range(ntile):
        s = (ntile - 1 - j) * SUBLANES
        for m in range(nm):
            dst[m, j * SUBLANES:(j + 1) * SUBLANES, :] = src[m, s:s + SUBLANES, :]


def _s5_kernel(uf_ref, ub_ref, wb_ref, wc_ref, ar_ref, ai_ref, yf_ref, yb_ref,
               uf8, ub8, ubr, bu, ym, ymr, carry_scr, *, t, nb):
    i = pl.program_id(0)
    nm = S5_WIDTH // LANES

    @pl.when(i == 0)
    def _():
        carry_scr[...] = jnp.zeros_like(carry_scr)
        uf8[...] = jnp.zeros_like(uf8)
        ub8[...] = jnp.zeros_like(ub8)

    for b in range(nb):
        for m in range(nm):
            uf8[m, pl.ds(b, t, stride=SUBLANES), :] = uf_ref[b, :, m * LANES:(m + 1) * LANES].astype(F32)
            ub8[m, pl.ds(nb + b, t, stride=SUBLANES), :] = ub_ref[b, :, m * LANES:(m + 1) * LANES].astype(F32)
    _reverse_tiles(ub8, ubr, nm, t)

    for m in range(nm):
        lhs = jnp.concatenate([uf8[m], ubr[m]], axis=1).astype(BF16)
        res = _dot(lhs, wb_ref[m])
        for q in range(2 * S5_PER):
            bu[2 * S5_PER * m + q] = res[:, q * LANES:(q + 1) * LANES]

    for g in range(S5_NBLK // S5_GRP):
        blocks = list(range(g * S5_GRP, (g + 1) * S5_GRP))
        ars = [ar_ref[n] for n in blocks]
        ais = [ai_ref[n] for n in blocks]

        def body(j, carry, blocks=blocks, ars=ars, ais=ais):
            row = j * SUBLANES
            new = []
            for k, n in enumerate(blocks):
                hr, hi = carry[2 * k], carry[2 * k + 1]
                nr = ars[k] * hr - ais[k] * hi + bu[2 * n, pl.ds(row, SUBLANES), :]
                ni = ars[k] * hi + ais[k] * hr + bu[2 * n + 1, pl.ds(row, SUBLANES), :]
                bu[2 * n, pl.ds(row, SUBLANES), :] = nr
                bu[2 * n + 1, pl.ds(row, SUBLANES), :] = ni
                new += [nr, ni]
            return tuple(new)

        init = []
        for n in blocks:
            init += [carry_scr[2 * n], carry_scr[2 * n + 1]]
        carry = tuple(init)
        for j in range(t):
            carry = body(j, carry)
        for k, n in enumerate(blocks):
            carry_scr[2 * n] = carry[2 * k]
            carry_scr[2 * n + 1] = carry[2 * k + 1]

    rows = lax.broadcasted_iota(jnp.int32, (SUBLANES * t, LANES), 0)
    fwd_row = (rows & (SUBLANES - 1)) < nb
    for m in range(nm):
        hcat = jnp.concatenate([bu[2 * S5_PER * m + q] for q in range(2 * S5_PER)], axis=1).astype(BF16)
        acc = _dot(hcat, wc_ref[m])
        ym[m] = jnp.where(fwd_row, acc[:, :LANES], acc[:, LANES:])
    _reverse_tiles(ym, ymr, nm, t)
    for b in range(nb):
        for m in range(nm):
            yf_ref[b, :, m * LANES:(m + 1) * LANES] = ym[m, pl.ds(b, t, stride=SUBLANES), :].astype(STREAM)
            yb_ref[b, :, m * LANES:(m + 1) * LANES] = ymr[m, pl.ds(nb + b, t, stride=SUBLANES), :].astype(STREAM)


def _s5(u3, wb, wc, ar, ai, l, t):
    nb, seq, c = u3.shape
    assert 2 * nb == SUBLANES
    nchunk = seq // t
    fwd = lambda i: (0, i, 0)
    bwd = lambda i: (0, nchunk - 1 - i, 0)
    nm = c // LANES
    rows = SUBLANES * t
    out = jax.ShapeDtypeStruct((nb, seq, c), STREAM)
    return pl.pallas_call(
        functools.partial(_s5_kernel, t=t, nb=nb),
        grid=(nchunk,),
        in_specs=[pl.BlockSpec((nb, t, c), fwd), pl.BlockSpec((nb, t, c), bwd),
                  _layer_spec(wb, l), _layer_spec(wc, l), _layer_spec(ar, l), _layer_spec(ai, l)],
        out_specs=[pl.BlockSpec((nb, t, c), fwd), pl.BlockSpec((nb, t, c), bwd)],
        out_shape=[out, out],
        scratch_shapes=[pltpu.VMEM((nm, rows, LANES), F32), pltpu.VMEM((nm, rows, LANES), F32),
                        pltpu.VMEM((nm, rows, LANES), F32),
                        pltpu.VMEM((2 * S5_NBLK, rows, LANES), F32),
                        pltpu.VMEM((nm, rows, LANES), F32), pltpu.VMEM((nm, rows, LANES), F32),
                        pltpu.VMEM((2 * S5_NBLK, SUBLANES, LANES), F32)],
        compiler_params=_params(("arbitrary",)),
        name="s5",
    )(u3, u3, wb, wc, ar, ai)


def _s5_group_mask():
    mask = np.zeros((S5_NBLK, LANES // S5_GROUP, LANES // S5_STATE), np.float32)
    for n in range(S5_NBLK):
        for s in range(LANES // S5_STATE):
            mask[n, (LANES // S5_STATE) * (n % S5_PER) + s, s] = 1.0
    return mask


def _s5_weights(p):
    lr = p["s5_a_re"].astype(F32)
    li = p["s5_a_im"].astype(F32)
    dt = jnp.exp(p["s5_log_dt"].astype(F32))[..., None]
    mag = jnp.exp(lr * dt)
    abar_r = mag * jnp.cos(li * dt)
    abar_i = mag * jnp.sin(li * dt)
    den = lr * lr + li * li
    nr = abar_r - 1.0
    ni = abar_i
    coef_r = ((nr * lr + ni * li) / den)[..., None]
    coef_i = ((ni * lr - nr * li) / den)[..., None]
    b_re = p["s5_b_re"].astype(F32)
    b_im = p["s5_b_im"].astype(F32)
    bbar = jnp.stack([coef_r * b_re - coef_i * b_im, coef_r * b_im + coef_i * b_re], axis=2)
    depth = lr.shape[0]
    half = LANES // S5_STATE
    mask = jnp.asarray(_s5_group_mask())
    bb = bbar.reshape(depth, N_DIR, 2, S5_NBLK, half, S5_STATE, S5_GROUP)
    nm = S5_WIDTH // LANES
    wb = jnp.einsum('ldrnspc,nks->lndkcrsp', bb, mask).reshape(depth, nm, S5_PER, 2 * LANES, 2 * LANES)
    wb = wb.transpose(0, 1, 3, 2, 4).reshape(depth, nm, 2 * LANES, S5_PER * 2 * LANES)
    cc = jnp.stack([p["s5_c_re"].astype(F32), -p["s5_c_im"].astype(F32)], axis=2)
    cc = cc.reshape(depth, N_DIR, 2, S5_NBLK, half, S5_GROUP, S5_STATE)
    wc = jnp.einsum('ldrnscp,nks->lnrspdkc', cc, mask).reshape(depth, nm, S5_PER * 2 * LANES, 2 * LANES)

    def tile_rows(a):
        a = a.reshape(depth, N_DIR, S5_NBLK, LANES).transpose(0, 2, 1, 3)
        return jnp.repeat(a, SUBLANES // N_DIR, axis=2)

    return wb.astype(BF16), wc.astype(BF16), tile_rows(abar_r), tile_rows(abar_i)


def _odd8(n):
    p = -(-n // SUBLANES)
    if p % 2 == 0:
        p += 1
    return p * SUBLANES


class _FftPlan:
    def __init__(self, seq):
        self.seq = seq
        r = 1
        while r * r < seq:
            r *= 2
        self.r = r
        self.nq = seq // r
        self.q = 2 * seq // r
        self.k1n = self.q // 2 + 1
        self.kp = -(-self.k1n // SUBLANES) * SUBLANES
        self.zpitch = _odd8(r)
        self.apitch = _odd8(2 * self.kp)
        self.cpitch = _odd8(2 * r)
        self.unroll = min(8, r)
        self.unroll2 = next(u for u in (13, 5, 4, 3, 2, 1) if self.k1n % u == 0)
        p = 2 * seq
        n1 = np.arange(self.nq)[None, :]
        k1 = np.arange(self.k1n)[:, None]
        ang = 2.0 * np.pi * n1 * k1 / self.q
        f1 = np.zeros((2 * self.kp, self.nq))
        f1[:self.k1n] = np.cos(ang)
        f1[self.kp:self.kp + self.k1n] = -np.sin(ang)
        self.f1 = f1
        w = np.full((self.k1n,), 2.0)
        w[0] = 1.0
        w[-1] = 1.0
        g1 = np.zeros((self.nq, 2 * self.kp))
        g1[:, :self.k1n] = (np.cos(ang) * w[:, None] / p).T
        g1[:, self.kp:self.kp + self.k1n] = (-np.sin(ang) * w[:, None] / p).T
        self.g1 = g1
        kk = np.arange(self.k1n)[:, None, None]
        k2 = np.arange(r)[None, :, None]
        n2 = np.arange(r)[None, None, :]
        ph = 2.0 * np.pi * (n2 * k2 / r + n2 * kk / p)
        tr, ti = np.cos(ph), -np.sin(ph)
        self.m2 = np.concatenate([np.concatenate([tr, -ti], axis=2),
                                  np.concatenate([ti, tr], axis=2)], axis=1)
        ur, ui = np.transpose(tr, (0, 2, 1)), -np.transpose(ti, (0, 2, 1))
        self.m2i = np.concatenate([np.concatenate([ur, -ui], axis=2),
                                   np.concatenate([ui, ur], axis=2)], axis=1)


def _fft_forward(plan, src_ref, f1_ref, m2_ref, zp, as_, emit):
    r, nq, kp = plan.r, plan.nq, plan.kp
    for n1 in range(nq):
        zp[n1 * plan.zpitch:n1 * plan.zpitch + r, :] = src_ref[n1 * r:(n1 + 1) * r, :].astype(F32)

    def stage1(n2, c):
        slab = zp[pl.ds(n2, nq, stride=plan.zpitch), :]
        a = _dot(f1_ref[...].astype(BF16), slab.astype(BF16))
        as_[pl.ds(pl.multiple_of(n2 * plan.apitch, SUBLANES), 2 * kp), :] = a
        return c

    lax.fori_loop(0, r, stage1, 0, unroll=plan.unroll)

    def stage2(k1, c):
        sr = as_[pl.ds(k1, r, stride=plan.apitch), :]
        si = as_[pl.ds(kp + k1, r, stride=plan.apitch), :]
        s = jnp.concatenate([sr, si], axis=0).astype(BF16)
        emit(k1, _dot(m2_ref[k1].astype(BF16), s))
        return c

    lax.fori_loop(0, plan.k1n, stage2, 0, unroll=plan.unroll2)


def _hy_spec_kernel(kf_ref, kb_ref, f1_ref, m2_ref, o_ref, zp, as_, *, plan):
    r = plan.r

    def emit_f(k1, x):
        o_ref[0, k1] = x

    def emit_b(k1, x):
        sign = jnp.where(lax.broadcasted_iota(jnp.int32, (2 * r, LANES), 0) < r, 1.0, -1.0)
        o_ref[0, k1] = o_ref[0, k1] + sign * x

    _fft_forward(plan, kf_ref, f1_ref, m2_ref, zp, as_, emit_f)
    _fft_forward(plan, kb_ref, f1_ref, m2_ref, zp, as_, emit_b)


def _hy_conv_kernel(z_ref, spec_ref, f1_ref, m2_ref, m2i_ref, g1_ref, o_ref, zp, as_, cs, *, plan):
    r, nq, kp, k1n = plan.r, plan.nq, plan.kp, plan.k1n
    if kp > k1n:
        cs[k1n * plan.cpitch:kp * plan.cpitch, :] = jnp.zeros(((kp - k1n) * plan.cpitch, LANES), F32)

    def emit(k1, x):
        kf = spec_ref[0, k1]
        xr, xi = x[:r], x[r:]
        kr, ki = kf[:r], kf[r:]
        prod = jnp.concatenate([xr * kr - xi * ki, xr * ki + xi * kr], axis=0).astype(BF16)
        c = _dot(m2i_ref[k1].astype(BF16), prod)
        cs[pl.ds(pl.multiple_of(k1 * plan.cpitch, SUBLANES), 2 * r), :] = c

    _fft_forward(plan, z_ref.at[0], f1_ref, m2_ref, zp, as_, emit)

    def stage3(n2, c):
        cr = cs[pl.ds(n2, kp, stride=plan.cpitch), :]
        ci = cs[pl.ds(r + n2, kp, stride=plan.cpitch), :]
        y = _dot(g1_ref[...].astype(BF16), jnp.concatenate([cr, ci], axis=0).astype(BF16))
        zp[pl.ds(n2, nq, stride=plan.zpitch), :] = y
        return c

    lax.fori_loop(0, r, stage3, 0, unroll=plan.unroll)
    for n1 in range(nq):
        o_ref[0, n1 * r:(n1 + 1) * r, :] = zp[n1 * plan.zpitch:n1 * plan.zpitch + r, :].astype(o_ref.dtype)


def _hy_filter_kernel(feat_ref, w1_ref, b1_ref, f1_ref, w2_ref, b2_ref, f2_ref, w3_ref, dl_ref, o_ref):
    feats = feat_ref[...]
    hid = jnp.sin(f1_ref[...] * (_dot(feats, w1_ref[...], HIGHEST) + b1_ref[...]))
    hid = jnp.sin(f2_ref[...] * (_dot(hid, w2_ref[...], HIGHEST) + b2_ref[...]))
    k = _dot(hid, w3_ref[...], HIGHEST)
    t = feats[:, 0:1]
    o_ref[...] = k * jnp.exp(-(t * dl_ref[...]))


def _hyena_features(seq):
    pos = np.arange(seq, dtype=np.float64)
    t = pos / max(seq - 1, 1)
    w = (2.0 * math.pi / seq) * pos
    bands = np.linspace(1e-4, HY_BANDS - 1, HY_BANDS, dtype=np.float64)
    ang = w[:, None] * bands
    feats = np.concatenate([t[:, None], np.cos(ang), -np.sin(ang)], axis=-1).astype(np.float32)
    out = np.zeros((seq, LANES), np.float32)
    out[:, :feats.shape[1]] = feats
    return out


def _hyena_deltas():
    max_decay = math.log(HY_DECAY_TARGET) / HY_FAST_DECAY
    min_decay = math.log(HY_DECAY_TARGET) / HY_SLOW_DECAY
    deltas = np.abs(np.linspace(min_decay, max_decay, HY_WIDTH, dtype=np.float64))
    return np.tile(deltas, 2)[None, :].astype(np.float32)


def _pad_to(x, rows, cols):
    x = x.astype(F32)
    return jnp.pad(x, ((0, 0), (0, rows - x.shape[1]), (0, cols - x.shape[2])))


def _hyena_filter_weights(p):
    hid = LANES
    vec = lambda a: _pad_to(a[:, None, :], 1, hid)
    return (_pad_to(p["hy_filt_w1"], LANES, hid), vec(p["hy_filt_b1"]), vec(p["hy_filt_freq1"]),
            _pad_to(p["hy_filt_w2"], hid, hid), vec(p["hy_filt_b2"]), vec(p["hy_filt_freq2"]),
            _pad_to(p["hy_filt_w3"], hid, 2 * HY_WIDTH))


def _hyena_filters(seq, fw, l, tf):
    feats = jnp.asarray(_hyena_features(seq))
    deltas = jnp.asarray(_hyena_deltas())
    return pl.pallas_call(
        _hy_filter_kernel,
        grid=(seq // tf,),
        in_specs=[pl.BlockSpec((tf, LANES), lambda i: (i, 0))] + [_layer_spec(a, l) for a in fw]
                 + [_full(deltas.shape)],
        out_specs=pl.BlockSpec((tf, 2 * HY_WIDTH), lambda i: (i, 0)),
        out_shape=jax.ShapeDtypeStruct((seq, 2 * HY_WIDTH), F32),
        compiler_params=_params(("parallel",)),
        name="hy_filter",
    )(feats, *fw, deltas)


def _hyena_conv(z3, filt, plan):
    nb, seq, c = z3.shape
    nh = c // LANES
    f1 = jnp.asarray(plan.f1, F32)
    m2 = jnp.asarray(plan.m2, F32)
    m2i = jnp.asarray(plan.m2i, F32)
    g1 = jnp.asarray(plan.g1, F32)
    r, kp, k1n = plan.r, plan.kp, plan.k1n
    zp_shape = (plan.nq * plan.zpitch, LANES)
    as_shape = (r * plan.apitch, LANES)
    cs_shape = (kp * plan.cpitch, LANES)
    spec = pl.pallas_call(
        functools.partial(_hy_spec_kernel, plan=plan),
        grid=(nh,),
        in_specs=[pl.BlockSpec((seq, LANES), lambda j: (0, j)),
                  pl.BlockSpec((seq, LANES), lambda j: (0, nh + j)),
                  _full(f1.shape), _full(m2.shape)],
        out_specs=pl.BlockSpec((1, k1n, 2 * r, LANES), lambda j: (j, 0, 0, 0)),
        out_shape=jax.ShapeDtypeStruct((nh, k1n, 2 * r, LANES), F32),
        scratch_shapes=[pltpu.VMEM(zp_shape, F32), pltpu.VMEM(as_shape, F32)],
        compiler_params=_params(("parallel",)),
        name="hy_spec",
    )(filt, filt, f1, m2)
    return pl.pallas_call(
        functools.partial(_hy_conv_kernel, plan=plan),
        grid=(nb, nh),
        in_specs=[pl.BlockSpec((1, seq, LANES), lambda b, j: (b, 0, j)),
                  pl.BlockSpec((1, k1n, 2 * r, LANES), lambda b, j: (j, 0, 0, 0)),
                  _full(f1.shape), _full(m2.shape), _full(m2i.shape), _full(g1.shape)],
        out_specs=pl.BlockSpec((1, seq, LANES), lambda b, j: (b, 0, j)),
        out_shape=jax.ShapeDtypeStruct((nb, seq, c), STREAM),
        scratch_shapes=[pltpu.VMEM(zp_shape, F32), pltpu.VMEM(as_shape, F32), pltpu.VMEM(cs_shape, F32)],
        compiler_params=_params(("parallel", "parallel")),
        name="hy_conv",
    )(z3, spec, f1, m2, m2i, g1)


def _mix_kernel(h_ref, hf_ref, hb_ref, ga_ref, yf_ref, yb_ref, ub_ref, yc_ref, z_ref, x0_ref,
                d_ref, gw_ref, gb_ref, hbias_ref, mg_ref, wo_ref, o_ref):
    f32 = lambda ref: ref[...].astype(F32)
    ya = (f32(hf_ref) + f32(hb_ref)) * _gelu(f32(ga_ref))
    yb = _gelu(f32(ub_ref) * d_ref[...] + f32(yf_ref) + f32(yb_ref))
    yb = yb * _sigmoid(_dot(yb.astype(BF16), gw_ref[...]) + gb_ref[...])
    yc = (f32(yc_ref) + f32(z_ref) * hbias_ref[...]) * f32(x0_ref)
    a1, a2 = RG_WIDTH, RG_WIDTH + S5_WIDTH
    na = (_rms_nogain(ya) * mg_ref[:, :a1]).astype(BF16)
    nb = (_rms_nogain(yb) * mg_ref[:, a1:a2]).astype(BF16)
    nc = (_rms_nogain(yc) * mg_ref[:, a2:]).astype(BF16)
    out = _dot(na, wo_ref[:a1, :]) + _dot(nb, wo_ref[a1:a2, :]) + _dot(nc, wo_ref[a2:, :])
    o_ref[...] = h_ref[...] + out


def _mix(rows, consts, l, tm):
    n, dm = rows[0].shape
    return pl.pallas_call(
        _mix_kernel,
        grid=(n // tm,),
        in_specs=[pl.BlockSpec((tm, a.shape[1]), lambda i: (i, 0)) for a in rows]
                 + [_layer_spec(a, l) for a in consts],
        out_specs=pl.BlockSpec((tm, dm), lambda i: (i, 0)),
        out_shape=jax.ShapeDtypeStruct((n, dm), F32),
        compiler_params=_params(("parallel",)),
        name="mix",
    )(*rows, *consts)


FFN_TILE = 256


def _ffn_kernel(h_ref, hp_ref, hn_ref, g_ref, wu_ref, cw_ref, cb_ref, wd_ref, fg_ref, o_ref,
                x_scr, u_scr, gated_scr, y_scr, *, tiles_per_seq, tm, final):
    i = pl.program_id(0)
    first = (i % tiles_per_seq) == 0
    last = (i % tiles_per_seq) == tiles_per_seq - 1
    nslab = h_ref.shape[1] // LANES
    ph = tm // SUBLANES
    g = g_ref[...]

    def norm(v):
        return v * lax.rsqrt(jnp.mean(v * v, axis=-1, keepdims=True) + RMS_EPS) * g

    xn = norm(h_ref[...])
    xp = norm(jnp.where(first, 0.0, hp_ref[...]))
    xq = norm(jnp.where(last, 0.0, hn_ref[...]))
    for c in range(nslab):
        lanes = slice(c * LANES, (c + 1) * LANES)
        for s in range(SUBLANES):
            x_scr[c, pl.ds(s, ph, stride=SUBLANES), :] = xn[s * ph:(s + 1) * ph, lanes]
        x_scr[c, tm:tm + HALO, :] = xp[:, lanes]
        x_scr[c, tm + HALO:tm + 2 * HALO, :] = xq[:, lanes]
    x = jnp.concatenate([x_scr[c] for c in range(nslab)], axis=1).astype(BF16)
    sub = lax.broadcasted_iota(jnp.int32, (SUBLANES, FFN_TILE), 0)
    assert cw_ref.shape[0] == 3
    for k in range(D_FF // FFN_TILE):
        halves = []
        for part in range(2):
            lo = part * D_FF + k * FFN_TILE
            slot = 2 * k + part
            u_scr[slot] = _dot(x, wu_ref[:, lo:lo + FFN_TILE])
            head = jnp.where(sub == 0, pltpu.roll(u_scr[slot, tm:tm + HALO, :], 1, 0),
                             pltpu.roll(u_scr[slot, tm - SUBLANES:tm, :], 1, 0))
            tail = jnp.where(sub == SUBLANES - 1,
                             pltpu.roll(u_scr[slot, tm + HALO:tm + 2 * HALO, :], SUBLANES - 1, 0),
                             pltpu.roll(u_scr[slot, 0:SUBLANES, :], SUBLANES - 1, 0))
            prv = jnp.concatenate([head, u_scr[slot, 0:tm - SUBLANES, :]], axis=0)
            nxt = jnp.concatenate([u_scr[slot, SUBLANES:tm, :], tail], axis=0)
            cols = slice(lo, lo + FFN_TILE)
            halves.append(cb_ref[:, cols] + prv * cw_ref[0:1, cols] + u_scr[slot, 0:tm, :] * cw_ref[1:2, cols]
                          + nxt * cw_ref[2:3, cols])
        gated_scr[:, k * FFN_TILE:(k + 1) * FFN_TILE] = (_gelu(halves[0]) * halves[1]).astype(BF16)
    y = _dot(gated_scr[...], wd_ref[...])
    for c in range(nslab):
        y_scr[c] = y[:, c * LANES:(c + 1) * LANES]
    for s in range(SUBLANES):
        for c in range(nslab):
            rows = slice(s * ph, (s + 1) * ph)
            lanes = slice(c * LANES, (c + 1) * LANES)
            o_ref[rows, lanes] = h_ref[rows, lanes] + y_scr[c, pl.ds(s, ph, stride=SUBLANES), :]
    if final:
        out = o_ref[...]
        o_ref[...] = out * lax.rsqrt(jnp.mean(out * out, axis=-1, keepdims=True) + RMS_EPS) * fg_ref[...]


def _resident(arr, *lead):
    rest = arr.shape[len(lead):]
    zeros = (0,) * len(rest)
    return pl.BlockSpec((None,) * len(lead) + rest, lambda *_: tuple(lead) + zeros,
                        pipeline_mode=pl.Buffered(1))


def _ffn(h, g, w_up, conv_w, conv_b, w_down, final_g, l, seq, tm, final):
    n, d = h.shape
    main, prev, nxt = _halo_specs(tm, n)
    return pl.pallas_call(
        functools.partial(_ffn_kernel, tiles_per_seq=seq // tm, tm=tm, final=final),
        grid=(n // tm,),
        in_specs=[pl.BlockSpec((tm, d), main), pl.BlockSpec((HALO, d), prev), pl.BlockSpec((HALO, d), nxt),
                  _layer_spec(g, l), _resident(w_up, l), _layer_spec(conv_w, l), _layer_spec(conv_b, l),
                  _resident(w_down, l), _full(final_g.shape)],
        out_specs=pl.BlockSpec((tm, d), lambda i: (i, 0)),
        out_shape=jax.ShapeDtypeStruct((n, d), F32),
        scratch_shapes=[pltpu.VMEM((d // LANES, tm + 2 * HALO, LANES), F32),
                        pltpu.VMEM((2 * (D_FF // FFN_TILE), tm + 2 * HALO, FFN_TILE), F32),
                        pltpu.VMEM((tm, D_FF), BF16),
                        pltpu.VMEM((d // LANES, tm, LANES), F32)],
        compiler_params=_params(("parallel",)),
        name="ffn_final" if final else "ffn",
    )(h, h, h, g, w_up, conv_w, conv_b, w_down, final_g)


def _row_vec(a):
    return a.astype(F32)[:, None, :]


def _trunk(x, p, *, tm, t_scan, t_s5, tf):
    nb, seq, d = x.shape
    n = nb * seq
    depth = p["w_in"].shape[0]
    plan = _FftPlan(seq)

    proj_c = (_row_vec(p["norm1_g"]), p["w_in"].astype(BF16), p["rg_conv_w"].astype(F32),
              _row_vec(p["rg_conv_b"]), p["hy_conv_w"].astype(F32), _row_vec(p["hy_conv_b"]))
    rg_w, rg_bias, rg_sp = _rglru_weights(p)
    s5_w = _s5_weights(p)
    hy_fw = _hyena_filter_weights(p)
    mix_c = (_row_vec(p["s5_d"]), p["s5_glu_w"].astype(BF16), _row_vec(p["s5_glu_b"]),
             _row_vec(p["hy_bias"]), _row_vec(p["mix_norm_g"]), p["w_out"].astype(BF16))
    ffn_c = (_row_vec(p["norm2_g"]), p["w_up"].astype(BF16), p["ffn_conv_w"].astype(F32),
             _row_vec(p["ffn_conv_b"]), p["w_down"].astype(BF16), p["final_norm_g"].astype(F32)[None])

    h = x.astype(F32).reshape(n, d)
    for l in range(depth):
        urg, ga, ub, x0, z = _proj(h, *proj_c, l, seq, tm)
        hs = [a.reshape(n, RG_WIDTH)
              for a in _rglru(urg.reshape(nb, seq, RG_WIDTH), rg_w, rg_bias, rg_sp, l, t_scan)]
        yf, yb = _s5(ub.reshape(nb, seq, S5_WIDTH), *s5_w, l, t_s5)
        filt = _hyena_filters(seq, hy_fw, l, tf)
        yc = _hyena_conv(z.reshape(nb, seq, HY_WIDTH), filt, plan).reshape(n, HY_WIDTH)
        rows = [h, hs[0], hs[1], ga, yf.reshape(n, S5_WIDTH), yb.reshape(n, S5_WIDTH), ub, yc, z, x0]
        h = _mix(rows, mix_c, l, tm)
        h = _ffn(h, *ffn_c, l, seq, tm, final=(l == depth - 1))
    return h.reshape(nb, seq, d).astype(x.dtype)


def kernel(x, norm1_g, w_in, rg_conv_w, rg_conv_b, rg_wa, rg_ba, rg_wx, rg_bx, rg_lambda, s5_a_re, s5_a_im, s5_log_dt, s5_b_re, s5_b_im, s5_c_re, s5_c_im, s5_d, s5_glu_w, s5_glu_b, hy_conv_w, hy_conv_b, hy_filt_w1, hy_filt_b1, hy_filt_freq1, hy_filt_w2, hy_filt_b2, hy_filt_freq2, hy_filt_w3, hy_bias, mix_norm_g, w_out, norm2_g, w_up, ffn_conv_w, ffn_conv_b, w_down, final_norm_g):
    p = dict(norm1_g=norm1_g, w_in=w_in, rg_conv_w=rg_conv_w, rg_conv_b=rg_conv_b, rg_wa=rg_wa, rg_ba=rg_ba,
             rg_wx=rg_wx, rg_bx=rg_bx, rg_lambda=rg_lambda, s5_a_re=s5_a_re, s5_a_im=s5_a_im,
             s5_log_dt=s5_log_dt, s5_b_re=s5_b_re, s5_b_im=s5_b_im, s5_c_re=s5_c_re, s5_c_im=s5_c_im,
             s5_d=s5_d, s5_glu_w=s5_glu_w, s5_glu_b=s5_glu_b, hy_conv_w=hy_conv_w, hy_conv_b=hy_conv_b,
             hy_filt_w1=hy_filt_w1, hy_filt_b1=hy_filt_b1, hy_filt_freq1=hy_filt_freq1, hy_filt_w2=hy_filt_w2,
             hy_filt_b2=hy_filt_b2, hy_filt_freq2=hy_filt_freq2, hy_filt_w3=hy_filt_w3, hy_bias=hy_bias,
             mix_norm_g=mix_norm_g, w_out=w_out, norm2_g=norm2_g, w_up=w_up, ffn_conv_w=ffn_conv_w,
             ffn_conv_b=ffn_conv_b, w_down=w_down, final_norm_g=final_norm_g)
    return _trunk(x, p, tm=512, t_scan=256, t_s5=128, tf=512)
```

```python
import functools
import math

import numpy as np
import jax
import jax.numpy as jnp
from jax import lax
from jax.experimental import pallas as pl
from jax.experimental.pallas import tpu as pltpu

F32 = jnp.float32
BF16 = jnp.bfloat16
STREAM = jnp.bfloat16

RMS_EPS = 1e-6
RG_WIDTH = 384
RG_HEADS = 6
RG_C = 8.0
S5_WIDTH = 384
S5_GROUP = 16
S5_GROUPS = 24
S5_STATE = 64
S5_NSTATE = S5_GROUPS * S5_STATE
HY_WIDTH = 256
HY_BANDS = 16
HY_FAST_DECAY = 0.3
HY_SLOW_DECAY = 1.5
HY_DECAY_TARGET = 1e-2
D_FF = 2816
N_DIR = 2

LANES = 128
SUBLANES = 8
HALO = SUBLANES
VMEM_LIMIT = 56 * 1024 * 1024

HIGHEST = lax.Precision.HIGHEST


def _dot(a, b, precision=None):
    return jnp.dot(a, b, preferred_element_type=F32, precision=precision)


def _gelu(x):
    c = math.sqrt(2.0 / math.pi)
    return 0.5 * x * (1.0 + jnp.tanh(c * (x + 0.044715 * (x * x * x))))


def _sigmoid(x):
    return 0.5 * jnp.tanh(0.5 * x) + 0.5


def _rms_nogain(x):
    return x * lax.rsqrt(jnp.mean(x * x, axis=-1, keepdims=True) + RMS_EPS)


def _params(sem):
    return pltpu.CompilerParams(dimension_semantics=sem, vmem_limit_bytes=VMEM_LIMIT)


def _full(shape):
    nd = len(shape)
    return pl.BlockSpec(shape, lambda *_: (0,) * nd)


def _layer_spec(arr, *lead):
    rest = arr.shape[len(lead):]
    zeros = (0,) * len(rest)
    return pl.BlockSpec((None,) * len(lead) + rest, lambda *_: tuple(lead) + zeros)


def _row_spec(seq, tm, c):
    tps = seq // tm
    return pl.BlockSpec((None, tm, c), lambda i: (i // tps, i % tps, 0))


def _halo_specs(seq, tm, c):
    tps = seq // tm
    bpt = tm // HALO
    prev = pl.BlockSpec((None, HALO, c), lambda i: (i // tps, jnp.maximum((i % tps) * bpt - 1, 0), 0))
    nxt = pl.BlockSpec((None, HALO, c),
                       lambda i: (i // tps, jnp.minimum((i % tps + 1) * bpt, seq // HALO - 1), 0))
    return prev, nxt


def _fill_normed(x_scr, h_ref, hp_ref, hn_ref, g, first, last, tm):
    def norm(x):
        return x * lax.rsqrt(jnp.mean(x * x, axis=-1, keepdims=True) + RMS_EPS) * g

    x_scr[0:HALO, :] = norm(jnp.where(first, 0.0, hp_ref[...]))
    x_scr[HALO:HALO + tm, :] = norm(h_ref[...])
    x_scr[HALO + tm:2 * HALO + tm, :] = norm(jnp.where(last, 0.0, hn_ref[...]))


def _proj_kernel(h_ref, hp_ref, hn_ref, g_ref, w_ref, rcw_ref, rcb_ref, hcw_ref, hcb_ref,
                 urg_ref, ga_ref, ub_ref, x0_ref, z_ref, x_scr, u_scr, *, tiles_per_seq, tm):
    i = pl.program_id(0)
    first = (i % tiles_per_seq) == 0
    last = (i % tiles_per_seq) == tiles_per_seq - 1
    _fill_normed(x_scr, h_ref, hp_ref, hn_ref, g_ref[...], first, last, tm)
    a0, a1, a2, a3, a4 = 0, RG_WIDTH, 2 * RG_WIDTH, 2 * RG_WIDTH + S5_WIDTH, w_ref.shape[1]
    x = x_scr[...].astype(BF16)
    for lo, hi in ((a0, a1), (a3, a4), (a1, a3)):
        u_scr[:, lo:hi] = _dot(x, w_ref[:, lo:hi])
    acc = rcb_ref[...]
    for j in range(rcw_ref.shape[0]):
        acc = acc + u_scr[pl.ds(HALO + j - 1, tm), a0:a1] * rcw_ref[j:j + 1, :]
    urg_ref[...] = acc
    ga_ref[...] = u_scr[pl.ds(HALO, tm), a1:a2].astype(ga_ref.dtype)
    ub_ref[...] = u_scr[pl.ds(HALO, tm), a2:a3].astype(ub_ref.dtype)
    q = []
    for part in range(3):
        lo = part * HY_WIDTH
        acc = hcb_ref[:, lo:lo + HY_WIDTH]
        for j in range(hcw_ref.shape[0]):
            acc = acc + (u_scr[pl.ds(HALO + j - 1, tm), a3 + lo:a3 + lo + HY_WIDTH]
                         * hcw_ref[j:j + 1, lo:lo + HY_WIDTH])
        q.append(acc)
    x0_ref[...] = q[0].astype(x0_ref.dtype)
    z_ref[...] = (q[2] * q[1]).astype(z_ref.dtype)


def _proj(h, g, w_in, rcw, rcb, hcw, hcb, l, tm):
    nb, seq, d = h.shape
    cols = w_in.shape[-1]
    prev, nxt = _halo_specs(seq, tm, d)
    row = lambda c: _row_spec(seq, tm, c)
    outs = [jax.ShapeDtypeStruct((nb, seq, c), dt) for c, dt in
            ((RG_WIDTH, F32), (RG_WIDTH, STREAM), (S5_WIDTH, STREAM), (HY_WIDTH, STREAM), (HY_WIDTH, STREAM))]
    consts = [g, w_in, rcw, rcb, hcw, hcb]
    return pl.pallas_call(
        functools.partial(_proj_kernel, tiles_per_seq=seq // tm, tm=tm),
        grid=(nb * seq // tm,),
        in_specs=[row(d), prev, nxt] + [_layer_spec(a, l) for a in consts],
        out_specs=[row(RG_WIDTH), row(RG_WIDTH), row(S5_WIDTH), row(HY_WIDTH), row(HY_WIDTH)],
        out_shape=outs,
        scratch_shapes=[pltpu.VMEM((tm + 2 * HALO, d), F32), pltpu.VMEM((tm + 2 * HALO, cols), F32)],
        compiler_params=_params(("parallel",)),
        name="proj",
    )(h, h, h, *consts)


RG_STAGES = 4


def _lo_mask():
    return lax.broadcasted_iota(jnp.int32, (SUBLANES, LANES), 0) < (SUBLANES // 2)


def _rglru_kernel(uf_ref, ub_ref, w_ref, bias_ref, sp_ref, of_ref, ob_ref,
                  af, ab, bf, bb, carry_scr, *, t, nb):
    i = pl.program_id(0)
    nm = RG_WIDTH // LANES

    @pl.when(i == 0)
    def _():
        carry_scr[...] = jnp.zeros_like(carry_scr)
        for scr in (af, ab, bf, bb):
            scr[...] = jnp.zeros_like(scr)

    tq = t // RG_STAGES
    for q in range(RG_STAGES):
        for dr, (u_ref, a_scr, b_scr) in enumerate(((uf_ref, af, bf), (ub_ref, ab, bb))):
            t0 = q * tq if dr == 0 else t - (q + 1) * tq
            u = u_ref[:, t0:t0 + tq, :].reshape(nb * tq, RG_WIDTH)
            gates = _dot(u.astype(BF16), w_ref[dr]) + bias_ref[dr]
            r = _sigmoid(gates[:, :RG_WIDTH])
            gi = _sigmoid(gates[:, RG_WIDTH:])
            log_a = (-RG_C) * r * sp_ref[dr]
            a = jnp.exp(log_a)
            m2 = -jnp.tanh(log_a) * (a * a + 1.0)
            mult = jnp.where(m2 > 0.0, m2 * lax.rsqrt(m2), 0.0)
            bin_ = mult * (gi * u)
            for b in range(nb):
                for m in range(nm):
                    rows = pl.ds(t0 * SUBLANES + dr * nb + b, tq, stride=SUBLANES)
                    a_scr[m, rows, :] = a[b * tq:(b + 1) * tq, m * LANES:(m + 1) * LANES]
                    b_scr[m, rows, :] = bin_[b * tq:(b + 1) * tq, m * LANES:(m + 1) * LANES]

    lo = _lo_mask()
    carry = [carry_scr[m] for m in range(nm)]
    for j in range(t):
        row = j * SUBLANES
        mrow = (t - 1 - j) * SUBLANES
        for m in range(nm):
            at = jnp.where(lo, af[m, row:row + SUBLANES, :], ab[m, mrow:mrow + SUBLANES, :])
            bt = jnp.where(lo, bf[m, row:row + SUBLANES, :], bb[m, mrow:mrow + SUBLANES, :])
            h = at * carry[m] + bt
            bf[m, row:row + SUBLANES, :] = h
            bb[m, mrow:mrow + SUBLANES, :] = h
            carry[m] = h
    for m in range(nm):
        carry_scr[m] = carry[m]
    for b in range(nb):
        for m in range(nm):
            of_ref[b, :, m * LANES:(m + 1) * LANES] = bf[m, pl.ds(b, t, stride=SUBLANES), :].astype(STREAM)
            ob_ref[b, :, m * LANES:(m + 1) * LANES] = bb[m, pl.ds(nb + b, t, stride=SUBLANES), :].astype(STREAM)


def _rglru(u3, w, bias, sp, l, t):
    nb, seq, c = u3.shape
    assert 2 * nb == SUBLANES
    nchunk = seq // t
    fwd = lambda i: (0, i, 0)
    bwd = lambda i: (0, nchunk - 1 - i, 0)
    nm = c // LANES
    out = jax.ShapeDtypeStruct((nb, seq, c), STREAM)
    scr = pltpu.VMEM((nm, SUBLANES * t, LANES), F32)
    return pl.pallas_call(
        functools.partial(_rglru_kernel, t=t, nb=nb),
        grid=(nchunk,),
        in_specs=[pl.BlockSpec((nb, t, c), fwd), pl.BlockSpec((nb, t, c), bwd),
                  _layer_spec(w, l), _layer_spec(bias, l), _layer_spec(sp, l)],
        out_specs=[pl.BlockSpec((nb, t, c), fwd), pl.BlockSpec((nb, t, c), bwd)],
        out_shape=[out, out],
        scratch_shapes=[scr, scr, scr, scr, pltpu.VMEM((nm, SUBLANES, LANES), F32)],
        compiler_params=_params(("arbitrary",)),
        name="rglru",
    )(u3, u3, w, bias, sp)


def _rglru_weights(p):
    eye = jnp.eye(RG_HEADS, dtype=F32)
    both = jnp.stack([p["rg_wa"], p["rg_wx"]], axis=2).astype(F32)
    w = jnp.einsum('ldqhij,hk->ldhiqkj', both, eye)
    depth = w.shape[0]
    w = w.reshape(depth, N_DIR, RG_WIDTH, 2 * RG_WIDTH).astype(BF16)
    bias = jnp.concatenate([p["rg_ba"], p["rg_bx"]], axis=-1).astype(F32)[:, :, None, :]
    x = -p["rg_lambda"].astype(F32)
    sp = (jnp.maximum(x, 0.0) + jnp.log1p(jnp.exp(-jnp.abs(x))))[:, :, None, :]
    return w, bias, sp


S5_NBLK = S5_NSTATE // LANES
S5_GRP = 4
S5_PER = S5_NBLK // (S5_WIDTH // LANES)


def _reverse_tiles(src, dst, nm, ntile):
    for j in range(ntile):
        s = (ntile - 1 - j) * SUBLANES
        for m in range(nm):
            dst[m, j * SUBLANES:(j + 1) * SUBLANES, :] = src[m, s:s + SUBLANES, :]


def _s5_kernel(uf_ref, ub_ref, wb_ref, wc_ref, ar_ref, ai_ref, yf_ref, yb_ref,
               uf8, ub8, ubr, bu, ym, ymr, carry_scr, *, t, nb):
    i = pl.program_id(0)
    nm = S5_WIDTH // LANES

    @pl.when(i == 0)
    def _():
        carry_scr[...] = jnp.zeros_like(carry_scr)
        uf8[...] = jnp.zeros_like(uf8)
        ub8[...] = jnp.zeros_like(ub8)

    for b in range(nb):
        for m in range(nm):
            uf8[m, pl.ds(b, t, stride=SUBLANES), :] = uf_ref[b, :, m * LANES:(m + 1) * LANES].astype(F32)
            ub8[m, pl.ds(nb + b, t, stride=SUBLANES), :] = ub_ref[b, :, m * LANES:(m + 1) * LANES].astype(F32)
    _reverse_tiles(ub8, ubr, nm, t)

    for m in range(nm):
        lhs = jnp.concatenate([uf8[m], ubr[m]], axis=1).astype(BF16)
        res = _dot(lhs, wb_ref[m])
        for q in range(2 * S5_PER):
            bu[2 * S5_PER * m + q] = res[:, q * LANES:(q + 1) * LANES]

    for g in range(S5_NBLK // S5_GRP):
        blocks = list(range(g * S5_GRP, (g + 1) * S5_GRP))
        ars = [ar_ref[n] for n in blocks]
        ais = [ai_ref[n] for n in blocks]

        def body(j, carry, blocks=blocks, ars=ars, ais=ais):
            row = j * SUBLANES
            new = []
            for k, n in enumerate(blocks):
                hr, hi = carry[2 * k], carry[2 * k + 1]
                nr = ars[k] * hr - ais[k] * hi + bu[2 * n, pl.ds(row, SUBLANES), :]
                ni = ars[k] * hi + ais[k] * hr + bu[2 * n + 1, pl.ds(row, SUBLANES), :]
                bu[2 * n, pl.ds(row, SUBLANES), :] = nr
                bu[2 * n + 1, pl.ds(row, SUBLANES), :] = ni
                new += [nr, ni]
            return tuple(new)

        init = []
        for n in blocks:
            init += [carry_scr[2 * n], carry_scr[2 * n + 1]]
        carry = tuple(init)
        for j in range(t):
            carry = body(j, carry)
        for k, n in enumerate(blocks):
            carry_scr[2 * n] = carry[2 * k]
            carry_scr[2 * n + 1] = carry[2 * k + 1]

    rows = lax.broadcasted_iota(jnp.int32, (SUBLANES * t, LANES), 0)
    fwd_row = (rows & (SUBLANES - 1)) < nb
    for m in range(nm):
        hcat = jnp.concatenate([bu[2 * S5_PER * m + q] for q in range(2 * S5_PER)], axis=1).astype(BF16)
        acc = _dot(hcat, wc_ref[m])
        ym[m] = jnp.where(fwd_row, acc[:, :LANES], acc[:, LANES:])
    _reverse_tiles(ym, ymr, nm, t)
    for b in range(nb):
        for m in range(nm):
            yf_ref[b, :, m * LANES:(m + 1) * LANES] = ym[m, pl.ds(b, t, stride=SUBLANES), :].astype(STREAM)
            yb_ref[b, :, m * LANES:(m + 1) * LANES] = ymr[m, pl.ds(nb + b, t, stride=SUBLANES), :].astype(STREAM)


def _s5(u3, wb, wc, ar, ai, l, t):
    nb, seq, c = u3.shape
    assert 2 * nb == SUBLANES
    nchunk = seq // t
    fwd = lambda i: (0, i, 0)
    bwd = lambda i: (0, nchunk - 1 - i, 0)
    nm = c // LANES
    rows = SUBLANES * t
    out = jax.ShapeDtypeStruct((nb, seq, c), STREAM)
    return pl.pallas_call(
        functools.partial(_s5_kernel, t=t, nb=nb),
        grid=(nchunk,),
        in_specs=[pl.BlockSpec((nb, t, c), fwd), pl.BlockSpec((nb, t, c), bwd),
                  _layer_spec(wb, l), _layer_spec(wc, l), _layer_spec(ar, l), _layer_spec(ai, l)],
        out_specs=[pl.BlockSpec((nb, t, c), fwd), pl.BlockSpec((nb, t, c), bwd)],
        out_shape=[out, out],
        scratch_shapes=[pltpu.VMEM((nm, rows, LANES), F32), pltpu.VMEM((nm, rows, LANES), F32),
                        pltpu.VMEM((nm, rows, LANES), F32),
                        pltpu.VMEM((2 * S5_NBLK, rows, LANES), F32),
                        pltpu.VMEM((nm, rows, LANES), F32), pltpu.VMEM((nm, rows, LANES), F32),
                        pltpu.VMEM((2 * S5_NBLK, SUBLANES, LANES), F32)],
        compiler_params=_params(("arbitrary",)),
        name="s5",
    )(u3, u3, wb, wc, ar, ai)


def _s5_group_mask():
    mask = np.zeros((S5_NBLK, LANES // S5_GROUP, LANES // S5_STATE), np.float32)
    for n in range(S5_NBLK):
        for s in range(LANES // S5_STATE):
            mask[n, (LANES // S5_STATE) * (n % S5_PER) + s, s] = 1.0
    return mask


def _s5_weights(p):
    lr = p["s5_a_re"].astype(F32)
    li = p["s5_a_im"].astype(F32)
    dt = jnp.exp(p["s5_log_dt"].astype(F32))[..., None]
    mag = jnp.exp(lr * dt)
    abar_r = mag * jnp.cos(li * dt)
    abar_i = mag * jnp.sin(li * dt)
    den = lr * lr + li * li
    nr = abar_r - 1.0
    ni = abar_i
    coef_r = ((nr * lr + ni * li) / den)[..., None]
    coef_i = ((ni * lr - nr * li) / den)[..., None]
    b_re = p["s5_b_re"].astype(F32)
    b_im = p["s5_b_im"].astype(F32)
    bbar = jnp.stack([coef_r * b_re - coef_i * b_im, coef_r * b_im + coef_i * b_re], axis=2)
    depth = lr.shape[0]
    half = LANES // S5_STATE
    mask = jnp.asarray(_s5_group_mask())
    bb = bbar.reshape(depth, N_DIR, 2, S5_NBLK, half, S5_STATE, S5_GROUP)
    nm = S5_WIDTH // LANES
    wb = jnp.einsum('ldrnspc,nks->lndkcrsp', bb, mask).reshape(depth, nm, S5_PER, 2 * LANES, 2 * LANES)
    wb = wb.transpose(0, 1, 3, 2, 4).reshape(depth, nm, 2 * LANES, S5_PER * 2 * LANES)
    cc = jnp.stack([p["s5_c_re"].astype(F32), -p["s5_c_im"].astype(F32)], axis=2)
    cc = cc.reshape(depth, N_DIR, 2, S5_NBLK, half, S5_GROUP, S5_STATE)
    wc = jnp.einsum('ldrnscp,nks->lnrspdkc', cc, mask).reshape(depth, nm, S5_PER * 2 * LANES, 2 * LANES)

    def tile_rows(a):
        a = a.reshape(depth, N_DIR, S5_NBLK, LANES).transpose(0, 2, 1, 3)
        return jnp.repeat(a, SUBLANES // N_DIR, axis=2)

    return wb.astype(BF16), wc.astype(BF16), tile_rows(abar_r), tile_rows(abar_i)


def _odd8(n):
    p = -(-n // SUBLANES)
    if p % 2 == 0:
        p += 1
    return p * SUBLANES


class _FftPlan:
    def __init__(self, seq):
        self.seq = seq
        r = 1
        while r * r < seq:
            r *= 2
        self.r = r
        self.nq = seq // r
        self.q = 2 * seq // r
        self.k1n = self.q // 2 + 1
        self.kp = -(-self.k1n // SUBLANES) * SUBLANES
        self.zpitch = _odd8(r)
        self.apitch = _odd8(2 * self.kp)
        self.cpitch = _odd8(2 * r)
        self.unroll = min(8, r)
        self.unroll2 = next(u for u in (13, 5, 4, 3, 2, 1) if self.k1n % u == 0)
        p = 2 * seq
        n1 = np.arange(self.nq)[None, :]
        k1 = np.arange(self.k1n)[:, None]
        ang = 2.0 * np.pi * n1 * k1 / self.q
        f1 = np.zeros((2 * self.kp, self.nq))
        f1[:self.k1n] = np.cos(ang)
        f1[self.kp:self.kp + self.k1n] = -np.sin(ang)
        self.f1 = f1
        w = np.full((self.k1n,), 2.0)
        w[0] = 1.0
        w[-1] = 1.0
        g1 = np.zeros((self.nq, 2 * self.kp))
        g1[:, :self.k1n] = (np.cos(ang) * w[:, None] / p).T
        g1[:, self.kp:self.kp + self.k1n] = (-np.sin(ang) * w[:, None] / p).T
        self.g1 = g1
        kk = np.arange(self.k1n)[:, None, None]
        k2 = np.arange(r)[None, :, None]
        n2 = np.arange(r)[None, None, :]
        ph = 2.0 * np.pi * (n2 * k2 / r + n2 * kk / p)
        tr, ti = np.cos(ph), -np.sin(ph)
        self.m2 = np.concatenate([np.concatenate([tr, -ti], axis=2),
                                  np.concatenate([ti, tr], axis=2)], axis=1)
        ur, ui = np.transpose(tr, (0, 2, 1)), -np.transpose(ti, (0, 2, 1))
        self.m2i = np.concatenate([np.concatenate([ur, -ui], axis=2),
                                   np.concatenate([ui, ur], axis=2)], axis=1)


def _fft_forward(plan, src_ref, f1_ref, m2_ref, zp, as_, emit):
    r, nq, kp = plan.r, plan.nq, plan.kp
    for n1 in range(nq):
        zp[n1 * plan.zpitch:n1 * plan.zpitch + r, :] = src_ref[n1 * r:(n1 + 1) * r, :].astype(F32)

    def stage1(n2, c):
        slab = zp[pl.ds(n2, nq, stride=plan.zpitch), :]
        a = _dot(f1_ref[...].astype(BF16), slab.astype(BF16))
        as_[pl.ds(pl.multiple_of(n2 * plan.apitch, SUBLANES), 2 * kp), :] = a
        return c

    lax.fori_loop(0, r, stage1, 0, unroll=plan.unroll)

    def stage2(k1, c):
        sr = as_[pl.ds(k1, r, stride=plan.apitch), :]
        si = as_[pl.ds(kp + k1, r, stride=plan.apitch), :]
        s = jnp.concatenate([sr, si], axis=0).astype(BF16)
        emit(k1, _dot(m2_ref[k1].astype(BF16), s))
        return c

    lax.fori_loop(0, plan.k1n, stage2, 0, unroll=plan.unroll2)


def _hy_spec_kernel(kf_ref, kb_ref, f1_ref, m2_ref, o_ref, zp, as_, *, plan):
    r = plan.r

    def emit_f(k1, x):
        o_ref[0, k1] = x

    def emit_b(k1, x):
        sign = jnp.where(lax.broadcasted_iota(jnp.int32, (2 * r, LANES), 0) < r, 1.0, -1.0)
        o_ref[0, k1] = o_ref[0, k1] + sign * x

    _fft_forward(plan, kf_ref, f1_ref, m2_ref, zp, as_, emit_f)
    _fft_forward(plan, kb_ref, f1_ref, m2_ref, zp, as_, emit_b)


def _hy_conv_kernel(z_ref, spec_ref, f1_ref, m2_ref, m2i_ref, g1_ref, o_ref, zp, as_, cs, *, plan):
    r, nq, kp, k1n = plan.r, plan.nq, plan.kp, plan.k1n
    if kp > k1n:
        cs[k1n * plan.cpitch:kp * plan.cpitch, :] = jnp.zeros(((kp - k1n) * plan.cpitch, LANES), F32)

    def emit(k1, x):
        kf = spec_ref[0, k1]
        xr, xi = x[:r], x[r:]
        kr, ki = kf[:r], kf[r:]
        prod = jnp.concatenate([xr * kr - xi * ki, xr * ki + xi * kr], axis=0).astype(BF16)
        c = _dot(m2i_ref[k1].astype(BF16), prod)
        cs[pl.ds(pl.multiple_of(k1 * plan.cpitch, SUBLANES), 2 * r), :] = c

    _fft_forward(plan, z_ref.at[0], f1_ref, m2_ref, zp, as_, emit)

    def stage3(n2, c):
        cr = cs[pl.ds(n2, kp, stride=plan.cpitch), :]
        ci = cs[pl.ds(r + n2, kp, stride=plan.cpitch), :]
        y = _dot(g1_ref[...].astype(BF16), jnp.concatenate([cr, ci], axis=0).astype(BF16))
        zp[pl.ds(n2, nq, stride=plan.zpitch), :] = y
        return c

    lax.fori_loop(0, r, stage3, 0, unroll=plan.unroll)
    for n1 in range(nq):
        o_ref[0, n1 * r:(n1 + 1) * r, :] = zp[n1 * plan.zpitch:n1 * plan.zpitch + r, :].astype(o_ref.dtype)


def _hy_filter_kernel(feat_ref, w1_ref, b1_ref, f1_ref, w2_ref, b2_ref, f2_ref, w3_ref, dl_ref, o_ref):
    feats = feat_ref[...]
    half = feats.shape[0] // 2
    cols = o_ref.shape[1]
    both = jnp.concatenate([feats[:half], feats[half:]], axis=1)
    hid = jnp.sin(f1_ref[...] * (_dot(both, w1_ref[...], HIGHEST) + b1_ref[...]))
    hid = jnp.sin(f2_ref[...] * (_dot(hid, w2_ref[...], HIGHEST) + b2_ref[...]))
    k = _dot(hid, w3_ref[...], HIGHEST)
    o_ref[:half, :] = k[:, :cols] * jnp.exp(-(feats[:half, 0:1] * dl_ref[...]))
    o_ref[half:, :] = k[:, cols:] * jnp.exp(-(feats[half:, 0:1] * dl_ref[...]))


def _hyena_features(seq):
    pos = np.arange(seq, dtype=np.float64)
    t = pos / max(seq - 1, 1)
    w = (2.0 * math.pi / seq) * pos
    bands = np.linspace(1e-4, HY_BANDS - 1, HY_BANDS, dtype=np.float64)
    ang = w[:, None] * bands
    feats = np.concatenate([t[:, None], np.cos(ang), -np.sin(ang)], axis=-1).astype(np.float32)
    out = np.zeros((seq, LANES), np.float32)
    out[:, :feats.shape[1]] = feats
    return out


def _hyena_deltas():
    max_decay = math.log(HY_DECAY_TARGET) / HY_FAST_DECAY
    min_decay = math.log(HY_DECAY_TARGET) / HY_SLOW_DECAY
    deltas = np.abs(np.linspace(min_decay, max_decay, HY_WIDTH, dtype=np.float64))
    return np.tile(deltas, 2)[None, :].astype(np.float32)


def _pad_to(x, rows, cols):
    x = x.astype(F32)
    return jnp.pad(x, ((0, 0), (0, rows - x.shape[1]), (0, cols - x.shape[2])))


def _block_diag2(a):
    a = a.astype(F32)
    z = jnp.zeros_like(a)
    return jnp.concatenate([jnp.concatenate([a, z], axis=2), jnp.concatenate([z, a], axis=2)], axis=1)


def _hyena_filter_weights(p):
    hidden = p["hy_filt_w1"].shape[2]
    assert 2 * hidden == LANES
    vec = lambda a: jnp.tile(a.astype(F32)[:, None, :], (1, 1, 2))
    return (_block_diag2(_pad_to(p["hy_filt_w1"], LANES, hidden)), vec(p["hy_filt_b1"]),
            vec(p["hy_filt_freq1"]), _block_diag2(p["hy_filt_w2"]), vec(p["hy_filt_b2"]),
            vec(p["hy_filt_freq2"]), _block_diag2(p["hy_filt_w3"]))


def _hyena_filters(seq, fw, l, tf):
    feats = jnp.asarray(_hyena_features(seq))
    deltas = jnp.asarray(_hyena_deltas())
    return pl.pallas_call(
        _hy_filter_kernel,
        grid=(seq // tf,),
        in_specs=[pl.BlockSpec((tf, LANES), lambda i: (i, 0))] + [_layer_spec(a, l) for a in fw]
                 + [_full(deltas.shape)],
        out_specs=pl.BlockSpec((tf, 2 * HY_WIDTH), lambda i: (i, 0)),
        out_shape=jax.ShapeDtypeStruct((seq, 2 * HY_WIDTH), F32),
        compiler_params=_params(("parallel",)),
        name="hy_filter",
    )(feats, *fw, deltas)


def _hyena_conv(z3, filt, plan):
    nb, seq, c = z3.shape
    nh = c // LANES
    f1 = jnp.asarray(plan.f1, F32)
    m2 = jnp.asarray(plan.m2, F32)
    m2i = jnp.asarray(plan.m2i, F32)
    g1 = jnp.asarray(plan.g1, F32)
    r, kp, k1n = plan.r, plan.kp, plan.k1n
    zp_shape = (plan.nq * plan.zpitch, LANES)
    as_shape = (r * plan.apitch, LANES)
    cs_shape = (kp * plan.cpitch, LANES)
    spec = pl.pallas_call(
        functools.partial(_hy_spec_kernel, plan=plan),
        grid=(nh,),
        in_specs=[pl.BlockSpec((seq, LANES), lambda j: (0, j)),
                  pl.BlockSpec((seq, LANES), lambda j: (0, nh + j)),
                  _full(f1.shape), _full(m2.shape)],
        out_specs=pl.BlockSpec((1, k1n, 2 * r, LANES), lambda j: (j, 0, 0, 0)),
        out_shape=jax.ShapeDtypeStruct((nh, k1n, 2 * r, LANES), F32),
        scratch_shapes=[pltpu.VMEM(zp_shape, F32), pltpu.VMEM(as_shape, F32)],
        compiler_params=_params(("parallel",)),
        name="hy_spec",
    )(filt, filt, f1, m2)
    return pl.pallas_call(
        functools.partial(_hy_conv_kernel, plan=plan),
        grid=(nb, nh),
        in_specs=[pl.BlockSpec((1, seq, LANES), lambda b, j: (b, 0, j)),
                  pl.BlockSpec((1, k1n, 2 * r, LANES), lambda b, j: (j, 0, 0, 0)),
                  _full(f1.shape), _full(m2.shape), _full(m2i.shape), _full(g1.shape)],
        out_specs=pl.BlockSpec((1, seq, LANES), lambda b, j: (b, 0, j)),
        out_shape=jax.ShapeDtypeStruct((nb, seq, c), STREAM),
        scratch_shapes=[pltpu.VMEM(zp_shape, F32), pltpu.VMEM(as_shape, F32), pltpu.VMEM(cs_shape, F32)],
        compiler_params=_params(("parallel", "parallel")),
        name="hy_conv",
    )(z3, spec, f1, m2, m2i, g1)


def _mix_kernel(h_ref, hf_ref, hb_ref, ga_ref, yf_ref, yb_ref, ub_ref, yc_ref, z_ref, x0_ref,
                d_ref, gw_ref, gb_ref, hbias_ref, mg_ref, wo_ref, o_ref):
    f32 = lambda ref: ref[...].astype(F32)
    ya = (f32(hf_ref) + f32(hb_ref)) * _gelu(f32(ga_ref))
    yb = _gelu(f32(ub_ref) * d_ref[...] + f32(yf_ref) + f32(yb_ref))
    yb = yb * _sigmoid(_dot(yb.astype(BF16), gw_ref[...]) + gb_ref[...])
    yc = (f32(yc_ref) + f32(z_ref) * hbias_ref[...]) * f32(x0_ref)
    a1, a2 = RG_WIDTH, RG_WIDTH + S5_WIDTH
    na = (_rms_nogain(ya) * mg_ref[:, :a1]).astype(BF16)
    nb = (_rms_nogain(yb) * mg_ref[:, a1:a2]).astype(BF16)
    nc = (_rms_nogain(yc) * mg_ref[:, a2:]).astype(BF16)
    out = _dot(na, wo_ref[:a1, :]) + _dot(nb, wo_ref[a1:a2, :]) + _dot(nc, wo_ref[a2:, :])
    o_ref[...] = h_ref[...] + out


def _mix(rows, consts, l, tm):
    nb, seq, dm = rows[0].shape
    return pl.pallas_call(
        _mix_kernel,
        grid=(nb * seq // tm,),
        in_specs=[_row_spec(seq, tm, a.shape[2]) for a in rows] + [_layer_spec(a, l) for a in consts],
        out_specs=_row_spec(seq, tm, dm),
        out_shape=jax.ShapeDtypeStruct((nb, seq, dm), F32),
        compiler_params=_params(("parallel",)),
        name="mix",
    )(*rows, *consts)


FFN_TILE = 256


def _ffn_kernel(h_ref, hp_ref, hn_ref, g_ref, wu_ref, cw_ref, cb_ref, wd_ref, fg_ref, o_ref,
                x_scr, u_scr, gated_scr, y_scr, *, tiles_per_seq, tm, final):
    i = pl.program_id(0)
    first = (i % tiles_per_seq) == 0
    last = (i % tiles_per_seq) == tiles_per_seq - 1
    nslab = h_ref.shape[1] // LANES
    ph = tm // SUBLANES
    g = g_ref[...]

    def norm(v):
        return v * lax.rsqrt(jnp.mean(v * v, axis=-1, keepdims=True) + RMS_EPS) * g

    xn = norm(h_ref[...])
    xp = norm(jnp.where(first, 0.0, hp_ref[...]))
    xq = norm(jnp.where(last, 0.0, hn_ref[...]))
    for c in range(nslab):
        lanes = slice(c * LANES, (c + 1) * LANES)
        for s in range(SUBLANES):
            x_scr[c, pl.ds(s, ph, stride=SUBLANES), :] = xn[s * ph:(s + 1) * ph, lanes]
        x_scr[c, tm:tm + HALO, :] = xp[:, lanes]
        x_scr[c, tm + HALO:tm + 2 * HALO, :] = xq[:, lanes]
    x = jnp.concatenate([x_scr[c] for c in range(nslab)], axis=1).astype(BF16)
    sub = lax.broadcasted_iota(jnp.int32, (SUBLANES, FFN_TILE), 0)
    assert cw_ref.shape[0] == 3
    for k in range(D_FF // FFN_TILE):
        halves = []
        for part in range(2):
            lo = part * D_FF + k * FFN_TILE
            slot = 2 * k + part
            u_scr[slot] = _dot(x, wu_ref[:, lo:lo + FFN_TILE])
            head = jnp.where(sub == 0, pltpu.roll(u_scr[slot, tm:tm + HALO, :], 1, 0),
                             pltpu.roll(u_scr[slot, tm - SUBLANES:tm, :], 1, 0))
            tail = jnp.where(sub == SUBLANES - 1,
                             pltpu.roll(u_scr[slot, tm + HALO:tm + 2 * HALO, :], SUBLANES - 1, 0),
                             pltpu.roll(u_scr[slot, 0:SUBLANES, :], SUBLANES - 1, 0))
            prv = jnp.concatenate([head, u_scr[slot, 0:tm - SUBLANES, :]], axis=0)
            nxt = jnp.concatenate([u_scr[slot, SUBLANES:tm, :], tail], axis=0)
            cols = slice(lo, lo + FFN_TILE)
            halves.append(cb_ref[:, cols] + prv * cw_ref[0:1, cols] + u_scr[slot, 0:tm, :] * cw_ref[1:2, cols]
                          + nxt * cw_ref[2:3, cols])
        gated_scr[:, k * FFN_TILE:(k + 1) * FFN_TILE] = (_gelu(halves[0]) * halves[1]).astype(BF16)
    y = _dot(gated_scr[...], wd_ref[...])
    for c in range(nslab):
        y_scr[c] = y[:, c * LANES:(c + 1) * LANES]
    for s in range(SUBLANES):
        for c in range(nslab):
            rows = slice(s * ph, (s + 1) * ph)
            lanes = slice(c * LANES, (c + 1) * LANES)
            o_ref[rows, lanes] = h_ref[rows, lanes] + y_scr[c, pl.ds(s, ph, stride=SUBLANES), :]
    if final:
        out = o_ref[...]
        o_ref[...] = out * lax.rsqrt(jnp.mean(out * out, axis=-1, keepdims=True) + RMS_EPS) * fg_ref[...]


def _resident(arr, *lead):
    rest = arr.shape[len(lead):]
    zeros = (0,) * len(rest)
    return pl.BlockSpec((None,) * len(lead) + rest, lambda *_: tuple(lead) + zeros,
                        pipeline_mode=pl.Buffered(1))


def _ffn(h, g, w_up, conv_w, conv_b, w_down, final_g, l, tm, final):
    nb, seq, d = h.shape
    prev, nxt = _halo_specs(seq, tm, d)
    return pl.pallas_call(
        functools.partial(_ffn_kernel, tiles_per_seq=seq // tm, tm=tm, final=final),
        grid=(nb * seq // tm,),
        in_specs=[_row_spec(seq, tm, d), prev, nxt,
                  _layer_spec(g, l), _resident(w_up, l), _layer_spec(conv_w, l), _layer_spec(conv_b, l),
                  _resident(w_down, l), _full(final_g.shape)],
        out_specs=_row_spec(seq, tm, d),
        out_shape=jax.ShapeDtypeStruct((nb, seq, d), F32),
        scratch_shapes=[pltpu.VMEM((d // LANES, tm + 2 * HALO, LANES), F32),
                        pltpu.VMEM((2 * (D_FF // FFN_TILE), tm + 2 * HALO, FFN_TILE), F32),
                        pltpu.VMEM((tm, D_FF), BF16),
                        pltpu.VMEM((d // LANES, tm, LANES), F32)],
        compiler_params=_params(("parallel",)),
        name="ffn_final" if final else "ffn",
    )(h, h, h, g, w_up, conv_w, conv_b, w_down, final_g)


def _row_vec(a):
    return a.astype(F32)[:, None, :]


def _trunk(x, p, *, tm, t_scan, t_s5, tf):
    seq = x.shape[1]
    depth = p["w_in"].shape[0]
    plan = _FftPlan(seq)

    proj_c = (_row_vec(p["norm1_g"]), p["w_in"].astype(BF16), p["rg_conv_w"].astype(F32),
              _row_vec(p["rg_conv_b"]), p["hy_conv_w"].astype(F32), _row_vec(p["hy_conv_b"]))
    rg_w, rg_bias, rg_sp = _rglru_weights(p)
    s5_w = _s5_weights(p)
    hy_fw = _hyena_filter_weights(p)
    mix_c = (_row_vec(p["s5_d"]), p["s5_glu_w"].astype(BF16), _row_vec(p["s5_glu_b"]),
             _row_vec(p["hy_bias"]), _row_vec(p["mix_norm_g"]), p["w_out"].astype(BF16))
    ffn_c = (_row_vec(p["norm2_g"]), p["w_up"].astype(BF16), p["ffn_conv_w"].astype(F32),
             _row_vec(p["ffn_conv_b"]), p["w_down"].astype(BF16), p["final_norm_g"].astype(F32)[None])

    h = x.astype(F32)
    for l in range(depth):
        urg, ga, ub, x0, z = _proj(h, *proj_c, l, tm)
        hf, hb = _rglru(urg, rg_w, rg_bias, rg_sp, l, t_scan)
        yf, yb = _s5(ub, *s5_w, l, t_s5)
        filt = _hyena_filters(seq, hy_fw, l, tf)
        yc = _hyena_conv(z, filt, plan)
        h = _mix([h, hf, hb, ga, yf, yb, ub, yc, z, x0], mix_c, l, tm)
        h = _ffn(h, *ffn_c, l, tm, final=(l == depth - 1))
    return h.astype(x.dtype)


def kernel(x, norm1_g, w_in, rg_conv_w, rg_conv_b, rg_wa, rg_ba, rg_wx, rg_bx, rg_lambda, s5_a_re, s5_a_im, s5_log_dt, s5_b_re, s5_b_im, s5_c_re, s5_c_im, s5_d, s5_glu_w, s5_glu_b, hy_conv_w, hy_conv_b, hy_filt_w1, hy_filt_b1, hy_filt_freq1, hy_filt_w2, hy_filt_b2, hy_filt_freq2, hy_filt_w3, hy_bias, mix_norm_g, w_out, norm2_g, w_up, ffn_conv_w, ffn_conv_b, w_down, final_norm_g):
    p = dict(norm1_g=norm1_g, w_in=w_in, rg_conv_w=rg_conv_w, rg_conv_b=rg_conv_b, rg_wa=rg_wa, rg_ba=rg_ba,
             rg_wx=rg_wx, rg_bx=rg_bx, rg_lambda=rg_lambda, s5_a_re=s5_a_re, s5_a_im=s5_a_im,
             s5_log_dt=s5_log_dt, s5_b_re=s5_b_re, s5_b_im=s5_b_im, s5_c_re=s5_c_re, s5_c_im=s5_c_im,
             s5_d=s5_d, s5_glu_w=s5_glu_w, s5_glu_b=s5_glu_b, hy_conv_w=hy_conv_w, hy_conv_b=hy_conv_b,
             hy_filt_w1=hy_filt_w1, hy_filt_b1=hy_filt_b1, hy_filt_freq1=hy_filt_freq1, hy_filt_w2=hy_filt_w2,
             hy_filt_b2=hy_filt_b2, hy_filt_freq2=hy_filt_freq2, hy_filt_w3=hy_filt_w3, hy_bias=hy_bias,
             mix_norm_g=mix_norm_g, w_out=w_out, norm2_g=norm2_g, w_up=w_up, ffn_conv_w=ffn_conv_w,
             ffn_conv_b=ffn_conv_b, w_down=w_down, final_norm_g=final_norm_g)
    return _trunk(x, p, tm=512, t_scan=256, t_s5=128, tf=512)
```

```python
import functools
import math

import numpy as np
import jax
import jax.numpy as jnp
from jax import lax
from jax.experimental import pallas as pl
from jax.experimental.pallas import tpu as pltpu

F32 = jnp.float32
BF16 = jnp.bfloat16
STREAM = jnp.bfloat16

RMS_EPS = 1e-6
RG_WIDTH = 384
RG_HEADS = 6
RG_C = 8.0
S5_WIDTH = 384
S5_GROUP = 16
S5_GROUPS = 24
S5_STATE = 64
S5_NSTATE = S5_GROUPS * S5_STATE
HY_WIDTH = 256
HY_BANDS = 16
HY_FAST_DECAY = 0.3
HY_SLOW_DECAY = 1.5
HY_DECAY_TARGET = 1e-2
D_FF = 2816
N_DIR = 2

LANES = 128
SUBLANES = 8
HALO = SUBLANES
VMEM_LIMIT = 56 * 1024 * 1024

HIGHEST = lax.Precision.HIGHEST


def _dot(a, b, precision=None):
    return jnp.dot(a, b, preferred_element_type=F32, precision=precision)


def _gelu(x):
    c = math.sqrt(2.0 / math.pi)
    return 0.5 * x * (1.0 + jnp.tanh(c * (x + 0.044715 * (x * x * x))))


def _sigmoid(x):
    return 0.5 * jnp.tanh(0.5 * x) + 0.5


def _rms_nogain(x):
    return x * lax.rsqrt(jnp.mean(x * x, axis=-1, keepdims=True) + RMS_EPS)


def _params(sem):
    return pltpu.CompilerParams(dimension_semantics=sem, vmem_limit_bytes=VMEM_LIMIT)


def _full(shape):
    nd = len(shape)
    return pl.BlockSpec(shape, lambda *_: (0,) * nd)


def _layer_spec(arr, *lead):
    rest = arr.shape[len(lead):]
    zeros = (0,) * len(rest)
    return pl.BlockSpec((None,) * len(lead) + rest, lambda *_: tuple(lead) + zeros)


def _row_spec(seq, tm, c):
    tps = seq // tm
    return pl.BlockSpec((None, tm, c), lambda i: (i // tps, i % tps, 0))


def _halo_specs(seq, tm, c):
    tps = seq // tm
    bpt = tm // HALO
    prev = pl.BlockSpec((None, HALO, c), lambda i: (i // tps, jnp.maximum((i % tps) * bpt - 1, 0), 0))
    nxt = pl.BlockSpec((None, HALO, c),
                       lambda i: (i // tps, jnp.minimum((i % tps + 1) * bpt, seq // HALO - 1), 0))
    return prev, nxt


def _fill_normed(x_scr, h_ref, hp_ref, hn_ref, g, first, last, tm):
    def norm(x):
        return x * lax.rsqrt(jnp.mean(x * x, axis=-1, keepdims=True) + RMS_EPS) * g

    x_scr[0:HALO, :] = norm(jnp.where(first, 0.0, hp_ref[...]))
    x_scr[HALO:HALO + tm, :] = norm(h_ref[...])
    x_scr[HALO + tm:2 * HALO + tm, :] = norm(jnp.where(last, 0.0, hn_ref[...]))


def _proj_kernel(h_ref, hp_ref, hn_ref, g_ref, w_ref, rcw_ref, rcb_ref, hcw_ref, hcb_ref,
                 urg_ref, ga_ref, ub_ref, x0_ref, z_ref, x_scr, u_scr, *, tiles_per_seq, tm):
    i = pl.program_id(0)
    first = (i % tiles_per_seq) == 0
    last = (i % tiles_per_seq) == tiles_per_seq - 1
    _fill_normed(x_scr, h_ref, hp_ref, hn_ref, g_ref[...], first, last, tm)
    a0, a1, a2, a3, a4 = 0, RG_WIDTH, 2 * RG_WIDTH, 2 * RG_WIDTH + S5_WIDTH, w_ref.shape[1]
    x = x_scr[...].astype(BF16)
    for lo, hi in ((a0, a1), (a3, a4), (a1, a3)):
        u_scr[:, lo:hi] = _dot(x, w_ref[:, lo:hi])
    acc = rcb_ref[...]
    for j in range(rcw_ref.shape[0]):
        acc = acc + u_scr[pl.ds(HALO + j - 1, tm), a0:a1] * rcw_ref[j:j + 1, :]
    urg_ref[...] = acc
    ga_ref[...] = u_scr[pl.ds(HALO, tm), a1:a2].astype(ga_ref.dtype)
    ub_ref[...] = u_scr[pl.ds(HALO, tm), a2:a3].astype(ub_ref.dtype)
    q = []
    for part in range(3):
        lo = part * HY_WIDTH
        acc = hcb_ref[:, lo:lo + HY_WIDTH]
        for j in range(hcw_ref.shape[0]):
            acc = acc + (u_scr[pl.ds(HALO + j - 1, tm), a3 + lo:a3 + lo + HY_WIDTH]
                         * hcw_ref[j:j + 1, lo:lo + HY_WIDTH])
        q.append(acc)
    x0_ref[...] = q[0].astype(x0_ref.dtype)
    z_ref[...] = (q[2] * q[1]).astype(z_ref.dtype)


def _proj(h, g, w_in, rcw, rcb, hcw, hcb, l, tm):
    nb, seq, d = h.shape
    cols = w_in.shape[-1]
    prev, nxt = _halo_specs(seq, tm, d)
    row = lambda c: _row_spec(seq, tm, c)
    outs = [jax.ShapeDtypeStruct((nb, seq, c), dt) for c, dt in
            ((RG_WIDTH, F32), (RG_WIDTH, STREAM), (S5_WIDTH, STREAM), (HY_WIDTH, STREAM), (HY_WIDTH, STREAM))]
    consts = [g, w_in, rcw, rcb, hcw, hcb]
    return pl.pallas_call(
        functools.partial(_proj_kernel, tiles_per_seq=seq // tm, tm=tm),
        grid=(nb * seq // tm,),
        in_specs=[row(d), prev, nxt] + [_layer_spec(a, l) for a in consts],
        out_specs=[row(RG_WIDTH), row(RG_WIDTH), row(S5_WIDTH), row(HY_WIDTH), row(HY_WIDTH)],
        out_shape=outs,
        scratch_shapes=[pltpu.VMEM((tm + 2 * HALO, d), F32), pltpu.VMEM((tm + 2 * HALO, cols), F32)],
        compiler_params=_params(("parallel",)),
        name="proj",
    )(h, h, h, *consts)


RG_STAGES = 4


def _lo_mask():
    return lax.broadcasted_iota(jnp.int32, (SUBLANES, LANES), 0) < (SUBLANES // 2)


def _rglru_kernel(uf_ref, ub_ref, w_ref, bias_ref, sp_ref, of_ref, ob_ref,
                  af, ab, bf, bb, carry_scr, *, t, nb):
    i = pl.program_id(0)
    nm = RG_WIDTH // LANES

    @pl.when(i == 0)
    def _():
        carry_scr[...] = jnp.zeros_like(carry_scr)
        for scr in (af, ab, bf, bb):
            scr[...] = jnp.zeros_like(scr)

    tq = t // RG_STAGES
    for q in range(RG_STAGES):
        for dr, (u_ref, a_scr, b_scr) in enumerate(((uf_ref, af, bf), (ub_ref, ab, bb))):
            t0 = q * tq if dr == 0 else t - (q + 1) * tq
            u = u_ref[:, t0:t0 + tq, :].reshape(nb * tq, RG_WIDTH)
            gates = _dot(u.astype(BF16), w_ref[dr]) + bias_ref[dr]
            r = _sigmoid(gates[:, :RG_WIDTH])
            gi = _sigmoid(gates[:, RG_WIDTH:])
            log_a = (-RG_C) * r * sp_ref[dr]
            a = jnp.exp(log_a)
            m2 = -jnp.tanh(log_a) * (a * a + 1.0)
            mult = jnp.where(m2 > 0.0, m2 * lax.rsqrt(m2), 0.0)
            bin_ = mult * (gi * u)
            for b in range(nb):
                for m in range(nm):
                    rows = pl.ds(t0 * SUBLANES + dr * nb + b, tq, stride=SUBLANES)
                    a_scr[m, rows, :] = a[b * tq:(b + 1) * tq, m * LANES:(m + 1) * LANES]
                    b_scr[m, rows, :] = bin_[b * tq:(b + 1) * tq, m * LANES:(m + 1) * LANES]

    lo = _lo_mask()
    carry = [carry_scr[m] for m in range(nm)]
    for j in range(t):
        row = j * SUBLANES
        mrow = (t - 1 - j) * SUBLANES
        for m in range(nm):
            at = jnp.where(lo, af[m, row:row + SUBLANES, :], ab[m, mrow:mrow + SUBLANES, :])
            bt = jnp.where(lo, bf[m, row:row + SUBLANES, :], bb[m, mrow:mrow + SUBLANES, :])
            h = at * carry[m] + bt
            bf[m, row:row + SUBLANES, :] = h
            bb[m, mrow:mrow + SUBLANES, :] = h
            carry[m] = h
    for m in range(nm):
        carry_scr[m] = carry[m]
    for b in range(nb):
        for m in range(nm):
            of_ref[b, :, m * LANES:(m + 1) * LANES] = bf[m, pl.ds(b, t, stride=SUBLANES), :].astype(STREAM)
            ob_ref[b, :, m * LANES:(m + 1) * LANES] = bb[m, pl.ds(nb + b, t, stride=SUBLANES), :].astype(STREAM)


def _rglru_weights(p):
    hd = RG_WIDTH // RG_HEADS
    place = np.zeros((2, RG_HEADS, hd, 2 * RG_WIDTH), np.float32)
    for q in range(2):
        for h in range(RG_HEADS):
            place[q, h, np.arange(hd), q * RG_WIDTH + h * hd + np.arange(hd)] = 1.0
    both = jnp.stack([p["rg_wa"], p["rg_wx"]], axis=2).astype(F32)
    depth = both.shape[0]
    w = jnp.einsum('ldqhij,qhjc->ldhic', both, jnp.asarray(place))
    w = w.reshape(depth, N_DIR, RG_WIDTH, 2 * RG_WIDTH).astype(BF16)
    bias = jnp.concatenate([p["rg_ba"], p["rg_bx"]], axis=-1).astype(F32)[:, :, None, :]
    x = -p["rg_lambda"].astype(F32)
    sp = (jnp.maximum(x, 0.0) + jnp.log1p(jnp.exp(-jnp.abs(x))))[:, :, None, :]
    return w, bias, sp


S5_NBLK = S5_NSTATE // LANES
S5_GRP = 4
S5_PER = S5_NBLK // (S5_WIDTH // LANES)


def _reverse_tiles(src, dst, nm, ntile):
    for j in range(ntile):
        s = (ntile - 1 - j) * SUBLANES
        for m in range(nm):
            dst[m, j * SUBLANES:(j + 1) * SUBLANES, :] = src[m, s:s + SUBLANES, :]


def _s5_kernel(uf_ref, ub_ref, wb_ref, wc_ref, ar_ref, ai_ref, yf_ref, yb_ref,
               uf8, ub8, ubr, bu, ym, ymr, carry_scr, *, t, nb):
    i = pl.program_id(0)
    nm = S5_WIDTH // LANES

    @pl.when(i == 0)
    def _():
        carry_scr[...] = jnp.zeros_like(carry_scr)
        uf8[...] = jnp.zeros_like(uf8)
        ub8[...] = jnp.zeros_like(ub8)

    for b in range(nb):
        for m in range(nm):
            uf8[m, pl.ds(b, t, stride=SUBLANES), :] = uf_ref[b, :, m * LANES:(m + 1) * LANES].astype(F32)
            ub8[m, pl.ds(nb + b, t, stride=SUBLANES), :] = ub_ref[b, :, m * LANES:(m + 1) * LANES].astype(F32)
    _reverse_tiles(ub8, ubr, nm, t)

    for m in range(nm):
        lhs = jnp.concatenate([uf8[m], ubr[m]], axis=1).astype(BF16)
        res = _dot(lhs, wb_ref[m])
        for q in range(2 * S5_PER):
            bu[2 * S5_PER * m + q] = res[:, q * LANES:(q + 1) * LANES]

    for g in range(S5_NBLK // S5_GRP):
        blocks = list(range(g * S5_GRP, (g + 1) * S5_GRP))
        ars = [ar_ref[n] for n in blocks]
        ais = [ai_ref[n] for n in blocks]

        def body(j, carry, blocks=blocks, ars=ars, ais=ais):
            row = j * SUBLANES
            new = []
            for k, n in enumerate(blocks):
                hr, hi = carry[2 * k], carry[2 * k + 1]
                nr = ars[k] * hr - ais[k] * hi + bu[2 * n, pl.ds(row, SUBLANES), :]
                ni = ars[k] * hi + ais[k] * hr + bu[2 * n + 1, pl.ds(row, SUBLANES), :]
                bu[2 * n, pl.ds(row, SUBLANES), :] = nr
                bu[2 * n + 1, pl.ds(row, SUBLANES), :] = ni
                new += [nr, ni]
            return tuple(new)

        init = []
        for n in blocks:
            init += [carry_scr[2 * n], carry_scr[2 * n + 1]]
        carry = tuple(init)
        for j in range(t):
            carry = body(j, carry)
        for k, n in enumerate(blocks):
            carry_scr[2 * n] = carry[2 * k]
            carry_scr[2 * n + 1] = carry[2 * k + 1]

    rows = lax.broadcasted_iota(jnp.int32, (SUBLANES * t, LANES), 0)
    fwd_row = (rows & (SUBLANES - 1)) < nb
    for m in range(nm):
        hcat = jnp.concatenate([bu[2 * S5_PER * m + q] for q in range(2 * S5_PER)], axis=1).astype(BF16)
        acc = _dot(hcat, wc_ref[m])
        ym[m] = jnp.where(fwd_row, acc[:, :LANES], acc[:, LANES:])
    _reverse_tiles(ym, ymr, nm, t)
    for b in range(nb):
        for m in range(nm):
            yf_ref[b, :, m * LANES:(m + 1) * LANES] = ym[m, pl.ds(b, t, stride=SUBLANES), :].astype(STREAM)
            yb_ref[b, :, m * LANES:(m + 1) * LANES] = ymr[m, pl.ds(nb + b, t, stride=SUBLANES), :].astype(STREAM)


def _scans_kernel(*refs, t, nb):
    rg_in, s5_in = refs[0:5], refs[5:11]
    rg_out, s5_out = refs[11:13], refs[13:15]
    rg_scr, s5_scr = refs[15:20], refs[20:27]
    _rglru_kernel(*rg_in, *rg_out, *rg_scr, t=t, nb=nb)
    _s5_kernel(*s5_in, *s5_out, *s5_scr, t=t, nb=nb)


def _scans(urg, ub, rg_w, rg_bias, rg_sp, wb, wc, ar, ai, l, t):
    nb, seq, c = urg.shape
    assert 2 * nb == SUBLANES and ub.shape == urg.shape
    nchunk = seq // t
    fwd = pl.BlockSpec((nb, t, c), lambda i: (0, i, 0))
    bwd = pl.BlockSpec((nb, t, c), lambda i: (0, nchunk - 1 - i, 0))
    nm = c // LANES
    rows = SUBLANES * t
    out = jax.ShapeDtypeStruct((nb, seq, c), STREAM)
    slab = pltpu.VMEM((nm, rows, LANES), F32)
    rg_scr = [slab, slab, slab, slab, pltpu.VMEM((nm, SUBLANES, LANES), F32)]
    s5_scr = [slab, slab, slab, pltpu.VMEM((2 * S5_NBLK, rows, LANES), F32), slab, slab,
              pltpu.VMEM((2 * S5_NBLK, SUBLANES, LANES), F32)]
    rg_c = [rg_w, rg_bias, rg_sp]
    s5_c = [wb, wc, ar, ai]
    return pl.pallas_call(
        functools.partial(_scans_kernel, t=t, nb=nb),
        grid=(nchunk,),
        in_specs=[fwd, bwd] + [_layer_spec(a, l) for a in rg_c]
                 + [fwd, bwd] + [_layer_spec(a, l) for a in s5_c],
        out_specs=[fwd, bwd, fwd, bwd],
        out_shape=[out, out, out, out],
        scratch_shapes=rg_scr + s5_scr,
        compiler_params=_params(("arbitrary",)),
        name="scans",
    )(urg, urg, *rg_c, ub, ub, *s5_c)


def _s5_place():
    slots = LANES // S5_GROUP
    half = LANES // S5_STATE
    place = np.zeros((slots, 2, S5_STATE, S5_PER * 2 * LANES), np.float32)
    for k in range(slots):
        for r in range(2):
            col = (k // half) * 2 * LANES + r * LANES + (k % half) * S5_STATE
            place[k, r, np.arange(S5_STATE), col + np.arange(S5_STATE)] = 1.0
    return place


def _s5_weights(p):
    lr = p["s5_a_re"].astype(F32)
    li = p["s5_a_im"].astype(F32)
    dt = jnp.exp(p["s5_log_dt"].astype(F32))[..., None]
    mag = jnp.exp(lr * dt)
    abar_r = mag * jnp.cos(li * dt)
    abar_i = mag * jnp.sin(li * dt)
    den = lr * lr + li * li
    nr = abar_r - 1.0
    ni = abar_i
    coef_r = ((nr * lr + ni * li) / den)[..., None]
    coef_i = ((ni * lr - nr * li) / den)[..., None]
    b_re = p["s5_b_re"].astype(F32)
    b_im = p["s5_b_im"].astype(F32)
    bbar = jnp.stack([coef_r * b_re - coef_i * b_im, coef_r * b_im + coef_i * b_re], axis=2)
    depth = lr.shape[0]
    nm = S5_WIDTH // LANES
    slots = LANES // S5_GROUP
    place = jnp.asarray(_s5_place())
    bb = bbar.reshape(depth, N_DIR, 2, nm, slots, S5_STATE, S5_GROUP)
    wb = jnp.einsum('ldrmkpc,krpx->lmdkcx', bb, place).reshape(depth, nm, 2 * LANES, S5_PER * 2 * LANES)
    cc = jnp.stack([p["s5_c_re"].astype(F32), -p["s5_c_im"].astype(F32)], axis=2)
    cc = cc.reshape(depth, N_DIR, 2, nm, slots, S5_GROUP, S5_STATE)
    wc = jnp.einsum('ldrmkcp,krpx->lmdkcx', cc, place).reshape(depth, nm, 2 * LANES, S5_PER * 2 * LANES)
    wc = wc.transpose(0, 1, 3, 2)

    def tile_rows(a):
        a = a.reshape(depth, N_DIR, S5_NBLK, LANES).transpose(0, 2, 1, 3)
        return jnp.repeat(a, SUBLANES // N_DIR, axis=2)

    return wb.astype(BF16), wc.astype(BF16), tile_rows(abar_r), tile_rows(abar_i)


def _odd8(n):
    p = -(-n // SUBLANES)
    if p % 2 == 0:
        p += 1
    return p * SUBLANES


class _FftPlan:
    def __init__(self, seq):
        self.seq = seq
        r = 1
        while r * r < seq:
            r *= 2
        self.r = r
        self.nq = seq // r
        self.q = 2 * seq // r
        self.k1n = self.q // 2 + 1
        self.kp = -(-self.k1n // SUBLANES) * SUBLANES
        self.zpitch = _odd8(r)
        self.apitch = _odd8(2 * self.kp)
        self.cpitch = _odd8(2 * r)
        self.unroll = min(8, r)
        self.unroll2 = next(u for u in (13, 5, 4, 3, 2, 1) if self.k1n % u == 0)
        p = 2 * seq
        n1 = np.arange(self.nq)[None, :]
        k1 = np.arange(self.k1n)[:, None]
        ang = 2.0 * np.pi * n1 * k1 / self.q
        f1 = np.zeros((2 * self.kp, self.nq))
        f1[:self.k1n] = np.cos(ang)
        f1[self.kp:self.kp + self.k1n] = -np.sin(ang)
        self.f1 = f1
        w = np.full((self.k1n,), 2.0)
        w[0] = 1.0
        w[-1] = 1.0
        g1 = np.zeros((self.nq, 2 * self.kp))
        g1[:, :self.k1n] = (np.cos(ang) * w[:, None] / p).T
        g1[:, self.kp:self.kp + self.k1n] = (-np.sin(ang) * w[:, None] / p).T
        self.g1 = g1
        kk = np.arange(self.k1n)[:, None, None]
        k2 = np.arange(r)[None, :, None]
        n2 = np.arange(r)[None, None, :]
        ph = 2.0 * np.pi * (n2 * k2 / r + n2 * kk / p)
        tr, ti = np.cos(ph), -np.sin(ph)
        self.m2 = np.concatenate([np.concatenate([tr, -ti], axis=2),
                                  np.concatenate([ti, tr], axis=2)], axis=1)
        ur, ui = np.transpose(tr, (0, 2, 1)), -np.transpose(ti, (0, 2, 1))
        self.m2i = np.concatenate([np.concatenate([ur, -ui], axis=2),
                                   np.concatenate([ui, ur], axis=2)], axis=1)


def _fft_forward(plan, src_ref, f1_ref, m2_ref, zp, as_, emit):
    r, nq, kp = plan.r, plan.nq, plan.kp
    for n1 in range(nq):
        zp[n1 * plan.zpitch:n1 * plan.zpitch + r, :] = src_ref[n1 * r:(n1 + 1) * r, :].astype(F32)

    def stage1(n2, c):
        slab = zp[pl.ds(n2, nq, stride=plan.zpitch), :]
        a = _dot(f1_ref[...].astype(BF16), slab.astype(BF16))
        as_[pl.ds(pl.multiple_of(n2 * plan.apitch, SUBLANES), 2 * kp), :] = a
        return c

    lax.fori_loop(0, r, stage1, 0, unroll=plan.unroll)

    def stage2(k1, c):
        sr = as_[pl.ds(k1, r, stride=plan.apitch), :]
        si = as_[pl.ds(kp + k1, r, stride=plan.apitch), :]
        s = jnp.concatenate([sr, si], axis=0).astype(BF16)
        emit(k1, _dot(m2_ref[k1].astype(BF16), s))
        return c

    lax.fori_loop(0, plan.k1n, stage2, 0, unroll=plan.unroll2)


def _hy_spec_kernel(kf_ref, kb_ref, f1_ref, m2_ref, o_ref, zp, as_, *, plan):
    r = plan.r

    def emit_f(k1, x):
        o_ref[0, k1] = x

    def emit_b(k1, x):
        sign = jnp.where(lax.broadcasted_iota(jnp.int32, (2 * r, LANES), 0) < r, 1.0, -1.0)
        o_ref[0, k1] = o_ref[0, k1] + sign * x

    _fft_forward(plan, kf_ref, f1_ref, m2_ref, zp, as_, emit_f)
    _fft_forward(plan, kb_ref, f1_ref, m2_ref, zp, as_, emit_b)


def _hy_conv_kernel(z_ref, spec_ref, f1_ref, m2_ref, m2i_ref, g1_ref, o_ref, zp, as_, cs, *, plan):
    r, nq, kp, k1n = plan.r, plan.nq, plan.kp, plan.k1n
    if kp > k1n:
        cs[k1n * plan.cpitch:kp * plan.cpitch, :] = jnp.zeros(((kp - k1n) * plan.cpitch, LANES), F32)

    def emit(k1, x):
        kf = spec_ref[0, k1]
        xr, xi = x[:r], x[r:]
        kr, ki = kf[:r], kf[r:]
        prod = jnp.concatenate([xr * kr - xi * ki, xr * ki + xi * kr], axis=0).astype(BF16)
        c = _dot(m2i_ref[k1].astype(BF16), prod)
        cs[pl.ds(pl.multiple_of(k1 * plan.cpitch, SUBLANES), 2 * r), :] = c

    _fft_forward(plan, z_ref.at[0], f1_ref, m2_ref, zp, as_, emit)

    def stage3(n2, c):
        cr = cs[pl.ds(n2, kp, stride=plan.cpitch), :]
        ci = cs[pl.ds(r + n2, kp, stride=plan.cpitch), :]
        y = _dot(g1_ref[...].astype(BF16), jnp.concatenate([cr, ci], axis=0).astype(BF16))
        zp[pl.ds(n2, nq, stride=plan.zpitch), :] = y
        return c

    lax.fori_loop(0, r, stage3, 0, unroll=plan.unroll)
    for n1 in range(nq):
        o_ref[0, n1 * r:(n1 + 1) * r, :] = zp[n1 * plan.zpitch:n1 * plan.zpitch + r, :].astype(o_ref.dtype)


def _hy_filter_kernel(feat_ref, w1_ref, b1_ref, f1_ref, w2_ref, b2_ref, f2_ref, w3_ref, dl_ref, o_ref):
    feats = feat_ref[...]
    half = feats.shape[0] // 2
    cols = o_ref.shape[1]
    both = jnp.concatenate([feats[:half], feats[half:]], axis=1)
    hid = jnp.sin(f1_ref[...] * (_dot(both, w1_ref[...], HIGHEST) + b1_ref[...]))
    hid = jnp.sin(f2_ref[...] * (_dot(hid, w2_ref[...], HIGHEST) + b2_ref[...]))
    k = _dot(hid, w3_ref[...], HIGHEST)
    o_ref[:half, :] = k[:, :cols] * jnp.exp(-(feats[:half, 0:1] * dl_ref[...]))
    o_ref[half:, :] = k[:, cols:] * jnp.exp(-(feats[half:, 0:1] * dl_ref[...]))


def _hyena_features(seq):
    pos = np.arange(seq, dtype=np.float64)
    t = pos / max(seq - 1, 1)
    w = (2.0 * math.pi / seq) * pos
    bands = np.linspace(1e-4, HY_BANDS - 1, HY_BANDS, dtype=np.float64)
    ang = w[:, None] * bands
    feats = np.concatenate([t[:, None], np.cos(ang), -np.sin(ang)], axis=-1).astype(np.float32)
    out = np.zeros((seq, LANES), np.float32)
    out[:, :feats.shape[1]] = feats
    return out


def _hyena_deltas():
    max_decay = math.log(HY_DECAY_TARGET) / HY_FAST_DECAY
    min_decay = math.log(HY_DECAY_TARGET) / HY_SLOW_DECAY
    deltas = np.abs(np.linspace(min_decay, max_decay, HY_WIDTH, dtype=np.float64))
    return np.tile(deltas, 2)[None, :].astype(np.float32)


def _pad_to(x, rows, cols):
    x = x.astype(F32)
    return jnp.pad(x, ((0, 0), (0, rows - x.shape[1]), (0, cols - x.shape[2])))


def _block_diag2(a):
    a = a.astype(F32)
    z = jnp.zeros_like(a)
    return jnp.concatenate([jnp.concatenate([a, z], axis=2), jnp.concatenate([z, a], axis=2)], axis=1)


def _hyena_filter_weights(p):
    hidden = p["hy_filt_w1"].shape[2]
    assert 2 * hidden == LANES
    vec = lambda a: jnp.tile(a.astype(F32)[:, None, :], (1, 1, 2))
    return (_block_diag2(_pad_to(p["hy_filt_w1"], LANES, hidden)), vec(p["hy_filt_b1"]),
            vec(p["hy_filt_freq1"]), _block_diag2(p["hy_filt_w2"]), vec(p["hy_filt_b2"]),
            vec(p["hy_filt_freq2"]), _block_diag2(p["hy_filt_w3"]))


def _hyena_filters(seq, fw, l, tf):
    feats = jnp.asarray(_hyena_features(seq))
    deltas = jnp.asarray(_hyena_deltas())
    return pl.pallas_call(
        _hy_filter_kernel,
        grid=(seq // tf,),
        in_specs=[pl.BlockSpec((tf, LANES), lambda i: (i, 0))] + [_layer_spec(a, l) for a in fw]
                 + [_full(deltas.shape)],
        out_specs=pl.BlockSpec((tf, 2 * HY_WIDTH), lambda i: (i, 0)),
        out_shape=jax.ShapeDtypeStruct((seq, 2 * HY_WIDTH), F32),
        compiler_params=_params(("parallel",)),
        name="hy_filter",
    )(feats, *fw, deltas)


def _hyena_conv(z3, filt, plan):
    nb, seq, c = z3.shape
    nh = c // LANES
    f1 = jnp.asarray(plan.f1, F32)
    m2 = jnp.asarray(plan.m2, F32)
    m2i = jnp.asarray(plan.m2i, F32)
    g1 = jnp.asarray(plan.g1, F32)
    r, kp, k1n = plan.r, plan.kp, plan.k1n
    zp_shape = (plan.nq * plan.zpitch, LANES)
    as_shape = (r * plan.apitch, LANES)
    cs_shape = (kp * plan.cpitch, LANES)
    spec = pl.pallas_call(
        functools.partial(_hy_spec_kernel, plan=plan),
        grid=(nh,),
        in_specs=[pl.BlockSpec((seq, LANES), lambda j: (0, j)),
                  pl.BlockSpec((seq, LANES), lambda j: (0, nh + j)),
                  _full(f1.shape), _full(m2.shape)],
        out_specs=pl.BlockSpec((1, k1n, 2 * r, LANES), lambda j: (j, 0, 0, 0)),
        out_shape=jax.ShapeDtypeStruct((nh, k1n, 2 * r, LANES), F32),
        scratch_shapes=[pltpu.VMEM(zp_shape, F32), pltpu.VMEM(as_shape, F32)],
        compiler_params=_params(("parallel",)),
        name="hy_spec",
    )(filt, filt, f1, m2)
    return pl.pallas_call(
        functools.partial(_hy_conv_kernel, plan=plan),
        grid=(nb, nh),
        in_specs=[pl.BlockSpec((1, seq, LANES), lambda b, j: (b, 0, j)),
                  pl.BlockSpec((1, k1n, 2 * r, LANES), lambda b, j: (j, 0, 0, 0)),
                  _full(f1.shape), _full(m2.shape), _full(m2i.shape), _full(g1.shape)],
        out_specs=pl.BlockSpec((1, seq, LANES), lambda b, j: (b, 0, j)),
        out_shape=jax.ShapeDtypeStruct((nb, seq, c), STREAM),
        scratch_shapes=[pltpu.VMEM(zp_shape, F32), pltpu.VMEM(as_shape, F32), pltpu.VMEM(cs_shape, F32)],
        compiler_params=_params(("parallel", "parallel")),
        name="hy_conv",
    )(z3, spec, f1, m2, m2i, g1)


def _mix_kernel(h_ref, hf_ref, hb_ref, ga_ref, yf_ref, yb_ref, ub_ref, yc_ref, z_ref, x0_ref,
                d_ref, gw_ref, gb_ref, hbias_ref, mg_ref, wo_ref, o_ref):
    f32 = lambda ref: ref[...].astype(F32)
    ya = (f32(hf_ref) + f32(hb_ref)) * _gelu(f32(ga_ref))
    yb = _gelu(f32(ub_ref) * d_ref[...] + f32(yf_ref) + f32(yb_ref))
    yb = yb * _sigmoid(_dot(yb.astype(BF16), gw_ref[...]) + gb_ref[...])
    yc = (f32(yc_ref) + f32(z_ref) * hbias_ref[...]) * f32(x0_ref)
    a1, a2 = RG_WIDTH, RG_WIDTH + S5_WIDTH
    na = (_rms_nogain(ya) * mg_ref[:, :a1]).astype(BF16)
    nb = (_rms_nogain(yb) * mg_ref[:, a1:a2]).astype(BF16)
    nc = (_rms_nogain(yc) * mg_ref[:, a2:]).astype(BF16)
    out = _dot(na, wo_ref[:a1, :]) + _dot(nb, wo_ref[a1:a2, :]) + _dot(nc, wo_ref[a2:, :])
    o_ref[...] = h_ref[...] + out


def _mix(rows, consts, l, tm):
    nb, seq, dm = rows[0].shape
    return pl.pallas_call(
        _mix_kernel,
        grid=(nb * seq // tm,),
        in_specs=[_row_spec(seq, tm, a.shape[2]) for a in rows] + [_layer_spec(a, l) for a in consts],
        out_specs=_row_spec(seq, tm, dm),
        out_shape=jax.ShapeDtypeStruct((nb, seq, dm), F32),
        compiler_params=_params(("parallel",)),
        name="mix",
    )(*rows, *consts)


FFN_TILE = 256


def _ffn_kernel(h_ref, hp_ref, hn_ref, g_ref, wu_ref, cw_ref, cb_ref, wd_ref, fg_ref, o_ref,
                x_scr, u_scr, gated_scr, y_scr, *, tiles_per_seq, tm, final):
    i = pl.program_id(0)
    first = (i % tiles_per_seq) == 0
    last = (i % tiles_per_seq) == tiles_per_seq - 1
    nslab = h_ref.shape[1] // LANES
    ph = tm // SUBLANES
    g = g_ref[...]

    def norm(v):
        return v * lax.rsqrt(jnp.mean(v * v, axis=-1, keepdims=True) + RMS_EPS) * g

    xn = norm(h_ref[...])
    xp = norm(jnp.where(first, 0.0, hp_ref[...]))
    xq = norm(jnp.where(last, 0.0, hn_ref[...]))
    for c in range(nslab):
        lanes = slice(c * LANES, (c + 1) * LANES)
        for s in range(SUBLANES):
            x_scr[c, pl.ds(s, ph, stride=SUBLANES), :] = xn[s * ph:(s + 1) * ph, lanes]
        x_scr[c, tm:tm + HALO, :] = xp[:, lanes]
        x_scr[c, tm + HALO:tm + 2 * HALO, :] = xq[:, lanes]
    x = jnp.concatenate([x_scr[c] for c in range(nslab)], axis=1).astype(BF16)
    sub = lax.broadcasted_iota(jnp.int32, (SUBLANES, FFN_TILE), 0)
    assert cw_ref.shape[0] == 3
    for k in range(D_FF // FFN_TILE):
        halves = []
        for part in range(2):
            lo = part * D_FF + k * FFN_TILE
            slot = 2 * k + part
            u_scr[slot] = _dot(x, wu_ref[:, lo:lo + FFN_TILE])
            head = jnp.where(sub == 0, pltpu.roll(u_scr[slot, tm:tm + HALO, :], 1, 0),
                             pltpu.roll(u_scr[slot, tm - SUBLANES:tm, :], 1, 0))
            tail = jnp.where(sub == SUBLANES - 1,
                             pltpu.roll(u_scr[slot, tm + HALO:tm + 2 * HALO, :], SUBLANES - 1, 0),
                             pltpu.roll(u_scr[slot, 0:SUBLANES, :], SUBLANES - 1, 0))
            prv = jnp.concatenate([head, u_scr[slot, 0:tm - SUBLANES, :]], axis=0)
            nxt = jnp.concatenate([u_scr[slot, SUBLANES:tm, :], tail], axis=0)
            cols = slice(lo, lo + FFN_TILE)
            halves.append(cb_ref[:, cols] + prv * cw_ref[0:1, cols] + u_scr[slot, 0:tm, :] * cw_ref[1:2, cols]
                          + nxt * cw_ref[2:3, cols])
        gated_scr[:, k * FFN_TILE:(k + 1) * FFN_TILE] = (_gelu(halves[0]) * halves[1]).astype(BF16)
    y = _dot(gated_scr[...], wd_ref[...])
    for c in range(nslab):
        y_scr[c] = y[:, c * LANES:(c + 1) * LANES]
    for s in range(SUBLANES):
        for c in range(nslab):
            rows = slice(s * ph, (s + 1) * ph)
            lanes = slice(c * LANES, (c + 1) * LANES)
            o_ref[rows, lanes] = h_ref[rows, lanes] + y_scr[c, pl.ds(s, ph, stride=SUBLANES), :]
    if final:
        out = o_ref[...]
        o_ref[...] = out * lax.rsqrt(jnp.mean(out * out, axis=-1, keepdims=True) + RMS_EPS) * fg_ref[...]


def _resident(arr, *lead):
    rest = arr.shape[len(lead):]
    zeros = (0,) * len(rest)
    return pl.BlockSpec((None,) * len(lead) + rest, lambda *_: tuple(lead) + zeros,
                        pipeline_mode=pl.Buffered(1))


def _ffn(h, g, w_up, conv_w, conv_b, w_down, final_g, l, tm, final):
    nb, seq, d = h.shape
    prev, nxt = _halo_specs(seq, tm, d)
    return pl.pallas_call(
        functools.partial(_ffn_kernel, tiles_per_seq=seq // tm, tm=tm, final=final),
        grid=(nb * seq // tm,),
        in_specs=[_row_spec(seq, tm, d), prev, nxt,
                  _layer_spec(g, l), _resident(w_up, l), _layer_spec(conv_w, l), _layer_spec(conv_b, l),
                  _resident(w_down, l), _full(final_g.shape)],
        out_specs=_row_spec(seq, tm, d),
        out_shape=jax.ShapeDtypeStruct((nb, seq, d), F32),
        scratch_shapes=[pltpu.VMEM((d // LANES, tm + 2 * HALO, LANES), F32),
                        pltpu.VMEM((2 * (D_FF // FFN_TILE), tm + 2 * HALO, FFN_TILE), F32),
                        pltpu.VMEM((tm, D_FF), BF16),
                        pltpu.VMEM((d // LANES, tm, LANES), F32)],
        compiler_params=_params(("parallel",)),
        name="ffn_final" if final else "ffn",
    )(h, h, h, g, w_up, conv_w, conv_b, w_down, final_g)


def _row_vec(a):
    return a.astype(F32)[:, None, :]


def _trunk(x, p, *, tm, t_scan, tf):
    seq = x.shape[1]
    depth = p["w_in"].shape[0]
    plan = _FftPlan(seq)

    proj_c = (_row_vec(p["norm1_g"]), p["w_in"].astype(BF16), p["rg_conv_w"].astype(F32),
              _row_vec(p["rg_conv_b"]), p["hy_conv_w"].astype(F32), _row_vec(p["hy_conv_b"]))
    rg_w, rg_bias, rg_sp = _rglru_weights(p)
    s5_w = _s5_weights(p)
    hy_fw = _hyena_filter_weights(p)
    mix_c = (_row_vec(p["s5_d"]), p["s5_glu_w"].astype(BF16), _row_vec(p["s5_glu_b"]),
             _row_vec(p["hy_bias"]), _row_vec(p["mix_norm_g"]), p["w_out"].astype(BF16))
    ffn_c = (_row_vec(p["norm2_g"]), p["w_up"].astype(BF16), p["ffn_conv_w"].astype(F32),
             _row_vec(p["ffn_conv_b"]), p["w_down"].astype(BF16), p["final_norm_g"].astype(F32)[None])

    h = x.astype(F32)
    for l in range(depth):
        urg, ga, ub, x0, z = _proj(h, *proj_c, l, tm)
        hf, hb, yf, yb = _scans(urg, ub, rg_w, rg_bias, rg_sp, *s5_w, l, t_scan)
        filt = _hyena_filters(seq, hy_fw, l, tf)
        yc = _hyena_conv(z, filt, plan)
        h = _mix([h, hf, hb, ga, yf, yb, ub, yc, z, x0], mix_c, l, tm)
        h = _ffn(h, *ffn_c, l, tm, final=(l == depth - 1))
    return h.astype(x.dtype)


def kernel(x, norm1_g, w_in, rg_conv_w, rg_conv_b, rg_wa, rg_ba, rg_wx, rg_bx, rg_lambda, s5_a_re, s5_a_im, s5_log_dt, s5_b_re, s5_b_im, s5_c_re, s5_c_im, s5_d, s5_glu_w, s5_glu_b, hy_conv_w, hy_conv_b, hy_filt_w1, hy_filt_b1, hy_filt_freq1, hy_filt_w2, hy_filt_b2, hy_filt_freq2, hy_filt_w3, hy_bias, mix_norm_g, w_out, norm2_g, w_up, ffn_conv_w, ffn_conv_b, w_down, final_norm_g):
    p = dict(norm1_g=norm1_g, w_in=w_in, rg_conv_w=rg_conv_w, rg_conv_b=rg_conv_b, rg_wa=rg_wa, rg_ba=rg_ba,
             rg_wx=rg_wx, rg_bx=rg_bx, rg_lambda=rg_lambda, s5_a_re=s5_a_re, s5_a_im=s5_a_im,
             s5_log_dt=s5_log_dt, s5_b_re=s5_b_re, s5_b_im=s5_b_im, s5_c_re=s5_c_re, s5_c_im=s5_c_im,
             s5_d=s5_d, s5_glu_w=s5_glu_w, s5_glu_b=s5_glu_b, hy_conv_w=hy_conv_w, hy_conv_b=hy_conv_b,
             hy_filt_w1=hy_filt_w1, hy_filt_b1=hy_filt_b1, hy_filt_freq1=hy_filt_freq1, hy_filt_w2=hy_filt_w2,
             hy_filt_b2=hy_filt_b2, hy_filt_freq2=hy_filt_freq2, hy_filt_w3=hy_filt_w3, hy_bias=hy_bias,
             mix_norm_g=mix_norm_g, w_out=w_out, norm2_g=norm2_g, w_up=w_up, ffn_conv_w=ffn_conv_w,
             ffn_conv_b=ffn_conv_b, w_down=w_down, final_norm_g=final_norm_g)
    return _trunk(x, p, tm=512, t_scan=128, tf=512)
```

```python
import functools
import math

import numpy as np
import jax
import jax.numpy as jnp
from jax import lax
from jax.experimental import pallas as pl
from jax.experimental.pallas import tpu as pltpu

F32 = jnp.float32
BF16 = jnp.bfloat16
STREAM = jnp.bfloat16

RMS_EPS = 1e-6
RG_WIDTH = 384
RG_HEADS = 6
RG_C = 8.0
S5_WIDTH = 384
S5_GROUP = 16
S5_GROUPS = 24
S5_STATE = 64
S5_NSTATE = S5_GROUPS * S5_STATE
HY_WIDTH = 256
HY_BANDS = 16
HY_FAST_DECAY = 0.3
HY_SLOW_DECAY = 1.5
HY_DECAY_TARGET = 1e-2
D_FF = 2816
N_DIR = 2

LANES = 128
SUBLANES = 8
HALO = SUBLANES
VMEM_LIMIT = 56 * 1024 * 1024

HIGHEST = lax.Precision.HIGHEST


def _dot(a, b, precision=None):
    return jnp.dot(a, b, preferred_element_type=F32, precision=precision)


def _gelu(x):
    c = math.sqrt(2.0 / math.pi)
    return 0.5 * x * (1.0 + jnp.tanh(c * (x + 0.044715 * (x * x * x))))


def _sigmoid(x):
    return 0.5 * jnp.tanh(0.5 * x) + 0.5


def _rms_nogain(x):
    return x * lax.rsqrt(jnp.mean(x * x, axis=-1, keepdims=True) + RMS_EPS)


def _params(sem):
    return pltpu.CompilerParams(dimension_semantics=sem, vmem_limit_bytes=VMEM_LIMIT)


def _full(shape):
    nd = len(shape)
    return pl.BlockSpec(shape, lambda *_: (0,) * nd)


def _layer_spec(arr, *lead):
    rest = arr.shape[len(lead):]
    zeros = (0,) * len(rest)
    return pl.BlockSpec((None,) * len(lead) + rest, lambda *_: tuple(lead) + zeros)


def _row_spec(seq, tm, c):
    tps = seq // tm
    return pl.BlockSpec((None, tm, c), lambda i: (i // tps, i % tps, 0))


def _halo_specs(seq, tm, c):
    tps = seq // tm
    bpt = tm // HALO
    prev = pl.BlockSpec((None, HALO, c), lambda i: (i // tps, jnp.maximum((i % tps) * bpt - 1, 0), 0))
    nxt = pl.BlockSpec((None, HALO, c),
                       lambda i: (i // tps, jnp.minimum((i % tps + 1) * bpt, seq // HALO - 1), 0))
    return prev, nxt


def _fill_normed(x_scr, h_ref, hp_ref, hn_ref, g, first, last, tm):
    def norm(x):
        return x * lax.rsqrt(jnp.mean(x * x, axis=-1, keepdims=True) + RMS_EPS) * g

    x_scr[0:HALO, :] = norm(jnp.where(first, 0.0, hp_ref[...]))
    x_scr[HALO:HALO + tm, :] = norm(h_ref[...])
    x_scr[HALO + tm:2 * HALO + tm, :] = norm(jnp.where(last, 0.0, hn_ref[...]))


def _proj_kernel(h_ref, hp_ref, hn_ref, g_ref, w_ref, rcw_ref, rcb_ref, hcw_ref, hcb_ref,
                 urg_ref, ga_ref, ub_ref, x0_ref, z_ref, x_scr, u_scr, *, tiles_per_seq, tm):
    i = pl.program_id(0)
    first = (i % tiles_per_seq) == 0
    last = (i % tiles_per_seq) == tiles_per_seq - 1
    _fill_normed(x_scr, h_ref, hp_ref, hn_ref, g_ref[...], first, last, tm)
    a0, a1, a2, a3, a4 = 0, RG_WIDTH, 2 * RG_WIDTH, 2 * RG_WIDTH + S5_WIDTH, w_ref.shape[1]
    x = x_scr[...].astype(BF16)
    for lo, hi in ((a0, a1), (a3, a4), (a1, a3)):
        u_scr[:, lo:hi] = _dot(x, w_ref[:, lo:hi])
    acc = rcb_ref[...]
    for j in range(rcw_ref.shape[0]):
        acc = acc + u_scr[pl.ds(HALO + j - 1, tm), a0:a1] * rcw_ref[j:j + 1, :]
    urg_ref[...] = acc
    ga_ref[...] = u_scr[pl.ds(HALO, tm), a1:a2].astype(ga_ref.dtype)
    ub_ref[...] = u_scr[pl.ds(HALO, tm), a2:a3].astype(ub_ref.dtype)
    q = []
    for part in range(3):
        lo = part * HY_WIDTH
        acc = hcb_ref[:, lo:lo + HY_WIDTH]
        for j in range(hcw_ref.shape[0]):
            acc = acc + (u_scr[pl.ds(HALO + j - 1, tm), a3 + lo:a3 + lo + HY_WIDTH]
                         * hcw_ref[j:j + 1, lo:lo + HY_WIDTH])
        q.append(acc)
    x0_ref[...] = q[0].astype(x0_ref.dtype)
    z_ref[...] = (q[2] * q[1]).astype(z_ref.dtype)


def _proj(h, g, w_in, rcw, rcb, hcw, hcb, l, tm):
    nb, seq, d = h.shape
    cols = w_in.shape[-1]
    prev, nxt = _halo_specs(seq, tm, d)
    row = lambda c: _row_spec(seq, tm, c)
    outs = [jax.ShapeDtypeStruct((nb, seq, c), dt) for c, dt in
            ((RG_WIDTH, F32), (RG_WIDTH, STREAM), (S5_WIDTH, STREAM), (HY_WIDTH, STREAM), (HY_WIDTH, STREAM))]
    consts = [g, w_in, rcw, rcb, hcw, hcb]
    return pl.pallas_call(
        functools.partial(_proj_kernel, tiles_per_seq=seq // tm, tm=tm),
        grid=(nb * seq // tm,),
        in_specs=[row(d), prev, nxt] + [_layer_spec(a, l) for a in consts],
        out_specs=[row(RG_WIDTH), row(RG_WIDTH), row(S5_WIDTH), row(HY_WIDTH), row(HY_WIDTH)],
        out_shape=outs,
        scratch_shapes=[pltpu.VMEM((tm + 2 * HALO, d), F32), pltpu.VMEM((tm + 2 * HALO, cols), F32)],
        compiler_params=_params(("parallel",)),
        name="proj",
    )(h, h, h, *consts)


RG_STAGES = 4


def _lo_mask():
    return lax.broadcasted_iota(jnp.int32, (SUBLANES, LANES), 0) < (SUBLANES // 2)


def _rglru_phases(uf_ref, ub_ref, w_ref, bias_ref, sp_ref, of_ref, ob_ref,
                  af, ab, bf, bb, carry_scr, *, t, nb):
    nm = RG_WIDTH // LANES
    tq = t // RG_STAGES

    def stage(q):
        for dr, (u_ref, a_scr, b_scr) in enumerate(((uf_ref, af, bf), (ub_ref, ab, bb))):
            t0 = q * tq if dr == 0 else t - (q + 1) * tq
            u = u_ref[:, t0:t0 + tq, :].reshape(nb * tq, RG_WIDTH)
            gates = _dot(u.astype(BF16), w_ref[dr]) + bias_ref[dr]
            r = _sigmoid(gates[:, :RG_WIDTH])
            gi = _sigmoid(gates[:, RG_WIDTH:])
            log_a = (-RG_C) * r * sp_ref[dr]
            a = jnp.exp(log_a)
            m2 = -jnp.tanh(log_a) * (a * a + 1.0)
            mult = jnp.where(m2 > 0.0, m2 * lax.rsqrt(m2), 0.0)
            bin_ = mult * (gi * u)
            for b in range(nb):
                for m in range(nm):
                    rows = pl.ds(t0 * SUBLANES + dr * nb + b, tq, stride=SUBLANES)
                    a_scr[m, rows, :] = a[b * tq:(b + 1) * tq, m * LANES:(m + 1) * LANES]
                    b_scr[m, rows, :] = bin_[b * tq:(b + 1) * tq, m * LANES:(m + 1) * LANES]

    def scan():
        lo = _lo_mask()
        carry = [carry_scr[m] for m in range(nm)]
        for j in range(t):
            row = j * SUBLANES
            mrow = (t - 1 - j) * SUBLANES
            for m in range(nm):
                at = jnp.where(lo, af[m, row:row + SUBLANES, :], ab[m, mrow:mrow + SUBLANES, :])
                bt = jnp.where(lo, bf[m, row:row + SUBLANES, :], bb[m, mrow:mrow + SUBLANES, :])
                h = at * carry[m] + bt
                bf[m, row:row + SUBLANES, :] = h
                bb[m, mrow:mrow + SUBLANES, :] = h
                carry[m] = h
        for m in range(nm):
            carry_scr[m] = carry[m]
        for b in range(nb):
            for m in range(nm):
                of_ref[b, :, m * LANES:(m + 1) * LANES] = bf[m, pl.ds(b, t, stride=SUBLANES), :].astype(STREAM)
                ob_ref[b, :, m * LANES:(m + 1) * LANES] = (
                    bb[m, pl.ds(nb + b, t, stride=SUBLANES), :].astype(STREAM))

    return [functools.partial(stage, q) for q in range(RG_STAGES)] + [scan]


def _rglru_weights(p):
    hd = RG_WIDTH // RG_HEADS
    place = np.zeros((2, RG_HEADS, hd, 2 * RG_WIDTH), np.float32)
    for q in range(2):
        for h in range(RG_HEADS):
            place[q, h, np.arange(hd), q * RG_WIDTH + h * hd + np.arange(hd)] = 1.0
    both = jnp.stack([p["rg_wa"], p["rg_wx"]], axis=2).astype(F32)
    depth = both.shape[0]
    w = jnp.einsum('ldqhij,qhjc->ldhic', both, jnp.asarray(place))
    w = w.reshape(depth, N_DIR, RG_WIDTH, 2 * RG_WIDTH).astype(BF16)
    bias = jnp.concatenate([p["rg_ba"], p["rg_bx"]], axis=-1).astype(F32)[:, :, None, :]
    x = -p["rg_lambda"].astype(F32)
    sp = (jnp.maximum(x, 0.0) + jnp.log1p(jnp.exp(-jnp.abs(x))))[:, :, None, :]
    return w, bias, sp


S5_NBLK = S5_NSTATE // LANES
S5_GRP = 4
S5_PER = S5_NBLK // (S5_WIDTH // LANES)


def _reverse_tiles(src, dst, nm, ntile):
    for j in range(ntile):
        s = (ntile - 1 - j) * SUBLANES
        for m in range(nm):
            dst[m, j * SUBLANES:(j + 1) * SUBLANES, :] = src[m, s:s + SUBLANES, :]


def _s5_phases(uf_ref, ub_ref, wb_ref, wc_ref, ar_ref, ai_ref, yf_ref, yb_ref,
               uf8, ub8, ubr, bu, ym, ymr, carry_scr, *, t, nb):
    nm = S5_WIDTH // LANES
    assert S5_GRP == S5_PER

    def load():
        for b in range(nb):
            for m in range(nm):
                lanes = slice(m * LANES, (m + 1) * LANES)
                uf8[m, pl.ds(b, t, stride=SUBLANES), :] = uf_ref[b, :, lanes].astype(F32)
                ub8[m, pl.ds(nb + b, t, stride=SUBLANES), :] = ub_ref[b, :, lanes].astype(F32)
        _reverse_tiles(ub8, ubr, nm, t)

    def bproj(m):
        lhs = jnp.concatenate([uf8[m], ubr[m]], axis=1).astype(BF16)
        res = _dot(lhs, wb_ref[m])
        for q in range(2 * S5_PER):
            bu[2 * S5_PER * m + q] = res[:, q * LANES:(q + 1) * LANES]

    def scan(g):
        blocks = list(range(g * S5_GRP, (g + 1) * S5_GRP))
        ars = [ar_ref[n] for n in blocks]
        ais = [ai_ref[n] for n in blocks]
        carry = []
        for n in blocks:
            carry += [carry_scr[2 * n], carry_scr[2 * n + 1]]
        for j in range(t):
            rows = slice(j * SUBLANES, (j + 1) * SUBLANES)
            for k, n in enumerate(blocks):
                hr, hi = carry[2 * k], carry[2 * k + 1]
                nr = ars[k] * hr - ais[k] * hi + bu[2 * n, rows, :]
                ni = ars[k] * hi + ais[k] * hr + bu[2 * n + 1, rows, :]
                bu[2 * n, rows, :] = nr
                bu[2 * n + 1, rows, :] = ni
                carry[2 * k], carry[2 * k + 1] = nr, ni
        for k, n in enumerate(blocks):
            carry_scr[2 * n] = carry[2 * k]
            carry_scr[2 * n + 1] = carry[2 * k + 1]

    def cproj(m):
        rows = lax.broadcasted_iota(jnp.int32, (SUBLANES * t, LANES), 0)
        fwd_row = (rows & (SUBLANES - 1)) < nb
        hcat = jnp.concatenate([bu[2 * S5_PER * m + q] for q in range(2 * S5_PER)], axis=1).astype(BF16)
        acc = _dot(hcat, wc_ref[m])
        ym[m] = jnp.where(fwd_row, acc[:, :LANES], acc[:, LANES:])

    def store():
        _reverse_tiles(ym, ymr, nm, t)
        for b in range(nb):
            for m in range(nm):
                lanes = slice(m * LANES, (m + 1) * LANES)
                yf_ref[b, :, lanes] = ym[m, pl.ds(b, t, stride=SUBLANES), :].astype(STREAM)
                yb_ref[b, :, lanes] = ymr[m, pl.ds(nb + b, t, stride=SUBLANES), :].astype(STREAM)

    part = functools.partial
    return (load, [part(bproj, m) for m in range(nm)], [part(scan, g) for g in range(nm)],
            [part(cproj, m) for m in range(nm)], store)


def _scans_kernel(*refs, t, nb):
    rg_in, s5_in = refs[0:5], refs[5:11]
    rg_out, s5_out = refs[11:13], refs[13:15]
    rg_scr, s5_scr = refs[15:20], refs[20:27]

    @pl.when(pl.program_id(0) == 0)
    def _():
        for scr in (*rg_scr, s5_scr[0], s5_scr[1], s5_scr[6]):
            scr[...] = jnp.zeros_like(scr)

    rg = _rglru_phases(*rg_in, *rg_out, *rg_scr, t=t, nb=nb)
    load, bproj, scan, cproj, store = _s5_phases(*s5_in, *s5_out, *s5_scr, t=t, nb=nb)
    order = [load, bproj[0], rg[0], bproj[1], rg[1], scan[0], bproj[2], rg[2], scan[1], cproj[0],
             rg[3], scan[2], cproj[1], rg[4], cproj[2], store]
    assert len(rg) == 5 and len(bproj) == 3
    for phase in order:
        phase()


def _scans(urg, ub, rg_w, rg_bias, rg_sp, wb, wc, ar, ai, l, t):
    nb, seq, c = urg.shape
    assert 2 * nb == SUBLANES and ub.shape == urg.shape
    nchunk = seq // t
    fwd = pl.BlockSpec((nb, t, c), lambda i: (0, i, 0))
    bwd = pl.BlockSpec((nb, t, c), lambda i: (0, nchunk - 1 - i, 0))
    nm = c // LANES
    rows = SUBLANES * t
    out = jax.ShapeDtypeStruct((nb, seq, c), STREAM)
    slab = pltpu.VMEM((nm, rows, LANES), F32)
    rg_scr = [slab, slab, slab, slab, pltpu.VMEM((nm, SUBLANES, LANES), F32)]
    s5_scr = [slab, slab, slab, pltpu.VMEM((2 * S5_NBLK, rows, LANES), F32), slab, slab,
              pltpu.VMEM((2 * S5_NBLK, SUBLANES, LANES), F32)]
    rg_c = [rg_w, rg_bias, rg_sp]
    s5_c = [wb, wc, ar, ai]
    return pl.pallas_call(
        functools.partial(_scans_kernel, t=t, nb=nb),
        grid=(nchunk,),
        in_specs=[fwd, bwd] + [_layer_spec(a, l) for a in rg_c]
                 + [fwd, bwd] + [_layer_spec(a, l) for a in s5_c],
        out_specs=[fwd, bwd, fwd, bwd],
        out_shape=[out, out, out, out],
        scratch_shapes=rg_scr + s5_scr,
        compiler_params=_params(("arbitrary",)),
        name="scans",
    )(urg, urg, *rg_c, ub, ub, *s5_c)


def _s5_place():
    slots = LANES // S5_GROUP
    half = LANES // S5_STATE
    place = np.zeros((slots, 2, S5_STATE, S5_PER * 2 * LANES), np.float32)
    for k in range(slots):
        for r in range(2):
            col = (k // half) * 2 * LANES + r * LANES + (k % half) * S5_STATE
            place[k, r, np.arange(S5_STATE), col + np.arange(S5_STATE)] = 1.0
    return place


def _s5_weights(p):
    lr = p["s5_a_re"].astype(F32)
    li = p["s5_a_im"].astype(F32)
    dt = jnp.exp(p["s5_log_dt"].astype(F32))[..., None]
    mag = jnp.exp(lr * dt)
    abar_r = mag * jnp.cos(li * dt)
    abar_i = mag * jnp.sin(li * dt)
    den = lr * lr + li * li
    nr = abar_r - 1.0
    ni = abar_i
    coef_r = ((nr * lr + ni * li) / den)[..., None]
    coef_i = ((ni * lr - nr * li) / den)[..., None]
    b_re = p["s5_b_re"].astype(F32)
    b_im = p["s5_b_im"].astype(F32)
    bbar = jnp.stack([coef_r * b_re - coef_i * b_im, coef_r * b_im + coef_i * b_re], axis=2)
    depth = lr.shape[0]
    nm = S5_WIDTH // LANES
    slots = LANES // S5_GROUP
    place = jnp.asarray(_s5_place())
    bb = bbar.reshape(depth, N_DIR, 2, nm, slots, S5_STATE, S5_GROUP)
    wb = jnp.einsum('ldrmkpc,krpx->lmdkcx', bb, place).reshape(depth, nm, 2 * LANES, S5_PER * 2 * LANES)
    cc = jnp.stack([p["s5_c_re"].astype(F32), -p["s5_c_im"].astype(F32)], axis=2)
    cc = cc.reshape(depth, N_DIR, 2, nm, slots, S5_GROUP, S5_STATE)
    wc = jnp.einsum('ldrmkcp,krpx->lmdkcx', cc, place).reshape(depth, nm, 2 * LANES, S5_PER * 2 * LANES)
    wc = wc.transpose(0, 1, 3, 2)

    def tile_rows(a):
        a = a.reshape(depth, N_DIR, S5_NBLK, LANES).transpose(0, 2, 1, 3)
        return jnp.repeat(a, SUBLANES // N_DIR, axis=2)

    return wb.astype(BF16), wc.astype(BF16), tile_rows(abar_r), tile_rows(abar_i)


def _odd8(n):
    p = -(-n // SUBLANES)
    if p % 2 == 0:
        p += 1
    return p * SUBLANES


class _FftPlan:
    def __init__(self, seq):
        self.seq = seq
        r = 1
        while r * r < seq:
            r *= 2
        self.r = r
        self.nq = seq // r
        self.q = 2 * seq // r
        self.k1n = self.q // 2 + 1
        self.kp = -(-self.k1n // SUBLANES) * SUBLANES
        self.zpitch = _odd8(r)
        self.apitch = _odd8(2 * self.kp)
        self.cpitch = _odd8(2 * r)
        self.unroll = min(8, r)
        self.unroll2 = next(u for u in (13, 5, 4, 3, 2, 1) if self.k1n % u == 0)
        p = 2 * seq
        n1 = np.arange(self.nq)[None, :]
        k1 = np.arange(self.k1n)[:, None]
        ang = 2.0 * np.pi * n1 * k1 / self.q
        f1 = np.zeros((2 * self.kp, self.nq))
        f1[:self.k1n] = np.cos(ang)
        f1[self.kp:self.kp + self.k1n] = -np.sin(ang)
        self.f1 = f1
        w = np.full((self.k1n,), 2.0)
        w[0] = 1.0
        w[-1] = 1.0
        g1 = np.zeros((self.nq, 2 * self.kp))
        g1[:, :self.k1n] = (np.cos(ang) * w[:, None] / p).T
        g1[:, self.kp:self.kp + self.k1n] = (-np.sin(ang) * w[:, None] / p).T
        self.g1 = g1
        kk = np.arange(self.k1n)[:, None, None]
        k2 = np.arange(r)[None, :, None]
        n2 = np.arange(r)[None, None, :]
        ph = 2.0 * np.pi * (n2 * k2 / r + n2 * kk / p)
        tr, ti = np.cos(ph), -np.sin(ph)
        self.m2 = np.concatenate([np.concatenate([tr, -ti], axis=2),
                                  np.concatenate([ti, tr], axis=2)], axis=1)
        ur, ui = np.transpose(tr, (0, 2, 1)), -np.transpose(ti, (0, 2, 1))
        self.m2i = np.concatenate([np.concatenate([ur, -ui], axis=2),
                                   np.concatenate([ui, ur], axis=2)], axis=1)


def _fft_forward(plan, src_ref, f1_ref, m2_ref, zp, as_, emit):
    r, nq, kp = plan.r, plan.nq, plan.kp
    for n1 in range(nq):
        zp[n1 * plan.zpitch:n1 * plan.zpitch + r, :] = src_ref[n1 * r:(n1 + 1) * r, :].astype(F32)

    def stage1(n2, c):
        slab = zp[pl.ds(n2, nq, stride=plan.zpitch), :]
        a = _dot(f1_ref[...].astype(BF16), slab.astype(BF16))
        as_[pl.ds(pl.multiple_of(n2 * plan.apitch, SUBLANES), 2 * kp), :] = a
        return c

    lax.fori_loop(0, r, stage1, 0, unroll=plan.unroll)

    def stage2(k1, c):
        sr = as_[pl.ds(k1, r, stride=plan.apitch), :]
        si = as_[pl.ds(kp + k1, r, stride=plan.apitch), :]
        s = jnp.concatenate([sr, si], axis=0).astype(BF16)
        emit(k1, _dot(m2_ref[k1].astype(BF16), s))
        return c

    lax.fori_loop(0, plan.k1n, stage2, 0, unroll=plan.unroll2)


def _hy_spec_kernel(kf_ref, kb_ref, f1_ref, m2_ref, o_ref, zp, as_, *, plan):
    r = plan.r

    def emit_f(k1, x):
        o_ref[0, k1] = x

    def emit_b(k1, x):
        sign = jnp.where(lax.broadcasted_iota(jnp.int32, (2 * r, LANES), 0) < r, 1.0, -1.0)
        o_ref[0, k1] = o_ref[0, k1] + sign * x

    _fft_forward(plan, kf_ref, f1_ref, m2_ref, zp, as_, emit_f)
    _fft_forward(plan, kb_ref, f1_ref, m2_ref, zp, as_, emit_b)


def _hy_conv_kernel(z_ref, spec_ref, f1_ref, m2_ref, m2i_ref, g1_ref, o_ref, zp, as_, cs, *, plan):
    r, nq, kp, k1n = plan.r, plan.nq, plan.kp, plan.k1n
    if kp > k1n:
        cs[k1n * plan.cpitch:kp * plan.cpitch, :] = jnp.zeros(((kp - k1n) * plan.cpitch, LANES), F32)

    def emit(k1, x):
        kf = spec_ref[0, k1]
        xr, xi = x[:r], x[r:]
        kr, ki = kf[:r], kf[r:]
        prod = jnp.concatenate([xr * kr - xi * ki, xr * ki + xi * kr], axis=0).astype(BF16)
        c = _dot(m2i_ref[k1].astype(BF16), prod)
        cs[pl.ds(pl.multiple_of(k1 * plan.cpitch, SUBLANES), 2 * r), :] = c

    _fft_forward(plan, z_ref.at[0], f1_ref, m2_ref, zp, as_, emit)

    def stage3(n2, c):
        cr = cs[pl.ds(n2, kp, stride=plan.cpitch), :]
        ci = cs[pl.ds(r + n2, kp, stride=plan.cpitch), :]
        y = _dot(g1_ref[...].astype(BF16), jnp.concatenate([cr, ci], axis=0).astype(BF16))
        zp[pl.ds(n2, nq, stride=plan.zpitch), :] = y
        return c

    lax.fori_loop(0, r, stage3, 0, unroll=plan.unroll)
    for n1 in range(nq):
        o_ref[0, n1 * r:(n1 + 1) * r, :] = zp[n1 * plan.zpitch:n1 * plan.zpitch + r, :].astype(o_ref.dtype)


def _hy_filter_kernel(feat_ref, w1_ref, b1_ref, f1_ref, w2_ref, b2_ref, f2_ref, w3_ref, dl_ref, o_ref):
    feats = feat_ref[...]
    half = feats.shape[0] // 2
    cols = o_ref.shape[1]
    both = jnp.concatenate([feats[:half], feats[half:]], axis=1)
    hid = jnp.sin(f1_ref[...] * (_dot(both, w1_ref[...], HIGHEST) + b1_ref[...]))
    hid = jnp.sin(f2_ref[...] * (_dot(hid, w2_ref[...], HIGHEST) + b2_ref[...]))
    k = _dot(hid, w3_ref[...], HIGHEST)
    o_ref[:half, :] = k[:, :cols] * jnp.exp(-(feats[:half, 0:1] * dl_ref[...]))
    o_ref[half:, :] = k[:, cols:] * jnp.exp(-(feats[half:, 0:1] * dl_ref[...]))


def _hyena_features(seq):
    pos = np.arange(seq, dtype=np.float64)
    t = pos / max(seq - 1, 1)
    w = (2.0 * math.pi / seq) * pos
    bands = np.linspace(1e-4, HY_BANDS - 1, HY_BANDS, dtype=np.float64)
    ang = w[:, None] * bands
    feats = np.concatenate([t[:, None], np.cos(ang), -np.sin(ang)], axis=-1).astype(np.float32)
    out = np.zeros((seq, LANES), np.float32)
    out[:, :feats.shape[1]] = feats
    return out


def _hyena_deltas():
    max_decay = math.log(HY_DECAY_TARGET) / HY_FAST_DECAY
    min_decay = math.log(HY_DECAY_TARGET) / HY_SLOW_DECAY
    deltas = np.abs(np.linspace(min_decay, max_decay, HY_WIDTH, dtype=np.float64))
    return np.tile(deltas, 2)[None, :].astype(np.float32)


def _pad_to(x, rows, cols):
    x = x.astype(F32)
    return jnp.pad(x, ((0, 0), (0, rows - x.shape[1]), (0, cols - x.shape[2])))


def _block_diag2(a):
    a = a.astype(F32)
    z = jnp.zeros_like(a)
    return jnp.concatenate([jnp.concatenate([a, z], axis=2), jnp.concatenate([z, a], axis=2)], axis=1)


def _hyena_filter_weights(p):
    hidden = p["hy_filt_w1"].shape[2]
    assert 2 * hidden == LANES
    vec = lambda a: jnp.tile(a.astype(F32)[:, None, :], (1, 1, 2))
    return (_block_diag2(_pad_to(p["hy_filt_w1"], LANES, hidden)), vec(p["hy_filt_b1"]),
            vec(p["hy_filt_freq1"]), _block_diag2(p["hy_filt_w2"]), vec(p["hy_filt_b2"]),
            vec(p["hy_filt_freq2"]), _block_diag2(p["hy_filt_w3"]))


def _hyena_filters(seq, fw, l, tf):
    feats = jnp.asarray(_hyena_features(seq))
    deltas = jnp.asarray(_hyena_deltas())
    return pl.pallas_call(
        _hy_filter_kernel,
        grid=(seq // tf,),
        in_specs=[pl.BlockSpec((tf, LANES), lambda i: (i, 0))] + [_layer_spec(a, l) for a in fw]
                 + [_full(deltas.shape)],
        out_specs=pl.BlockSpec((tf, 2 * HY_WIDTH), lambda i: (i, 0)),
        out_shape=jax.ShapeDtypeStruct((seq, 2 * HY_WIDTH), F32),
        compiler_params=_params(("parallel",)),
        name="hy_filter",
    )(feats, *fw, deltas)


def _hyena_conv(z3, filt, plan):
    nb, seq, c = z3.shape
    nh = c // LANES
    f1 = jnp.asarray(plan.f1, F32)
    m2 = jnp.asarray(plan.m2, F32)
    m2i = jnp.asarray(plan.m2i, F32)
    g1 = jnp.asarray(plan.g1, F32)
    r, kp, k1n = plan.r, plan.kp, plan.k1n
    zp_shape = (plan.nq * plan.zpitch, LANES)
    as_shape = (r * plan.apitch, LANES)
    cs_shape = (kp * plan.cpitch, LANES)
    spec = pl.pallas_call(
        functools.partial(_hy_spec_kernel, plan=plan),
        grid=(nh,),
        in_specs=[pl.BlockSpec((seq, LANES), lambda j: (0, j)),
                  pl.BlockSpec((seq, LANES), lambda j: (0, nh + j)),
                  _full(f1.shape), _full(m2.shape)],
        out_specs=pl.BlockSpec((1, k1n, 2 * r, LANES), lambda j: (j, 0, 0, 0)),
        out_shape=jax.ShapeDtypeStruct((nh, k1n, 2 * r, LANES), F32),
        scratch_shapes=[pltpu.VMEM(zp_shape, F32), pltpu.VMEM(as_shape, F32)],
        compiler_params=_params(("parallel",)),
        name="hy_spec",
    )(filt, filt, f1, m2)
    return pl.pallas_call(
        functools.partial(_hy_conv_kernel, plan=plan),
        grid=(nb, nh),
        in_specs=[pl.BlockSpec((1, seq, LANES), lambda b, j: (b, 0, j)),
                  pl.BlockSpec((1, k1n, 2 * r, LANES), lambda b, j: (j, 0, 0, 0)),
                  _full(f1.shape), _full(m2.shape), _full(m2i.shape), _full(g1.shape)],
        out_specs=pl.BlockSpec((1, seq, LANES), lambda b, j: (b, 0, j)),
        out_shape=jax.ShapeDtypeStruct((nb, seq, c), STREAM),
        scratch_shapes=[pltpu.VMEM(zp_shape, F32), pltpu.VMEM(as_shape, F32), pltpu.VMEM(cs_shape, F32)],
        compiler_params=_params(("parallel", "parallel")),
        name="hy_conv",
    )(z3, spec, f1, m2, m2i, g1)


def _mix_kernel(h_ref, hf_ref, hb_ref, ga_ref, yf_ref, yb_ref, ub_ref, yc_ref, z_ref, x0_ref,
                d_ref, gw_ref, gb_ref, hbias_ref, mg_ref, wo_ref, o_ref):
    f32 = lambda ref: ref[...].astype(F32)
    ya = (f32(hf_ref) + f32(hb_ref)) * _gelu(f32(ga_ref))
    yb = _gelu(f32(ub_ref) * d_ref[...] + f32(yf_ref) + f32(yb_ref))
    yb = yb * _sigmoid(_dot(yb.astype(BF16), gw_ref[...]) + gb_ref[...])
    yc = (f32(yc_ref) + f32(z_ref) * hbias_ref[...]) * f32(x0_ref)
    a1, a2 = RG_WIDTH, RG_WIDTH + S5_WIDTH
    na = (_rms_nogain(ya) * mg_ref[:, :a1]).astype(BF16)
    nb = (_rms_nogain(yb) * mg_ref[:, a1:a2]).astype(BF16)
    nc = (_rms_nogain(yc) * mg_ref[:, a2:]).astype(BF16)
    out = _dot(na, wo_ref[:a1, :]) + _dot(nb, wo_ref[a1:a2, :]) + _dot(nc, wo_ref[a2:, :])
    o_ref[...] = h_ref[...] + out


def _mix(rows, consts, l, tm):
    nb, seq, dm = rows[0].shape
    return pl.pallas_call(
        _mix_kernel,
        grid=(nb * seq // tm,),
        in_specs=[_row_spec(seq, tm, a.shape[2]) for a in rows] + [_layer_spec(a, l) for a in consts],
        out_specs=_row_spec(seq, tm, dm),
        out_shape=jax.ShapeDtypeStruct((nb, seq, dm), F32),
        compiler_params=_params(("parallel",)),
        name="mix",
    )(*rows, *consts)


FFN_TILE = 256


def _ffn_kernel(h_ref, hp_ref, hn_ref, g_ref, wu_ref, cw_ref, cb_ref, wd_ref, fg_ref, o_ref,
                x_scr, u_scr, gated_scr, y_scr, *, tiles_per_seq, tm, final):
    i = pl.program_id(0)
    first = (i % tiles_per_seq) == 0
    last = (i % tiles_per_seq) == tiles_per_seq - 1
    nslab = h_ref.shape[1] // LANES
    ph = tm // SUBLANES
    g = g_ref[...]

    def norm(v):
        return v * lax.rsqrt(jnp.mean(v * v, axis=-1, keepdims=True) + RMS_EPS) * g

    xn = norm(h_ref[...])
    xp = norm(jnp.where(first, 0.0, hp_ref[...]))
    xq = norm(jnp.where(last, 0.0, hn_ref[...]))
    for c in range(nslab):
        lanes = slice(c * LANES, (c + 1) * LANES)
        for s in range(SUBLANES):
            x_scr[c, pl.ds(s, ph, stride=SUBLANES), :] = xn[s * ph:(s + 1) * ph, lanes]
        x_scr[c, tm:tm + HALO, :] = xp[:, lanes]
        x_scr[c, tm + HALO:tm + 2 * HALO, :] = xq[:, lanes]
    x = jnp.concatenate([x_scr[c] for c in range(nslab)], axis=1).astype(BF16)
    sub = lax.broadcasted_iota(jnp.int32, (SUBLANES, FFN_TILE), 0)
    assert cw_ref.shape[0] == 3
    for k in range(D_FF // FFN_TILE):
        halves = []
        for part in range(2):
            lo = part * D_FF + k * FFN_TILE
            slot = 2 * k + part
            u_scr[slot] = _dot(x, wu_ref[:, lo:lo + FFN_TILE])
            head = jnp.where(sub == 0, pltpu.roll(u_scr[slot, tm:tm + HALO, :], 1, 0),
                             pltpu.roll(u_scr[slot, tm - SUBLANES:tm, :], 1, 0))
            tail = jnp.where(sub == SUBLANES - 1,
                             pltpu.roll(u_scr[slot, tm + HALO:tm + 2 * HALO, :], SUBLANES - 1, 0),
                             pltpu.roll(u_scr[slot, 0:SUBLANES, :], SUBLANES - 1, 0))
            prv = jnp.concatenate([head, u_scr[slot, 0:tm - SUBLANES, :]], axis=0)
            nxt = jnp.concatenate([u_scr[slot, SUBLANES:tm, :], tail], axis=0)
            cols = slice(lo, lo + FFN_TILE)
            halves.append(cb_ref[:, cols] + prv * cw_ref[0:1, cols] + u_scr[slot, 0:tm, :] * cw_ref[1:2, cols]
                          + nxt * cw_ref[2:3, cols])
        gated_scr[:, k * FFN_TILE:(k + 1) * FFN_TILE] = (_gelu(halves[0]) * halves[1]).astype(BF16)
    y = _dot(gated_scr[...], wd_ref[...])
    for c in range(nslab):
        y_scr[c] = y[:, c * LANES:(c + 1) * LANES]
    for s in range(SUBLANES):
        for c in range(nslab):
            rows = slice(s * ph, (s + 1) * ph)
            lanes = slice(c * LANES, (c + 1) * LANES)
            o_ref[rows, lanes] = h_ref[rows, lanes] + y_scr[c, pl.ds(s, ph, stride=SUBLANES), :]
    if final:
        out = o_ref[...]
        o_ref[...] = out * lax.rsqrt(jnp.mean(out * out, axis=-1, keepdims=True) + RMS_EPS) * fg_ref[...]


def _resident(arr, *lead):
    rest = arr.shape[len(lead):]
    zeros = (0,) * len(rest)
    return pl.BlockSpec((None,) * len(lead) + rest, lambda *_: tuple(lead) + zeros,
                        pipeline_mode=pl.Buffered(1))


def _ffn(h, g, w_up, conv_w, conv_b, w_down, final_g, l, tm, final):
    nb, seq, d = h.shape
    prev, nxt = _halo_specs(seq, tm, d)
    return pl.pallas_call(
        functools.partial(_ffn_kernel, tiles_per_seq=seq // tm, tm=tm, final=final),
        grid=(nb * seq // tm,),
        in_specs=[_row_spec(seq, tm, d), prev, nxt,
                  _layer_spec(g, l), _resident(w_up, l), _layer_spec(conv_w, l), _layer_spec(conv_b, l),
                  _resident(w_down, l), _full(final_g.shape)],
        out_specs=_row_spec(seq, tm, d),
        out_shape=jax.ShapeDtypeStruct((nb, seq, d), F32),
        scratch_shapes=[pltpu.VMEM((d // LANES, tm + 2 * HALO, LANES), F32),
                        pltpu.VMEM((2 * (D_FF // FFN_TILE), tm + 2 * HALO, FFN_TILE), F32),
                        pltpu.VMEM((tm, D_FF), BF16),
                        pltpu.VMEM((d // LANES, tm, LANES), F32)],
        compiler_params=_params(("parallel",)),
        name="ffn_final" if final else "ffn",
    )(h, h, h, g, w_up, conv_w, conv_b, w_down, final_g)


def _row_vec(a):
    return a.astype(F32)[:, None, :]


def _trunk(x, p, *, tm, tm_mix, t_scan, tf):
    seq = x.shape[1]
    depth = p["w_in"].shape[0]
    plan = _FftPlan(seq)

    proj_c = (_row_vec(p["norm1_g"]), p["w_in"].astype(BF16), p["rg_conv_w"].astype(F32),
              _row_vec(p["rg_conv_b"]), p["hy_conv_w"].astype(F32), _row_vec(p["hy_conv_b"]))
    rg_w, rg_bias, rg_sp = _rglru_weights(p)
    s5_w = _s5_weights(p)
    hy_fw = _hyena_filter_weights(p)
    mix_c = (_row_vec(p["s5_d"]), p["s5_glu_w"].astype(BF16), _row_vec(p["s5_glu_b"]),
             _row_vec(p["hy_bias"]), _row_vec(p["mix_norm_g"]), p["w_out"].astype(BF16))
    ffn_c = (_row_vec(p["norm2_g"]), p["w_up"].astype(BF16), p["ffn_conv_w"].astype(F32),
             _row_vec(p["ffn_conv_b"]), p["w_down"].astype(BF16), p["final_norm_g"].astype(F32)[None])

    h = x.astype(F32)
    for l in range(depth):
        urg, ga, ub, x0, z = _proj(h, *proj_c, l, tm)
        hf, hb, yf, yb = _scans(urg, ub, rg_w, rg_bias, rg_sp, *s5_w, l, t_scan)
        filt = _hyena_filters(seq, hy_fw, l, tf)
        yc = _hyena_conv(z, filt, plan)
        h = _mix([h, hf, hb, ga, yf, yb, ub, yc, z, x0], mix_c, l, tm_mix)
        h = _ffn(h, *ffn_c, l, tm, final=(l == depth - 1))
    return h.astype(x.dtype)


def kernel(x, norm1_g, w_in, rg_conv_w, rg_conv_b, rg_wa, rg_ba, rg_wx, rg_bx, rg_lambda, s5_a_re, s5_a_im, s5_log_dt, s5_b_re, s5_b_im, s5_c_re, s5_c_im, s5_d, s5_glu_w, s5_glu_b, hy_conv_w, hy_conv_b, hy_filt_w1, hy_filt_b1, hy_filt_freq1, hy_filt_w2, hy_filt_b2, hy_filt_freq2, hy_filt_w3, hy_bias, mix_norm_g, w_out, norm2_g, w_up, ffn_conv_w, ffn_conv_b, w_down, final_norm_g):
    p = dict(norm1_g=norm1_g, w_in=w_in, rg_conv_w=rg_conv_w, rg_conv_b=rg_conv_b, rg_wa=rg_wa, rg_ba=rg_ba,
             rg_wx=rg_wx, rg_bx=rg_bx, rg_lambda=rg_lambda, s5_a_re=s5_a_re, s5_a_im=s5_a_im,
             s5_log_dt=s5_log_dt, s5_b_re=s5_b_re, s5_b_im=s5_b_im, s5_c_re=s5_c_re, s5_c_im=s5_c_im,
             s5_d=s5_d, s5_glu_w=s5_glu_w, s5_glu_b=s5_glu_b, hy_conv_w=hy_conv_w, hy_conv_b=hy_conv_b,
             hy_filt_w1=hy_filt_w1, hy_filt_b1=hy_filt_b1, hy_filt_freq1=hy_filt_freq1, hy_filt_w2=hy_filt_w2,
             hy_filt_b2=hy_filt_b2, hy_filt_freq2=hy_filt_freq2, hy_filt_w3=hy_filt_w3, hy_bias=hy_bias,
             mix_norm_g=mix_norm_g, w_out=w_out, norm2_g=norm2_g, w_up=w_up, ffn_conv_w=ffn_conv_w,
             ffn_conv_b=ffn_conv_b, w_down=w_down, final_norm_g=final_norm_g)
    return _trunk(x, p, tm=512, tm_mix=1024, t_scan=128, tf=512)
```

```python
import functools
import math

import numpy as np
import jax
import jax.numpy as jnp
from jax import lax
from jax.experimental import pallas as pl
from jax.experimental.pallas import tpu as pltpu

F32 = jnp.float32
BF16 = jnp.bfloat16
STREAM = jnp.bfloat16

RMS_EPS = 1e-6
RG_WIDTH = 384
RG_HEADS = 6
RG_C = 8.0
S5_WIDTH = 384
S5_GROUP = 16
S5_GROUPS = 24
S5_STATE = 64
S5_NSTATE = S5_GROUPS * S5_STATE
HY_WIDTH = 256
HY_BANDS = 16
HY_FAST_DECAY = 0.3
HY_SLOW_DECAY = 1.5
HY_DECAY_TARGET = 1e-2
D_FF = 2816
N_DIR = 2

LANES = 128
SUBLANES = 8
HALO = SUBLANES
VMEM_LIMIT = 56 * 1024 * 1024

HIGHEST = lax.Precision.HIGHEST


def _dot(a, b, precision=None):
    return jnp.dot(a, b, preferred_element_type=F32, precision=precision)


def _gelu(x):
    c = math.sqrt(2.0 / math.pi)
    return 0.5 * x * (1.0 + jnp.tanh(c * (x + 0.044715 * (x * x * x))))


def _sigmoid(x):
    return 0.5 * jnp.tanh(0.5 * x) + 0.5


def _rms_nogain(x):
    return x * lax.rsqrt(jnp.mean(x * x, axis=-1, keepdims=True) + RMS_EPS)


def _params(sem):
    return pltpu.CompilerParams(dimension_semantics=sem, vmem_limit_bytes=VMEM_LIMIT)


def _full(shape):
    nd = len(shape)
    return pl.BlockSpec(shape, lambda *_: (0,) * nd)


def _layer_spec(arr, *lead):
    rest = arr.shape[len(lead):]
    zeros = (0,) * len(rest)
    return pl.BlockSpec((None,) * len(lead) + rest, lambda *_: tuple(lead) + zeros)


def _row_spec(seq, tm, c):
    tps = seq // tm
    return pl.BlockSpec((None, tm, c), lambda i: (i // tps, i % tps, 0))


def _halo_specs(seq, tm, c):
    tps = seq // tm
    bpt = tm // HALO
    prev = pl.BlockSpec((None, HALO, c), lambda i: (i // tps, jnp.maximum((i % tps) * bpt - 1, 0), 0))
    nxt = pl.BlockSpec((None, HALO, c),
                       lambda i: (i // tps, jnp.minimum((i % tps + 1) * bpt, seq // HALO - 1), 0))
    return prev, nxt


def _fill_normed(x_scr, h_ref, hp_ref, hn_ref, g, first, last, tm):
    def norm(x):
        return x * lax.rsqrt(jnp.mean(x * x, axis=-1, keepdims=True) + RMS_EPS) * g

    x_scr[0:HALO, :] = norm(jnp.where(first, 0.0, hp_ref[...]))
    x_scr[HALO:HALO + tm, :] = norm(h_ref[...])
    x_scr[HALO + tm:2 * HALO + tm, :] = norm(jnp.where(last, 0.0, hn_ref[...]))


def _proj_kernel(h_ref, hp_ref, hn_ref, g_ref, w_ref, rcw_ref, rcb_ref, hcw_ref, hcb_ref,
                 urg_ref, ga_ref, ub_ref, x0_ref, z_ref, x_scr, u_scr, *, tiles_per_seq, tm):
    i = pl.program_id(0)
    first = (i % tiles_per_seq) == 0
    last = (i % tiles_per_seq) == tiles_per_seq - 1
    _fill_normed(x_scr, h_ref, hp_ref, hn_ref, g_ref[...], first, last, tm)
    a0, a1, a2, a3, a4 = 0, RG_WIDTH, 2 * RG_WIDTH, 2 * RG_WIDTH + S5_WIDTH, w_ref.shape[1]
    x = x_scr[...].astype(BF16)
    for lo, hi in ((a0, a1), (a3, a4), (a1, a3)):
        u_scr[:, lo:hi] = _dot(x, w_ref[:, lo:hi])
    acc = rcb_ref[...]
    for j in range(rcw_ref.shape[0]):
        acc = acc + u_scr[pl.ds(HALO + j - 1, tm), a0:a1] * rcw_ref[j:j + 1, :]
    urg_ref[...] = acc
    ga_ref[...] = u_scr[pl.ds(HALO, tm), a1:a2].astype(ga_ref.dtype)
    ub_ref[...] = u_scr[pl.ds(HALO, tm), a2:a3].astype(ub_ref.dtype)
    q = []
    for part in range(3):
        lo = part * HY_WIDTH
        acc = hcb_ref[:, lo:lo + HY_WIDTH]
        for j in range(hcw_ref.shape[0]):
            acc = acc + (u_scr[pl.ds(HALO + j - 1, tm), a3 + lo:a3 + lo + HY_WIDTH]
                         * hcw_ref[j:j + 1, lo:lo + HY_WIDTH])
        q.append(acc)
    x0_ref[...] = q[0].astype(x0_ref.dtype)
    z_ref[...] = (q[2] * q[1]).astype(z_ref.dtype)


def _proj(h, g, w_in, rcw, rcb, hcw, hcb, l, tm):
    nb, seq, d = h.shape
    cols = w_in.shape[-1]
    prev, nxt = _halo_specs(seq, tm, d)
    row = lambda c: _row_spec(seq, tm, c)
    outs = [jax.ShapeDtypeStruct((nb, seq, c), dt) for c, dt in
            ((RG_WIDTH, F32), (RG_WIDTH, STREAM), (S5_WIDTH, STREAM), (HY_WIDTH, STREAM), (HY_WIDTH, STREAM))]
    consts = [g, w_in, rcw, rcb, hcw, hcb]
    return pl.pallas_call(
        functools.partial(_proj_kernel, tiles_per_seq=seq // tm, tm=tm),
        grid=(nb * seq // tm,),
        in_specs=[row(d), prev, nxt] + [_layer_spec(a, l) for a in consts],
        out_specs=[row(RG_WIDTH), row(RG_WIDTH), row(S5_WIDTH), row(HY_WIDTH), row(HY_WIDTH)],
        out_shape=outs,
        scratch_shapes=[pltpu.VMEM((tm + 2 * HALO, d), F32), pltpu.VMEM((tm + 2 * HALO, cols), F32)],
        compiler_params=_params(("parallel",)),
        name="proj",
    )(h, h, h, *consts)


RG_STAGES = 4


def _lo_mask():
    return lax.broadcasted_iota(jnp.int32, (SUBLANES, LANES), 0) < (SUBLANES // 2)


def _rglru_phases(uf_ref, ub_ref, w_ref, bias_ref, sp_ref, of_ref, ob_ref,
                  af, ab, bf, bb, carry_scr, *, t, nb):
    nm = RG_WIDTH // LANES
    tq = t // RG_STAGES

    def stage(q):
        for dr, (u_ref, a_scr, b_scr) in enumerate(((uf_ref, af, bf), (ub_ref, ab, bb))):
            t0 = q * tq if dr == 0 else t - (q + 1) * tq
            u = u_ref[:, t0:t0 + tq, :].reshape(nb * tq, RG_WIDTH)
            gates = _dot(u.astype(BF16), w_ref[dr]) + bias_ref[dr]
            r = _sigmoid(gates[:, :RG_WIDTH])
            gi = _sigmoid(gates[:, RG_WIDTH:])
            log_a = (-RG_C) * r * sp_ref[dr]
            a = jnp.exp(log_a)
            m2 = -jnp.tanh(log_a) * (a * a + 1.0)
            mult = jnp.where(m2 > 0.0, m2 * lax.rsqrt(m2), 0.0)
            bin_ = mult * (gi * u)
            for b in range(nb):
                for m in range(nm):
                    rows = pl.ds(t0 * SUBLANES + dr * nb + b, tq, stride=SUBLANES)
                    a_scr[m, rows, :] = a[b * tq:(b + 1) * tq, m * LANES:(m + 1) * LANES]
                    b_scr[m, rows, :] = bin_[b * tq:(b + 1) * tq, m * LANES:(m + 1) * LANES]

    def scan():
        lo = _lo_mask()
        carry = [carry_scr[m] for m in range(nm)]
        for j in range(t):
            row = j * SUBLANES
            mrow = (t - 1 - j) * SUBLANES
            for m in range(nm):
                at = jnp.where(lo, af[m, row:row + SUBLANES, :], ab[m, mrow:mrow + SUBLANES, :])
                bt = jnp.where(lo, bf[m, row:row + SUBLANES, :], bb[m, mrow:mrow + SUBLANES, :])
                h = at * carry[m] + bt
                bf[m, row:row + SUBLANES, :] = h
                bb[m, mrow:mrow + SUBLANES, :] = h
                carry[m] = h
        for m in range(nm):
            carry_scr[m] = carry[m]
        for b in range(nb):
            for m in range(nm):
                of_ref[b, :, m * LANES:(m + 1) * LANES] = bf[m, pl.ds(b, t, stride=SUBLANES), :].astype(STREAM)
                ob_ref[b, :, m * LANES:(m + 1) * LANES] = (
                    bb[m, pl.ds(nb + b, t, stride=SUBLANES), :].astype(STREAM))

    return [functools.partial(stage, q) for q in range(RG_STAGES)] + [scan]


def _rglru_weights(p):
    hd = RG_WIDTH // RG_HEADS
    place = np.zeros((2, RG_HEADS, hd, 2 * RG_WIDTH), np.float32)
    for q in range(2):
        for h in range(RG_HEADS):
            place[q, h, np.arange(hd), q * RG_WIDTH + h * hd + np.arange(hd)] = 1.0
    both = jnp.stack([p["rg_wa"], p["rg_wx"]], axis=2).astype(F32)
    depth = both.shape[0]
    w = jnp.einsum('ldqhij,qhjc->ldhic', both, jnp.asarray(place))
    w = w.reshape(depth, N_DIR, RG_WIDTH, 2 * RG_WIDTH).astype(BF16)
    bias = jnp.concatenate([p["rg_ba"], p["rg_bx"]], axis=-1).astype(F32)[:, :, None, :]
    x = -p["rg_lambda"].astype(F32)
    sp = (jnp.maximum(x, 0.0) + jnp.log1p(jnp.exp(-jnp.abs(x))))[:, :, None, :]
    return w, bias, sp


S5_NBLK = S5_NSTATE // LANES
S5_GRP = 4
S5_PER = S5_NBLK // (S5_WIDTH // LANES)


def _reverse_tiles(src, dst, nm, ntile):
    for j in range(ntile):
        s = (ntile - 1 - j) * SUBLANES
        for m in range(nm):
            dst[m, j * SUBLANES:(j + 1) * SUBLANES, :] = src[m, s:s + SUBLANES, :]


def _s5_phases(uf_ref, ub_ref, wb_ref, wc_ref, ar_ref, ai_ref, yf_ref, yb_ref,
               uf8, ub8, ubr, bu, ym, ymr, carry_scr, *, t, nb):
    nm = S5_WIDTH // LANES
    assert S5_GRP == S5_PER

    def load():
        for b in range(nb):
            for m in range(nm):
                lanes = slice(m * LANES, (m + 1) * LANES)
                uf8[m, pl.ds(b, t, stride=SUBLANES), :] = uf_ref[b, :, lanes].astype(F32)
                ub8[m, pl.ds(nb + b, t, stride=SUBLANES), :] = ub_ref[b, :, lanes].astype(F32)
        _reverse_tiles(ub8, ubr, nm, t)

    def bproj(m):
        lhs = jnp.concatenate([uf8[m], ubr[m]], axis=1).astype(BF16)
        res = _dot(lhs, wb_ref[m])
        for q in range(2 * S5_PER):
            bu[2 * S5_PER * m + q] = res[:, q * LANES:(q + 1) * LANES]

    def scan(g):
        blocks = list(range(g * S5_GRP, (g + 1) * S5_GRP))
        ars = [ar_ref[n] for n in blocks]
        ais = [ai_ref[n] for n in blocks]
        carry = []
        for n in blocks:
            carry += [carry_scr[2 * n], carry_scr[2 * n + 1]]
        for j in range(t):
            rows = slice(j * SUBLANES, (j + 1) * SUBLANES)
            for k, n in enumerate(blocks):
                hr, hi = carry[2 * k], carry[2 * k + 1]
                nr = ars[k] * hr - ais[k] * hi + bu[2 * n, rows, :]
                ni = ars[k] * hi + ais[k] * hr + bu[2 * n + 1, rows, :]
                bu[2 * n, rows, :] = nr
                bu[2 * n + 1, rows, :] = ni
                carry[2 * k], carry[2 * k + 1] = nr, ni
        for k, n in enumerate(blocks):
            carry_scr[2 * n] = carry[2 * k]
            carry_scr[2 * n + 1] = carry[2 * k + 1]

    def cproj(m):
        rows = lax.broadcasted_iota(jnp.int32, (SUBLANES * t, LANES), 0)
        fwd_row = (rows & (SUBLANES - 1)) < nb
        hcat = jnp.concatenate([bu[2 * S5_PER * m + q] for q in range(2 * S5_PER)], axis=1).astype(BF16)
        acc = _dot(hcat, wc_ref[m])
        ym[m] = jnp.where(fwd_row, acc[:, :LANES], acc[:, LANES:])

    def store():
        _reverse_tiles(ym, ymr, nm, t)
        for b in range(nb):
            for m in range(nm):
                lanes = slice(m * LANES, (m + 1) * LANES)
                yf_ref[b, :, lanes] = ym[m, pl.ds(b, t, stride=SUBLANES), :].astype(STREAM)
                yb_ref[b, :, lanes] = ymr[m, pl.ds(nb + b, t, stride=SUBLANES), :].astype(STREAM)

    part = functools.partial
    return (load, [part(bproj, m) for m in range(nm)], [part(scan, g) for g in range(nm)],
            [part(cproj, m) for m in range(nm)], store)


def _scans_kernel(*refs, t, nb):
    rg_in, s5_in = refs[0:5], refs[5:11]
    rg_out, s5_out = refs[11:13], refs[13:15]
    rg_scr, s5_scr = refs[15:20], refs[20:27]

    @pl.when(pl.program_id(0) == 0)
    def _():
        for scr in (*rg_scr, s5_scr[0], s5_scr[1], s5_scr[6]):
            scr[...] = jnp.zeros_like(scr)

    rg = _rglru_phases(*rg_in, *rg_out, *rg_scr, t=t, nb=nb)
    load, bproj, scan, cproj, store = _s5_phases(*s5_in, *s5_out, *s5_scr, t=t, nb=nb)
    order = [load, bproj[0], rg[0], bproj[1], rg[1], scan[0], bproj[2], rg[2], scan[1], cproj[0],
             rg[3], scan[2], cproj[1], rg[4], cproj[2], store]
    assert len(rg) == 5 and len(bproj) == 3
    for phase in order:
        phase()


def _scans(urg, ub, rg_w, rg_bias, rg_sp, wb, wc, ar, ai, l, t):
    nb, seq, c = urg.shape
    assert 2 * nb == SUBLANES and ub.shape == urg.shape
    nchunk = seq // t
    fwd = pl.BlockSpec((nb, t, c), lambda i: (0, i, 0))
    bwd = pl.BlockSpec((nb, t, c), lambda i: (0, nchunk - 1 - i, 0))
    nm = c // LANES
    rows = SUBLANES * t
    out = jax.ShapeDtypeStruct((nb, seq, c), STREAM)
    slab = pltpu.VMEM((nm, rows, LANES), F32)
    rg_scr = [slab, slab, slab, slab, pltpu.VMEM((nm, SUBLANES, LANES), F32)]
    s5_scr = [slab, slab, slab, pltpu.VMEM((2 * S5_NBLK, rows, LANES), F32), slab, slab,
              pltpu.VMEM((2 * S5_NBLK, SUBLANES, LANES), F32)]
    rg_c = [rg_w, rg_bias, rg_sp]
    s5_c = [wb, wc, ar, ai]
    return pl.pallas_call(
        functools.partial(_scans_kernel, t=t, nb=nb),
        grid=(nchunk,),
        in_specs=[fwd, bwd] + [_layer_spec(a, l) for a in rg_c]
                 + [fwd, bwd] + [_layer_spec(a, l) for a in s5_c],
        out_specs=[fwd, bwd, fwd, bwd],
        out_shape=[out, out, out, out],
        scratch_shapes=rg_scr + s5_scr,
        compiler_params=_params(("arbitrary",)),
        name="scans",
    )(urg, urg, *rg_c, ub, ub, *s5_c)


def _s5_place():
    slots = LANES // S5_GROUP
    half = LANES // S5_STATE
    place = np.zeros((slots, 2, S5_STATE, S5_PER * 2 * LANES), np.float32)
    for k in range(slots):
        for r in range(2):
            col = (k // half) * 2 * LANES + r * LANES + (k % half) * S5_STATE
            place[k, r, np.arange(S5_STATE), col + np.arange(S5_STATE)] = 1.0
    return place


def _s5_weights(p):
    lr = p["s5_a_re"].astype(F32)
    li = p["s5_a_im"].astype(F32)
    dt = jnp.exp(p["s5_log_dt"].astype(F32))[..., None]
    mag = jnp.exp(lr * dt)
    abar_r = mag * jnp.cos(li * dt)
    abar_i = mag * jnp.sin(li * dt)
    den = lr * lr + li * li
    nr = abar_r - 1.0
    ni = abar_i
    coef_r = ((nr * lr + ni * li) / den)[..., None]
    coef_i = ((ni * lr - nr * li) / den)[..., None]
    b_re = p["s5_b_re"].astype(F32)
    b_im = p["s5_b_im"].astype(F32)
    bbar = jnp.stack([coef_r * b_re - coef_i * b_im, coef_r * b_im + coef_i * b_re], axis=2)
    depth = lr.shape[0]
    nm = S5_WIDTH // LANES
    slots = LANES // S5_GROUP
    place = jnp.asarray(_s5_place())
    bb = bbar.reshape(depth, N_DIR, 2, nm, slots, S5_STATE, S5_GROUP)
    wb = jnp.einsum('ldrmkpc,krpx->lmdkcx', bb, place).reshape(depth, nm, 2 * LANES, S5_PER * 2 * LANES)
    cc = jnp.stack([p["s5_c_re"].astype(F32), -p["s5_c_im"].astype(F32)], axis=2)
    cc = cc.reshape(depth, N_DIR, 2, nm, slots, S5_GROUP, S5_STATE)
    wc = jnp.einsum('ldrmkcp,krpx->lmdkcx', cc, place).reshape(depth, nm, 2 * LANES, S5_PER * 2 * LANES)
    wc = wc.transpose(0, 1, 3, 2)

    def tile_rows(a):
        a = a.reshape(depth, N_DIR, S5_NBLK, LANES).transpose(0, 2, 1, 3)
        return jnp.repeat(a, SUBLANES // N_DIR, axis=2)

    return wb.astype(BF16), wc.astype(BF16), tile_rows(abar_r), tile_rows(abar_i)


def _odd8(n):
    p = -(-n // SUBLANES)
    if p % 2 == 0:
        p += 1
    return p * SUBLANES


class _FftPlan:
    def __init__(self, seq):
        self.seq = seq
        r = 1
        while r * r < seq:
            r *= 2
        self.r = r
        self.nq = seq // r
        self.q = 2 * seq // r
        self.k1n = self.q // 2 + 1
        self.kp = -(-self.k1n // SUBLANES) * SUBLANES
        self.zpitch = _odd8(r)
        self.apitch = _odd8(2 * self.kp)
        self.cpitch = _odd8(2 * r)
        self.unroll = min(8, r)
        self.unroll2 = next(u for u in (13, 5, 4, 3, 2, 1) if self.k1n % u == 0)
        p = 2 * seq
        n1 = np.arange(self.nq)[None, :]
        k1 = np.arange(self.k1n)[:, None]
        ang = 2.0 * np.pi * n1 * k1 / self.q
        f1 = np.zeros((2 * self.kp, self.nq))
        f1[:self.k1n] = np.cos(ang)
        f1[self.kp:self.kp + self.k1n] = -np.sin(ang)
        self.f1 = f1
        w = np.full((self.k1n,), 2.0)
        w[0] = 1.0
        w[-1] = 1.0
        g1 = np.zeros((self.nq, 2 * self.kp))
        g1[:, :self.k1n] = (np.cos(ang) * w[:, None] / p).T
        g1[:, self.kp:self.kp + self.k1n] = (-np.sin(ang) * w[:, None] / p).T
        self.g1 = g1
        kk = np.arange(self.k1n)[:, None, None]
        k2 = np.arange(r)[None, :, None]
        n2 = np.arange(r)[None, None, :]
        ph = 2.0 * np.pi * (n2 * k2 / r + n2 * kk / p)
        tr, ti = np.cos(ph), -np.sin(ph)
        self.m2 = np.concatenate([np.concatenate([tr, -ti], axis=2),
                                  np.concatenate([ti, tr], axis=2)], axis=1)
        ur, ui = np.transpose(tr, (0, 2, 1)), -np.transpose(ti, (0, 2, 1))
        self.m2i = np.concatenate([np.concatenate([ur, -ui], axis=2),
                                   np.concatenate([ui, ur], axis=2)], axis=1)


def _slab_load(scr, rows):
    return jnp.concatenate([scr[k, rows, :] for k in range(scr.shape[0])], axis=1)


def _slab_store(scr, rows, val):
    for k in range(scr.shape[0]):
        scr[k, rows, :] = val[:, k * LANES:(k + 1) * LANES]


def _fft_forward(plan, src_ref, f1_ref, m2_ref, zp, as_, emit):
    r, nq, kp = plan.r, plan.nq, plan.kp
    for n1 in range(nq):
        _slab_store(zp, slice(n1 * plan.zpitch, n1 * plan.zpitch + r), src_ref[n1 * r:(n1 + 1) * r, :].astype(F32))

    def stage1(n2, c):
        slab = _slab_load(zp, pl.ds(n2, nq, stride=plan.zpitch))
        a = _dot(f1_ref[...].astype(BF16), slab.astype(BF16))
        _slab_store(as_, pl.ds(pl.multiple_of(n2 * plan.apitch, SUBLANES), 2 * kp), a)
        return c

    lax.fori_loop(0, r, stage1, 0, unroll=plan.unroll)

    def stage2(k1, c):
        sr = _slab_load(as_, pl.ds(k1, r, stride=plan.apitch))
        si = _slab_load(as_, pl.ds(kp + k1, r, stride=plan.apitch))
        s = jnp.concatenate([sr, si], axis=0).astype(BF16)
        emit(k1, _dot(m2_ref[k1].astype(BF16), s))
        return c

    lax.fori_loop(0, plan.k1n, stage2, 0, unroll=plan.unroll2)


def _hy_spec_kernel(kf_ref, kb_ref, f1_ref, m2_ref, o_ref, zp, as_, *, plan):
    r = plan.r

    def emit_f(k1, x):
        o_ref[k1] = x

    def emit_b(k1, x):
        sign = jnp.where(lax.broadcasted_iota(jnp.int32, x.shape, 0) < r, 1.0, -1.0)
        o_ref[k1] = o_ref[k1] + sign * x

    _fft_forward(plan, kf_ref, f1_ref, m2_ref, zp, as_, emit_f)
    _fft_forward(plan, kb_ref, f1_ref, m2_ref, zp, as_, emit_b)


def _hy_conv_kernel(z_ref, spec_ref, f1_ref, m2_ref, m2i_ref, g1_ref, o_ref, zp, as_, cs, *, plan):
    r, nq, kp, k1n = plan.r, plan.nq, plan.kp, plan.k1n
    if kp > k1n:
        pad = slice(k1n * plan.cpitch, kp * plan.cpitch)
        for k in range(cs.shape[0]):
            cs[k, pad, :] = jnp.zeros(((kp - k1n) * plan.cpitch, LANES), F32)

    def emit(k1, x):
        kf = spec_ref[k1]
        xr, xi = x[:r], x[r:]
        kr, ki = kf[:r], kf[r:]
        prod = jnp.concatenate([xr * kr - xi * ki, xr * ki + xi * kr], axis=0).astype(BF16)
        c = _dot(m2i_ref[k1].astype(BF16), prod)
        _slab_store(cs, pl.ds(pl.multiple_of(k1 * plan.cpitch, SUBLANES), 2 * r), c)

    _fft_forward(plan, z_ref, f1_ref, m2_ref, zp, as_, emit)

    def stage3(n2, c):
        cr = _slab_load(cs, pl.ds(n2, kp, stride=plan.cpitch))
        ci = _slab_load(cs, pl.ds(r + n2, kp, stride=plan.cpitch))
        y = _dot(g1_ref[...].astype(BF16), jnp.concatenate([cr, ci], axis=0).astype(BF16))
        _slab_store(zp, pl.ds(n2, nq, stride=plan.zpitch), y)
        return c

    lax.fori_loop(0, r, stage3, 0, unroll=plan.unroll)
    for n1 in range(nq):
        o_ref[n1 * r:(n1 + 1) * r, :] = _slab_load(
            zp, slice(n1 * plan.zpitch, n1 * plan.zpitch + r)).astype(o_ref.dtype)


def _hy_filter_kernel(feat_ref, w1_ref, b1_ref, f1_ref, w2_ref, b2_ref, f2_ref, w3_ref, dl_ref, o_ref):
    feats = feat_ref[...]
    half = feats.shape[0] // 2
    cols = o_ref.shape[1]
    both = jnp.concatenate([feats[:half], feats[half:]], axis=1)
    hid = jnp.sin(f1_ref[...] * (_dot(both, w1_ref[...], HIGHEST) + b1_ref[...]))
    hid = jnp.sin(f2_ref[...] * (_dot(hid, w2_ref[...], HIGHEST) + b2_ref[...]))
    k = _dot(hid, w3_ref[...], HIGHEST)
    o_ref[:half, :] = k[:, :cols] * jnp.exp(-(feats[:half, 0:1] * dl_ref[...]))
    o_ref[half:, :] = k[:, cols:] * jnp.exp(-(feats[half:, 0:1] * dl_ref[...]))


def _hyena_features(seq):
    pos = np.arange(seq, dtype=np.float64)
    t = pos / max(seq - 1, 1)
    w = (2.0 * math.pi / seq) * pos
    bands = np.linspace(1e-4, HY_BANDS - 1, HY_BANDS, dtype=np.float64)
    ang = w[:, None] * bands
    feats = np.concatenate([t[:, None], np.cos(ang), -np.sin(ang)], axis=-1).astype(np.float32)
    out = np.zeros((seq, LANES), np.float32)
    out[:, :feats.shape[1]] = feats
    return out


def _hyena_deltas():
    max_decay = math.log(HY_DECAY_TARGET) / HY_FAST_DECAY
    min_decay = math.log(HY_DECAY_TARGET) / HY_SLOW_DECAY
    deltas = np.abs(np.linspace(min_decay, max_decay, HY_WIDTH, dtype=np.float64))
    return np.tile(deltas, 2)[None, :].astype(np.float32)


def _pad_to(x, rows, cols):
    x = x.astype(F32)
    return jnp.pad(x, ((0, 0), (0, rows - x.shape[1]), (0, cols - x.shape[2])))


def _block_diag2(a):
    a = a.astype(F32)
    z = jnp.zeros_like(a)
    return jnp.concatenate([jnp.concatenate([a, z], axis=2), jnp.concatenate([z, a], axis=2)], axis=1)


def _hyena_filter_weights(p):
    hidden = p["hy_filt_w1"].shape[2]
    assert 2 * hidden == LANES
    vec = lambda a: jnp.tile(a.astype(F32)[:, None, :], (1, 1, 2))
    return (_block_diag2(_pad_to(p["hy_filt_w1"], LANES, hidden)), vec(p["hy_filt_b1"]),
            vec(p["hy_filt_freq1"]), _block_diag2(p["hy_filt_w2"]), vec(p["hy_filt_b2"]),
            vec(p["hy_filt_freq2"]), _block_diag2(p["hy_filt_w3"]))


def _hyena_filters(seq, fw, l, tf):
    feats = jnp.asarray(_hyena_features(seq))
    deltas = jnp.asarray(_hyena_deltas())
    return pl.pallas_call(
        _hy_filter_kernel,
        grid=(seq // tf,),
        in_specs=[pl.BlockSpec((tf, LANES), lambda i: (i, 0))] + [_layer_spec(a, l) for a in fw]
                 + [_full(deltas.shape)],
        out_specs=pl.BlockSpec((tf, 2 * HY_WIDTH), lambda i: (i, 0)),
        out_shape=jax.ShapeDtypeStruct((seq, 2 * HY_WIDTH), F32),
        compiler_params=_params(("parallel",)),
        name="hy_filter",
    )(feats, *fw, deltas)


def _hyena_conv(z3, filt, plan):
    nb, seq, c = z3.shape
    nh = c // LANES
    f1 = jnp.asarray(plan.f1, F32)
    m2 = jnp.asarray(plan.m2, F32)
    m2i = jnp.asarray(plan.m2i, F32)
    g1 = jnp.asarray(plan.g1, F32)
    r, kp, k1n = plan.r, plan.kp, plan.k1n
    zp = pltpu.VMEM((nh, plan.nq * plan.zpitch, LANES), F32)
    as_ = pltpu.VMEM((nh, r * plan.apitch, LANES), F32)
    cs = pltpu.VMEM((nh, kp * plan.cpitch, LANES), F32)
    once = lambda a: pl.BlockSpec(a.shape, lambda *_: (0,) * a.ndim, pipeline_mode=pl.Buffered(1))
    spec = pl.pallas_call(
        functools.partial(_hy_spec_kernel, plan=plan),
        grid=(1,),
        in_specs=[pl.BlockSpec((seq, c), lambda j: (0, 0)), pl.BlockSpec((seq, c), lambda j: (0, 1)),
                  _full(f1.shape), _full(m2.shape)],
        out_specs=_full((k1n, 2 * r, c)),
        out_shape=jax.ShapeDtypeStruct((k1n, 2 * r, c), F32),
        scratch_shapes=[zp, as_],
        compiler_params=_params(("arbitrary",)),
        name="hy_spec",
    )(filt, filt, f1, m2)
    return pl.pallas_call(
        functools.partial(_hy_conv_kernel, plan=plan),
        grid=(nb,),
        in_specs=[pl.BlockSpec((None, seq, c), lambda b: (b, 0, 0)),
                  once(spec), once(f1), once(m2), once(m2i), once(g1)],
        out_specs=pl.BlockSpec((None, seq, c), lambda b: (b, 0, 0)),
        out_shape=jax.ShapeDtypeStruct((nb, seq, c), STREAM),
        scratch_shapes=[zp, as_, cs],
        compiler_params=_params(("parallel",)),
        name="hy_conv",
    )(z3, spec, f1, m2, m2i, g1)


def _mix_kernel(h_ref, hf_ref, hb_ref, ga_ref, yf_ref, yb_ref, ub_ref, yc_ref, z_ref, x0_ref,
                d_ref, gw_ref, gb_ref, hbias_ref, mg_ref, wo_ref, o_ref):
    f32 = lambda ref: ref[...].astype(F32)
    ya = (f32(hf_ref) + f32(hb_ref)) * _gelu(f32(ga_ref))
    yb = _gelu(f32(ub_ref) * d_ref[...] + f32(yf_ref) + f32(yb_ref))
    yb = yb * _sigmoid(_dot(yb.astype(BF16), gw_ref[...]) + gb_ref[...])
    yc = (f32(yc_ref) + f32(z_ref) * hbias_ref[...]) * f32(x0_ref)
    a1, a2 = RG_WIDTH, RG_WIDTH + S5_WIDTH
    na = (_rms_nogain(ya) * mg_ref[:, :a1]).astype(BF16)
    nb = (_rms_nogain(yb) * mg_ref[:, a1:a2]).astype(BF16)
    nc = (_rms_nogain(yc) * mg_ref[:, a2:]).astype(BF16)
    out = _dot(na, wo_ref[:a1, :]) + _dot(nb, wo_ref[a1:a2, :]) + _dot(nc, wo_ref[a2:, :])
    o_ref[...] = h_ref[...] + out


def _mix(rows, consts, l, tm):
    nb, seq, dm = rows[0].shape
    return pl.pallas_call(
        _mix_kernel,
        grid=(nb * seq // tm,),
        in_specs=[_row_spec(seq, tm, a.shape[2]) for a in rows] + [_layer_spec(a, l) for a in consts],
        out_specs=_row_spec(seq, tm, dm),
        out_shape=jax.ShapeDtypeStruct((nb, seq, dm), F32),
        compiler_params=_params(("parallel",)),
        name="mix",
    )(*rows, *consts)


FFN_TILE = 256


def _ffn_kernel(h_ref, hp_ref, hn_ref, g_ref, wu_ref, cw_ref, cb_ref, wd_ref, fg_ref, o_ref,
                x_scr, u_scr, gated_scr, y_scr, *, tiles_per_seq, tm, final):
    i = pl.program_id(0)
    first = (i % tiles_per_seq) == 0
    last = (i % tiles_per_seq) == tiles_per_seq - 1
    nslab = h_ref.shape[1] // LANES
    ph = tm // SUBLANES
    g = g_ref[...]

    def norm(v):
        return v * lax.rsqrt(jnp.mean(v * v, axis=-1, keepdims=True) + RMS_EPS) * g

    xn = norm(h_ref[...])
    xp = norm(jnp.where(first, 0.0, hp_ref[...]))
    xq = norm(jnp.where(last, 0.0, hn_ref[...]))
    for c in range(nslab):
        lanes = slice(c * LANES, (c + 1) * LANES)
        for s in range(SUBLANES):
            x_scr[c, pl.ds(s, ph, stride=SUBLANES), :] = xn[s * ph:(s + 1) * ph, lanes]
        x_scr[c, tm:tm + HALO, :] = xp[:, lanes]
        x_scr[c, tm + HALO:tm + 2 * HALO, :] = xq[:, lanes]
    x = jnp.concatenate([x_scr[c] for c in range(nslab)], axis=1).astype(BF16)
    sub = lax.broadcasted_iota(jnp.int32, (SUBLANES, FFN_TILE), 0)
    assert cw_ref.shape[0] == 3
    for k in range(D_FF // FFN_TILE):
        halves = []
        for part in range(2):
            lo = part * D_FF + k * FFN_TILE
            slot = 2 * k + part
            u_scr[slot] = _dot(x, wu_ref[:, lo:lo + FFN_TILE])
            head = jnp.where(sub == 0, pltpu.roll(u_scr[slot, tm:tm + HALO, :], 1, 0),
                             pltpu.roll(u_scr[slot, tm - SUBLANES:tm, :], 1, 0))
            tail = jnp.where(sub == SUBLANES - 1,
                             pltpu.roll(u_scr[slot, tm + HALO:tm + 2 * HALO, :], SUBLANES - 1, 0),
                             pltpu.roll(u_scr[slot, 0:SUBLANES, :], SUBLANES - 1, 0))
            prv = jnp.concatenate([head, u_scr[slot, 0:tm - SUBLANES, :]], axis=0)
            nxt = jnp.concatenate([u_scr[slot, SUBLANES:tm, :], tail], axis=0)
            cols = slice(lo, lo + FFN_TILE)
            halves.append(cb_ref[:, cols] + prv * cw_ref[0:1, cols] + u_scr[slot, 0:tm, :] * cw_ref[1:2, cols]
                          + nxt * cw_ref[2:3, cols])
        gated_scr[:, k * FFN_TILE:(k + 1) * FFN_TILE] = (_gelu(halves[0]) * halves[1]).astype(BF16)
    y = _dot(gated_scr[...], wd_ref[...])
    for c in range(nslab):
        y_scr[c] = y[:, c * LANES:(c + 1) * LANES]
    for s in range(SUBLANES):
        for c in range(nslab):
            rows = slice(s * ph, (s + 1) * ph)
            lanes = slice(c * LANES, (c + 1) * LANES)
            o_ref[rows, lanes] = h_ref[rows, lanes] + y_scr[c, pl.ds(s, ph, stride=SUBLANES), :]
    if final:
        out = o_ref[...]
        o_ref[...] = out * lax.rsqrt(jnp.mean(out * out, axis=-1, keepdims=True) + RMS_EPS) * fg_ref[...]


def _resident(arr, *lead):
    rest = arr.shape[len(lead):]
    zeros = (0,) * len(rest)
    return pl.BlockSpec((None,) * len(lead) + rest, lambda *_: tuple(lead) + zeros,
                        pipeline_mode=pl.Buffered(1))


def _ffn(h, g, w_up, conv_w, conv_b, w_down, final_g, l, tm, final):
    nb, seq, d = h.shape
    prev, nxt = _halo_specs(seq, tm, d)
    return pl.pallas_call(
        functools.partial(_ffn_kernel, tiles_per_seq=seq // tm, tm=tm, final=final),
        grid=(nb * seq // tm,),
        in_specs=[_row_spec(seq, tm, d), prev, nxt,
                  _layer_spec(g, l), _resident(w_up, l), _layer_spec(conv_w, l), _layer_spec(conv_b, l),
                  _resident(w_down, l), _full(final_g.shape)],
        out_specs=_row_spec(seq, tm, d),
        out_shape=jax.ShapeDtypeStruct((nb, seq, d), F32),
        scratch_shapes=[pltpu.VMEM((d // LANES, tm + 2 * HALO, LANES), F32),
                        pltpu.VMEM((2 * (D_FF // FFN_TILE), tm + 2 * HALO, FFN_TILE), F32),
                        pltpu.VMEM((tm, D_FF), BF16),
                        pltpu.VMEM((d // LANES, tm, LANES), F32)],
        compiler_params=_params(("parallel",)),
        name="ffn_final" if final else "ffn",
    )(h, h, h, g, w_up, conv_w, conv_b, w_down, final_g)


def _row_vec(a):
    return a.astype(F32)[:, None, :]


def _trunk(x, p, *, tm, tm_mix, t_scan, tf):
    seq = x.shape[1]
    depth = p["w_in"].shape[0]
    plan = _FftPlan(seq)

    proj_c = (_row_vec(p["norm1_g"]), p["w_in"].astype(BF16), p["rg_conv_w"].astype(F32),
              _row_vec(p["rg_conv_b"]), p["hy_conv_w"].astype(F32), _row_vec(p["hy_conv_b"]))
    rg_w, rg_bias, rg_sp = _rglru_weights(p)
    s5_w = _s5_weights(p)
    hy_fw = _hyena_filter_weights(p)
    mix_c = (_row_vec(p["s5_d"]), p["s5_glu_w"].astype(BF16), _row_vec(p["s5_glu_b"]),
             _row_vec(p["hy_bias"]), _row_vec(p["mix_norm_g"]), p["w_out"].astype(BF16))
    ffn_c = (_row_vec(p["norm2_g"]), p["w_up"].astype(BF16), p["ffn_conv_w"].astype(F32),
             _row_vec(p["ffn_conv_b"]), p["w_down"].astype(BF16), p["final_norm_g"].astype(F32)[None])

    h = x.astype(F32)
    for l in range(depth):
        urg, ga, ub, x0, z = _proj(h, *proj_c, l, tm_mix)
        hf, hb, yf, yb = _scans(urg, ub, rg_w, rg_bias, rg_sp, *s5_w, l, t_scan)
        filt = _hyena_filters(seq, hy_fw, l, tf)
        yc = _hyena_conv(z, filt, plan)
        h = _mix([h, hf, hb, ga, yf, yb, ub, yc, z, x0], mix_c, l, tm_mix)
        h = _ffn(h, *ffn_c, l, tm, final=(l == depth - 1))
    return h.astype(x.dtype)


def kernel(x, norm1_g, w_in, rg_conv_w, rg_conv_b, rg_wa, rg_ba, rg_wx, rg_bx, rg_lambda, s5_a_re, s5_a_im, s5_log_dt, s5_b_re, s5_b_im, s5_c_re, s5_c_im, s5_d, s5_glu_w, s5_glu_b, hy_conv_w, hy_conv_b, hy_filt_w1, hy_filt_b1, hy_filt_freq1, hy_filt_w2, hy_filt_b2, hy_filt_freq2, hy_filt_w3, hy_bias, mix_norm_g, w_out, norm2_g, w_up, ffn_conv_w, ffn_conv_b, w_down, final_norm_g):
    p = dict(norm1_g=norm1_g, w_in=w_in, rg_conv_w=rg_conv_w, rg_conv_b=rg_conv_b, rg_wa=rg_wa, rg_ba=rg_ba,
             rg_wx=rg_wx, rg_bx=rg_bx, rg_lambda=rg_lambda, s5_a_re=s5_a_re, s5_a_im=s5_a_im,
             s5_log_dt=s5_log_dt, s5_b_re=s5_b_re, s5_b_im=s5_b_im, s5_c_re=s5_c_re, s5_c_im=s5_c_im,
             s5_d=s5_d, s5_glu_w=s5_glu_w, s5_glu_b=s5_glu_b, hy_conv_w=hy_conv_w, hy_conv_b=hy_conv_b,
             hy_filt_w1=hy_filt_w1, hy_filt_b1=hy_filt_b1, hy_filt_freq1=hy_filt_freq1, hy_filt_w2=hy_filt_w2,
             hy_filt_b2=hy_filt_b2, hy_filt_freq2=hy_filt_freq2, hy_filt_w3=hy_filt_w3, hy_bias=hy_bias,
             mix_norm_g=mix_norm_g, w_out=w_out, norm2_g=norm2_g, w_up=w_up, ffn_conv_w=ffn_conv_w,
             ffn_conv_b=ffn_conv_b, w_down=w_down, final_norm_g=final_norm_g)
    return _trunk(x, p, tm=512, tm_mix=1024, t_scan=128, tf=512)
```

```python
import functools
import math

import numpy as np
import jax
import jax.numpy as jnp
from jax import lax
from jax.experimental import pallas as pl
from jax.experimental.pallas import tpu as pltpu

F32 = jnp.float32
BF16 = jnp.bfloat16
STREAM = jnp.bfloat16

RMS_EPS = 1e-6
RG_WIDTH = 384
RG_HEADS = 6
RG_C = 8.0
S5_WIDTH = 384
S5_GROUP = 16
S5_GROUPS = 24
S5_STATE = 64
S5_NSTATE = S5_GROUPS * S5_STATE
HY_WIDTH = 256
HY_BANDS = 16
HY_FAST_DECAY = 0.3
HY_SLOW_DECAY = 1.5
HY_DECAY_TARGET = 1e-2
D_FF = 2816
N_DIR = 2

LANES = 128
SUBLANES = 8
HALO = SUBLANES
VMEM_LIMIT = 56 * 1024 * 1024

HIGHEST = lax.Precision.HIGHEST


def _dot(a, b, precision=None):
    return jnp.dot(a, b, preferred_element_type=F32, precision=precision)


def _gelu(x):
    c = math.sqrt(2.0 / math.pi)
    return 0.5 * x * (1.0 + jnp.tanh(c * (x + 0.044715 * (x * x * x))))


def _sigmoid(x):
    return 0.5 * jnp.tanh(0.5 * x) + 0.5


def _rms_nogain(x):
    return x * lax.rsqrt(jnp.mean(x * x, axis=-1, keepdims=True) + RMS_EPS)


def _params(sem):
    return pltpu.CompilerParams(dimension_semantics=sem, vmem_limit_bytes=VMEM_LIMIT)


def _full(shape):
    nd = len(shape)
    return pl.BlockSpec(shape, lambda *_: (0,) * nd)


def _layer_spec(arr, *lead):
    rest = arr.shape[len(lead):]
    zeros = (0,) * len(rest)
    return pl.BlockSpec((None,) * len(lead) + rest, lambda *_: tuple(lead) + zeros)


def _row_spec(seq, tm, c):
    tps = seq // tm
    return pl.BlockSpec((None, tm, c), lambda i: (i // tps, i % tps, 0))


def _halo_specs(seq, tm, c):
    tps = seq // tm
    bpt = tm // HALO
    prev = pl.BlockSpec((None, HALO, c), lambda i: (i // tps, jnp.maximum((i % tps) * bpt - 1, 0), 0))
    nxt = pl.BlockSpec((None, HALO, c),
                       lambda i: (i // tps, jnp.minimum((i % tps + 1) * bpt, seq // HALO - 1), 0))
    return prev, nxt


def _fill_normed(x_scr, h_ref, hp_ref, hn_ref, g, first, last, tm):
    def norm(x):
        return x * lax.rsqrt(jnp.mean(x * x, axis=-1, keepdims=True) + RMS_EPS) * g

    x_scr[0:HALO, :] = norm(jnp.where(first, 0.0, hp_ref[...]))
    x_scr[HALO:HALO + tm, :] = norm(h_ref[...])
    x_scr[HALO + tm:2 * HALO + tm, :] = norm(jnp.where(last, 0.0, hn_ref[...]))


def _proj_kernel(h_ref, hp_ref, hn_ref, g_ref, w_ref, rcw_ref, rcb_ref, hcw_ref, hcb_ref,
                 urg_ref, ga_ref, ub_ref, x0_ref, z_ref, x_scr, u_scr, *, tiles_per_seq, tm):
    i = pl.program_id(0)
    first = (i % tiles_per_seq) == 0
    last = (i % tiles_per_seq) == tiles_per_seq - 1
    _fill_normed(x_scr, h_ref, hp_ref, hn_ref, g_ref[...], first, last, tm)
    a0, a1, a2, a3, a4 = 0, RG_WIDTH, 2 * RG_WIDTH, 2 * RG_WIDTH + S5_WIDTH, w_ref.shape[1]
    x = x_scr[...].astype(BF16)
    for lo, hi in ((a0, a1), (a3, a4), (a1, a3)):
        u_scr[:, lo:hi] = _dot(x, w_ref[:, lo:hi])
    acc = rcb_ref[...]
    for j in range(rcw_ref.shape[0]):
        acc = acc + u_scr[pl.ds(HALO + j - 1, tm), a0:a1] * rcw_ref[j:j + 1, :]
    urg_ref[...] = acc
    ga_ref[...] = u_scr[pl.ds(HALO, tm), a1:a2].astype(ga_ref.dtype)
    ub_ref[...] = u_scr[pl.ds(HALO, tm), a2:a3].astype(ub_ref.dtype)
    q = []
    for part in range(3):
        lo = part * HY_WIDTH
        acc = hcb_ref[:, lo:lo + HY_WIDTH]
        for j in range(hcw_ref.shape[0]):
            acc = acc + (u_scr[pl.ds(HALO + j - 1, tm), a3 + lo:a3 + lo + HY_WIDTH]
                         * hcw_ref[j:j + 1, lo:lo + HY_WIDTH])
        q.append(acc)
    x0_ref[...] = q[0].astype(x0_ref.dtype)
    z_ref[...] = (q[2] * q[1]).astype(z_ref.dtype)


def _proj(h, g, w_in, rcw, rcb, hcw, hcb, l, tm):
    nb, seq, d = h.shape
    cols = w_in.shape[-1]
    prev, nxt = _halo_specs(seq, tm, d)
    row = lambda c: _row_spec(seq, tm, c)
    outs = [jax.ShapeDtypeStruct((nb, seq, c), dt) for c, dt in
            ((RG_WIDTH, F32), (RG_WIDTH, STREAM), (S5_WIDTH, STREAM), (HY_WIDTH, STREAM), (HY_WIDTH, STREAM))]
    consts = [g, w_in, rcw, rcb, hcw, hcb]
    return pl.pallas_call(
        functools.partial(_proj_kernel, tiles_per_seq=seq // tm, tm=tm),
        grid=(nb * seq // tm,),
        in_specs=[row(d), prev, nxt] + [_layer_spec(a, l) for a in consts],
        out_specs=[row(RG_WIDTH), row(RG_WIDTH), row(S5_WIDTH), row(HY_WIDTH), row(HY_WIDTH)],
        out_shape=outs,
        scratch_shapes=[pltpu.VMEM((tm + 2 * HALO, d), F32), pltpu.VMEM((tm + 2 * HALO, cols), F32)],
        compiler_params=_params(("parallel",)),
        name="proj",
    )(h, h, h, *consts)


RG_STAGES = 4


def _lo_mask():
    return lax.broadcasted_iota(jnp.int32, (SUBLANES, LANES), 0) < (SUBLANES // 2)


def _rglru_phases(uf_ref, ub_ref, w_ref, bias_ref, sp_ref, of_ref, ob_ref,
                  af, ab, bf, bb, carry_scr, *, t, nb):
    nm = RG_WIDTH // LANES
    tq = t // RG_STAGES

    def stage(q):
        for dr, (u_ref, a_scr, b_scr) in enumerate(((uf_ref, af, bf), (ub_ref, ab, bb))):
            t0 = q * tq if dr == 0 else t - (q + 1) * tq
            u = u_ref[:, t0:t0 + tq, :].reshape(nb * tq, RG_WIDTH)
            gates = _dot(u.astype(BF16), w_ref[dr]) + bias_ref[dr]
            r = _sigmoid(gates[:, :RG_WIDTH])
            gi = _sigmoid(gates[:, RG_WIDTH:])
            log_a = (-RG_C) * r * sp_ref[dr]
            a = jnp.exp(log_a)
            m2 = -jnp.tanh(log_a) * (a * a + 1.0)
            mult = jnp.where(m2 > 0.0, m2 * lax.rsqrt(m2), 0.0)
            bin_ = mult * (gi * u)
            for b in range(nb):
                for m in range(nm):
                    rows = pl.ds(t0 * SUBLANES + dr * nb + b, tq, stride=SUBLANES)
                    a_scr[m, rows, :] = a[b * tq:(b + 1) * tq, m * LANES:(m + 1) * LANES]
                    b_scr[m, rows, :] = bin_[b * tq:(b + 1) * tq, m * LANES:(m + 1) * LANES]

    def scan():
        lo = _lo_mask()
        carry = [carry_scr[m] for m in range(nm)]
        for j in range(t):
            row = j * SUBLANES
            mrow = (t - 1 - j) * SUBLANES
            for m in range(nm):
                at = jnp.where(lo, af[m, row:row + SUBLANES, :], ab[m, mrow:mrow + SUBLANES, :])
                bt = jnp.where(lo, bf[m, row:row + SUBLANES, :], bb[m, mrow:mrow + SUBLANES, :])
                h = at * carry[m] + bt
                bf[m, row:row + SUBLANES, :] = h
                bb[m, mrow:mrow + SUBLANES, :] = h
                carry[m] = h
        for m in range(nm):
            carry_scr[m] = carry[m]
        for b in range(nb):
            for m in range(nm):
                of_ref[b, :, m * LANES:(m + 1) * LANES] = bf[m, pl.ds(b, t, stride=SUBLANES), :].astype(STREAM)
                ob_ref[b, :, m * LANES:(m + 1) * LANES] = (
                    bb[m, pl.ds(nb + b, t, stride=SUBLANES), :].astype(STREAM))

    return [functools.partial(stage, q) for q in range(RG_STAGES)] + [scan]


def _rglru_weights(p):
    hd = RG_WIDTH // RG_HEADS
    place = np.zeros((2, RG_HEADS, hd, 2 * RG_WIDTH), np.float32)
    for q in range(2):
        for h in range(RG_HEADS):
            place[q, h, np.arange(hd), q * RG_WIDTH + h * hd + np.arange(hd)] = 1.0
    both = jnp.stack([p["rg_wa"], p["rg_wx"]], axis=2).astype(F32)
    depth = both.shape[0]
    w = jnp.einsum('ldqhij,qhjc->ldhic', both, jnp.asarray(place))
    w = w.reshape(depth, N_DIR, RG_WIDTH, 2 * RG_WIDTH).astype(BF16)
    bias = jnp.concatenate([p["rg_ba"], p["rg_bx"]], axis=-1).astype(F32)[:, :, None, :]
    x = -p["rg_lambda"].astype(F32)
    sp = (jnp.maximum(x, 0.0) + jnp.log1p(jnp.exp(-jnp.abs(x))))[:, :, None, :]
    return w, bias, sp


S5_NBLK = S5_NSTATE // LANES
S5_GRP = 4
S5_PER = S5_NBLK // (S5_WIDTH // LANES)


def _reverse_tiles(src, dst, nm, ntile):
    for j in range(ntile):
        s = (ntile - 1 - j) * SUBLANES
        for m in range(nm):
            dst[m, j * SUBLANES:(j + 1) * SUBLANES, :] = src[m, s:s + SUBLANES, :]


def _s5_phases(uf_ref, ub_ref, wb_ref, wc_ref, ar_ref, ai_ref, yf_ref, yb_ref,
               uf8, ub8, ubr, bu, ym, ymr, carry_scr, *, t, nb):
    nm = S5_WIDTH // LANES
    assert S5_GRP == S5_PER

    def load():
        for b in range(nb):
            for m in range(nm):
                lanes = slice(m * LANES, (m + 1) * LANES)
                uf8[m, pl.ds(b, t, stride=SUBLANES), :] = uf_ref[b, :, lanes].astype(F32)
                ub8[m, pl.ds(nb + b, t, stride=SUBLANES), :] = ub_ref[b, :, lanes].astype(F32)
        _reverse_tiles(ub8, ubr, nm, t)

    def bproj(m):
        lhs = jnp.concatenate([uf8[m], ubr[m]], axis=1).astype(BF16)
        res = _dot(lhs, wb_ref[m])
        for q in range(2 * S5_PER):
            bu[2 * S5_PER * m + q] = res[:, q * LANES:(q + 1) * LANES]

    def scan(g):
        blocks = list(range(g * S5_GRP, (g + 1) * S5_GRP))
        ars = [ar_ref[n] for n in blocks]
        ais = [ai_ref[n] for n in blocks]
        carry = []
        for n in blocks:
            carry += [carry_scr[2 * n], carry_scr[2 * n + 1]]
        for j in range(t):
            rows = slice(j * SUBLANES, (j + 1) * SUBLANES)
            for k, n in enumerate(blocks):
                hr, hi = carry[2 * k], carry[2 * k + 1]
                nr = ars[k] * hr - ais[k] * hi + bu[2 * n, rows, :]
                ni = ars[k] * hi + ais[k] * hr + bu[2 * n + 1, rows, :]
                bu[2 * n, rows, :] = nr
                bu[2 * n + 1, rows, :] = ni
                carry[2 * k], carry[2 * k + 1] = nr, ni
        for k, n in enumerate(blocks):
            carry_scr[2 * n] = carry[2 * k]
            carry_scr[2 * n + 1] = carry[2 * k + 1]

    def cproj(m):
        rows = lax.broadcasted_iota(jnp.int32, (SUBLANES * t, LANES), 0)
        fwd_row = (rows & (SUBLANES - 1)) < nb
        hcat = jnp.concatenate([bu[2 * S5_PER * m + q] for q in range(2 * S5_PER)], axis=1).astype(BF16)
        acc = _dot(hcat, wc_ref[m])
        ym[m] = jnp.where(fwd_row, acc[:, :LANES], acc[:, LANES:])

    def store():
        _reverse_tiles(ym, ymr, nm, t)
        for b in range(nb):
            for m in range(nm):
                lanes = slice(m * LANES, (m + 1) * LANES)
                yf_ref[b, :, lanes] = ym[m, pl.ds(b, t, stride=SUBLANES), :].astype(STREAM)
                yb_ref[b, :, lanes] = ymr[m, pl.ds(nb + b, t, stride=SUBLANES), :].astype(STREAM)

    part = functools.partial
    return (load, [part(bproj, m) for m in range(nm)], [part(scan, g) for g in range(nm)],
            [part(cproj, m) for m in range(nm)], store)


def _scans_kernel(*refs, t, nb):
    rg_in, s5_in = refs[0:5], refs[5:11]
    rg_out, s5_out = refs[11:13], refs[13:15]
    rg_scr, s5_scr = refs[15:20], refs[20:27]

    @pl.when(pl.program_id(0) == 0)
    def _():
        for scr in (*rg_scr, s5_scr[0], s5_scr[1], s5_scr[6]):
            scr[...] = jnp.zeros_like(scr)

    rg = _rglru_phases(*rg_in, *rg_out, *rg_scr, t=t, nb=nb)
    load, bproj, scan, cproj, store = _s5_phases(*s5_in, *s5_out, *s5_scr, t=t, nb=nb)
    order = [load, bproj[0], rg[0], bproj[1], rg[1], scan[0], bproj[2], rg[2], scan[1], cproj[0],
             rg[3], scan[2], cproj[1], rg[4], cproj[2], store]
    assert len(rg) == 5 and len(bproj) == 3
    for phase in order:
        phase()


def _scans(urg, ub, rg_w, rg_bias, rg_sp, wb, wc, ar, ai, l, t):
    nb, seq, c = urg.shape
    assert 2 * nb == SUBLANES and ub.shape == urg.shape
    nchunk = seq // t
    fwd = pl.BlockSpec((nb, t, c), lambda i: (0, i, 0))
    bwd = pl.BlockSpec((nb, t, c), lambda i: (0, nchunk - 1 - i, 0))
    nm = c // LANES
    rows = SUBLANES * t
    out = jax.ShapeDtypeStruct((nb, seq, c), STREAM)
    slab = pltpu.VMEM((nm, rows, LANES), F32)
    rg_scr = [slab, slab, slab, slab, pltpu.VMEM((nm, SUBLANES, LANES), F32)]
    s5_scr = [slab, slab, slab, pltpu.VMEM((2 * S5_NBLK, rows, LANES), F32), slab, slab,
              pltpu.VMEM((2 * S5_NBLK, SUBLANES, LANES), F32)]
    rg_c = [rg_w, rg_bias, rg_sp]
    s5_c = [wb, wc, ar, ai]
    return pl.pallas_call(
        functools.partial(_scans_kernel, t=t, nb=nb),
        grid=(nchunk,),
        in_specs=[fwd, bwd] + [_layer_spec(a, l) for a in rg_c]
                 + [fwd, bwd] + [_layer_spec(a, l) for a in s5_c],
        out_specs=[fwd, bwd, fwd, bwd],
        out_shape=[out, out, out, out],
        scratch_shapes=rg_scr + s5_scr,
        compiler_params=_params(("arbitrary",)),
        name="scans",
    )(urg, urg, *rg_c, ub, ub, *s5_c)


def _s5_place():
    slots = LANES // S5_GROUP
    half = LANES // S5_STATE
    place = np.zeros((slots, 2, S5_STATE, S5_PER * 2 * LANES), np.float32)
    for k in range(slots):
        for r in range(2):
            col = (k // half) * 2 * LANES + r * LANES + (k % half) * S5_STATE
            place[k, r, np.arange(S5_STATE), col + np.arange(S5_STATE)] = 1.0
    return place


def _s5_weights(p):
    lr = p["s5_a_re"].astype(F32)
    li = p["s5_a_im"].astype(F32)
    dt = jnp.exp(p["s5_log_dt"].astype(F32))[..., None]
    mag = jnp.exp(lr * dt)
    abar_r = mag * jnp.cos(li * dt)
    abar_i = mag * jnp.sin(li * dt)
    den = lr * lr + li * li
    nr = abar_r - 1.0
    ni = abar_i
    coef_r = ((nr * lr + ni * li) / den)[..., None]
    coef_i = ((ni * lr - nr * li) / den)[..., None]
    b_re = p["s5_b_re"].astype(F32)
    b_im = p["s5_b_im"].astype(F32)
    bbar = jnp.stack([coef_r * b_re - coef_i * b_im, coef_r * b_im + coef_i * b_re], axis=2)
    depth = lr.shape[0]
    nm = S5_WIDTH // LANES
    slots = LANES // S5_GROUP
    place = jnp.asarray(_s5_place())
    bb = bbar.reshape(depth, N_DIR, 2, nm, slots, S5_STATE, S5_GROUP)
    wb = jnp.einsum('ldrmkpc,krpx->lmdkcx', bb, place).reshape(depth, nm, 2 * LANES, S5_PER * 2 * LANES)
    cc = jnp.stack([p["s5_c_re"].astype(F32), -p["s5_c_im"].astype(F32)], axis=2)
    cc = cc.reshape(depth, N_DIR, 2, nm, slots, S5_GROUP, S5_STATE)
    wc = jnp.einsum('ldrmkcp,krpx->lmdkcx', cc, place).reshape(depth, nm, 2 * LANES, S5_PER * 2 * LANES)
    wc = wc.transpose(0, 1, 3, 2)

    def tile_rows(a):
        a = a.reshape(depth, N_DIR, S5_NBLK, LANES).transpose(0, 2, 1, 3)
        return jnp.repeat(a, SUBLANES // N_DIR, axis=2)

    return wb.astype(BF16), wc.astype(BF16), tile_rows(abar_r), tile_rows(abar_i)


def _odd8(n):
    p = -(-n // SUBLANES)
    if p % 2 == 0:
        p += 1
    return p * SUBLANES


class _FftPlan:
    def __init__(self, seq):
        self.seq = seq
        r = 1
        while r * r < seq:
            r *= 2
        self.r = r
        self.nq = seq // r
        self.q = 2 * seq // r
        self.k1n = self.q // 2 + 1
        self.kp = -(-self.k1n // SUBLANES) * SUBLANES
        self.zpitch = _odd8(r)
        self.apitch = _odd8(2 * self.kp)
        self.cpitch = _odd8(2 * r)
        self.unroll = min(8, r)
        self.unroll2 = next(u for u in (13, 5, 4, 3, 2, 1) if self.k1n % u == 0)
        p = 2 * seq
        n1 = np.arange(self.nq)[None, :]
        k1 = np.arange(self.k1n)[:, None]
        ang = 2.0 * np.pi * n1 * k1 / self.q
        f1 = np.zeros((2 * self.kp, self.nq))
        f1[:self.k1n] = np.cos(ang)
        f1[self.kp:self.kp + self.k1n] = -np.sin(ang)
        self.f1 = f1
        w = np.full((self.k1n,), 2.0)
        w[0] = 1.0
        w[-1] = 1.0
        g1 = np.zeros((self.nq, 2 * self.kp))
        g1[:, :self.k1n] = (np.cos(ang) * w[:, None] / p).T
        g1[:, self.kp:self.kp + self.k1n] = (-np.sin(ang) * w[:, None] / p).T
        self.g1 = g1
        kk = np.arange(self.k1n)[:, None, None]
        k2 = np.arange(r)[None, :, None]
        n2 = np.arange(r)[None, None, :]
        ph = 2.0 * np.pi * (n2 * k2 / r + n2 * kk / p)
        tr, ti = np.cos(ph), -np.sin(ph)
        self.m2 = np.concatenate([np.concatenate([tr, -ti], axis=2),
                                  np.concatenate([ti, tr], axis=2)], axis=1)
        ur, ui = np.transpose(tr, (0, 2, 1)), -np.transpose(ti, (0, 2, 1))
        self.m2i = np.concatenate([np.concatenate([ur, -ui], axis=2),
                                   np.concatenate([ui, ur], axis=2)], axis=1)


def _slab_load(scr, rows):
    return jnp.concatenate([scr[k, rows, :] for k in range(scr.shape[0])], axis=1)


def _slab_store(scr, rows, val):
    for k in range(scr.shape[0]):
        scr[k, rows, :] = val[:, k * LANES:(k + 1) * LANES]


def _fft_forward(plan, src_ref, f1_ref, m2_ref, zp, as_, emit):
    r, nq, kp = plan.r, plan.nq, plan.kp
    for n1 in range(nq):
        _slab_store(zp, slice(n1 * plan.zpitch, n1 * plan.zpitch + r), src_ref[n1 * r:(n1 + 1) * r, :].astype(F32))

    def stage1(n2, c):
        slab = _slab_load(zp, pl.ds(n2, nq, stride=plan.zpitch))
        a = _dot(f1_ref[...].astype(BF16), slab.astype(BF16))
        _slab_store(as_, pl.ds(pl.multiple_of(n2 * plan.apitch, SUBLANES), 2 * kp), a)
        return c

    lax.fori_loop(0, r, stage1, 0, unroll=plan.unroll)

    def stage2(k1, c):
        sr = _slab_load(as_, pl.ds(k1, r, stride=plan.apitch))
        si = _slab_load(as_, pl.ds(kp + k1, r, stride=plan.apitch))
        s = jnp.concatenate([sr, si], axis=0).astype(BF16)
        emit(k1, _dot(m2_ref[k1].astype(BF16), s))
        return c

    lax.fori_loop(0, plan.k1n, stage2, 0, unroll=plan.unroll2)


def _hy_spec_kernel(kf_ref, kb_ref, f1_ref, m2_ref, o_ref, zp, as_, *, plan):
    r = plan.r

    def emit_f(k1, x):
        o_ref[k1] = x

    def emit_b(k1, x):
        sign = jnp.where(lax.broadcasted_iota(jnp.int32, x.shape, 0) < r, 1.0, -1.0)
        o_ref[k1] = o_ref[k1] + sign * x

    _fft_forward(plan, kf_ref, f1_ref, m2_ref, zp, as_, emit_f)
    _fft_forward(plan, kb_ref, f1_ref, m2_ref, zp, as_, emit_b)


def _hy_conv_kernel(z_ref, spec_ref, f1_ref, m2_ref, m2i_ref, g1_ref, o_ref, zp, as_, cs, *, plan):
    r, nq, kp, k1n = plan.r, plan.nq, plan.kp, plan.k1n
    if kp > k1n:
        pad = slice(k1n * plan.cpitch, kp * plan.cpitch)
        for k in range(cs.shape[0]):
            cs[k, pad, :] = jnp.zeros(((kp - k1n) * plan.cpitch, LANES), F32)

    def emit(k1, x):
        kf = spec_ref[k1]
        xr, xi = x[:r], x[r:]
        kr, ki = kf[:r], kf[r:]
        prod = jnp.concatenate([xr * kr - xi * ki, xr * ki + xi * kr], axis=0).astype(BF16)
        c = _dot(m2i_ref[k1].astype(BF16), prod)
        _slab_store(cs, pl.ds(pl.multiple_of(k1 * plan.cpitch, SUBLANES), 2 * r), c)

    _fft_forward(plan, z_ref, f1_ref, m2_ref, zp, as_, emit)

    def stage3(n2, c):
        cr = _slab_load(cs, pl.ds(n2, kp, stride=plan.cpitch))
        ci = _slab_load(cs, pl.ds(r + n2, kp, stride=plan.cpitch))
        y = _dot(g1_ref[...].astype(BF16), jnp.concatenate([cr, ci], axis=0).astype(BF16))
        _slab_store(zp, pl.ds(n2, nq, stride=plan.zpitch), y)
        return c

    lax.fori_loop(0, r, stage3, 0, unroll=plan.unroll)
    for n1 in range(nq):
        o_ref[n1 * r:(n1 + 1) * r, :] = _slab_load(
            zp, slice(n1 * plan.zpitch, n1 * plan.zpitch + r)).astype(o_ref.dtype)


def _hy_filter_kernel(feat_ref, w1_ref, b1_ref, f1_ref, w2_ref, b2_ref, f2_ref, w3_ref, dl_ref, o_ref):
    feats = feat_ref[...]
    half = feats.shape[0] // 2
    cols = o_ref.shape[1]
    both = jnp.concatenate([feats[:half], feats[half:]], axis=1)
    hid = jnp.sin(f1_ref[...] * (_dot(both, w1_ref[...], HIGHEST) + b1_ref[...]))
    hid = jnp.sin(f2_ref[...] * (_dot(hid, w2_ref[...], HIGHEST) + b2_ref[...]))
    k = _dot(hid, w3_ref[...], HIGHEST)
    o_ref[:half, :] = k[:, :cols] * jnp.exp(-(feats[:half, 0:1] * dl_ref[...]))
    o_ref[half:, :] = k[:, cols:] * jnp.exp(-(feats[half:, 0:1] * dl_ref[...]))


def _hyena_features(seq):
    pos = np.arange(seq, dtype=np.float64)
    t = pos / max(seq - 1, 1)
    w = (2.0 * math.pi / seq) * pos
    bands = np.linspace(1e-4, HY_BANDS - 1, HY_BANDS, dtype=np.float64)
    ang = w[:, None] * bands
    feats = np.concatenate([t[:, None], np.cos(ang), -np.sin(ang)], axis=-1).astype(np.float32)
    out = np.zeros((seq, LANES), np.float32)
    out[:, :feats.shape[1]] = feats
    return out


def _hyena_deltas():
    max_decay = math.log(HY_DECAY_TARGET) / HY_FAST_DECAY
    min_decay = math.log(HY_DECAY_TARGET) / HY_SLOW_DECAY
    deltas = np.abs(np.linspace(min_decay, max_decay, HY_WIDTH, dtype=np.float64))
    return np.tile(deltas, 2)[None, :].astype(np.float32)


def _pad_to(x, rows, cols):
    x = x.astype(F32)
    return jnp.pad(x, ((0, 0), (0, rows - x.shape[1]), (0, cols - x.shape[2])))


def _block_diag2(a):
    a = a.astype(F32)
    z = jnp.zeros_like(a)
    return jnp.concatenate([jnp.concatenate([a, z], axis=2), jnp.concatenate([z, a], axis=2)], axis=1)


def _hyena_filter_weights(p):
    hidden = p["hy_filt_w1"].shape[2]
    assert 2 * hidden == LANES
    vec = lambda a: jnp.tile(a.astype(F32)[:, None, :], (1, 1, 2))
    return (_block_diag2(_pad_to(p["hy_filt_w1"], LANES, hidden)), vec(p["hy_filt_b1"]),
            vec(p["hy_filt_freq1"]), _block_diag2(p["hy_filt_w2"]), vec(p["hy_filt_b2"]),
            vec(p["hy_filt_freq2"]), _block_diag2(p["hy_filt_w3"]))


def _once(a):
    return pl.BlockSpec(a.shape, lambda *_: (0,) * a.ndim, pipeline_mode=pl.Buffered(1))


def _hyena_scratch(plan, c):
    nh = c // LANES
    return [pltpu.VMEM((nh, plan.nq * plan.zpitch, LANES), F32),
            pltpu.VMEM((nh, plan.r * plan.apitch, LANES), F32),
            pltpu.VMEM((nh, plan.kp * plan.cpitch, LANES), F32)]


def _hyena_spectra(seq, fw, plan, tf):
    depth = fw[0].shape[0]
    feats = jnp.asarray(_hyena_features(seq))
    deltas = jnp.asarray(_hyena_deltas())
    per_layer = lambda a: pl.BlockSpec((None,) + a.shape[1:], lambda l, i: (l,) + (0,) * (a.ndim - 1))
    filt = pl.pallas_call(
        _hy_filter_kernel,
        grid=(depth, seq // tf),
        in_specs=[pl.BlockSpec((tf, LANES), lambda l, i: (i, 0))] + [per_layer(a) for a in fw]
                 + [_full(deltas.shape)],
        out_specs=pl.BlockSpec((None, tf, 2 * HY_WIDTH), lambda l, i: (l, i, 0)),
        out_shape=jax.ShapeDtypeStruct((depth, seq, 2 * HY_WIDTH), F32),
        compiler_params=_params(("parallel", "parallel")),
        name="hy_filter",
    )(feats, *fw, deltas)
    f1 = jnp.asarray(plan.f1, F32)
    m2 = jnp.asarray(plan.m2, F32)
    c = HY_WIDTH
    return pl.pallas_call(
        functools.partial(_hy_spec_kernel, plan=plan),
        grid=(depth,),
        in_specs=[pl.BlockSpec((None, seq, c), lambda l: (l, 0, 0)),
                  pl.BlockSpec((None, seq, c), lambda l: (l, 0, 1)), _once(f1), _once(m2)],
        out_specs=pl.BlockSpec((None, plan.k1n, 2 * plan.r, c), lambda l: (l, 0, 0, 0)),
        out_shape=jax.ShapeDtypeStruct((depth, plan.k1n, 2 * plan.r, c), F32),
        scratch_shapes=_hyena_scratch(plan, c)[:2],
        compiler_params=_params(("arbitrary",)),
        name="hy_spec",
    )(filt, filt, f1, m2)


def _hyena_conv(z3, spec, l, plan):
    nb, seq, c = z3.shape
    f1 = jnp.asarray(plan.f1, F32)
    m2 = jnp.asarray(plan.m2, F32)
    m2i = jnp.asarray(plan.m2i, F32)
    g1 = jnp.asarray(plan.g1, F32)
    return pl.pallas_call(
        functools.partial(_hy_conv_kernel, plan=plan),
        grid=(nb,),
        in_specs=[pl.BlockSpec((None, seq, c), lambda b: (b, 0, 0)),
                  _resident(spec, l), _once(f1), _once(m2), _once(m2i), _once(g1)],
        out_specs=pl.BlockSpec((None, seq, c), lambda b: (b, 0, 0)),
        out_shape=jax.ShapeDtypeStruct((nb, seq, c), STREAM),
        scratch_shapes=_hyena_scratch(plan, c),
        compiler_params=_params(("parallel",)),
        name="hy_conv",
    )(z3, spec, f1, m2, m2i, g1)


def _mix_kernel(h_ref, hf_ref, hb_ref, ga_ref, yf_ref, yb_ref, ub_ref, yc_ref, z_ref, x0_ref,
                d_ref, gw_ref, gb_ref, hbias_ref, mg_ref, wo_ref, o_ref):
    f32 = lambda ref: ref[...].astype(F32)
    ya = (f32(hf_ref) + f32(hb_ref)) * _gelu(f32(ga_ref))
    yb = _gelu(f32(ub_ref) * d_ref[...] + f32(yf_ref) + f32(yb_ref))
    yb = yb * _sigmoid(_dot(yb.astype(BF16), gw_ref[...]) + gb_ref[...])
    yc = (f32(yc_ref) + f32(z_ref) * hbias_ref[...]) * f32(x0_ref)
    a1, a2 = RG_WIDTH, RG_WIDTH + S5_WIDTH
    na = (_rms_nogain(ya) * mg_ref[:, :a1]).astype(BF16)
    nb = (_rms_nogain(yb) * mg_ref[:, a1:a2]).astype(BF16)
    nc = (_rms_nogain(yc) * mg_ref[:, a2:]).astype(BF16)
    out = _dot(na, wo_ref[:a1, :]) + _dot(nb, wo_ref[a1:a2, :]) + _dot(nc, wo_ref[a2:, :])
    o_ref[...] = h_ref[...] + out


def _mix(rows, consts, l, tm):
    nb, seq, dm = rows[0].shape
    return pl.pallas_call(
        _mix_kernel,
        grid=(nb * seq // tm,),
        in_specs=[_row_spec(seq, tm, a.shape[2]) for a in rows] + [_layer_spec(a, l) for a in consts],
        out_specs=_row_spec(seq, tm, dm),
        out_shape=jax.ShapeDtypeStruct((nb, seq, dm), F32),
        compiler_params=_params(("parallel",)),
        name="mix",
    )(*rows, *consts)


FFN_TILE = 256


def _ffn_kernel(h_ref, hp_ref, hn_ref, g_ref, wu_ref, cw_ref, cb_ref, wd_ref, fg_ref, o_ref,
                x_scr, u_scr, gated_scr, y_scr, *, tiles_per_seq, tm, final):
    i = pl.program_id(0)
    first = (i % tiles_per_seq) == 0
    last = (i % tiles_per_seq) == tiles_per_seq - 1
    nslab = h_ref.shape[1] // LANES
    ph = tm // SUBLANES
    g = g_ref[...]

    def norm(v):
        return v * lax.rsqrt(jnp.mean(v * v, axis=-1, keepdims=True) + RMS_EPS) * g

    xn = norm(h_ref[...])
    xp = norm(jnp.where(first, 0.0, hp_ref[...]))
    xq = norm(jnp.where(last, 0.0, hn_ref[...]))
    for c in range(nslab):
        lanes = slice(c * LANES, (c + 1) * LANES)
        for s in range(SUBLANES):
            x_scr[c, pl.ds(s, ph, stride=SUBLANES), :] = xn[s * ph:(s + 1) * ph, lanes]
        x_scr[c, tm:tm + HALO, :] = xp[:, lanes]
        x_scr[c, tm + HALO:tm + 2 * HALO, :] = xq[:, lanes]
    x = jnp.concatenate([x_scr[c] for c in range(nslab)], axis=1).astype(BF16)
    sub = lax.broadcasted_iota(jnp.int32, (SUBLANES, FFN_TILE), 0)
    assert cw_ref.shape[0] == 3
    for k in range(D_FF // FFN_TILE):
        halves = []
        for part in range(2):
            lo = part * D_FF + k * FFN_TILE
            slot = 2 * k + part
            u_scr[slot] = _dot(x, wu_ref[:, lo:lo + FFN_TILE])
            head = jnp.where(sub == 0, pltpu.roll(u_scr[slot, tm:tm + HALO, :], 1, 0),
                             pltpu.roll(u_scr[slot, tm - SUBLANES:tm, :], 1, 0))
            tail = jnp.where(sub == SUBLANES - 1,
                             pltpu.roll(u_scr[slot, tm + HALO:tm + 2 * HALO, :], SUBLANES - 1, 0),
                             pltpu.roll(u_scr[slot, 0:SUBLANES, :], SUBLANES - 1, 0))
            prv = jnp.concatenate([head, u_scr[slot, 0:tm - SUBLANES, :]], axis=0)
            nxt = jnp.concatenate([u_scr[slot, SUBLANES:tm, :], tail], axis=0)
            cols = slice(lo, lo + FFN_TILE)
            halves.append(cb_ref[:, cols] + prv * cw_ref[0:1, cols] + u_scr[slot, 0:tm, :] * cw_ref[1:2, cols]
                          + nxt * cw_ref[2:3, cols])
        gated_scr[:, k * FFN_TILE:(k + 1) * FFN_TILE] = (_gelu(halves[0]) * halves[1]).astype(BF16)
    y = _dot(gated_scr[...], wd_ref[...])
    for c in range(nslab):
        y_scr[c] = y[:, c * LANES:(c + 1) * LANES]
    for s in range(SUBLANES):
        for c in range(nslab):
            rows = slice(s * ph, (s + 1) * ph)
            lanes = slice(c * LANES, (c + 1) * LANES)
            o_ref[rows, lanes] = h_ref[rows, lanes] + y_scr[c, pl.ds(s, ph, stride=SUBLANES), :]
    if final:
        out = o_ref[...]
        o_ref[...] = out * lax.rsqrt(jnp.mean(out * out, axis=-1, keepdims=True) + RMS_EPS) * fg_ref[...]


def _resident(arr, *lead):
    rest = arr.shape[len(lead):]
    zeros = (0,) * len(rest)
    return pl.BlockSpec((None,) * len(lead) + rest, lambda *_: tuple(lead) + zeros,
                        pipeline_mode=pl.Buffered(1))


def _ffn(h, g, w_up, conv_w, conv_b, w_down, final_g, l, tm, final):
    nb, seq, d = h.shape
    prev, nxt = _halo_specs(seq, tm, d)
    return pl.pallas_call(
        functools.partial(_ffn_kernel, tiles_per_seq=seq // tm, tm=tm, final=final),
        grid=(nb * seq // tm,),
        in_specs=[_row_spec(seq, tm, d), prev, nxt,
                  _layer_spec(g, l), _resident(w_up, l), _layer_spec(conv_w, l), _layer_spec(conv_b, l),
                  _resident(w_down, l), _full(final_g.shape)],
        out_specs=_row_spec(seq, tm, d),
        out_shape=jax.ShapeDtypeStruct((nb, seq, d), F32),
        scratch_shapes=[pltpu.VMEM((d // LANES, tm + 2 * HALO, LANES), F32),
                        pltpu.VMEM((2 * (D_FF // FFN_TILE), tm + 2 * HALO, FFN_TILE), F32),
                        pltpu.VMEM((tm, D_FF), BF16),
                        pltpu.VMEM((d // LANES, tm, LANES), F32)],
        compiler_params=_params(("parallel",)),
        name="ffn_final" if final else "ffn",
    )(h, h, h, g, w_up, conv_w, conv_b, w_down, final_g)


def _row_vec(a):
    return a.astype(F32)[:, None, :]


def _trunk(x, p, *, tm, tm_wide, t_scan, tf):
    seq = x.shape[1]
    depth = p["w_in"].shape[0]
    plan = _FftPlan(seq)

    proj_c = (_row_vec(p["norm1_g"]), p["w_in"].astype(BF16), p["rg_conv_w"].astype(F32),
              _row_vec(p["rg_conv_b"]), p["hy_conv_w"].astype(F32), _row_vec(p["hy_conv_b"]))
    rg_w, rg_bias, rg_sp = _rglru_weights(p)
    s5_w = _s5_weights(p)
    hy_spec = _hyena_spectra(seq, _hyena_filter_weights(p), plan, tf)
    mix_c = (_row_vec(p["s5_d"]), p["s5_glu_w"].astype(BF16), _row_vec(p["s5_glu_b"]),
             _row_vec(p["hy_bias"]), _row_vec(p["mix_norm_g"]), p["w_out"].astype(BF16))
    ffn_c = (_row_vec(p["norm2_g"]), p["w_up"].astype(BF16), p["ffn_conv_w"].astype(F32),
             _row_vec(p["ffn_conv_b"]), p["w_down"].astype(BF16), p["final_norm_g"].astype(F32)[None])

    h = x.astype(F32)
    for l in range(depth):
        urg, ga, ub, x0, z = _proj(h, *proj_c, l, tm_wide)
        hf, hb, yf, yb = _scans(urg, ub, rg_w, rg_bias, rg_sp, *s5_w, l, t_scan)
        yc = _hyena_conv(z, hy_spec, l, plan)
        h = _mix([h, hf, hb, ga, yf, yb, ub, yc, z, x0], mix_c, l, tm_wide)
        h = _ffn(h, *ffn_c, l, tm, final=(l == depth - 1))
    return h.astype(x.dtype)


def kernel(x, norm1_g, w_in, rg_conv_w, rg_conv_b, rg_wa, rg_ba, rg_wx, rg_bx, rg_lambda, s5_a_re, s5_a_im, s5_log_dt, s5_b_re, s5_b_im, s5_c_re, s5_c_im, s5_d, s5_glu_w, s5_glu_b, hy_conv_w, hy_conv_b, hy_filt_w1, hy_filt_b1, hy_filt_freq1, hy_filt_w2, hy_filt_b2, hy_filt_freq2, hy_filt_w3, hy_bias, mix_norm_g, w_out, norm2_g, w_up, ffn_conv_w, ffn_conv_b, w_down, final_norm_g):
    p = dict(norm1_g=norm1_g, w_in=w_in, rg_conv_w=rg_conv_w, rg_conv_b=rg_conv_b, rg_wa=rg_wa, rg_ba=rg_ba,
             rg_wx=rg_wx, rg_bx=rg_bx, rg_lambda=rg_lambda, s5_a_re=s5_a_re, s5_a_im=s5_a_im,
             s5_log_dt=s5_log_dt, s5_b_re=s5_b_re, s5_b_im=s5_b_im, s5_c_re=s5_c_re, s5_c_im=s5_c_im,
             s5_d=s5_d, s5_glu_w=s5_glu_w, s5_glu_b=s5_glu_b, hy_conv_w=hy_conv_w, hy_conv_b=hy_conv_b,
             hy_filt_w1=hy_filt_w1, hy_filt_b1=hy_filt_b1, hy_filt_freq1=hy_filt_freq1, hy_filt_w2=hy_filt_w2,
             hy_filt_b2=hy_filt_b2, hy_filt_freq2=hy_filt_freq2, hy_filt_w3=hy_filt_w3, hy_bias=hy_bias,
             mix_norm_g=mix_norm_g, w_out=w_out, norm2_g=norm2_g, w_up=w_up, ffn_conv_w=ffn_conv_w,
             ffn_conv_b=ffn_conv_b, w_down=w_down, final_norm_g=final_norm_g)
    return _trunk(x, p, tm=512, tm_wide=1024, t_scan=128, tf=512)
```

```python
import functools
import math

import numpy as np
import jax
import jax.numpy as jnp
from jax import lax
from jax.experimental import pallas as pl
from jax.experimental.pallas import tpu as pltpu

F32 = jnp.float32
BF16 = jnp.bfloat16
STREAM = jnp.bfloat16

RMS_EPS = 1e-6
RG_WIDTH = 384
RG_HEADS = 6
RG_C = 8.0
S5_WIDTH = 384
S5_GROUP = 16
S5_GROUPS = 24
S5_STATE = 64
S5_NSTATE = S5_GROUPS * S5_STATE
HY_WIDTH = 256
HY_BANDS = 16
HY_FAST_DECAY = 0.3
HY_SLOW_DECAY = 1.5
HY_DECAY_TARGET = 1e-2
D_FF = 2816
N_DIR = 2

LANES = 128
SUBLANES = 8
HALO = SUBLANES
VMEM_LIMIT = 56 * 1024 * 1024

HIGHEST = lax.Precision.HIGHEST


def _dot(a, b, precision=None):
    return jnp.dot(a, b, preferred_element_type=F32, precision=precision)


def _gelu(x):
    c = math.sqrt(2.0 / math.pi)
    return 0.5 * x * (1.0 + jnp.tanh(c * (x + 0.044715 * (x * x * x))))


def _sigmoid(x):
    return 0.5 * jnp.tanh(0.5 * x) + 0.5


def _rms_nogain(x):
    return x * lax.rsqrt(jnp.mean(x * x, axis=-1, keepdims=True) + RMS_EPS)


def _params(sem):
    return pltpu.CompilerParams(dimension_semantics=sem, vmem_limit_bytes=VMEM_LIMIT)


def _full(shape):
    nd = len(shape)
    return pl.BlockSpec(shape, lambda *_: (0,) * nd)


def _layer_spec(arr, *lead):
    rest = arr.shape[len(lead):]
    zeros = (0,) * len(rest)
    return pl.BlockSpec((None,) * len(lead) + rest, lambda *_: tuple(lead) + zeros)


def _row_spec(seq, tm, c):
    tps = seq // tm
    return pl.BlockSpec((None, tm, c), lambda i: (i // tps, i % tps, 0))


def _halo_specs(seq, tm, c):
    tps = seq // tm
    bpt = tm // HALO
    prev = pl.BlockSpec((None, HALO, c), lambda i: (i // tps, jnp.maximum((i % tps) * bpt - 1, 0), 0))
    nxt = pl.BlockSpec((None, HALO, c),
                       lambda i: (i // tps, jnp.minimum((i % tps + 1) * bpt, seq // HALO - 1), 0))
    return prev, nxt


def _fill_normed_phase_major(x_scr, h, hp, hn, g, first, last, tm):
    ph = tm // SUBLANES

    def norm(v):
        return v * lax.rsqrt(jnp.mean(v * v, axis=-1, keepdims=True) + RMS_EPS) * g

    xn = norm(h)
    xp = norm(jnp.where(first, 0.0, hp))
    xq = norm(jnp.where(last, 0.0, hn))
    for c in range(x_scr.shape[0]):
        lanes = slice(c * LANES, (c + 1) * LANES)
        for s in range(SUBLANES):
            x_scr[c, pl.ds(s, ph, stride=SUBLANES), :] = xn[s * ph:(s + 1) * ph, lanes]
        x_scr[c, tm:tm + HALO, :] = xp[:, lanes]
        x_scr[c, tm + HALO:tm + 2 * HALO, :] = xq[:, lanes]


def _time_shifted(u_ref, cols, d, tm):
    width = cols.stop - cols.start
    sub = lax.broadcasted_iota(jnp.int32, (SUBLANES, width), 0)
    tile = lambda j: u_ref[j * SUBLANES:(j + 1) * SUBLANES, cols]
    prev_halo = u_ref[tm:tm + HALO, cols]
    next_halo = u_ref[tm + HALO:tm + 2 * HALO, cols]
    ph = tm // SUBLANES
    if d == 0:
        return u_ref[0:tm, cols]
    if d < 0:
        assert d == -1
        head = jnp.where(sub == 0, pltpu.roll(prev_halo, 1, 0), pltpu.roll(tile(ph - 1), 1, 0))
        return jnp.concatenate([head, u_ref[0:tm - SUBLANES, cols]], axis=0)
    tails = [jnp.where(sub == SUBLANES - 1, pltpu.roll(next_halo, SUBLANES - 1 - jj, 0),
                       pltpu.roll(tile(jj), SUBLANES - 1, 0)) for jj in range(d)]
    return jnp.concatenate([u_ref[d * SUBLANES:tm, cols]] + tails, axis=0)


def _dwconv(u_ref, cols, w_ref, b_ref, wcols, tm):
    taps = w_ref.shape[0]
    left = (taps - 1) // 2
    acc = b_ref[:, wcols]
    for j in range(taps):
        acc = acc + _time_shifted(u_ref, cols, j - left, tm) * w_ref[j:j + 1, wcols]
    return acc


def _store_time_major(o_ref, scr, val, lo, tm):
    ph = tm // SUBLANES
    for k in range(val.shape[1] // LANES):
        slab = lo // LANES + k
        scr[slab] = val[:, k * LANES:(k + 1) * LANES]
        for s in range(SUBLANES):
            o_ref[s * ph:(s + 1) * ph, k * LANES:(k + 1) * LANES] = (
                scr[slab, pl.ds(s, ph, stride=SUBLANES), :].astype(o_ref.dtype))


def _proj_kernel(h_ref, hp_ref, hn_ref, g_ref, w_ref, rcw_ref, rcb_ref, hcw_ref, hcb_ref,
                 urg_ref, ga_ref, ub_ref, x0_ref, z_ref, x_scr, u_scr, o_scr, *, tiles_per_seq, tm):
    i = pl.program_id(0)
    first = (i % tiles_per_seq) == 0
    last = (i % tiles_per_seq) == tiles_per_seq - 1
    _fill_normed_phase_major(x_scr, h_ref[...], hp_ref[...], hn_ref[...], g_ref[...], first, last, tm)
    a0, a1, a2, a3, a4 = 0, RG_WIDTH, 2 * RG_WIDTH, 2 * RG_WIDTH + S5_WIDTH, w_ref.shape[1]
    x = jnp.concatenate([x_scr[c] for c in range(x_scr.shape[0])], axis=1).astype(BF16)
    for lo, hi in ((a0, a1), (a3, a4), (a1, a3)):
        u_scr[:, lo:hi] = _dot(x, w_ref[:, lo:hi].astype(BF16))
    full = slice(None)
    _store_time_major(urg_ref, o_scr, _dwconv(u_scr, slice(a0, a1), rcw_ref, rcb_ref, full, tm), a0, tm)
    _store_time_major(ga_ref, o_scr, u_scr[0:tm, a1:a2], a1, tm)
    _store_time_major(ub_ref, o_scr, u_scr[0:tm, a2:a3], a2, tm)
    q = [_dwconv(u_scr, slice(a3 + k * HY_WIDTH, a3 + (k + 1) * HY_WIDTH), hcw_ref, hcb_ref,
                 slice(k * HY_WIDTH, (k + 1) * HY_WIDTH), tm) for k in range(3)]
    _store_time_major(x0_ref, o_scr, q[0], a3, tm)
    _store_time_major(z_ref, o_scr, q[2] * q[1], a3 + HY_WIDTH, tm)


def _proj(h, g, w_in, rcw, rcb, hcw, hcb, l, tm):
    nb, seq, d = h.shape
    cols = w_in.shape[-1]
    prev, nxt = _halo_specs(seq, tm, d)
    row = lambda c: _row_spec(seq, tm, c)
    outs = [jax.ShapeDtypeStruct((nb, seq, c), dt) for c, dt in
            ((RG_WIDTH, F32), (RG_WIDTH, STREAM), (S5_WIDTH, STREAM), (HY_WIDTH, STREAM), (HY_WIDTH, STREAM))]
    consts = [g, w_in, rcw, rcb, hcw, hcb]
    return pl.pallas_call(
        functools.partial(_proj_kernel, tiles_per_seq=seq // tm, tm=tm),
        grid=(nb * seq // tm,),
        in_specs=[row(d), prev, nxt] + [_layer_spec(g, l), _resident(w_in, l)]
                 + [_layer_spec(a, l) for a in consts[2:]],
        out_specs=[row(RG_WIDTH), row(RG_WIDTH), row(S5_WIDTH), row(HY_WIDTH), row(HY_WIDTH)],
        out_shape=outs,
        scratch_shapes=[pltpu.VMEM((d // LANES, tm + 2 * HALO, LANES), F32),
                        pltpu.VMEM((tm + 2 * HALO, cols), F32),
                        pltpu.VMEM((cols // LANES, tm, LANES), F32)],
        compiler_params=_params(("parallel",)),
        name="proj",
    )(h, h, h, *consts)


RG_STAGES = 4


def _lo_mask():
    return lax.broadcasted_iota(jnp.int32, (SUBLANES, LANES), 0) < (SUBLANES // 2)


def _rglru_phases(uf_ref, ub_ref, w_ref, bias_ref, sp_ref, of_ref, ob_ref,
                  af, ab, bf, bb, carry_scr, *, t, nb):
    nm = RG_WIDTH // LANES
    tq = t // RG_STAGES

    def stage(q):
        for dr, (u_ref, a_scr, b_scr) in enumerate(((uf_ref, af, bf), (ub_ref, ab, bb))):
            t0 = q * tq if dr == 0 else t - (q + 1) * tq
            u = u_ref[:, t0:t0 + tq, :].reshape(nb * tq, RG_WIDTH)
            gates = _dot(u.astype(BF16), w_ref[dr]) + bias_ref[dr]
            r = _sigmoid(gates[:, :RG_WIDTH])
            gi = _sigmoid(gates[:, RG_WIDTH:])
            log_a = (-RG_C) * r * sp_ref[dr]
            a = jnp.exp(log_a)
            m2 = -jnp.tanh(log_a) * (a * a + 1.0)
            mult = jnp.where(m2 > 0.0, m2 * lax.rsqrt(m2), 0.0)
            bin_ = mult * (gi * u)
            for b in range(nb):
                for m in range(nm):
                    rows = pl.ds(t0 * SUBLANES + dr * nb + b, tq, stride=SUBLANES)
                    a_scr[m, rows, :] = a[b * tq:(b + 1) * tq, m * LANES:(m + 1) * LANES]
                    b_scr[m, rows, :] = bin_[b * tq:(b + 1) * tq, m * LANES:(m + 1) * LANES]

    def scan():
        lo = _lo_mask()
        carry = [carry_scr[m] for m in range(nm)]
        for j in range(t):
            row = j * SUBLANES
            mrow = (t - 1 - j) * SUBLANES
            for m in range(nm):
                at = jnp.where(lo, af[m, row:row + SUBLANES, :], ab[m, mrow:mrow + SUBLANES, :])
                bt = jnp.where(lo, bf[m, row:row + SUBLANES, :], bb[m, mrow:mrow + SUBLANES, :])
                h = at * carry[m] + bt
                bf[m, row:row + SUBLANES, :] = h
                bb[m, mrow:mrow + SUBLANES, :] = h
                carry[m] = h
        for m in range(nm):
            carry_scr[m] = carry[m]
        for b in range(nb):
            for m in range(nm):
                of_ref[b, :, m * LANES:(m + 1) * LANES] = bf[m, pl.ds(b, t, stride=SUBLANES), :].astype(STREAM)
                ob_ref[b, :, m * LANES:(m + 1) * LANES] = (
                    bb[m, pl.ds(nb + b, t, stride=SUBLANES), :].astype(STREAM))

    return [functools.partial(stage, q) for q in range(RG_STAGES)] + [scan]


def _rglru_weights(p):
    hd = RG_WIDTH // RG_HEADS
    place = np.zeros((2, RG_HEADS, hd, 2 * RG_WIDTH), np.float32)
    for q in range(2):
        for h in range(RG_HEADS):
            place[q, h, np.arange(hd), q * RG_WIDTH + h * hd + np.arange(hd)] = 1.0
    both = jnp.stack([p["rg_wa"], p["rg_wx"]], axis=2).astype(F32)
    depth = both.shape[0]
    w = jnp.einsum('ldqhij,qhjc->ldhic', both, jnp.asarray(place))
    w = w.reshape(depth, N_DIR, RG_WIDTH, 2 * RG_WIDTH).astype(BF16)
    bias = jnp.concatenate([p["rg_ba"], p["rg_bx"]], axis=-1).astype(F32)[:, :, None, :]
    x = -p["rg_lambda"].astype(F32)
    sp = (jnp.maximum(x, 0.0) + jnp.log1p(jnp.exp(-jnp.abs(x))))[:, :, None, :]
    return w, bias, sp


S5_NBLK = S5_NSTATE // LANES
S5_GRP = 4
S5_PER = S5_NBLK // (S5_WIDTH // LANES)


def _reverse_tiles(src, dst, nm, ntile):
    for j in range(ntile):
        s = (ntile - 1 - j) * SUBLANES
        for m in range(nm):
            dst[m, j * SUBLANES:(j + 1) * SUBLANES, :] = src[m, s:s + SUBLANES, :]


def _s5_phases(uf_ref, ub_ref, wb_ref, wc_ref, ar_ref, ai_ref, yf_ref, yb_ref,
               uf8, ub8, ubr, bu, ym, ymr, carry_scr, *, t, nb):
    nm = S5_WIDTH // LANES
    assert S5_GRP == S5_PER

    def load():
        for b in range(nb):
            for m in range(nm):
                lanes = slice(m * LANES, (m + 1) * LANES)
                uf8[m, pl.ds(b, t, stride=SUBLANES), :] = uf_ref[b, :, lanes].astype(F32)
                ub8[m, pl.ds(nb + b, t, stride=SUBLANES), :] = ub_ref[b, :, lanes].astype(F32)
        _reverse_tiles(ub8, ubr, nm, t)

    def bproj(m):
        lhs = jnp.concatenate([uf8[m], ubr[m]], axis=1).astype(BF16)
        res = _dot(lhs, wb_ref[m])
        for q in range(2 * S5_PER):
            bu[2 * S5_PER * m + q] = res[:, q * LANES:(q + 1) * LANES]

    def scan(g):
        blocks = list(range(g * S5_GRP, (g + 1) * S5_GRP))
        ars = [ar_ref[n] for n in blocks]
        ais = [ai_ref[n] for n in blocks]
        carry = []
        for n in blocks:
            carry += [carry_scr[2 * n], carry_scr[2 * n + 1]]
        for j in range(t):
            rows = slice(j * SUBLANES, (j + 1) * SUBLANES)
            for k, n in enumerate(blocks):
                hr, hi = carry[2 * k], carry[2 * k + 1]
                nr = ars[k] * hr - ais[k] * hi + bu[2 * n, rows, :]
                ni = ars[k] * hi + ais[k] * hr + bu[2 * n + 1, rows, :]
                bu[2 * n, rows, :] = nr
                bu[2 * n + 1, rows, :] = ni
                carry[2 * k], carry[2 * k + 1] = nr, ni
        for k, n in enumerate(blocks):
            carry_scr[2 * n] = carry[2 * k]
            carry_scr[2 * n + 1] = carry[2 * k + 1]

    def cproj(m):
        rows = lax.broadcasted_iota(jnp.int32, (SUBLANES * t, LANES), 0)
        fwd_row = (rows & (SUBLANES - 1)) < nb
        hcat = jnp.concatenate([bu[2 * S5_PER * m + q] for q in range(2 * S5_PER)], axis=1).astype(BF16)
        acc = _dot(hcat, wc_ref[m])
        ym[m] = jnp.where(fwd_row, acc[:, :LANES], acc[:, LANES:])

    def store():
        _reverse_tiles(ym, ymr, nm, t)
        for b in range(nb):
            for m in range(nm):
                lanes = slice(m * LANES, (m + 1) * LANES)
                yf_ref[b, :, lanes] = ym[m, pl.ds(b, t, stride=SUBLANES), :].astype(STREAM)
                yb_ref[b, :, lanes] = ymr[m, pl.ds(nb + b, t, stride=SUBLANES), :].astype(STREAM)

    part = functools.partial
    return (load, [part(bproj, m) for m in range(nm)], [part(scan, g) for g in range(nm)],
            [part(cproj, m) for m in range(nm)], store)


def _scans_kernel(*refs, t, nb):
    rg_in, s5_in = refs[0:5], refs[5:11]
    rg_out, s5_out = refs[11:13], refs[13:15]
    rg_scr, s5_scr = refs[15:20], refs[20:27]

    @pl.when(pl.program_id(0) == 0)
    def _():
        for scr in (*rg_scr, s5_scr[0], s5_scr[1], s5_scr[6]):
            scr[...] = jnp.zeros_like(scr)

    rg = _rglru_phases(*rg_in, *rg_out, *rg_scr, t=t, nb=nb)
    load, bproj, scan, cproj, store = _s5_phases(*s5_in, *s5_out, *s5_scr, t=t, nb=nb)
    order = [load, bproj[0], rg[0], bproj[1], rg[1], scan[0], bproj[2], rg[2], scan[1], cproj[0],
             rg[3], scan[2], cproj[1], rg[4], cproj[2], store]
    assert len(rg) == 5 and len(bproj) == 3
    for phase in order:
        phase()


def _scans(urg, ub, rg_w, rg_bias, rg_sp, wb, wc, ar, ai, l, t):
    nb, seq, c = urg.shape
    assert 2 * nb == SUBLANES and ub.shape == urg.shape
    nchunk = seq // t
    fwd = pl.BlockSpec((nb, t, c), lambda i: (0, i, 0))
    bwd = pl.BlockSpec((nb, t, c), lambda i: (0, nchunk - 1 - i, 0))
    nm = c // LANES
    rows = SUBLANES * t
    out = jax.ShapeDtypeStruct((nb, seq, c), STREAM)
    slab = pltpu.VMEM((nm, rows, LANES), F32)
    rg_scr = [slab, slab, slab, slab, pltpu.VMEM((nm, SUBLANES, LANES), F32)]
    s5_scr = [slab, slab, slab, pltpu.VMEM((2 * S5_NBLK, rows, LANES), F32), slab, slab,
              pltpu.VMEM((2 * S5_NBLK, SUBLANES, LANES), F32)]
    rg_c = [rg_w, rg_bias, rg_sp]
    s5_c = [wb, wc, ar, ai]
    return pl.pallas_call(
        functools.partial(_scans_kernel, t=t, nb=nb),
        grid=(nchunk,),
        in_specs=[fwd, bwd] + [_layer_spec(a, l) for a in rg_c]
                 + [fwd, bwd] + [_layer_spec(a, l) for a in s5_c],
        out_specs=[fwd, bwd, fwd, bwd],
        out_shape=[out, out, out, out],
        scratch_shapes=rg_scr + s5_scr,
        compiler_params=_params(("arbitrary",)),
        name="scans",
    )(urg, urg, *rg_c, ub, ub, *s5_c)


def _s5_place():
    slots = LANES // S5_GROUP
    half = LANES // S5_STATE
    place = np.zeros((slots, 2, S5_STATE, S5_PER * 2 * LANES), np.float32)
    for k in range(slots):
        for r in range(2):
            col = (k // half) * 2 * LANES + r * LANES + (k % half) * S5_STATE
            place[k, r, np.arange(S5_STATE), col + np.arange(S5_STATE)] = 1.0
    return place


def _s5_weights(p):
    lr = p["s5_a_re"].astype(F32)
    li = p["s5_a_im"].astype(F32)
    dt = jnp.exp(p["s5_log_dt"].astype(F32))[..., None]
    mag = jnp.exp(lr * dt)
    abar_r = mag * jnp.cos(li * dt)
    abar_i = mag * jnp.sin(li * dt)
    den = lr * lr + li * li
    nr = abar_r - 1.0
    ni = abar_i
    coef_r = ((nr * lr + ni * li) / den)[..., None]
    coef_i = ((ni * lr - nr * li) / den)[..., None]
    b_re = p["s5_b_re"].astype(F32)
    b_im = p["s5_b_im"].astype(F32)
    bbar = jnp.stack([coef_r * b_re - coef_i * b_im, coef_r * b_im + coef_i * b_re], axis=2)
    depth = lr.shape[0]
    nm = S5_WIDTH // LANES
    slots = LANES // S5_GROUP
    place = jnp.asarray(_s5_place())
    bb = bbar.reshape(depth, N_DIR, 2, nm, slots, S5_STATE, S5_GROUP)
    wb = jnp.einsum('ldrmkpc,krpx->lmdkcx', bb, place).reshape(depth, nm, 2 * LANES, S5_PER * 2 * LANES)
    cc = jnp.stack([p["s5_c_re"].astype(F32), -p["s5_c_im"].astype(F32)], axis=2)
    cc = cc.reshape(depth, N_DIR, 2, nm, slots, S5_GROUP, S5_STATE)
    wc = jnp.einsum('ldrmkcp,krpx->lmdkcx', cc, place).reshape(depth, nm, 2 * LANES, S5_PER * 2 * LANES)
    wc = wc.transpose(0, 1, 3, 2)

    def tile_rows(a):
        a = a.reshape(depth, N_DIR, S5_NBLK, LANES).transpose(0, 2, 1, 3)
        return jnp.repeat(a, SUBLANES // N_DIR, axis=2)

    return wb.astype(BF16), wc.astype(BF16), tile_rows(abar_r), tile_rows(abar_i)


def _odd8(n):
    p = -(-n // SUBLANES)
    if p % 2 == 0:
        p += 1
    return p * SUBLANES


class _FftPlan:
    def __init__(self, seq):
        self.seq = seq
        r = 1
        while r * r < seq:
            r *= 2
        self.r = r
        self.nq = seq // r
        self.q = 2 * seq // r
        self.k1n = self.q // 2 + 1
        self.kp = -(-self.k1n // SUBLANES) * SUBLANES
        self.zpitch = _odd8(r)
        self.apitch = _odd8(2 * self.kp)
        self.cpitch = _odd8(2 * r)
        self.unroll = min(8, r)
        self.unroll2 = next(u for u in (13, 5, 4, 3, 2, 1) if self.k1n % u == 0)
        p = 2 * seq
        n1 = np.arange(self.nq)[None, :]
        k1 = np.arange(self.k1n)[:, None]
        ang = 2.0 * np.pi * n1 * k1 / self.q
        f1 = np.zeros((2 * self.kp, self.nq))
        f1[:self.k1n] = np.cos(ang)
        f1[self.kp:self.kp + self.k1n] = -np.sin(ang)
        self.f1 = f1
        w = np.full((self.k1n,), 2.0)
        w[0] = 1.0
        w[-1] = 1.0
        g1 = np.zeros((self.nq, 2 * self.kp))
        g1[:, :self.k1n] = (np.cos(ang) * w[:, None] / p).T
        g1[:, self.kp:self.kp + self.k1n] = (-np.sin(ang) * w[:, None] / p).T
        self.g1 = g1
        kk = np.arange(self.k1n)[:, None, None]
        k2 = np.arange(r)[None, :, None]
        n2 = np.arange(r)[None, None, :]
        ph = 2.0 * np.pi * (n2 * k2 / r + n2 * kk / p)
        tr, ti = np.cos(ph), -np.sin(ph)
        self.m2 = np.concatenate([np.concatenate([tr, -ti], axis=2),
                                  np.concatenate([ti, tr], axis=2)], axis=1)
        ur, ui = np.transpose(tr, (0, 2, 1)), -np.transpose(ti, (0, 2, 1))
        self.m2i = np.concatenate([np.concatenate([ur, -ui], axis=2),
                                   np.concatenate([ui, ur], axis=2)], axis=1)


def _slab_load(scr, rows):
    return jnp.concatenate([scr[k, rows, :] for k in range(scr.shape[0])], axis=1)


def _slab_store(scr, rows, val):
    for k in range(scr.shape[0]):
        scr[k, rows, :] = val[:, k * LANES:(k + 1) * LANES]


def _fft_forward(plan, src_ref, f1_ref, m2_ref, zp, as_, emit):
    r, nq, kp = plan.r, plan.nq, plan.kp
    for n1 in range(nq):
        _slab_store(zp, slice(n1 * plan.zpitch, n1 * plan.zpitch + r), src_ref[n1 * r:(n1 + 1) * r, :].astype(F32))

    def stage1(n2, c):
        slab = _slab_load(zp, pl.ds(n2, nq, stride=plan.zpitch))
        a = _dot(f1_ref[...].astype(BF16), slab.astype(BF16))
        _slab_store(as_, pl.ds(pl.multiple_of(n2 * plan.apitch, SUBLANES), 2 * kp), a)
        return c

    lax.fori_loop(0, r, stage1, 0, unroll=plan.unroll)

    def stage2(k1, c):
        sr = _slab_load(as_, pl.ds(k1, r, stride=plan.apitch))
        si = _slab_load(as_, pl.ds(kp + k1, r, stride=plan.apitch))
        s = jnp.concatenate([sr, si], axis=0).astype(BF16)
        emit(k1, _dot(m2_ref[k1].astype(BF16), s))
        return c

    lax.fori_loop(0, plan.k1n, stage2, 0, unroll=plan.unroll2)


def _hy_spec_kernel(kf_ref, kb_ref, f1_ref, m2_ref, o_ref, zp, as_, *, plan):
    r = plan.r

    def emit_f(k1, x):
        o_ref[k1] = x

    def emit_b(k1, x):
        sign = jnp.where(lax.broadcasted_iota(jnp.int32, x.shape, 0) < r, 1.0, -1.0)
        o_ref[k1] = o_ref[k1] + sign * x

    _fft_forward(plan, kf_ref, f1_ref, m2_ref, zp, as_, emit_f)
    _fft_forward(plan, kb_ref, f1_ref, m2_ref, zp, as_, emit_b)


def _hy_conv_kernel(z_ref, spec_ref, f1_ref, m2_ref, m2i_ref, g1_ref, o_ref, zp, as_, cs, *, plan):
    r, nq, kp, k1n = plan.r, plan.nq, plan.kp, plan.k1n
    if kp > k1n:
        pad = slice(k1n * plan.cpitch, kp * plan.cpitch)
        for k in range(cs.shape[0]):
            cs[k, pad, :] = jnp.zeros(((kp - k1n) * plan.cpitch, LANES), F32)

    def emit(k1, x):
        kf = spec_ref[k1]
        xr, xi = x[:r], x[r:]
        kr, ki = kf[:r], kf[r:]
        prod = jnp.concatenate([xr * kr - xi * ki, xr * ki + xi * kr], axis=0).astype(BF16)
        c = _dot(m2i_ref[k1].astype(BF16), prod)
        _slab_store(cs, pl.ds(pl.multiple_of(k1 * plan.cpitch, SUBLANES), 2 * r), c)

    _fft_forward(plan, z_ref, f1_ref, m2_ref, zp, as_, emit)

    def stage3(n2, c):
        cr = _slab_load(cs, pl.ds(n2, kp, stride=plan.cpitch))
        ci = _slab_load(cs, pl.ds(r + n2, kp, stride=plan.cpitch))
        y = _dot(g1_ref[...].astype(BF16), jnp.concatenate([cr, ci], axis=0).astype(BF16))
        _slab_store(zp, pl.ds(n2, nq, stride=plan.zpitch), y)
        return c

    lax.fori_loop(0, r, stage3, 0, unroll=plan.unroll)
    for n1 in range(nq):
        o_ref[n1 * r:(n1 + 1) * r, :] = _slab_load(
            zp, slice(n1 * plan.zpitch, n1 * plan.zpitch + r)).astype(o_ref.dtype)


def _hy_filter_kernel(feat_ref, w1_ref, b1_ref, f1_ref, w2_ref, b2_ref, f2_ref, w3_ref, dl_ref, o_ref):
    feats = feat_ref[...]
    half = feats.shape[0] // 2
    cols = o_ref.shape[1]
    both = jnp.concatenate([feats[:half], feats[half:]], axis=1)
    hid = jnp.sin(f1_ref[...] * (_dot(both, w1_ref[...], HIGHEST) + b1_ref[...]))
    hid = jnp.sin(f2_ref[...] * (_dot(hid, w2_ref[...], HIGHEST) + b2_ref[...]))
    k = _dot(hid, w3_ref[...], HIGHEST)
    o_ref[:half, :] = k[:, :cols] * jnp.exp(-(feats[:half, 0:1] * dl_ref[...]))
    o_ref[half:, :] = k[:, cols:] * jnp.exp(-(feats[half:, 0:1] * dl_ref[...]))


def _hyena_features(seq):
    pos = np.arange(seq, dtype=np.float64)
    t = pos / max(seq - 1, 1)
    w = (2.0 * math.pi / seq) * pos
    bands = np.linspace(1e-4, HY_BANDS - 1, HY_BANDS, dtype=np.float64)
    ang = w[:, None] * bands
    feats = np.concatenate([t[:, None], np.cos(ang), -np.sin(ang)], axis=-1).astype(np.float32)
    out = np.zeros((seq, LANES), np.float32)
    out[:, :feats.shape[1]] = feats
    return out


def _hyena_deltas():
    max_decay = math.log(HY_DECAY_TARGET) / HY_FAST_DECAY
    min_decay = math.log(HY_DECAY_TARGET) / HY_SLOW_DECAY
    deltas = np.abs(np.linspace(min_decay, max_decay, HY_WIDTH, dtype=np.float64))
    return np.tile(deltas, 2)[None, :].astype(np.float32)


def _pad_to(x, rows, cols):
    x = x.astype(F32)
    return jnp.pad(x, ((0, 0), (0, rows - x.shape[1]), (0, cols - x.shape[2])))


def _block_diag2(a):
    a = a.astype(F32)
    z = jnp.zeros_like(a)
    return jnp.concatenate([jnp.concatenate([a, z], axis=2), jnp.concatenate([z, a], axis=2)], axis=1)


def _hyena_filter_weights(p):
    hidden = p["hy_filt_w1"].shape[2]
    assert 2 * hidden == LANES
    vec = lambda a: jnp.tile(a.astype(F32)[:, None, :], (1, 1, 2))
    return (_block_diag2(_pad_to(p["hy_filt_w1"], LANES, hidden)), vec(p["hy_filt_b1"]),
            vec(p["hy_filt_freq1"]), _block_diag2(p["hy_filt_w2"]), vec(p["hy_filt_b2"]),
            vec(p["hy_filt_freq2"]), _block_diag2(p["hy_filt_w3"]))


def _once(a):
    return pl.BlockSpec(a.shape, lambda *_: (0,) * a.ndim, pipeline_mode=pl.Buffered(1))


def _hyena_scratch(plan, c):
    nh = c // LANES
    return [pltpu.VMEM((nh, plan.nq * plan.zpitch, LANES), F32),
            pltpu.VMEM((nh, plan.r * plan.apitch, LANES), F32),
            pltpu.VMEM((nh, plan.kp * plan.cpitch, LANES), F32)]


def _hyena_spectra(seq, fw, plan, tf):
    depth = fw[0].shape[0]
    feats = jnp.asarray(_hyena_features(seq))
    deltas = jnp.asarray(_hyena_deltas())
    per_layer = lambda a: pl.BlockSpec((None,) + a.shape[1:], lambda l, i: (l,) + (0,) * (a.ndim - 1))
    filt = pl.pallas_call(
        _hy_filter_kernel,
        grid=(depth, seq // tf),
        in_specs=[pl.BlockSpec((tf, LANES), lambda l, i: (i, 0))] + [per_layer(a) for a in fw]
                 + [_full(deltas.shape)],
        out_specs=pl.BlockSpec((None, tf, 2 * HY_WIDTH), lambda l, i: (l, i, 0)),
        out_shape=jax.ShapeDtypeStruct((depth, seq, 2 * HY_WIDTH), F32),
        compiler_params=_params(("parallel", "parallel")),
        name="hy_filter",
    )(feats, *fw, deltas)
    f1 = jnp.asarray(plan.f1, F32)
    m2 = jnp.asarray(plan.m2, F32)
    c = HY_WIDTH
    return pl.pallas_call(
        functools.partial(_hy_spec_kernel, plan=plan),
        grid=(depth,),
        in_specs=[pl.BlockSpec((None, seq, c), lambda l: (l, 0, 0)),
                  pl.BlockSpec((None, seq, c), lambda l: (l, 0, 1)), _once(f1), _once(m2)],
        out_specs=pl.BlockSpec((None, plan.k1n, 2 * plan.r, c), lambda l: (l, 0, 0, 0)),
        out_shape=jax.ShapeDtypeStruct((depth, plan.k1n, 2 * plan.r, c), F32),
        scratch_shapes=_hyena_scratch(plan, c)[:2],
        compiler_params=_params(("arbitrary",)),
        name="hy_spec",
    )(filt, filt, f1, m2)


def _hyena_conv(z3, spec, l, plan):
    nb, seq, c = z3.shape
    f1 = jnp.asarray(plan.f1, F32)
    m2 = jnp.asarray(plan.m2, F32)
    m2i = jnp.asarray(plan.m2i, F32)
    g1 = jnp.asarray(plan.g1, F32)
    return pl.pallas_call(
        functools.partial(_hy_conv_kernel, plan=plan),
        grid=(nb,),
        in_specs=[pl.BlockSpec((None, seq, c), lambda b: (b, 0, 0)),
                  _resident(spec, l), _once(f1), _once(m2), _once(m2i), _once(g1)],
        out_specs=pl.BlockSpec((None, seq, c), lambda b: (b, 0, 0)),
        out_shape=jax.ShapeDtypeStruct((nb, seq, c), STREAM),
        scratch_shapes=_hyena_scratch(plan, c),
        compiler_params=_params(("parallel",)),
        name="hy_conv",
    )(z3, spec, f1, m2, m2i, g1)


def _mix_kernel(h_ref, hf_ref, hb_ref, ga_ref, yf_ref, yb_ref, ub_ref, yc_ref, z_ref, x0_ref,
                d_ref, gw_ref, gb_ref, hbias_ref, mg_ref, wo_ref, o_ref):
    f32 = lambda ref: ref[...].astype(F32)
    ya = (f32(hf_ref) + f32(hb_ref)) * _gelu(f32(ga_ref))
    yb = _gelu(f32(ub_ref) * d_ref[...] + f32(yf_ref) + f32(yb_ref))
    yb = yb * _sigmoid(_dot(yb.astype(BF16), gw_ref[...].astype(BF16)) + gb_ref[...])
    yc = (f32(yc_ref) + f32(z_ref) * hbias_ref[...]) * f32(x0_ref)
    a1, a2 = RG_WIDTH, RG_WIDTH + S5_WIDTH
    na = (_rms_nogain(ya) * mg_ref[:, :a1]).astype(BF16)
    nb = (_rms_nogain(yb) * mg_ref[:, a1:a2]).astype(BF16)
    nc = (_rms_nogain(yc) * mg_ref[:, a2:]).astype(BF16)
    wo = lambda lo, hi: wo_ref[lo:hi, :].astype(BF16)
    out = _dot(na, wo(0, a1)) + _dot(nb, wo(a1, a2)) + _dot(nc, wo(a2, wo_ref.shape[0]))
    o_ref[...] = h_ref[...] + out


def _mix(rows, consts, l, tm):
    nb, seq, dm = rows[0].shape
    return pl.pallas_call(
        _mix_kernel,
        grid=(nb * seq // tm,),
        in_specs=[_row_spec(seq, tm, a.shape[2]) for a in rows] + [_layer_spec(a, l) for a in consts[:-1]]
                 + [_resident(consts[-1], l)],
        out_specs=_row_spec(seq, tm, dm),
        out_shape=jax.ShapeDtypeStruct((nb, seq, dm), F32),
        compiler_params=_params(("parallel",)),
        name="mix",
    )(*rows, *consts)


FFN_TILE = 256


def _ffn_kernel(h_ref, hp_ref, hn_ref, g_ref, wu_ref, cw_ref, cb_ref, wd_ref, fg_ref, o_ref,
                x_scr, u_scr, gated_scr, y_scr, *, tiles_per_seq, tm, final):
    i = pl.program_id(0)
    first = (i % tiles_per_seq) == 0
    last = (i % tiles_per_seq) == tiles_per_seq - 1
    nslab = h_ref.shape[1] // LANES
    ph = tm // SUBLANES
    _fill_normed_phase_major(x_scr, h_ref[...], hp_ref[...], hn_ref[...], g_ref[...], first, last, tm)
    x = jnp.concatenate([x_scr[c] for c in range(nslab)], axis=1).astype(BF16)
    for k in range(D_FF // FFN_TILE):
        halves = []
        for part in range(2):
            lo = part * D_FF + k * FFN_TILE
            slot = 2 * k + part
            u_scr[slot] = _dot(x, wu_ref[:, lo:lo + FFN_TILE])
            halves.append(_dwconv(u_scr.at[slot], slice(0, FFN_TILE), cw_ref, cb_ref,
                                  slice(lo, lo + FFN_TILE), tm))
        gated_scr[:, k * FFN_TILE:(k + 1) * FFN_TILE] = (_gelu(halves[0]) * halves[1]).astype(BF16)
    y = _dot(gated_scr[...], wd_ref[...])
    for c in range(nslab):
        y_scr[c] = y[:, c * LANES:(c + 1) * LANES]
    for s in range(SUBLANES):
        for c in range(nslab):
            rows = slice(s * ph, (s + 1) * ph)
            lanes = slice(c * LANES, (c + 1) * LANES)
            o_ref[rows, lanes] = h_ref[rows, lanes] + y_scr[c, pl.ds(s, ph, stride=SUBLANES), :]
    if final:
        out = o_ref[...]
        o_ref[...] = out * lax.rsqrt(jnp.mean(out * out, axis=-1, keepdims=True) + RMS_EPS) * fg_ref[...]


def _resident(arr, *lead):
    rest = arr.shape[len(lead):]
    zeros = (0,) * len(rest)
    return pl.BlockSpec((None,) * len(lead) + rest, lambda *_: tuple(lead) + zeros,
                        pipeline_mode=pl.Buffered(1))


def _ffn(h, g, w_up, conv_w, conv_b, w_down, final_g, l, tm, final):
    nb, seq, d = h.shape
    prev, nxt = _halo_specs(seq, tm, d)
    return pl.pallas_call(
        functools.partial(_ffn_kernel, tiles_per_seq=seq // tm, tm=tm, final=final),
        grid=(nb * seq // tm,),
        in_specs=[_row_spec(seq, tm, d), prev, nxt,
                  _layer_spec(g, l), _resident(w_up, l), _layer_spec(conv_w, l), _layer_spec(conv_b, l),
                  _resident(w_down, l), _full(final_g.shape)],
        out_specs=_row_spec(seq, tm, d),
        out_shape=jax.ShapeDtypeStruct((nb, seq, d), F32),
        scratch_shapes=[pltpu.VMEM((d // LANES, tm + 2 * HALO, LANES), F32),
                        pltpu.VMEM((2 * (D_FF // FFN_TILE), tm + 2 * HALO, FFN_TILE), F32),
                        pltpu.VMEM((tm, D_FF), BF16),
                        pltpu.VMEM((d // LANES, tm, LANES), F32)],
        compiler_params=_params(("parallel",)),
        name="ffn_final" if final else "ffn",
    )(h, h, h, g, w_up, conv_w, conv_b, w_down, final_g)


def _row_vec(a):
    return a.astype(F32)[:, None, :]


def _trunk(x, p, *, tm, tm_wide, t_scan, tf):
    seq = x.shape[1]
    depth = p["w_in"].shape[0]
    plan = _FftPlan(seq)

    proj_c = (_row_vec(p["norm1_g"]), p["w_in"].astype(F32), p["rg_conv_w"].astype(F32),
              _row_vec(p["rg_conv_b"]), p["hy_conv_w"].astype(F32), _row_vec(p["hy_conv_b"]))
    rg_w, rg_bias, rg_sp = _rglru_weights(p)
    s5_w = _s5_weights(p)
    hy_spec = _hyena_spectra(seq, _hyena_filter_weights(p), plan, tf)
    mix_c = (_row_vec(p["s5_d"]), p["s5_glu_w"].astype(F32), _row_vec(p["s5_glu_b"]),
             _row_vec(p["hy_bias"]), _row_vec(p["mix_norm_g"]), p["w_out"].astype(F32))
    ffn_c = (_row_vec(p["norm2_g"]), p["w_up"].astype(BF16), p["ffn_conv_w"].astype(F32),
             _row_vec(p["ffn_conv_b"]), p["w_down"].astype(BF16), p["final_norm_g"].astype(F32)[None])

    h = x.astype(F32)
    for l in range(depth):
        urg, ga, ub, x0, z = _proj(h, *proj_c, l, tm_wide)
        hf, hb, yf, yb = _scans(urg, ub, rg_w, rg_bias, rg_sp, *s5_w, l, t_scan)
        yc = _hyena_conv(z, hy_spec, l, plan)
        h = _mix([h, hf, hb, ga, yf, yb, ub, yc, z, x0], mix_c, l, tm_wide)
        h = _ffn(h, *ffn_c, l, tm, final=(l == depth - 1))
    return h.astype(x.dtype)


def kernel(x, norm1_g, w_in, rg_conv_w, rg_conv_b, rg_wa, rg_ba, rg_wx, rg_bx, rg_lambda, s5_a_re, s5_a_im, s5_log_dt, s5_b_re, s5_b_im, s5_c_re, s5_c_im, s5_d, s5_glu_w, s5_glu_b, hy_conv_w, hy_conv_b, hy_filt_w1, hy_filt_b1, hy_filt_freq1, hy_filt_w2, hy_filt_b2, hy_filt_freq2, hy_filt_w3, hy_bias, mix_norm_g, w_out, norm2_g, w_up, ffn_conv_w, ffn_conv_b, w_down, final_norm_g):
    p = dict(norm1_g=norm1_g, w_in=w_in, rg_conv_w=rg_conv_w, rg_conv_b=rg_conv_b, rg_wa=rg_wa, rg_ba=rg_ba,
             rg_wx=rg_wx, rg_bx=rg_bx, rg_lambda=rg_lambda, s5_a_re=s5_a_re, s5_a_im=s5_a_im,
             s5_log_dt=s5_log_dt, s5_b_re=s5_b_re, s5_b_im=s5_b_im, s5_c_re=s5_c_re, s5_c_im=s5_c_im,
             s5_d=s5_d, s5_glu_w=s5_glu_w, s5_glu_b=s5_glu_b, hy_conv_w=hy_conv_w, hy_conv_b=hy_conv_b,
             hy_filt_w1=hy_filt_w1, hy_filt_b1=hy_filt_b1, hy_filt_freq1=hy_filt_freq1, hy_filt_w2=hy_filt_w2,
             hy_filt_b2=hy_filt_b2, hy_filt_freq2=hy_filt_freq2, hy_filt_w3=hy_filt_w3, hy_bias=hy_bias,
             mix_norm_g=mix_norm_g, w_out=w_out, norm2_g=norm2_g, w_up=w_up, ffn_conv_w=ffn_conv_w,
             ffn_conv_b=ffn_conv_b, w_down=w_down, final_norm_g=final_norm_g)
    return _trunk(x, p, tm=512, tm_wide=1024, t_scan=128, tf=512)
```

```python
import functools
import math

import numpy as np
import jax
import jax.numpy as jnp
from jax import lax
from jax.experimental import pallas as pl
from jax.experimental.pallas import tpu as pltpu

F32 = jnp.float32
BF16 = jnp.bfloat16
STREAM = jnp.bfloat16

RMS_EPS = 1e-6
RG_WIDTH = 384
RG_HEADS = 6
RG_C = 8.0
S5_WIDTH = 384
S5_GROUP = 16
S5_GROUPS = 24
S5_STATE = 64
S5_NSTATE = S5_GROUPS * S5_STATE
HY_WIDTH = 256
HY_BANDS = 16
HY_FAST_DECAY = 0.3
HY_SLOW_DECAY = 1.5
HY_DECAY_TARGET = 1e-2
D_FF = 2816
N_DIR = 2

LANES = 128
SUBLANES = 8
HALO = SUBLANES
VMEM_LIMIT = 56 * 1024 * 1024

HIGHEST = lax.Precision.HIGHEST


def _dot(a, b, precision=None):
    return jnp.dot(a, b, preferred_element_type=F32, precision=precision)


def _gelu(x):
    c = math.sqrt(2.0 / math.pi)
    return 0.5 * x * (1.0 + jnp.tanh(c * (x + 0.044715 * (x * x * x))))


def _sigmoid(x):
    return 0.5 * jnp.tanh(0.5 * x) + 0.5


def _rms_nogain(x):
    return x * lax.rsqrt(jnp.mean(x * x, axis=-1, keepdims=True) + RMS_EPS)


def _params(sem):
    return pltpu.CompilerParams(dimension_semantics=sem, vmem_limit_bytes=VMEM_LIMIT)


def _full(shape):
    nd = len(shape)
    return pl.BlockSpec(shape, lambda *_: (0,) * nd)


def _layer_spec(arr, *lead):
    rest = arr.shape[len(lead):]
    zeros = (0,) * len(rest)
    return pl.BlockSpec((None,) * len(lead) + rest, lambda *_: tuple(lead) + zeros)


def _row_spec(seq, tm, c):
    tps = seq // tm
    return pl.BlockSpec((None, tm, c), lambda i: (i // tps, i % tps, 0))


def _halo_specs(seq, tm, c):
    tps = seq // tm
    bpt = tm // HALO
    prev = pl.BlockSpec((None, HALO, c), lambda i: (i // tps, jnp.maximum((i % tps) * bpt - 1, 0), 0))
    nxt = pl.BlockSpec((None, HALO, c),
                       lambda i: (i // tps, jnp.minimum((i % tps + 1) * bpt, seq // HALO - 1), 0))
    return prev, nxt


def _fill_normed_phase_major(x_scr, h, hp, hn, g, first, last, tm):
    ph = tm // SUBLANES

    def norm(v):
        return v * lax.rsqrt(jnp.mean(v * v, axis=-1, keepdims=True) + RMS_EPS) * g

    xn = norm(h)
    xp = norm(jnp.where(first, 0.0, hp))
    xq = norm(jnp.where(last, 0.0, hn))
    for c in range(x_scr.shape[0]):
        lanes = slice(c * LANES, (c + 1) * LANES)
        for s in range(SUBLANES):
            x_scr[c, pl.ds(s, ph, stride=SUBLANES), :] = xn[s * ph:(s + 1) * ph, lanes]
        x_scr[c, tm:tm + HALO, :] = xp[:, lanes]
        x_scr[c, tm + HALO:tm + 2 * HALO, :] = xq[:, lanes]


def _time_shifted(u_ref, cols, d, tm):
    width = cols.stop - cols.start
    sub = lax.broadcasted_iota(jnp.int32, (SUBLANES, width), 0)
    tile = lambda j: u_ref[j * SUBLANES:(j + 1) * SUBLANES, cols]
    prev_halo = u_ref[tm:tm + HALO, cols]
    next_halo = u_ref[tm + HALO:tm + 2 * HALO, cols]
    ph = tm // SUBLANES
    if d == 0:
        return u_ref[0:tm, cols]
    if d < 0:
        assert d == -1
        head = jnp.where(sub == 0, pltpu.roll(prev_halo, 1, 0), pltpu.roll(tile(ph - 1), 1, 0))
        return jnp.concatenate([head, u_ref[0:tm - SUBLANES, cols]], axis=0)
    tails = [jnp.where(sub == SUBLANES - 1, pltpu.roll(next_halo, SUBLANES - 1 - jj, 0),
                       pltpu.roll(tile(jj), SUBLANES - 1, 0)) for jj in range(d)]
    return jnp.concatenate([u_ref[d * SUBLANES:tm, cols]] + tails, axis=0)


def _dwconv(u_ref, cols, w_ref, b_ref, wcols, tm):
    taps = w_ref.shape[0]
    left = (taps - 1) // 2
    acc = b_ref[:, wcols]
    for j in range(taps):
        acc = acc + _time_shifted(u_ref, cols, j - left, tm) * w_ref[j:j + 1, wcols]
    return acc


def _proj_kernel(h_ref, hp_ref, hn_ref, g_ref, w_ref, rcw_ref, rcb_ref, hcw_ref, hcb_ref,
                 urg_ref, ga_ref, ub_ref, x0_ref, z_ref, x_scr, u_scr, *, tiles_per_seq, tm):
    i = pl.program_id(0)
    first = (i % tiles_per_seq) == 0
    last = (i % tiles_per_seq) == tiles_per_seq - 1
    g = g_ref[...]

    def norm(v):
        return v * lax.rsqrt(jnp.mean(v * v, axis=-1, keepdims=True) + RMS_EPS) * g

    x_scr[0:HALO, :] = norm(jnp.where(first, 0.0, hp_ref[...]))
    x_scr[HALO:HALO + tm, :] = norm(h_ref[...])
    x_scr[HALO + tm:2 * HALO + tm, :] = norm(jnp.where(last, 0.0, hn_ref[...]))
    a0, a1, a2, a3, a4 = 0, RG_WIDTH, 2 * RG_WIDTH, 2 * RG_WIDTH + S5_WIDTH, w_ref.shape[1]
    x = x_scr[...].astype(BF16)
    for lo, hi in ((a0, a1), (a3, a4), (a1, a3)):
        u_scr[:, lo:hi] = _dot(x, w_ref[:, lo:hi].astype(BF16))
    acc = rcb_ref[...]
    for j in range(rcw_ref.shape[0]):
        acc = acc + u_scr[pl.ds(HALO + j - 1, tm), a0:a1] * rcw_ref[j:j + 1, :]
    urg_ref[...] = acc
    ga_ref[...] = u_scr[pl.ds(HALO, tm), a1:a2].astype(ga_ref.dtype)
    ub_ref[...] = u_scr[pl.ds(HALO, tm), a2:a3].astype(ub_ref.dtype)
    q = []
    for part in range(3):
        lo = part * HY_WIDTH
        acc = hcb_ref[:, lo:lo + HY_WIDTH]
        for j in range(hcw_ref.shape[0]):
            acc = acc + (u_scr[pl.ds(HALO + j - 1, tm), a3 + lo:a3 + lo + HY_WIDTH]
                         * hcw_ref[j:j + 1, lo:lo + HY_WIDTH])
        q.append(acc)
    x0_ref[...] = q[0].astype(x0_ref.dtype)
    z_ref[...] = (q[2] * q[1]).astype(z_ref.dtype)


def _proj(h, g, w_in, rcw, rcb, hcw, hcb, l, tm):
    nb, seq, d = h.shape
    cols = w_in.shape[-1]
    prev, nxt = _halo_specs(seq, tm, d)
    row = lambda c: _row_spec(seq, tm, c)
    outs = [jax.ShapeDtypeStruct((nb, seq, c), dt) for c, dt in
            ((RG_WIDTH, F32), (RG_WIDTH, STREAM), (S5_WIDTH, STREAM), (HY_WIDTH, STREAM), (HY_WIDTH, STREAM))]
    consts = [g, w_in, rcw, rcb, hcw, hcb]
    return pl.pallas_call(
        functools.partial(_proj_kernel, tiles_per_seq=seq // tm, tm=tm),
        grid=(nb * seq // tm,),
        in_specs=[row(d), prev, nxt] + [_layer_spec(g, l), _resident(w_in, l)]
                 + [_layer_spec(a, l) for a in consts[2:]],
        out_specs=[row(RG_WIDTH), row(RG_WIDTH), row(S5_WIDTH), row(HY_WIDTH), row(HY_WIDTH)],
        out_shape=outs,
        scratch_shapes=[pltpu.VMEM((tm + 2 * HALO, d), F32), pltpu.VMEM((tm + 2 * HALO, cols), F32)],
        compiler_params=_params(("parallel",)),
        name="proj",
    )(h, h, h, *consts)


RG_STAGES = 4


def _lo_mask():
    return lax.broadcasted_iota(jnp.int32, (SUBLANES, LANES), 0) < (SUBLANES // 2)


def _rglru_phases(uf_ref, ub_ref, w_ref, bias_ref, sp_ref, of_ref, ob_ref,
                  af, ab, bf, bb, carry_scr, *, t, nb):
    nm = RG_WIDTH // LANES
    tq = t // RG_STAGES

    def stage(q):
        for dr, (u_ref, a_scr, b_scr) in enumerate(((uf_ref, af, bf), (ub_ref, ab, bb))):
            t0 = q * tq if dr == 0 else t - (q + 1) * tq
            u = u_ref[:, t0:t0 + tq, :].reshape(nb * tq, RG_WIDTH)
            gates = _dot(u.astype(BF16), w_ref[dr]) + bias_ref[dr]
            r = _sigmoid(gates[:, :RG_WIDTH])
            gi = _sigmoid(gates[:, RG_WIDTH:])
            log_a = (-RG_C) * r * sp_ref[dr]
            a = jnp.exp(log_a)
            m2 = -jnp.tanh(log_a) * (a * a + 1.0)
            mult = jnp.where(m2 > 0.0, m2 * lax.rsqrt(m2), 0.0)
            bin_ = mult * (gi * u)
            for b in range(nb):
                for m in range(nm):
                    rows = pl.ds(t0 * SUBLANES + dr * nb + b, tq, stride=SUBLANES)
                    a_scr[m, rows, :] = a[b * tq:(b + 1) * tq, m * LANES:(m + 1) * LANES]
                    b_scr[m, rows, :] = bin_[b * tq:(b + 1) * tq, m * LANES:(m + 1) * LANES]

    def scan():
        lo = _lo_mask()
        carry = [carry_scr[m] for m in range(nm)]
        for j in range(t):
            row = j * SUBLANES
            mrow = (t - 1 - j) * SUBLANES
            for m in range(nm):
                at = jnp.where(lo, af[m, row:row + SUBLANES, :], ab[m, mrow:mrow + SUBLANES, :])
                bt = jnp.where(lo, bf[m, row:row + SUBLANES, :], bb[m, mrow:mrow + SUBLANES, :])
                h = at * carry[m] + bt
                bf[m, row:row + SUBLANES, :] = h
                bb[m, mrow:mrow + SUBLANES, :] = h
                carry[m] = h
        for m in range(nm):
            carry_scr[m] = carry[m]
        for b in range(nb):
            for m in range(nm):
                of_ref[b, :, m * LANES:(m + 1) * LANES] = bf[m, pl.ds(b, t, stride=SUBLANES), :].astype(STREAM)
                ob_ref[b, :, m * LANES:(m + 1) * LANES] = (
                    bb[m, pl.ds(nb + b, t, stride=SUBLANES), :].astype(STREAM))

    return [functools.partial(stage, q) for q in range(RG_STAGES)] + [scan]


def _rglru_weights(p):
    hd = RG_WIDTH // RG_HEADS
    place = np.zeros((2, RG_HEADS, hd, 2 * RG_WIDTH), np.float32)
    for q in range(2):
        for h in range(RG_HEADS):
            place[q, h, np.arange(hd), q * RG_WIDTH + h * hd + np.arange(hd)] = 1.0
    both = jnp.stack([p["rg_wa"], p["rg_wx"]], axis=2).astype(F32)
    depth = both.shape[0]
    w = jnp.einsum('ldqhij,qhjc->ldhic', both, jnp.asarray(place))
    w = w.reshape(depth, N_DIR, RG_WIDTH, 2 * RG_WIDTH).astype(BF16)
    bias = jnp.concatenate([p["rg_ba"], p["rg_bx"]], axis=-1).astype(F32)[:, :, None, :]
    x = -p["rg_lambda"].astype(F32)
    sp = (jnp.maximum(x, 0.0) + jnp.log1p(jnp.exp(-jnp.abs(x))))[:, :, None, :]
    return w, bias, sp


S5_NBLK = S5_NSTATE // LANES
S5_GRP = 4
S5_PER = S5_NBLK // (S5_WIDTH // LANES)


def _reverse_tiles(src, dst, nm, ntile):
    for j in range(ntile):
        s = (ntile - 1 - j) * SUBLANES
        for m in range(nm):
            dst[m, j * SUBLANES:(j + 1) * SUBLANES, :] = src[m, s:s + SUBLANES, :]


def _s5_phases(uf_ref, ub_ref, wb_ref, wc_ref, ar_ref, ai_ref, yf_ref, yb_ref,
               uf8, ub8, ubr, bu, ym, ymr, carry_scr, *, t, nb):
    nm = S5_WIDTH // LANES
    assert S5_GRP == S5_PER

    def load():
        for b in range(nb):
            for m in range(nm):
                lanes = slice(m * LANES, (m + 1) * LANES)
                uf8[m, pl.ds(b, t, stride=SUBLANES), :] = uf_ref[b, :, lanes].astype(F32)
                ub8[m, pl.ds(nb + b, t, stride=SUBLANES), :] = ub_ref[b, :, lanes].astype(F32)
        _reverse_tiles(ub8, ubr, nm, t)

    def bproj(m):
        lhs = jnp.concatenate([uf8[m], ubr[m]], axis=1).astype(BF16)
        res = _dot(lhs, wb_ref[m])
        for q in range(2 * S5_PER):
            bu[2 * S5_PER * m + q] = res[:, q * LANES:(q + 1) * LANES]

    def scan(g):
        blocks = list(range(g * S5_GRP, (g + 1) * S5_GRP))
        ars = [ar_ref[n] for n in blocks]
        ais = [ai_ref[n] for n in blocks]
        carry = []
        for n in blocks:
            carry += [carry_scr[2 * n], carry_scr[2 * n + 1]]
        for j in range(t):
            rows = slice(j * SUBLANES, (j + 1) * SUBLANES)
            for k, n in enumerate(blocks):
                hr, hi = carry[2 * k], carry[2 * k + 1]
                nr = ars[k] * hr - ais[k] * hi + bu[2 * n, rows, :]
                ni = ars[k] * hi + ais[k] * hr + bu[2 * n + 1, rows, :]
                bu[2 * n, rows, :] = nr
                bu[2 * n + 1, rows, :] = ni
                carry[2 * k], carry[2 * k + 1] = nr, ni
        for k, n in enumerate(blocks):
            carry_scr[2 * n] = carry[2 * k]
            carry_scr[2 * n + 1] = carry[2 * k + 1]

    def cproj(m):
        rows = lax.broadcasted_iota(jnp.int32, (SUBLANES * t, LANES), 0)
        fwd_row = (rows & (SUBLANES - 1)) < nb
        hcat = jnp.concatenate([bu[2 * S5_PER * m + q] for q in range(2 * S5_PER)], axis=1).astype(BF16)
        acc = _dot(hcat, wc_ref[m])
        ym[m] = jnp.where(fwd_row, acc[:, :LANES], acc[:, LANES:])

    def store():
        _reverse_tiles(ym, ymr, nm, t)
        for b in range(nb):
            for m in range(nm):
                lanes = slice(m * LANES, (m + 1) * LANES)
                yf_ref[b, :, lanes] = ym[m, pl.ds(b, t, stride=SUBLANES), :].astype(STREAM)
                yb_ref[b, :, lanes] = ymr[m, pl.ds(nb + b, t, stride=SUBLANES), :].astype(STREAM)

    part = functools.partial
    return (load, [part(bproj, m) for m in range(nm)], [part(scan, g) for g in range(nm)],
            [part(cproj, m) for m in range(nm)], store)


def _scans_kernel(*refs, t, nb):
    rg_in, s5_in = refs[0:5], refs[5:11]
    rg_out, s5_out = refs[11:13], refs[13:15]
    rg_scr, s5_scr = refs[15:20], refs[20:27]

    @pl.when(pl.program_id(0) == 0)
    def _():
        for scr in (*rg_scr, s5_scr[0], s5_scr[1], s5_scr[6]):
            scr[...] = jnp.zeros_like(scr)

    rg = _rglru_phases(*rg_in, *rg_out, *rg_scr, t=t, nb=nb)
    load, bproj, scan, cproj, store = _s5_phases(*s5_in, *s5_out, *s5_scr, t=t, nb=nb)
    order = [load, bproj[0], rg[0], bproj[1], rg[1], scan[0], bproj[2], rg[2], scan[1], cproj[0],
             rg[3], scan[2], cproj[1], rg[4], cproj[2], store]
    assert len(rg) == 5 and len(bproj) == 3
    for phase in order:
        phase()


def _scans(urg, ub, rg_w, rg_bias, rg_sp, wb, wc, ar, ai, l, t):
    nb, seq, c = urg.shape
    assert 2 * nb == SUBLANES and ub.shape == urg.shape
    nchunk = seq // t
    fwd = pl.BlockSpec((nb, t, c), lambda i: (0, i, 0))
    bwd = pl.BlockSpec((nb, t, c), lambda i: (0, nchunk - 1 - i, 0))
    nm = c // LANES
    rows = SUBLANES * t
    out = jax.ShapeDtypeStruct((nb, seq, c), STREAM)
    slab = pltpu.VMEM((nm, rows, LANES), F32)
    rg_scr = [slab, slab, slab, slab, pltpu.VMEM((nm, SUBLANES, LANES), F32)]
    s5_scr = [slab, slab, slab, pltpu.VMEM((2 * S5_NBLK, rows, LANES), F32), slab, slab,
              pltpu.VMEM((2 * S5_NBLK, SUBLANES, LANES), F32)]
    rg_c = [rg_w, rg_bias, rg_sp]
    s5_c = [wb, wc, ar, ai]
    return pl.pallas_call(
        functools.partial(_scans_kernel, t=t, nb=nb),
        grid=(nchunk,),
        in_specs=[fwd, bwd] + [_layer_spec(a, l) for a in rg_c]
                 + [fwd, bwd] + [_layer_spec(a, l) for a in s5_c],
        out_specs=[fwd, bwd, fwd, bwd],
        out_shape=[out, out, out, out],
        scratch_shapes=rg_scr + s5_scr,
        compiler_params=_params(("arbitrary",)),
        name="scans",
    )(urg, urg, *rg_c, ub, ub, *s5_c)


def _s5_place():
    slots = LANES // S5_GROUP
    half = LANES // S5_STATE
    place = np.zeros((slots, 2, S5_STATE, S5_PER * 2 * LANES), np.float32)
    for k in range(slots):
        for r in range(2):
            col = (k // half) * 2 * LANES + r * LANES + (k % half) * S5_STATE
            place[k, r, np.arange(S5_STATE), col + np.arange(S5_STATE)] = 1.0
    return place


def _s5_weights(p):
    lr = p["s5_a_re"].astype(F32)
    li = p["s5_a_im"].astype(F32)
    dt = jnp.exp(p["s5_log_dt"].astype(F32))[..., None]
    mag = jnp.exp(lr * dt)
    abar_r = mag * jnp.cos(li * dt)
    abar_i = mag * jnp.sin(li * dt)
    den = lr * lr + li * li
    nr = abar_r - 1.0
    ni = abar_i
    coef_r = ((nr * lr + ni * li) / den)[..., None]
    coef_i = ((ni * lr - nr * li) / den)[..., None]
    b_re = p["s5_b_re"].astype(F32)
    b_im = p["s5_b_im"].astype(F32)
    bbar = jnp.stack([coef_r * b_re - coef_i * b_im, coef_r * b_im + coef_i * b_re], axis=2)
    depth = lr.shape[0]
    nm = S5_WIDTH // LANES
    slots = LANES // S5_GROUP
    place = jnp.asarray(_s5_place())
    bb = bbar.reshape(depth, N_DIR, 2, nm, slots, S5_STATE, S5_GROUP)
    wb = jnp.einsum('ldrmkpc,krpx->lmdkcx', bb, place).reshape(depth, nm, 2 * LANES, S5_PER * 2 * LANES)
    cc = jnp.stack([p["s5_c_re"].astype(F32), -p["s5_c_im"].astype(F32)], axis=2)
    cc = cc.reshape(depth, N_DIR, 2, nm, slots, S5_GROUP, S5_STATE)
    wc = jnp.einsum('ldrmkcp,krpx->lmdkcx', cc, place).reshape(depth, nm, 2 * LANES, S5_PER * 2 * LANES)
    wc = wc.transpose(0, 1, 3, 2)

    def tile_rows(a):
        a = a.reshape(depth, N_DIR, S5_NBLK, LANES).transpose(0, 2, 1, 3)
        return jnp.repeat(a, SUBLANES // N_DIR, axis=2)

    return wb.astype(BF16), wc.astype(BF16), tile_rows(abar_r), tile_rows(abar_i)


def _odd8(n):
    p = -(-n // SUBLANES)
    if p % 2 == 0:
        p += 1
    return p * SUBLANES


class _FftPlan:
    def __init__(self, seq):
        self.seq = seq
        r = 1
        while r * r < seq:
            r *= 2
        self.r = r
        self.nq = seq // r
        self.q = 2 * seq // r
        self.k1n = self.q // 2 + 1
        self.kp = -(-self.k1n // SUBLANES) * SUBLANES
        self.zpitch = _odd8(r)
        self.apitch = _odd8(2 * self.kp)
        self.cpitch = _odd8(2 * r)
        self.unroll = min(8, r)
        self.unroll2 = next(u for u in (13, 5, 4, 3, 2, 1) if self.k1n % u == 0)
        p = 2 * seq
        n1 = np.arange(self.nq)[None, :]
        k1 = np.arange(self.k1n)[:, None]
        ang = 2.0 * np.pi * n1 * k1 / self.q
        f1 = np.zeros((2 * self.kp, self.nq))
        f1[:self.k1n] = np.cos(ang)
        f1[self.kp:self.kp + self.k1n] = -np.sin(ang)
        self.f1 = f1
        w = np.full((self.k1n,), 2.0)
        w[0] = 1.0
        w[-1] = 1.0
        g1 = np.zeros((self.nq, 2 * self.kp))
        g1[:, :self.k1n] = (np.cos(ang) * w[:, None] / p).T
        g1[:, self.kp:self.kp + self.k1n] = (-np.sin(ang) * w[:, None] / p).T
        self.g1 = g1
        kk = np.arange(self.k1n)[:, None, None]
        k2 = np.arange(r)[None, :, None]
        n2 = np.arange(r)[None, None, :]
        ph = 2.0 * np.pi * (n2 * k2 / r + n2 * kk / p)
        tr, ti = np.cos(ph), -np.sin(ph)
        self.m2 = np.concatenate([np.concatenate([tr, -ti], axis=2),
                                  np.concatenate([ti, tr], axis=2)], axis=1)
        ur, ui = np.transpose(tr, (0, 2, 1)), -np.transpose(ti, (0, 2, 1))
        self.m2i = np.concatenate([np.concatenate([ur, -ui], axis=2),
                                   np.concatenate([ui, ur], axis=2)], axis=1)


def _slab_load(scr, rows):
    return jnp.concatenate([scr[k, rows, :] for k in range(scr.shape[0])], axis=1)


def _slab_store(scr, rows, val):
    for k in range(scr.shape[0]):
        scr[k, rows, :] = val[:, k * LANES:(k + 1) * LANES]


def _fft_forward(plan, src_ref, f1_ref, m2_ref, zp, as_, emit):
    r, nq, kp = plan.r, plan.nq, plan.kp
    for n1 in range(nq):
        _slab_store(zp, slice(n1 * plan.zpitch, n1 * plan.zpitch + r), src_ref[n1 * r:(n1 + 1) * r, :].astype(F32))

    def stage1(n2, c):
        slab = _slab_load(zp, pl.ds(n2, nq, stride=plan.zpitch))
        a = _dot(f1_ref[...].astype(BF16), slab.astype(BF16))
        _slab_store(as_, pl.ds(pl.multiple_of(n2 * plan.apitch, SUBLANES), 2 * kp), a)
        return c

    lax.fori_loop(0, r, stage1, 0, unroll=plan.unroll)

    def stage2(k1, c):
        sr = _slab_load(as_, pl.ds(k1, r, stride=plan.apitch))
        si = _slab_load(as_, pl.ds(kp + k1, r, stride=plan.apitch))
        s = jnp.concatenate([sr, si], axis=0).astype(BF16)
        emit(k1, _dot(m2_ref[k1].astype(BF16), s))
        return c

    lax.fori_loop(0, plan.k1n, stage2, 0, unroll=plan.unroll2)


def _hy_spec_kernel(kf_ref, kb_ref, f1_ref, m2_ref, o_ref, zp, as_, *, plan):
    r = plan.r

    def emit_f(k1, x):
        o_ref[k1] = x

    def emit_b(k1, x):
        sign = jnp.where(lax.broadcasted_iota(jnp.int32, x.shape, 0) < r, 1.0, -1.0)
        o_ref[k1] = o_ref[k1] + sign * x

    _fft_forward(plan, kf_ref, f1_ref, m2_ref, zp, as_, emit_f)
    _fft_forward(plan, kb_ref, f1_ref, m2_ref, zp, as_, emit_b)


def _hy_conv_kernel(z_ref, spec_ref, f1_ref, m2_ref, m2i_ref, g1_ref, o_ref, zp, as_, cs, *, plan):
    r, nq, kp, k1n = plan.r, plan.nq, plan.kp, plan.k1n
    if kp > k1n:
        pad = slice(k1n * plan.cpitch, kp * plan.cpitch)
        for k in range(cs.shape[0]):
            cs[k, pad, :] = jnp.zeros(((kp - k1n) * plan.cpitch, LANES), F32)

    def emit(k1, x):
        kf = spec_ref[k1]
        xr, xi = x[:r], x[r:]
        kr, ki = kf[:r], kf[r:]
        prod = jnp.concatenate([xr * kr - xi * ki, xr * ki + xi * kr], axis=0).astype(BF16)
        c = _dot(m2i_ref[k1].astype(BF16), prod)
        _slab_store(cs, pl.ds(pl.multiple_of(k1 * plan.cpitch, SUBLANES), 2 * r), c)

    _fft_forward(plan, z_ref, f1_ref, m2_ref, zp, as_, emit)

    def stage3(n2, c):
        cr = _slab_load(cs, pl.ds(n2, kp, stride=plan.cpitch))
        ci = _slab_load(cs, pl.ds(r + n2, kp, stride=plan.cpitch))
        y = _dot(g1_ref[...].astype(BF16), jnp.concatenate([cr, ci], axis=0).astype(BF16))
        _slab_store(zp, pl.ds(n2, nq, stride=plan.zpitch), y)
        return c

    lax.fori_loop(0, r, stage3, 0, unroll=plan.unroll)
    for n1 in range(nq):
        o_ref[n1 * r:(n1 + 1) * r, :] = _slab_load(
            zp, slice(n1 * plan.zpitch, n1 * plan.zpitch + r)).astype(o_ref.dtype)


def _hy_filter_kernel(feat_ref, w1_ref, b1_ref, f1_ref, w2_ref, b2_ref, f2_ref, w3_ref, dl_ref, o_ref):
    feats = feat_ref[...]
    half = feats.shape[0] // 2
    cols = o_ref.shape[1]
    both = jnp.concatenate([feats[:half], feats[half:]], axis=1)
    hid = jnp.sin(f1_ref[...] * (_dot(both, w1_ref[...], HIGHEST) + b1_ref[...]))
    hid = jnp.sin(f2_ref[...] * (_dot(hid, w2_ref[...], HIGHEST) + b2_ref[...]))
    k = _dot(hid, w3_ref[...], HIGHEST)
    o_ref[:half, :] = k[:, :cols] * jnp.exp(-(feats[:half, 0:1] * dl_ref[...]))
    o_ref[half:, :] = k[:, cols:] * jnp.exp(-(feats[half:, 0:1] * dl_ref[...]))


def _hyena_features(seq):
    pos = np.arange(seq, dtype=np.float64)
    t = pos / max(seq - 1, 1)
    w = (2.0 * math.pi / seq) * pos
    bands = np.linspace(1e-4, HY_BANDS - 1, HY_BANDS, dtype=np.float64)
    ang = w[:, None] * bands
    feats = np.concatenate([t[:, None], np.cos(ang), -np.sin(ang)], axis=-1).astype(np.float32)
    out = np.zeros((seq, LANES), np.float32)
    out[:, :feats.shape[1]] = feats
    return out


def _hyena_deltas():
    max_decay = math.log(HY_DECAY_TARGET) / HY_FAST_DECAY
    min_decay = math.log(HY_DECAY_TARGET) / HY_SLOW_DECAY
    deltas = np.abs(np.linspace(min_decay, max_decay, HY_WIDTH, dtype=np.float64))
    return np.tile(deltas, 2)[None, :].astype(np.float32)


def _pad_to(x, rows, cols):
    x = x.astype(F32)
    return jnp.pad(x, ((0, 0), (0, rows - x.shape[1]), (0, cols - x.shape[2])))


def _block_diag2(a):
    a = a.astype(F32)
    z = jnp.zeros_like(a)
    return jnp.concatenate([jnp.concatenate([a, z], axis=2), jnp.concatenate([z, a], axis=2)], axis=1)


def _hyena_filter_weights(p):
    hidden = p["hy_filt_w1"].shape[2]
    assert 2 * hidden == LANES
    vec = lambda a: jnp.tile(a.astype(F32)[:, None, :], (1, 1, 2))
    return (_block_diag2(_pad_to(p["hy_filt_w1"], LANES, hidden)), vec(p["hy_filt_b1"]),
            vec(p["hy_filt_freq1"]), _block_diag2(p["hy_filt_w2"]), vec(p["hy_filt_b2"]),
            vec(p["hy_filt_freq2"]), _block_diag2(p["hy_filt_w3"]))


def _once(a):
    return pl.BlockSpec(a.shape, lambda *_: (0,) * a.ndim, pipeline_mode=pl.Buffered(1))


def _hyena_scratch(plan, c):
    nh = c // LANES
    return [pltpu.VMEM((nh, plan.nq * plan.zpitch, LANES), F32),
            pltpu.VMEM((nh, plan.r * plan.apitch, LANES), F32),
            pltpu.VMEM((nh, plan.kp * plan.cpitch, LANES), F32)]


def _hyena_spectra(seq, fw, plan, tf):
    depth = fw[0].shape[0]
    feats = jnp.asarray(_hyena_features(seq))
    deltas = jnp.asarray(_hyena_deltas())
    per_layer = lambda a: pl.BlockSpec((None,) + a.shape[1:], lambda l, i: (l,) + (0,) * (a.ndim - 1))
    filt = pl.pallas_call(
        _hy_filter_kernel,
        grid=(depth, seq // tf),
        in_specs=[pl.BlockSpec((tf, LANES), lambda l, i: (i, 0))] + [per_layer(a) for a in fw]
                 + [_full(deltas.shape)],
        out_specs=pl.BlockSpec((None, tf, 2 * HY_WIDTH), lambda l, i: (l, i, 0)),
        out_shape=jax.ShapeDtypeStruct((depth, seq, 2 * HY_WIDTH), F32),
        compiler_params=_params(("parallel", "parallel")),
        name="hy_filter",
    )(feats, *fw, deltas)
    f1 = jnp.asarray(plan.f1, F32)
    m2 = jnp.asarray(plan.m2, F32)
    c = HY_WIDTH
    return pl.pallas_call(
        functools.partial(_hy_spec_kernel, plan=plan),
        grid=(depth,),
        in_specs=[pl.BlockSpec((None, seq, c), lambda l: (l, 0, 0)),
                  pl.BlockSpec((None, seq, c), lambda l: (l, 0, 1)), _once(f1), _once(m2)],
        out_specs=pl.BlockSpec((None, plan.k1n, 2 * plan.r, c), lambda l: (l, 0, 0, 0)),
        out_shape=jax.ShapeDtypeStruct((depth, plan.k1n, 2 * plan.r, c), F32),
        scratch_shapes=_hyena_scratch(plan, c)[:2],
        compiler_params=_params(("arbitrary",)),
        name="hy_spec",
    )(filt, filt, f1, m2)


def _hyena_conv(z3, spec, l, plan):
    nb, seq, c = z3.shape
    f1 = jnp.asarray(plan.f1, F32)
    m2 = jnp.asarray(plan.m2, F32)
    m2i = jnp.asarray(plan.m2i, F32)
    g1 = jnp.asarray(plan.g1, F32)
    return pl.pallas_call(
        functools.partial(_hy_conv_kernel, plan=plan),
        grid=(nb,),
        in_specs=[pl.BlockSpec((None, seq, c), lambda b: (b, 0, 0)),
                  _resident(spec, l), _once(f1), _once(m2), _once(m2i), _once(g1)],
        out_specs=pl.BlockSpec((None, seq, c), lambda b: (b, 0, 0)),
        out_shape=jax.ShapeDtypeStruct((nb, seq, c), STREAM),
        scratch_shapes=_hyena_scratch(plan, c),
        compiler_params=_params(("parallel",)),
        name="hy_conv",
    )(z3, spec, f1, m2, m2i, g1)


def _mix_kernel(h_ref, hf_ref, hb_ref, ga_ref, yf_ref, yb_ref, ub_ref, yc_ref, z_ref, x0_ref,
                d_ref, gw_ref, gb_ref, hbias_ref, mg_ref, wo_ref, o_ref):
    f32 = lambda ref: ref[...].astype(F32)
    ya = (f32(hf_ref) + f32(hb_ref)) * _gelu(f32(ga_ref))
    yb = _gelu(f32(ub_ref) * d_ref[...] + f32(yf_ref) + f32(yb_ref))
    yb = yb * _sigmoid(_dot(yb.astype(BF16), gw_ref[...].astype(BF16)) + gb_ref[...])
    yc = (f32(yc_ref) + f32(z_ref) * hbias_ref[...]) * f32(x0_ref)
    a1, a2 = RG_WIDTH, RG_WIDTH + S5_WIDTH
    na = (_rms_nogain(ya) * mg_ref[:, :a1]).astype(BF16)
    nb = (_rms_nogain(yb) * mg_ref[:, a1:a2]).astype(BF16)
    nc = (_rms_nogain(yc) * mg_ref[:, a2:]).astype(BF16)
    wo = lambda lo, hi: wo_ref[lo:hi, :].astype(BF16)
    out = _dot(na, wo(0, a1)) + _dot(nb, wo(a1, a2)) + _dot(nc, wo(a2, wo_ref.shape[0]))
    o_ref[...] = h_ref[...] + out


def _mix(rows, consts, l, tm):
    nb, seq, dm = rows[0].shape
    return pl.pallas_call(
        _mix_kernel,
        grid=(nb * seq // tm,),
        in_specs=[_row_spec(seq, tm, a.shape[2]) for a in rows] + [_layer_spec(a, l) for a in consts[:-1]]
                 + [_resident(consts[-1], l)],
        out_specs=_row_spec(seq, tm, dm),
        out_shape=jax.ShapeDtypeStruct((nb, seq, dm), F32),
        compiler_params=_params(("parallel",)),
        name="mix",
    )(*rows, *consts)


FFN_TILE = 256


def _ffn_kernel(h_ref, hp_ref, hn_ref, g_ref, wu_ref, cw_ref, cb_ref, wd_ref, fg_ref, o_ref,
                x_scr, u_scr, gated_scr, y_scr, *, tiles_per_seq, tm, final):
    i = pl.program_id(0)
    first = (i % tiles_per_seq) == 0
    last = (i % tiles_per_seq) == tiles_per_seq - 1
    nslab = h_ref.shape[1] // LANES
    ph = tm // SUBLANES
    _fill_normed_phase_major(x_scr, h_ref[...], hp_ref[...], hn_ref[...], g_ref[...], first, last, tm)
    x = jnp.concatenate([x_scr[c] for c in range(nslab)], axis=1).astype(BF16)
    for k in range(D_FF // FFN_TILE):
        halves = []
        for part in range(2):
            lo = part * D_FF + k * FFN_TILE
            slot = 2 * k + part
            u_scr[slot] = _dot(x, wu_ref[:, lo:lo + FFN_TILE])
            halves.append(_dwconv(u_scr.at[slot], slice(0, FFN_TILE), cw_ref, cb_ref,
                                  slice(lo, lo + FFN_TILE), tm))
        gated_scr[:, k * FFN_TILE:(k + 1) * FFN_TILE] = (_gelu(halves[0]) * halves[1]).astype(BF16)
    y = _dot(gated_scr[...], wd_ref[...])
    for c in range(nslab):
        y_scr[c] = y[:, c * LANES:(c + 1) * LANES]
    for s in range(SUBLANES):
        for c in range(nslab):
            rows = slice(s * ph, (s + 1) * ph)
            lanes = slice(c * LANES, (c + 1) * LANES)
            o_ref[rows, lanes] = h_ref[rows, lanes] + y_scr[c, pl.ds(s, ph, stride=SUBLANES), :]
    if final:
        out = o_ref[...]
        o_ref[...] = out * lax.rsqrt(jnp.mean(out * out, axis=-1, keepdims=True) + RMS_EPS) * fg_ref[...]


def _resident(arr, *lead):
    rest = arr.shape[len(lead):]
    zeros = (0,) * len(rest)
    return pl.BlockSpec((None,) * len(lead) + rest, lambda *_: tuple(lead) + zeros,
                        pipeline_mode=pl.Buffered(1))


def _ffn(h, g, w_up, conv_w, conv_b, w_down, final_g, l, tm, final):
    nb, seq, d = h.shape
    prev, nxt = _halo_specs(seq, tm, d)
    return pl.pallas_call(
        functools.partial(_ffn_kernel, tiles_per_seq=seq // tm, tm=tm, final=final),
        grid=(nb * seq // tm,),
        in_specs=[_row_spec(seq, tm, d), prev, nxt,
                  _layer_spec(g, l), _resident(w_up, l), _layer_spec(conv_w, l), _layer_spec(conv_b, l),
                  _resident(w_down, l), _full(final_g.shape)],
        out_specs=_row_spec(seq, tm, d),
        out_shape=jax.ShapeDtypeStruct((nb, seq, d), F32),
        scratch_shapes=[pltpu.VMEM((d // LANES, tm + 2 * HALO, LANES), F32),
                        pltpu.VMEM((2 * (D_FF // FFN_TILE), tm + 2 * HALO, FFN_TILE), F32),
                        pltpu.VMEM((tm, D_FF), BF16),
                        pltpu.VMEM((d // LANES, tm, LANES), F32)],
        compiler_params=_params(("parallel",)),
        name="ffn_final" if final else "ffn",
    )(h, h, h, g, w_up, conv_w, conv_b, w_down, final_g)


def _row_vec(a):
    return a.astype(F32)[:, None, :]


def _trunk(x, p, *, tm, tm_wide, t_scan, tf):
    seq = x.shape[1]
    depth = p["w_in"].shape[0]
    plan = _FftPlan(seq)

    proj_c = (_row_vec(p["norm1_g"]), p["w_in"].astype(F32), p["rg_conv_w"].astype(F32),
              _row_vec(p["rg_conv_b"]), p["hy_conv_w"].astype(F32), _row_vec(p["hy_conv_b"]))
    rg_w, rg_bias, rg_sp = _rglru_weights(p)
    s5_w = _s5_weights(p)
    hy_spec = _hyena_spectra(seq, _hyena_filter_weights(p), plan, tf)
    mix_c = (_row_vec(p["s5_d"]), p["s5_glu_w"].astype(F32), _row_vec(p["s5_glu_b"]),
             _row_vec(p["hy_bias"]), _row_vec(p["mix_norm_g"]), p["w_out"].astype(F32))
    ffn_c = (_row_vec(p["norm2_g"]), p["w_up"].astype(BF16), p["ffn_conv_w"].astype(F32),
             _row_vec(p["ffn_conv_b"]), p["w_down"].astype(BF16), p["final_norm_g"].astype(F32)[None])

    h = x.astype(F32)
    for l in range(depth):
        urg, ga, ub, x0, z = _proj(h, *proj_c, l, tm_wide)
        hf, hb, yf, yb = _scans(urg, ub, rg_w, rg_bias, rg_sp, *s5_w, l, t_scan)
        yc = _hyena_conv(z, hy_spec, l, plan)
        h = _mix([h, hf, hb, ga, yf, yb, ub, yc, z, x0], mix_c, l, tm_wide)
        h = _ffn(h, *ffn_c, l, tm, final=(l == depth - 1))
    return h.astype(x.dtype)


def kernel(x, norm1_g, w_in, rg_conv_w, rg_conv_b, rg_wa, rg_ba, rg_wx, rg_bx, rg_lambda, s5_a_re, s5_a_im, s5_log_dt, s5_b_re, s5_b_im, s5_c_re, s5_c_im, s5_d, s5_glu_w, s5_glu_b, hy_conv_w, hy_conv_b, hy_filt_w1, hy_filt_b1, hy_filt_freq1, hy_filt_w2, hy_filt_b2, hy_filt_freq2, hy_filt_w3, hy_bias, mix_norm_g, w_out, norm2_g, w_up, ffn_conv_w, ffn_conv_b, w_down, final_norm_g):
    p = dict(norm1_g=norm1_g, w_in=w_in, rg_conv_w=rg_conv_w, rg_conv_b=rg_conv_b, rg_wa=rg_wa, rg_ba=rg_ba,
             rg_wx=rg_wx, rg_bx=rg_bx, rg_lambda=rg_lambda, s5_a_re=s5_a_re, s5_a_im=s5_a_im,
             s5_log_dt=s5_log_dt, s5_b_re=s5_b_re, s5_b_im=s5_b_im, s5_c_re=s5_c_re, s5_c_im=s5_c_im,
             s5_d=s5_d, s5_glu_w=s5_glu_w, s5_glu_b=s5_glu_b, hy_conv_w=hy_conv_w, hy_conv_b=hy_conv_b,
             hy_filt_w1=hy_filt_w1, hy_filt_b1=hy_filt_b1, hy_filt_freq1=hy_filt_freq1, hy_filt_w2=hy_filt_w2,
             hy_filt_b2=hy_filt_b2, hy_filt_freq2=hy_filt_freq2, hy_filt_w3=hy_filt_w3, hy_bias=hy_bias,
             mix_norm_g=mix_norm_g, w_out=w_out, norm2_g=norm2_g, w_up=w_up, ffn_conv_w=ffn_conv_w,
             ffn_conv_b=ffn_conv_b, w_down=w_down, final_norm_g=final_norm_g)
    return _trunk(x, p, tm=512, tm_wide=1024, t_scan=128, tf=512)
```

```python
import functools
import math

import numpy as np
import jax
import jax.numpy as jnp
from jax import lax
from jax.experimental import pallas as pl
from jax.experimental.pallas import tpu as pltpu

F32 = jnp.float32
BF16 = jnp.bfloat16
STREAM = jnp.bfloat16

RMS_EPS = 1e-6
RG_WIDTH = 384
RG_HEADS = 6
RG_C = 8.0
S5_WIDTH = 384
S5_GROUP = 16
S5_GROUPS = 24
S5_STATE = 64
S5_NSTATE = S5_GROUPS * S5_STATE
HY_WIDTH = 256
HY_BANDS = 16
HY_FAST_DECAY = 0.3
HY_SLOW_DECAY = 1.5
HY_DECAY_TARGET = 1e-2
D_FF = 2816
N_DIR = 2

LANES = 128
SUBLANES = 8
HALO = SUBLANES
VMEM_LIMIT = 56 * 1024 * 1024

HIGHEST = lax.Precision.HIGHEST


def _dot(a, b, precision=None):
    return jnp.dot(a, b, preferred_element_type=F32, precision=precision)


def _gelu(x):
    c = math.sqrt(2.0 / math.pi)
    return 0.5 * x * (1.0 + jnp.tanh(c * (x + 0.044715 * (x * x * x))))


def _sigmoid(x):
    return 0.5 * jnp.tanh(0.5 * x) + 0.5


def _rms_nogain(x):
    return x * lax.rsqrt(jnp.mean(x * x, axis=-1, keepdims=True) + RMS_EPS)


def _params(sem):
    return pltpu.CompilerParams(dimension_semantics=sem, vmem_limit_bytes=VMEM_LIMIT)


def _full(shape):
    nd = len(shape)
    return pl.BlockSpec(shape, lambda *_: (0,) * nd)


def _layer_spec(arr, *lead):
    rest = arr.shape[len(lead):]
    zeros = (0,) * len(rest)
    return pl.BlockSpec((None,) * len(lead) + rest, lambda *_: tuple(lead) + zeros)


def _row_spec(seq, tm, c):
    tps = seq // tm
    return pl.BlockSpec((None, tm, c), lambda i: (i // tps, i % tps, 0))


def _halo_specs(seq, tm, c):
    tps = seq // tm
    bpt = tm // HALO
    prev = pl.BlockSpec((None, HALO, c), lambda i: (i // tps, jnp.maximum((i % tps) * bpt - 1, 0), 0))
    nxt = pl.BlockSpec((None, HALO, c),
                       lambda i: (i // tps, jnp.minimum((i % tps + 1) * bpt, seq // HALO - 1), 0))
    return prev, nxt


def _fill_normed_phase_major(x_scr, h, hp, hn, g, first, last, tm):
    ph = tm // SUBLANES

    def norm(v):
        return v * lax.rsqrt(jnp.mean(v * v, axis=-1, keepdims=True) + RMS_EPS) * g

    xn = norm(h)
    xp = norm(jnp.where(first, 0.0, hp))
    xq = norm(jnp.where(last, 0.0, hn))
    for c in range(x_scr.shape[0]):
        lanes = slice(c * LANES, (c + 1) * LANES)
        for s in range(SUBLANES):
            x_scr[c, pl.ds(s, ph, stride=SUBLANES), :] = xn[s * ph:(s + 1) * ph, lanes]
        x_scr[c, tm:tm + HALO, :] = xp[:, lanes]
        x_scr[c, tm + HALO:tm + 2 * HALO, :] = xq[:, lanes]


def _time_shifted(u_ref, cols, d, tm):
    width = cols.stop - cols.start
    sub = lax.broadcasted_iota(jnp.int32, (SUBLANES, width), 0)
    tile = lambda j: u_ref[j * SUBLANES:(j + 1) * SUBLANES, cols]
    prev_halo = u_ref[tm:tm + HALO, cols]
    next_halo = u_ref[tm + HALO:tm + 2 * HALO, cols]
    ph = tm // SUBLANES
    if d == 0:
        return u_ref[0:tm, cols]
    if d < 0:
        assert d == -1
        head = jnp.where(sub == 0, pltpu.roll(prev_halo, 1, 0), pltpu.roll(tile(ph - 1), 1, 0))
        return jnp.concatenate([head, u_ref[0:tm - SUBLANES, cols]], axis=0)
    tails = [jnp.where(sub == SUBLANES - 1, pltpu.roll(next_halo, SUBLANES - 1 - jj, 0),
                       pltpu.roll(tile(jj), SUBLANES - 1, 0)) for jj in range(d)]
    return jnp.concatenate([u_ref[d * SUBLANES:tm, cols]] + tails, axis=0)


def _dwconv(u_ref, cols, w_ref, b_ref, wcols, tm):
    taps = w_ref.shape[0]
    left = (taps - 1) // 2
    acc = b_ref[:, wcols]
    for j in range(taps):
        acc = acc + _time_shifted(u_ref, cols, j - left, tm) * w_ref[j:j + 1, wcols]
    return acc


def _proj_kernel(h_ref, hp_ref, hn_ref, g_ref, w_ref, rcw_ref, rcb_ref, hcw_ref, hcb_ref,
                 urg_ref, ga_ref, ub_ref, x0_ref, z_ref, x_scr, u_scr, w_scr, *, tiles_per_seq, tm):
    i = pl.program_id(0)
    first = (i % tiles_per_seq) == 0
    last = (i % tiles_per_seq) == tiles_per_seq - 1
    g = g_ref[...]

    @pl.when(i == 0)
    def _():
        w_scr[...] = w_ref[...].astype(BF16)

    def norm(v):
        return v * lax.rsqrt(jnp.mean(v * v, axis=-1, keepdims=True) + RMS_EPS) * g

    x_scr[0:HALO, :] = norm(jnp.where(first, 0.0, hp_ref[...]))
    x_scr[HALO:HALO + tm, :] = norm(h_ref[...])
    x_scr[HALO + tm:2 * HALO + tm, :] = norm(jnp.where(last, 0.0, hn_ref[...]))
    a0, a1, a2, a3, a4 = 0, RG_WIDTH, 2 * RG_WIDTH, 2 * RG_WIDTH + S5_WIDTH, w_ref.shape[1]
    x = x_scr[...].astype(BF16)
    for lo, hi in ((a0, a1), (a3, a4), (a1, a3)):
        u_scr[:, lo:hi] = _dot(x, w_scr[:, lo:hi])
    acc = rcb_ref[...]
    for j in range(rcw_ref.shape[0]):
        acc = acc + u_scr[pl.ds(HALO + j - 1, tm), a0:a1] * rcw_ref[j:j + 1, :]
    urg_ref[...] = acc
    ga_ref[...] = u_scr[pl.ds(HALO, tm), a1:a2].astype(ga_ref.dtype)
    ub_ref[...] = u_scr[pl.ds(HALO, tm), a2:a3].astype(ub_ref.dtype)
    q = []
    for part in range(3):
        lo = part * HY_WIDTH
        acc = hcb_ref[:, lo:lo + HY_WIDTH]
        for j in range(hcw_ref.shape[0]):
            acc = acc + (u_scr[pl.ds(HALO + j - 1, tm), a3 + lo:a3 + lo + HY_WIDTH]
                         * hcw_ref[j:j + 1, lo:lo + HY_WIDTH])
        q.append(acc)
    x0_ref[...] = q[0].astype(x0_ref.dtype)
    z_ref[...] = (q[2] * q[1]).astype(z_ref.dtype)


def _proj(h, g, w_in, rcw, rcb, hcw, hcb, l, tm):
    nb, seq, d = h.shape
    cols = w_in.shape[-1]
    prev, nxt = _halo_specs(seq, tm, d)
    row = lambda c: _row_spec(seq, tm, c)
    outs = [jax.ShapeDtypeStruct((nb, seq, c), dt) for c, dt in
            ((RG_WIDTH, F32), (RG_WIDTH, STREAM), (S5_WIDTH, STREAM), (HY_WIDTH, STREAM), (HY_WIDTH, STREAM))]
    consts = [g, w_in, rcw, rcb, hcw, hcb]
    return pl.pallas_call(
        functools.partial(_proj_kernel, tiles_per_seq=seq // tm, tm=tm),
        grid=(nb * seq // tm,),
        in_specs=[row(d), prev, nxt] + [_layer_spec(g, l), _resident(w_in, l)]
                 + [_layer_spec(a, l) for a in consts[2:]],
        out_specs=[row(RG_WIDTH), row(RG_WIDTH), row(S5_WIDTH), row(HY_WIDTH), row(HY_WIDTH)],
        out_shape=outs,
        scratch_shapes=[pltpu.VMEM((tm + 2 * HALO, d), F32), pltpu.VMEM((tm + 2 * HALO, cols), F32),
                        pltpu.VMEM((d, cols), BF16)],
        compiler_params=_params(("arbitrary",)),
        name="proj",
    )(h, h, h, *consts)


RG_STAGES = 4


def _lo_mask():
    return lax.broadcasted_iota(jnp.int32, (SUBLANES, LANES), 0) < (SUBLANES // 2)


def _rglru_phases(uf_ref, ub_ref, w_ref, bias_ref, sp_ref, of_ref, ob_ref,
                  af, ab, bf, bb, carry_scr, *, t, nb):
    nm = RG_WIDTH // LANES
    tq = t // RG_STAGES

    def stage(q):
        for dr, (u_ref, a_scr, b_scr) in enumerate(((uf_ref, af, bf), (ub_ref, ab, bb))):
            t0 = q * tq if dr == 0 else t - (q + 1) * tq
            u = u_ref[:, t0:t0 + tq, :].reshape(nb * tq, RG_WIDTH)
            gates = _dot(u.astype(BF16), w_ref[dr]) + bias_ref[dr]
            r = _sigmoid(gates[:, :RG_WIDTH])
            gi = _sigmoid(gates[:, RG_WIDTH:])
            log_a = (-RG_C) * r * sp_ref[dr]
            a = jnp.exp(log_a)
            m2 = -jnp.tanh(log_a) * (a * a + 1.0)
            mult = jnp.where(m2 > 0.0, m2 * lax.rsqrt(m2), 0.0)
            bin_ = mult * (gi * u)
            for b in range(nb):
                for m in range(nm):
                    rows = pl.ds(t0 * SUBLANES + dr * nb + b, tq, stride=SUBLANES)
                    a_scr[m, rows, :] = a[b * tq:(b + 1) * tq, m * LANES:(m + 1) * LANES]
                    b_scr[m, rows, :] = bin_[b * tq:(b + 1) * tq, m * LANES:(m + 1) * LANES]

    def scan():
        lo = _lo_mask()
        carry = [carry_scr[m] for m in range(nm)]
        for j in range(t):
            row = j * SUBLANES
            mrow = (t - 1 - j) * SUBLANES
            for m in range(nm):
                at = jnp.where(lo, af[m, row:row + SUBLANES, :], ab[m, mrow:mrow + SUBLANES, :])
                bt = jnp.where(lo, bf[m, row:row + SUBLANES, :], bb[m, mrow:mrow + SUBLANES, :])
                h = at * carry[m] + bt
                bf[m, row:row + SUBLANES, :] = h
                bb[m, mrow:mrow + SUBLANES, :] = h
                carry[m] = h
        for m in range(nm):
            carry_scr[m] = carry[m]
        for b in range(nb):
            for m in range(nm):
                of_ref[b, :, m * LANES:(m + 1) * LANES] = bf[m, pl.ds(b, t, stride=SUBLANES), :].astype(STREAM)
                ob_ref[b, :, m * LANES:(m + 1) * LANES] = (
                    bb[m, pl.ds(nb + b, t, stride=SUBLANES), :].astype(STREAM))

    return [functools.partial(stage, q) for q in range(RG_STAGES)] + [scan]


def _rglru_weights(p):
    hd = RG_WIDTH // RG_HEADS
    place = np.zeros((2, RG_HEADS, hd, 2 * RG_WIDTH), np.float32)
    for q in range(2):
        for h in range(RG_HEADS):
            place[q, h, np.arange(hd), q * RG_WIDTH + h * hd + np.arange(hd)] = 1.0
    both = jnp.stack([p["rg_wa"], p["rg_wx"]], axis=2).astype(F32)
    depth = both.shape[0]
    w = jnp.einsum('ldqhij,qhjc->ldhic', both, jnp.asarray(place))
    w = w.reshape(depth, N_DIR, RG_WIDTH, 2 * RG_WIDTH).astype(BF16)
    bias = jnp.concatenate([p["rg_ba"], p["rg_bx"]], axis=-1).astype(F32)[:, :, None, :]
    x = -p["rg_lambda"].astype(F32)
    sp = (jnp.maximum(x, 0.0) + jnp.log1p(jnp.exp(-jnp.abs(x))))[:, :, None, :]
    return w, bias, sp


S5_NBLK = S5_NSTATE // LANES
S5_GRP = 4
S5_PER = S5_NBLK // (S5_WIDTH // LANES)


def _reverse_tiles(src, dst, nm, ntile):
    for j in range(ntile):
        s = (ntile - 1 - j) * SUBLANES
        for m in range(nm):
            dst[m, j * SUBLANES:(j + 1) * SUBLANES, :] = src[m, s:s + SUBLANES, :]


def _s5_phases(uf_ref, ub_ref, wb_ref, wc_ref, ar_ref, ai_ref, yf_ref, yb_ref,
               uf8, ub8, ubr, bu, ym, ymr, carry_scr, *, t, nb):
    nm = S5_WIDTH // LANES
    assert S5_GRP == S5_PER

    def load():
        for b in range(nb):
            for m in range(nm):
                lanes = slice(m * LANES, (m + 1) * LANES)
                uf8[m, pl.ds(b, t, stride=SUBLANES), :] = uf_ref[b, :, lanes].astype(F32)
                ub8[m, pl.ds(nb + b, t, stride=SUBLANES), :] = ub_ref[b, :, lanes].astype(F32)
        _reverse_tiles(ub8, ubr, nm, t)

    def bproj(m):
        lhs = jnp.concatenate([uf8[m], ubr[m]], axis=1).astype(BF16)
        res = _dot(lhs, wb_ref[m])
        for q in range(2 * S5_PER):
            bu[2 * S5_PER * m + q] = res[:, q * LANES:(q + 1) * LANES]

    def scan(g):
        blocks = list(range(g * S5_GRP, (g + 1) * S5_GRP))
        ars = [ar_ref[n] for n in blocks]
        ais = [ai_ref[n] for n in blocks]
        carry = []
        for n in blocks:
            carry += [carry_scr[2 * n], carry_scr[2 * n + 1]]
        for j in range(t):
            rows = slice(j * SUBLANES, (j + 1) * SUBLANES)
            for k, n in enumerate(blocks):
                hr, hi = carry[2 * k], carry[2 * k + 1]
                nr = ars[k] * hr - ais[k] * hi + bu[2 * n, rows, :]
                ni = ars[k] * hi + ais[k] * hr + bu[2 * n + 1, rows, :]
                bu[2 * n, rows, :] = nr
                bu[2 * n + 1, rows, :] = ni
                carry[2 * k], carry[2 * k + 1] = nr, ni
        for k, n in enumerate(blocks):
            carry_scr[2 * n] = carry[2 * k]
            carry_scr[2 * n + 1] = carry[2 * k + 1]

    def cproj(m):
        rows = lax.broadcasted_iota(jnp.int32, (SUBLANES * t, LANES), 0)
        fwd_row = (rows & (SUBLANES - 1)) < nb
        hcat = jnp.concatenate([bu[2 * S5_PER * m + q] for q in range(2 * S5_PER)], axis=1).astype(BF16)
        acc = _dot(hcat, wc_ref[m])
        ym[m] = jnp.where(fwd_row, acc[:, :LANES], acc[:, LANES:])

    def store():
        _reverse_tiles(ym, ymr, nm, t)
        for b in range(nb):
            for m in range(nm):
                lanes = slice(m * LANES, (m + 1) * LANES)
                yf_ref[b, :, lanes] = ym[m, pl.ds(b, t, stride=SUBLANES), :].astype(STREAM)
                yb_ref[b, :, lanes] = ymr[m, pl.ds(nb + b, t, stride=SUBLANES), :].astype(STREAM)

    part = functools.partial
    return (load, [part(bproj, m) for m in range(nm)], [part(scan, g) for g in range(nm)],
            [part(cproj, m) for m in range(nm)], store)


def _scans_kernel(*refs, t, nb):
    rg_in, s5_in = refs[0:5], refs[5:11]
    rg_out, s5_out = refs[11:13], refs[13:15]
    rg_scr, s5_scr = refs[15:20], refs[20:27]

    @pl.when(pl.program_id(0) == 0)
    def _():
        for scr in (*rg_scr, s5_scr[0], s5_scr[1], s5_scr[6]):
            scr[...] = jnp.zeros_like(scr)

    rg = _rglru_phases(*rg_in, *rg_out, *rg_scr, t=t, nb=nb)
    load, bproj, scan, cproj, store = _s5_phases(*s5_in, *s5_out, *s5_scr, t=t, nb=nb)
    order = [load, bproj[0], rg[0], bproj[1], rg[1], scan[0], bproj[2], rg[2], scan[1], cproj[0],
             rg[3], scan[2], cproj[1], rg[4], cproj[2], store]
    assert len(rg) == 5 and len(bproj) == 3
    for phase in order:
        phase()


def _scans(urg, ub, rg_w, rg_bias, rg_sp, wb, wc, ar, ai, l, t):
    nb, seq, c = urg.shape
    assert 2 * nb == SUBLANES and ub.shape == urg.shape
    nchunk = seq // t
    fwd = pl.BlockSpec((nb, t, c), lambda i: (0, i, 0))
    bwd = pl.BlockSpec((nb, t, c), lambda i: (0, nchunk - 1 - i, 0))
    nm = c // LANES
    rows = SUBLANES * t
    out = jax.ShapeDtypeStruct((nb, seq, c), STREAM)
    slab = pltpu.VMEM((nm, rows, LANES), F32)
    rg_scr = [slab, slab, slab, slab, pltpu.VMEM((nm, SUBLANES, LANES), F32)]
    s5_scr = [slab, slab, slab, pltpu.VMEM((2 * S5_NBLK, rows, LANES), F32), slab, slab,
              pltpu.VMEM((2 * S5_NBLK, SUBLANES, LANES), F32)]
    rg_c = [rg_w, rg_bias, rg_sp]
    s5_c = [wb, wc, ar, ai]
    return pl.pallas_call(
        functools.partial(_scans_kernel, t=t, nb=nb),
        grid=(nchunk,),
        in_specs=[fwd, bwd] + [_layer_spec(a, l) for a in rg_c]
                 + [fwd, bwd] + [_layer_spec(a, l) for a in s5_c],
        out_specs=[fwd, bwd, fwd, bwd],
        out_shape=[out, out, out, out],
        scratch_shapes=rg_scr + s5_scr,
        compiler_params=_params(("arbitrary",)),
        name="scans",
    )(urg, urg, *rg_c, ub, ub, *s5_c)


def _s5_place():
    slots = LANES // S5_GROUP
    half = LANES // S5_STATE
    place = np.zeros((slots, 2, S5_STATE, S5_PER * 2 * LANES), np.float32)
    for k in range(slots):
        for r in range(2):
            col = (k // half) * 2 * LANES + r * LANES + (k % half) * S5_STATE
            place[k, r, np.arange(S5_STATE), col + np.arange(S5_STATE)] = 1.0
    return place


def _s5_weights(p):
    lr = p["s5_a_re"].astype(F32)
    li = p["s5_a_im"].astype(F32)
    dt = jnp.exp(p["s5_log_dt"].astype(F32))[..., None]
    mag = jnp.exp(lr * dt)
    abar_r = mag * jnp.cos(li * dt)
    abar_i = mag * jnp.sin(li * dt)
    den = lr * lr + li * li
    nr = abar_r - 1.0
    ni = abar_i
    coef_r = ((nr * lr + ni * li) / den)[..., None]
    coef_i = ((ni * lr - nr * li) / den)[..., None]
    b_re = p["s5_b_re"].astype(F32)
    b_im = p["s5_b_im"].astype(F32)
    bbar = jnp.stack([coef_r * b_re - coef_i * b_im, coef_r * b_im + coef_i * b_re], axis=2)
    depth = lr.shape[0]
    nm = S5_WIDTH // LANES
    slots = LANES // S5_GROUP
    place = jnp.asarray(_s5_place())
    bb = bbar.reshape(depth, N_DIR, 2, nm, slots, S5_STATE, S5_GROUP)
    wb = jnp.einsum('ldrmkpc,krpx->lmdkcx', bb, place).reshape(depth, nm, 2 * LANES, S5_PER * 2 * LANES)
    cc = jnp.stack([p["s5_c_re"].astype(F32), -p["s5_c_im"].astype(F32)], axis=2)
    cc = cc.reshape(depth, N_DIR, 2, nm, slots, S5_GROUP, S5_STATE)
    wc = jnp.einsum('ldrmkcp,krpx->lmdkcx', cc, place).reshape(depth, nm, 2 * LANES, S5_PER * 2 * LANES)
    wc = wc.transpose(0, 1, 3, 2)

    def tile_rows(a):
        a = a.reshape(depth, N_DIR, S5_NBLK, LANES).transpose(0, 2, 1, 3)
        return jnp.repeat(a, SUBLANES // N_DIR, axis=2)

    return wb.astype(BF16), wc.astype(BF16), tile_rows(abar_r), tile_rows(abar_i)


def _odd8(n):
    p = -(-n // SUBLANES)
    if p % 2 == 0:
        p += 1
    return p * SUBLANES


class _FftPlan:
    def __init__(self, seq):
        self.seq = seq
        r = 1
        while r * r < seq:
            r *= 2
        self.r = r
        self.nq = seq // r
        self.q = 2 * seq // r
        self.k1n = self.q // 2 + 1
        self.kp = -(-self.k1n // SUBLANES) * SUBLANES
        self.zpitch = _odd8(r)
        self.apitch = _odd8(2 * self.kp)
        self.cpitch = _odd8(2 * r)
        self.unroll = min(8, r)
        self.unroll2 = next(u for u in (13, 5, 4, 3, 2, 1) if self.k1n % u == 0)
        p = 2 * seq
        n1 = np.arange(self.nq)[None, :]
        k1 = np.arange(self.k1n)[:, None]
        ang = 2.0 * np.pi * n1 * k1 / self.q
        f1 = np.zeros((2 * self.kp, self.nq))
        f1[:self.k1n] = np.cos(ang)
        f1[self.kp:self.kp + self.k1n] = -np.sin(ang)
        self.f1 = f1
        w = np.full((self.k1n,), 2.0)
        w[0] = 1.0
        w[-1] = 1.0
        g1 = np.zeros((self.nq, 2 * self.kp))
        g1[:, :self.k1n] = (np.cos(ang) * w[:, None] / p).T
        g1[:, self.kp:self.kp + self.k1n] = (-np.sin(ang) * w[:, None] / p).T
        self.g1 = g1
        kk = np.arange(self.k1n)[:, None, None]
        k2 = np.arange(r)[None, :, None]
        n2 = np.arange(r)[None, None, :]
        ph = 2.0 * np.pi * (n2 * k2 / r + n2 * kk / p)
        tr, ti = np.cos(ph), -np.sin(ph)
        self.m2 = np.concatenate([np.concatenate([tr, -ti], axis=2),
                                  np.concatenate([ti, tr], axis=2)], axis=1)
        ur, ui = np.transpose(tr, (0, 2, 1)), -np.transpose(ti, (0, 2, 1))
        self.m2i = np.concatenate([np.concatenate([ur, -ui], axis=2),
                                   np.concatenate([ui, ur], axis=2)], axis=1)


def _slab_load(scr, rows):
    return jnp.concatenate([scr[k, rows, :] for k in range(scr.shape[0])], axis=1)


def _slab_store(scr, rows, val):
    for k in range(scr.shape[0]):
        scr[k, rows, :] = val[:, k * LANES:(k + 1) * LANES]


def _fft_forward(plan, src_ref, f1_ref, m2_ref, zp, as_, emit):
    r, nq, kp = plan.r, plan.nq, plan.kp
    for n1 in range(nq):
        _slab_store(zp, slice(n1 * plan.zpitch, n1 * plan.zpitch + r), src_ref[n1 * r:(n1 + 1) * r, :].astype(F32))

    def stage1(n2, c):
        slab = _slab_load(zp, pl.ds(n2, nq, stride=plan.zpitch))
        a = _dot(f1_ref[...].astype(BF16), slab.astype(BF16))
        _slab_store(as_, pl.ds(pl.multiple_of(n2 * plan.apitch, SUBLANES), 2 * kp), a)
        return c

    lax.fori_loop(0, r, stage1, 0, unroll=plan.unroll)

    def stage2(k1, c):
        sr = _slab_load(as_, pl.ds(k1, r, stride=plan.apitch))
        si = _slab_load(as_, pl.ds(kp + k1, r, stride=plan.apitch))
        s = jnp.concatenate([sr, si], axis=0).astype(BF16)
        emit(k1, _dot(m2_ref[k1].astype(BF16), s))
        return c

    lax.fori_loop(0, plan.k1n, stage2, 0, unroll=plan.unroll2)


def _hy_spec_kernel(kf_ref, kb_ref, f1_ref, m2_ref, o_ref, zp, as_, *, plan):
    r = plan.r

    def emit_f(k1, x):
        o_ref[k1] = x

    def emit_b(k1, x):
        sign = jnp.where(lax.broadcasted_iota(jnp.int32, x.shape, 0) < r, 1.0, -1.0)
        o_ref[k1] = o_ref[k1] + sign * x

    _fft_forward(plan, kf_ref, f1_ref, m2_ref, zp, as_, emit_f)
    _fft_forward(plan, kb_ref, f1_ref, m2_ref, zp, as_, emit_b)


def _hy_conv_kernel(z_ref, spec_ref, f1_ref, m2_ref, m2i_ref, g1_ref, o_ref, zp, as_, cs, *, plan):
    r, nq, kp, k1n = plan.r, plan.nq, plan.kp, plan.k1n
    if kp > k1n:
        pad = slice(k1n * plan.cpitch, kp * plan.cpitch)
        for k in range(cs.shape[0]):
            cs[k, pad, :] = jnp.zeros(((kp - k1n) * plan.cpitch, LANES), F32)

    def emit(k1, x):
        kf = spec_ref[k1]
        xr, xi = x[:r], x[r:]
        kr, ki = kf[:r], kf[r:]
        prod = jnp.concatenate([xr * kr - xi * ki, xr * ki + xi * kr], axis=0).astype(BF16)
        c = _dot(m2i_ref[k1].astype(BF16), prod)
        _slab_store(cs, pl.ds(pl.multiple_of(k1 * plan.cpitch, SUBLANES), 2 * r), c)

    _fft_forward(plan, z_ref, f1_ref, m2_ref, zp, as_, emit)

    def stage3(n2, c):
        cr = _slab_load(cs, pl.ds(n2, kp, stride=plan.cpitch))
        ci = _slab_load(cs, pl.ds(r + n2, kp, stride=plan.cpitch))
        y = _dot(g1_ref[...].astype(BF16), jnp.concatenate([cr, ci], axis=0).astype(BF16))
        _slab_store(zp, pl.ds(n2, nq, stride=plan.zpitch), y)
        return c

    lax.fori_loop(0, r, stage3, 0, unroll=plan.unroll)
    for n1 in range(nq):
        o_ref[n1 * r:(n1 + 1) * r, :] = _slab_load(
            zp, slice(n1 * plan.zpitch, n1 * plan.zpitch + r)).astype(o_ref.dtype)


def _hy_filter_kernel(feat_ref, w1_ref, b1_ref, f1_ref, w2_ref, b2_ref, f2_ref, w3_ref, dl_ref, o_ref):
    feats = feat_ref[...]
    half = feats.shape[0] // 2
    cols = o_ref.shape[1]
    both = jnp.concatenate([feats[:half], feats[half:]], axis=1)
    hid = jnp.sin(f1_ref[...] * (_dot(both, w1_ref[...], HIGHEST) + b1_ref[...]))
    hid = jnp.sin(f2_ref[...] * (_dot(hid, w2_ref[...], HIGHEST) + b2_ref[...]))
    k = _dot(hid, w3_ref[...], HIGHEST)
    o_ref[:half, :] = k[:, :cols] * jnp.exp(-(feats[:half, 0:1] * dl_ref[...]))
    o_ref[half:, :] = k[:, cols:] * jnp.exp(-(feats[half:, 0:1] * dl_ref[...]))


def _hyena_features(seq):
    pos = np.arange(seq, dtype=np.float64)
    t = pos / max(seq - 1, 1)
    w = (2.0 * math.pi / seq) * pos
    bands = np.linspace(1e-4, HY_BANDS - 1, HY_BANDS, dtype=np.float64)
    ang = w[:, None] * bands
    feats = np.concatenate([t[:, None], np.cos(ang), -np.sin(ang)], axis=-1).astype(np.float32)
    out = np.zeros((seq, LANES), np.float32)
    out[:, :feats.shape[1]] = feats
    return out


def _hyena_deltas():
    max_decay = math.log(HY_DECAY_TARGET) / HY_FAST_DECAY
    min_decay = math.log(HY_DECAY_TARGET) / HY_SLOW_DECAY
    deltas = np.abs(np.linspace(min_decay, max_decay, HY_WIDTH, dtype=np.float64))
    return np.tile(deltas, 2)[None, :].astype(np.float32)


def _pad_to(x, rows, cols):
    x = x.astype(F32)
    return jnp.pad(x, ((0, 0), (0, rows - x.shape[1]), (0, cols - x.shape[2])))


def _block_diag2(a):
    a = a.astype(F32)
    z = jnp.zeros_like(a)
    return jnp.concatenate([jnp.concatenate([a, z], axis=2), jnp.concatenate([z, a], axis=2)], axis=1)


def _hyena_filter_weights(p):
    hidden = p["hy_filt_w1"].shape[2]
    assert 2 * hidden == LANES
    vec = lambda a: jnp.tile(a.astype(F32)[:, None, :], (1, 1, 2))
    return (_block_diag2(_pad_to(p["hy_filt_w1"], LANES, hidden)), vec(p["hy_filt_b1"]),
            vec(p["hy_filt_freq1"]), _block_diag2(p["hy_filt_w2"]), vec(p["hy_filt_b2"]),
            vec(p["hy_filt_freq2"]), _block_diag2(p["hy_filt_w3"]))


def _once(a):
    return pl.BlockSpec(a.shape, lambda *_: (0,) * a.ndim, pipeline_mode=pl.Buffered(1))


def _hyena_scratch(plan, c):
    nh = c // LANES
    return [pltpu.VMEM((nh, plan.nq * plan.zpitch, LANES), F32),
            pltpu.VMEM((nh, plan.r * plan.apitch, LANES), F32),
            pltpu.VMEM((nh, plan.kp * plan.cpitch, LANES), F32)]


def _hyena_spectra(seq, fw, plan, tf):
    depth = fw[0].shape[0]
    feats = jnp.asarray(_hyena_features(seq))
    deltas = jnp.asarray(_hyena_deltas())
    per_layer = lambda a: pl.BlockSpec((None,) + a.shape[1:], lambda l, i: (l,) + (0,) * (a.ndim - 1))
    filt = pl.pallas_call(
        _hy_filter_kernel,
        grid=(depth, seq // tf),
        in_specs=[pl.BlockSpec((tf, LANES), lambda l, i: (i, 0))] + [per_layer(a) for a in fw]
                 + [_full(deltas.shape)],
        out_specs=pl.BlockSpec((None, tf, 2 * HY_WIDTH), lambda l, i: (l, i, 0)),
        out_shape=jax.ShapeDtypeStruct((depth, seq, 2 * HY_WIDTH), F32),
        compiler_params=_params(("parallel", "parallel")),
        name="hy_filter",
    )(feats, *fw, deltas)
    f1 = jnp.asarray(plan.f1, F32)
    m2 = jnp.asarray(plan.m2, F32)
    c = HY_WIDTH
    return pl.pallas_call(
        functools.partial(_hy_spec_kernel, plan=plan),
        grid=(depth,),
        in_specs=[pl.BlockSpec((None, seq, c), lambda l: (l, 0, 0)),
                  pl.BlockSpec((None, seq, c), lambda l: (l, 0, 1)), _once(f1), _once(m2)],
        out_specs=pl.BlockSpec((None, plan.k1n, 2 * plan.r, c), lambda l: (l, 0, 0, 0)),
        out_shape=jax.ShapeDtypeStruct((depth, plan.k1n, 2 * plan.r, c), F32),
        scratch_shapes=_hyena_scratch(plan, c)[:2],
        compiler_params=_params(("arbitrary",)),
        name="hy_spec",
    )(filt, filt, f1, m2)


def _hyena_conv(z3, spec, l, plan):
    nb, seq, c = z3.shape
    f1 = jnp.asarray(plan.f1, F32)
    m2 = jnp.asarray(plan.m2, F32)
    m2i = jnp.asarray(plan.m2i, F32)
    g1 = jnp.asarray(plan.g1, F32)
    return pl.pallas_call(
        functools.partial(_hy_conv_kernel, plan=plan),
        grid=(nb,),
        in_specs=[pl.BlockSpec((None, seq, c), lambda b: (b, 0, 0)),
                  _resident(spec, l), _once(f1), _once(m2), _once(m2i), _once(g1)],
        out_specs=pl.BlockSpec((None, seq, c), lambda b: (b, 0, 0)),
        out_shape=jax.ShapeDtypeStruct((nb, seq, c), STREAM),
        scratch_shapes=_hyena_scratch(plan, c),
        compiler_params=_params(("parallel",)),
        name="hy_conv",
    )(z3, spec, f1, m2, m2i, g1)


def _mix_kernel(h_ref, hf_ref, hb_ref, ga_ref, yf_ref, yb_ref, ub_ref, yc_ref, z_ref, x0_ref,
                d_ref, gw_ref, gb_ref, hbias_ref, mg_ref, wo_ref, o_ref):
    f32 = lambda ref: ref[...].astype(F32)
    ya = (f32(hf_ref) + f32(hb_ref)) * _gelu(f32(ga_ref))
    yb = _gelu(f32(ub_ref) * d_ref[...] + f32(yf_ref) + f32(yb_ref))
    yb = yb * _sigmoid(_dot(yb.astype(BF16), gw_ref[...].astype(BF16)) + gb_ref[...])
    yc = (f32(yc_ref) + f32(z_ref) * hbias_ref[...]) * f32(x0_ref)
    a1, a2 = RG_WIDTH, RG_WIDTH + S5_WIDTH
    na = (_rms_nogain(ya) * mg_ref[:, :a1]).astype(BF16)
    nb = (_rms_nogain(yb) * mg_ref[:, a1:a2]).astype(BF16)
    nc = (_rms_nogain(yc) * mg_ref[:, a2:]).astype(BF16)
    wo = lambda lo, hi: wo_ref[lo:hi, :].astype(BF16)
    out = _dot(na, wo(0, a1)) + _dot(nb, wo(a1, a2)) + _dot(nc, wo(a2, wo_ref.shape[0]))
    o_ref[...] = h_ref[...] + out


def _mix(rows, consts, l, tm):
    nb, seq, dm = rows[0].shape
    return pl.pallas_call(
        _mix_kernel,
        grid=(nb * seq // tm,),
        in_specs=[_row_spec(seq, tm, a.shape[2]) for a in rows] + [_layer_spec(a, l) for a in consts[:-1]]
                 + [_resident(consts[-1], l)],
        out_specs=_row_spec(seq, tm, dm),
        out_shape=jax.ShapeDtypeStruct((nb, seq, dm), F32),
        compiler_params=_params(("parallel",)),
        name="mix",
    )(*rows, *consts)


FFN_TILE = 256


def _ffn_kernel(h_ref, hp_ref, hn_ref, g_ref, wu_ref, cw_ref, cb_ref, wd_ref, fg_ref, o_ref,
                x_scr, u_scr, gated_scr, y_scr, *, tiles_per_seq, tm, final):
    i = pl.program_id(0)
    first = (i % tiles_per_seq) == 0
    last = (i % tiles_per_seq) == tiles_per_seq - 1
    nslab = h_ref.shape[1] // LANES
    ph = tm // SUBLANES
    _fill_normed_phase_major(x_scr, h_ref[...], hp_ref[...], hn_ref[...], g_ref[...], first, last, tm)
    x = jnp.concatenate([x_scr[c] for c in range(nslab)], axis=1).astype(BF16)
    for k in range(D_FF // FFN_TILE):
        halves = []
        for part in range(2):
            lo = part * D_FF + k * FFN_TILE
            slot = 2 * k + part
            u_scr[slot] = _dot(x, wu_ref[:, lo:lo + FFN_TILE])
            halves.append(_dwconv(u_scr.at[slot], slice(0, FFN_TILE), cw_ref, cb_ref,
                                  slice(lo, lo + FFN_TILE), tm))
        gated_scr[:, k * FFN_TILE:(k + 1) * FFN_TILE] = (_gelu(halves[0]) * halves[1]).astype(BF16)
    y = _dot(gated_scr[...], wd_ref[...])
    for c in range(nslab):
        y_scr[c] = y[:, c * LANES:(c + 1) * LANES]
    for s in range(SUBLANES):
        for c in range(nslab):
            rows = slice(s * ph, (s + 1) * ph)
            lanes = slice(c * LANES, (c + 1) * LANES)
            o_ref[rows, lanes] = h_ref[rows, lanes] + y_scr[c, pl.ds(s, ph, stride=SUBLANES), :]
    if final:
        out = o_ref[...]
        o_ref[...] = out * lax.rsqrt(jnp.mean(out * out, axis=-1, keepdims=True) + RMS_EPS) * fg_ref[...]


def _resident(arr, *lead):
    rest = arr.shape[len(lead):]
    zeros = (0,) * len(rest)
    return pl.BlockSpec((None,) * len(lead) + rest, lambda *_: tuple(lead) + zeros,
                        pipeline_mode=pl.Buffered(1))


def _ffn(h, g, w_up, conv_w, conv_b, w_down, final_g, l, tm, final):
    nb, seq, d = h.shape
    prev, nxt = _halo_specs(seq, tm, d)
    return pl.pallas_call(
        functools.partial(_ffn_kernel, tiles_per_seq=seq // tm, tm=tm, final=final),
        grid=(nb * seq // tm,),
        in_specs=[_row_spec(seq, tm, d), prev, nxt,
                  _layer_spec(g, l), _resident(w_up, l), _layer_spec(conv_w, l), _layer_spec(conv_b, l),
                  _resident(w_down, l), _full(final_g.shape)],
        out_specs=_row_spec(seq, tm, d),
        out_shape=jax.ShapeDtypeStruct((nb, seq, d), F32),
        scratch_shapes=[pltpu.VMEM((d // LANES, tm + 2 * HALO, LANES), F32),
                        pltpu.VMEM((2 * (D_FF // FFN_TILE), tm + 2 * HALO, FFN_TILE), F32),
                        pltpu.VMEM((tm, D_FF), BF16),
                        pltpu.VMEM((d // LANES, tm, LANES), F32)],
        compiler_params=_params(("parallel",)),
        name="ffn_final" if final else "ffn",
    )(h, h, h, g, w_up, conv_w, conv_b, w_down, final_g)


def _row_vec(a):
    return a.astype(F32)[:, None, :]


def _trunk(x, p, *, tm, tm_wide, t_scan, tf):
    seq = x.shape[1]
    depth = p["w_in"].shape[0]
    plan = _FftPlan(seq)

    proj_c = (_row_vec(p["norm1_g"]), p["w_in"].astype(F32), p["rg_conv_w"].astype(F32),
              _row_vec(p["rg_conv_b"]), p["hy_conv_w"].astype(F32), _row_vec(p["hy_conv_b"]))
    rg_w, rg_bias, rg_sp = _rglru_weights(p)
    s5_w = _s5_weights(p)
    hy_spec = _hyena_spectra(seq, _hyena_filter_weights(p), plan, tf)
    mix_c = (_row_vec(p["s5_d"]), p["s5_glu_w"].astype(F32), _row_vec(p["s5_glu_b"]),
             _row_vec(p["hy_bias"]), _row_vec(p["mix_norm_g"]), p["w_out"].astype(F32))
    ffn_c = (_row_vec(p["norm2_g"]), p["w_up"].astype(BF16), p["ffn_conv_w"].astype(F32),
             _row_vec(p["ffn_conv_b"]), p["w_down"].astype(BF16), p["final_norm_g"].astype(F32)[None])

    h = x.astype(F32)
    for l in range(depth):
        urg, ga, ub, x0, z = _proj(h, *proj_c, l, tm_wide)
        hf, hb, yf, yb = _scans(urg, ub, rg_w, rg_bias, rg_sp, *s5_w, l, t_scan)
        yc = _hyena_conv(z, hy_spec, l, plan)
        h = _mix([h, hf, hb, ga, yf, yb, ub, yc, z, x0], mix_c, l, tm_wide)
        h = _ffn(h, *ffn_c, l, tm, final=(l == depth - 1))
    return h.astype(x.dtype)


def kernel(x, norm1_g, w_in, rg_conv_w, rg_conv_b, rg_wa, rg_ba, rg_wx, rg_bx, rg_lambda, s5_a_re, s5_a_im, s5_log_dt, s5_b_re, s5_b_im, s5_c_re, s5_c_im, s5_d, s5_glu_w, s5_glu_b, hy_conv_w, hy_conv_b, hy_filt_w1, hy_filt_b1, hy_filt_freq1, hy_filt_w2, hy_filt_b2, hy_filt_freq2, hy_filt_w3, hy_bias, mix_norm_g, w_out, norm2_g, w_up, ffn_conv_w, ffn_conv_b, w_down, final_norm_g):
    p = dict(norm1_g=norm1_g, w_in=w_in, rg_conv_w=rg_conv_w, rg_conv_b=rg_conv_b, rg_wa=rg_wa, rg_ba=rg_ba,
             rg_wx=rg_wx, rg_bx=rg_bx, rg_lambda=rg_lambda, s5_a_re=s5_a_re, s5_a_im=s5_a_im,
             s5_log_dt=s5_log_dt, s5_b_re=s5_b_re, s5_b_im=s5_b_im, s5_c_re=s5_c_re, s5_c_im=s5_c_im,
             s5_d=s5_d, s5_glu_w=s5_glu_w, s5_glu_b=s5_glu_b, hy_conv_w=hy_conv_w, hy_conv_b=hy_conv_b,
             hy_filt_w1=hy_filt_w1, hy_filt_b1=hy_filt_b1, hy_filt_freq1=hy_filt_freq1, hy_filt_w2=hy_filt_w2,
             hy_filt_b2=hy_filt_b2, hy_filt_freq2=hy_filt_freq2, hy_filt_w3=hy_filt_w3, hy_bias=hy_bias,
             mix_norm_g=mix_norm_g, w_out=w_out, norm2_g=norm2_g, w_up=w_up, ffn_conv_w=ffn_conv_w,
             ffn_conv_b=ffn_conv_b, w_down=w_down, final_norm_g=final_norm_g)
    return _trunk(x, p, tm=512, tm_wide=1024, t_scan=128, tf=512)
```

```python
import functools
import math

import numpy as np
import jax
import jax.numpy as jnp
from jax import lax
from jax.experimental import pallas as pl
from jax.experimental.pallas import tpu as pltpu

F32 = jnp.float32
BF16 = jnp.bfloat16
STREAM = jnp.bfloat16

RMS_EPS = 1e-6
RG_WIDTH = 384
RG_HEADS = 6
RG_C = 8.0
S5_WIDTH = 384
S5_GROUP = 16
S5_GROUPS = 24
S5_STATE = 64
S5_NSTATE = S5_GROUPS * S5_STATE
HY_WIDTH = 256
HY_BANDS = 16
HY_FAST_DECAY = 0.3
HY_SLOW_DECAY = 1.5
HY_DECAY_TARGET = 1e-2
D_FF = 2816
N_DIR = 2

LANES = 128
SUBLANES = 8
HALO = SUBLANES
VMEM_LIMIT = 56 * 1024 * 1024

HIGHEST = lax.Precision.HIGHEST


def _dot(a, b, precision=None):
    return jnp.dot(a, b, preferred_element_type=F32, precision=precision)


def _gelu(x):
    c = math.sqrt(2.0 / math.pi)
    return 0.5 * x * (1.0 + jnp.tanh(c * (x + 0.044715 * (x * x * x))))


def _sigmoid(x):
    return 0.5 * jnp.tanh(0.5 * x) + 0.5


def _rms_nogain(x):
    return x * lax.rsqrt(jnp.mean(x * x, axis=-1, keepdims=True) + RMS_EPS)


def _params(sem):
    return pltpu.CompilerParams(dimension_semantics=sem, vmem_limit_bytes=VMEM_LIMIT)


def _full(shape):
    nd = len(shape)
    return pl.BlockSpec(shape, lambda *_: (0,) * nd)


def _layer_spec(arr, *lead):
    rest = arr.shape[len(lead):]
    zeros = (0,) * len(rest)
    return pl.BlockSpec((None,) * len(lead) + rest, lambda *_: tuple(lead) + zeros)


def _row_spec(seq, tm, c):
    tps = seq // tm
    return pl.BlockSpec((None, tm, c), lambda i: (i // tps, i % tps, 0))


def _halo_specs(seq, tm, c):
    tps = seq // tm
    bpt = tm // HALO
    prev = pl.BlockSpec((None, HALO, c), lambda i: (i // tps, jnp.maximum((i % tps) * bpt - 1, 0), 0))
    nxt = pl.BlockSpec((None, HALO, c),
                       lambda i: (i // tps, jnp.minimum((i % tps + 1) * bpt, seq // HALO - 1), 0))
    return prev, nxt


def _fill_normed_phase_major(x_scr, h, hp, hn, g, first, last, tm):
    ph = tm // SUBLANES

    def norm(v):
        return v * lax.rsqrt(jnp.mean(v * v, axis=-1, keepdims=True) + RMS_EPS) * g

    xn = norm(h)
    xp = norm(jnp.where(first, 0.0, hp))
    xq = norm(jnp.where(last, 0.0, hn))
    for c in range(x_scr.shape[0]):
        lanes = slice(c * LANES, (c + 1) * LANES)
        for s in range(SUBLANES):
            x_scr[c, pl.ds(s, ph, stride=SUBLANES), :] = xn[s * ph:(s + 1) * ph, lanes]
        x_scr[c, tm:tm + HALO, :] = xp[:, lanes]
        x_scr[c, tm + HALO:tm + 2 * HALO, :] = xq[:, lanes]


def _time_shifted(u_ref, cols, d, tm):
    width = cols.stop - cols.start
    sub = lax.broadcasted_iota(jnp.int32, (SUBLANES, width), 0)
    tile = lambda j: u_ref[j * SUBLANES:(j + 1) * SUBLANES, cols]
    prev_halo = u_ref[tm:tm + HALO, cols]
    next_halo = u_ref[tm + HALO:tm + 2 * HALO, cols]
    ph = tm // SUBLANES
    if d == 0:
        return u_ref[0:tm, cols]
    if d < 0:
        assert d == -1
        head = jnp.where(sub == 0, pltpu.roll(prev_halo, 1, 0), pltpu.roll(tile(ph - 1), 1, 0))
        return jnp.concatenate([head, u_ref[0:tm - SUBLANES, cols]], axis=0)
    tails = [jnp.where(sub == SUBLANES - 1, pltpu.roll(next_halo, SUBLANES - 1 - jj, 0),
                       pltpu.roll(tile(jj), SUBLANES - 1, 0)) for jj in range(d)]
    return jnp.concatenate([u_ref[d * SUBLANES:tm, cols]] + tails, axis=0)


def _dwconv(u_ref, cols, w_ref, b_ref, wcols, tm):
    taps = w_ref.shape[0]
    left = (taps - 1) // 2
    acc = b_ref[:, wcols]
    for j in range(taps):
        acc = acc + _time_shifted(u_ref, cols, j - left, tm) * w_ref[j:j + 1, wcols]
    return acc


def _proj_kernel(h_ref, hp_ref, hn_ref, g_ref, w_ref, rcw_ref, rcb_ref, hcw_ref, hcb_ref,
                 urg_ref, ga_ref, ub_ref, x0_ref, z_ref, x_scr, u_scr, w_scr, *, tiles_per_seq, tm):
    i = pl.program_id(0)
    first = (i % tiles_per_seq) == 0
    last = (i % tiles_per_seq) == tiles_per_seq - 1
    g = g_ref[...]

    @pl.when(i == 0)
    def _():
        w_scr[...] = w_ref[...].astype(BF16)

    def norm(v):
        return v * lax.rsqrt(jnp.mean(v * v, axis=-1, keepdims=True) + RMS_EPS) * g

    x_scr[0:HALO, :] = norm(jnp.where(first, 0.0, hp_ref[...]))
    x_scr[HALO:HALO + tm, :] = norm(h_ref[...])
    x_scr[HALO + tm:2 * HALO + tm, :] = norm(jnp.where(last, 0.0, hn_ref[...]))
    a0, a1, a2, a3, a4 = 0, RG_WIDTH, 2 * RG_WIDTH, 2 * RG_WIDTH + S5_WIDTH, w_ref.shape[1]
    x = x_scr[...].astype(BF16)
    for lo, hi in ((a0, a1), (a3, a4), (a1, a3)):
        u_scr[:, lo:hi] = _dot(x, w_scr[:, lo:hi])
    acc = rcb_ref[...]
    for j in range(rcw_ref.shape[0]):
        acc = acc + u_scr[pl.ds(HALO + j - 1, tm), a0:a1] * rcw_ref[j:j + 1, :]
    urg_ref[...] = acc
    ga_ref[...] = u_scr[pl.ds(HALO, tm), a1:a2].astype(ga_ref.dtype)
    ub_ref[...] = u_scr[pl.ds(HALO, tm), a2:a3].astype(ub_ref.dtype)
    q = []
    for part in range(3):
        lo = part * HY_WIDTH
        acc = hcb_ref[:, lo:lo + HY_WIDTH]
        for j in range(hcw_ref.shape[0]):
            acc = acc + (u_scr[pl.ds(HALO + j - 1, tm), a3 + lo:a3 + lo + HY_WIDTH]
                         * hcw_ref[j:j + 1, lo:lo + HY_WIDTH])
        q.append(acc)
    x0_ref[...] = q[0].astype(x0_ref.dtype)
    z_ref[...] = (q[2] * q[1]).astype(z_ref.dtype)


def _proj(h, g, w_in, rcw, rcb, hcw, hcb, l, tm):
    nb, seq, d = h.shape
    cols = w_in.shape[-1]
    prev, nxt = _halo_specs(seq, tm, d)
    row = lambda c: _row_spec(seq, tm, c)
    outs = [jax.ShapeDtypeStruct((nb, seq, c), dt) for c, dt in
            ((RG_WIDTH, F32), (RG_WIDTH, STREAM), (S5_WIDTH, STREAM), (HY_WIDTH, STREAM), (HY_WIDTH, STREAM))]
    consts = [g, w_in, rcw, rcb, hcw, hcb]
    return pl.pallas_call(
        functools.partial(_proj_kernel, tiles_per_seq=seq // tm, tm=tm),
        grid=(nb * seq // tm,),
        in_specs=[row(d), prev, nxt] + [_layer_spec(g, l), _resident(w_in, l)]
                 + [_layer_spec(a, l) for a in consts[2:]],
        out_specs=[row(RG_WIDTH), row(RG_WIDTH), row(S5_WIDTH), row(HY_WIDTH), row(HY_WIDTH)],
        out_shape=outs,
        scratch_shapes=[pltpu.VMEM((tm + 2 * HALO, d), F32), pltpu.VMEM((tm + 2 * HALO, cols), F32),
                        pltpu.VMEM((d, cols), BF16)],
        compiler_params=_params(("arbitrary",)),
        name="proj",
    )(h, h, h, *consts)


RG_STAGES = 4


def _lo_mask():
    return lax.broadcasted_iota(jnp.int32, (SUBLANES, LANES), 0) < (SUBLANES // 2)


def _rglru_phases(uf_ref, ub_ref, w_ref, bias_ref, sp_ref, of_ref, ob_ref,
                  af, ab, bf, bb, carry_scr, *, t, nb):
    nm = RG_WIDTH // LANES
    tq = t // RG_STAGES

    def stage(q):
        for dr, (u_ref, a_scr, b_scr) in enumerate(((uf_ref, af, bf), (ub_ref, ab, bb))):
            t0 = q * tq if dr == 0 else t - (q + 1) * tq
            u = u_ref[:, t0:t0 + tq, :].reshape(nb * tq, RG_WIDTH)
            gates = _dot(u.astype(BF16), w_ref[dr]) + bias_ref[dr]
            r = _sigmoid(gates[:, :RG_WIDTH])
            gi = _sigmoid(gates[:, RG_WIDTH:])
            log_a = (-RG_C) * r * sp_ref[dr]
            a = jnp.exp(log_a)
            m2 = -jnp.tanh(log_a) * (a * a + 1.0)
            mult = jnp.where(m2 > 0.0, m2 * lax.rsqrt(m2), 0.0)
            bin_ = mult * (gi * u)
            for b in range(nb):
                for m in range(nm):
                    rows = pl.ds(t0 * SUBLANES + dr * nb + b, tq, stride=SUBLANES)
                    a_scr[m, rows, :] = a[b * tq:(b + 1) * tq, m * LANES:(m + 1) * LANES]
                    b_scr[m, rows, :] = bin_[b * tq:(b + 1) * tq, m * LANES:(m + 1) * LANES]

    def scan():
        lo = _lo_mask()
        carry = [carry_scr[m] for m in range(nm)]
        for j in range(t):
            row = j * SUBLANES
            mrow = (t - 1 - j) * SUBLANES
            for m in range(nm):
                at = jnp.where(lo, af[m, row:row + SUBLANES, :], ab[m, mrow:mrow + SUBLANES, :])
                bt = jnp.where(lo, bf[m, row:row + SUBLANES, :], bb[m, mrow:mrow + SUBLANES, :])
                h = at * carry[m] + bt
                bf[m, row:row + SUBLANES, :] = h
                bb[m, mrow:mrow + SUBLANES, :] = h
                carry[m] = h
        for m in range(nm):
            carry_scr[m] = carry[m]
        for b in range(nb):
            for m in range(nm):
                of_ref[b, :, m * LANES:(m + 1) * LANES] = bf[m, pl.ds(b, t, stride=SUBLANES), :].astype(STREAM)
                ob_ref[b, :, m * LANES:(m + 1) * LANES] = (
                    bb[m, pl.ds(nb + b, t, stride=SUBLANES), :].astype(STREAM))

    return [functools.partial(stage, q) for q in range(RG_STAGES)] + [scan]


def _rglru_weights(p):
    hd = RG_WIDTH // RG_HEADS
    place = np.zeros((2, RG_HEADS, hd, 2 * RG_WIDTH), np.float32)
    for q in range(2):
        for h in range(RG_HEADS):
            place[q, h, np.arange(hd), q * RG_WIDTH + h * hd + np.arange(hd)] = 1.0
    both = jnp.stack([p["rg_wa"], p["rg_wx"]], axis=2).astype(F32)
    depth = both.shape[0]
    w = jnp.einsum('ldqhij,qhjc->ldhic', both, jnp.asarray(place))
    w = w.reshape(depth, N_DIR, RG_WIDTH, 2 * RG_WIDTH).astype(BF16)
    bias = jnp.concatenate([p["rg_ba"], p["rg_bx"]], axis=-1).astype(F32)[:, :, None, :]
    x = -p["rg_lambda"].astype(F32)
    sp = (jnp.maximum(x, 0.0) + jnp.log1p(jnp.exp(-jnp.abs(x))))[:, :, None, :]
    return w, bias, sp


S5_NBLK = S5_NSTATE // LANES
S5_GRP = 4
S5_PER = S5_NBLK // (S5_WIDTH // LANES)


def _reverse_tiles(src, dst, nm, ntile):
    for j in range(ntile):
        s = (ntile - 1 - j) * SUBLANES
        for m in range(nm):
            dst[m, j * SUBLANES:(j + 1) * SUBLANES, :] = src[m, s:s + SUBLANES, :]


def _s5_phases(uf_ref, ub_ref, wb_ref, wc_ref, ar_ref, ai_ref, yf_ref, yb_ref,
               uf8, ub8, ubr, bu, ym, ymr, carry_scr, *, t, nb):
    nm = S5_WIDTH // LANES
    assert S5_GRP == S5_PER

    def load():
        for b in range(nb):
            for m in range(nm):
                lanes = slice(m * LANES, (m + 1) * LANES)
                uf8[m, pl.ds(b, t, stride=SUBLANES), :] = uf_ref[b, :, lanes].astype(F32)
                ub8[m, pl.ds(nb + b, t, stride=SUBLANES), :] = ub_ref[b, :, lanes].astype(F32)
        _reverse_tiles(ub8, ubr, nm, t)

    def bproj(m):
        lhs = jnp.concatenate([uf8[m], ubr[m]], axis=1).astype(BF16)
        res = _dot(lhs, wb_ref[m])
        for q in range(2 * S5_PER):
            bu[2 * S5_PER * m + q] = res[:, q * LANES:(q + 1) * LANES]

    def scan(g):
        blocks = list(range(g * S5_GRP, (g + 1) * S5_GRP))
        ars = [ar_ref[n] for n in blocks]
        ais = [ai_ref[n] for n in blocks]
        carry = []
        for n in blocks:
            carry += [carry_scr[2 * n], carry_scr[2 * n + 1]]
        for j in range(t):
            rows = slice(j * SUBLANES, (j + 1) * SUBLANES)
            for k, n in enumerate(blocks):
                hr, hi = carry[2 * k], carry[2 * k + 1]
                nr = ars[k] * hr - ais[k] * hi + bu[2 * n, rows, :]
                ni = ars[k] * hi + ais[k] * hr + bu[2 * n + 1, rows, :]
                bu[2 * n, rows, :] = nr
                bu[2 * n + 1, rows, :] = ni
                carry[2 * k], carry[2 * k + 1] = nr, ni
        for k, n in enumerate(blocks):
            carry_scr[2 * n] = carry[2 * k]
            carry_scr[2 * n + 1] = carry[2 * k + 1]

    def cproj(m):
        rows = lax.broadcasted_iota(jnp.int32, (SUBLANES * t, LANES), 0)
        fwd_row = (rows & (SUBLANES - 1)) < nb
        hcat = jnp.concatenate([bu[2 * S5_PER * m + q] for q in range(2 * S5_PER)], axis=1).astype(BF16)
        acc = _dot(hcat, wc_ref[m])
        ym[m] = jnp.where(fwd_row, acc[:, :LANES], acc[:, LANES:])

    def store():
        _reverse_tiles(ym, ymr, nm, t)
        for b in range(nb):
            for m in range(nm):
                lanes = slice(m * LANES, (m + 1) * LANES)
                yf_ref[b, :, lanes] = ym[m, pl.ds(b, t, stride=SUBLANES), :].astype(STREAM)
                yb_ref[b, :, lanes] = ymr[m, pl.ds(nb + b, t, stride=SUBLANES), :].astype(STREAM)

    part = functools.partial
    return (load, [part(bproj, m) for m in range(nm)], [part(scan, g) for g in range(nm)],
            [part(cproj, m) for m in range(nm)], store)


def _scans_kernel(*refs, t, nb):
    rg_in, s5_in = refs[0:5], refs[5:11]
    rg_out, s5_out = refs[11:13], refs[13:15]
    rg_scr, s5_scr = refs[15:20], refs[20:27]

    @pl.when(pl.program_id(0) == 0)
    def _():
        for scr in (*rg_scr, s5_scr[0], s5_scr[1], s5_scr[6]):
            scr[...] = jnp.zeros_like(scr)

    rg = _rglru_phases(*rg_in, *rg_out, *rg_scr, t=t, nb=nb)
    load, bproj, scan, cproj, store = _s5_phases(*s5_in, *s5_out, *s5_scr, t=t, nb=nb)
    order = [load, bproj[0], rg[0], bproj[1], rg[1], scan[0], bproj[2], rg[2], scan[1], cproj[0],
             rg[3], scan[2], cproj[1], rg[4], cproj[2], store]
    assert len(rg) == 5 and len(bproj) == 3
    for phase in order:
        phase()


def _scans(urg, ub, rg_w, rg_bias, rg_sp, wb, wc, ar, ai, l, t):
    nb, seq, c = urg.shape
    assert 2 * nb == SUBLANES and ub.shape == urg.shape
    nchunk = seq // t
    fwd = pl.BlockSpec((nb, t, c), lambda i: (0, i, 0))
    bwd = pl.BlockSpec((nb, t, c), lambda i: (0, nchunk - 1 - i, 0))
    nm = c // LANES
    rows = SUBLANES * t
    out = jax.ShapeDtypeStruct((nb, seq, c), STREAM)
    slab = pltpu.VMEM((nm, rows, LANES), F32)
    rg_scr = [slab, slab, slab, slab, pltpu.VMEM((nm, SUBLANES, LANES), F32)]
    s5_scr = [slab, slab, slab, pltpu.VMEM((2 * S5_NBLK, rows, LANES), F32), slab, slab,
              pltpu.VMEM((2 * S5_NBLK, SUBLANES, LANES), F32)]
    rg_c = [rg_w, rg_bias, rg_sp]
    s5_c = [wb, wc, ar, ai]
    return pl.pallas_call(
        functools.partial(_scans_kernel, t=t, nb=nb),
        grid=(nchunk,),
        in_specs=[fwd, bwd] + [_layer_spec(a, l) for a in rg_c]
                 + [fwd, bwd] + [_layer_spec(a, l) for a in s5_c],
        out_specs=[fwd, bwd, fwd, bwd],
        out_shape=[out, out, out, out],
        scratch_shapes=rg_scr + s5_scr,
        compiler_params=_params(("arbitrary",)),
        name="scans",
    )(urg, urg, *rg_c, ub, ub, *s5_c)


def _s5_place():
    slots = LANES // S5_GROUP
    half = LANES // S5_STATE
    place = np.zeros((slots, 2, S5_STATE, S5_PER * 2 * LANES), np.float32)
    for k in range(slots):
        for r in range(2):
            col = (k // half) * 2 * LANES + r * LANES + (k % half) * S5_STATE
            place[k, r, np.arange(S5_STATE), col + np.arange(S5_STATE)] = 1.0
    return place


def _s5_weights(p):
    lr = p["s5_a_re"].astype(F32)
    li = p["s5_a_im"].astype(F32)
    dt = jnp.exp(p["s5_log_dt"].astype(F32))[..., None]
    mag = jnp.exp(lr * dt)
    abar_r = mag * jnp.cos(li * dt)
    abar_i = mag * jnp.sin(li * dt)
    den = lr * lr + li * li
    nr = abar_r - 1.0
    ni = abar_i
    coef_r = ((nr * lr + ni * li) / den)[..., None]
    coef_i = ((ni * lr - nr * li) / den)[..., None]
    b_re = p["s5_b_re"].astype(F32)
    b_im = p["s5_b_im"].astype(F32)
    bbar = jnp.stack([coef_r * b_re - coef_i * b_im, coef_r * b_im + coef_i * b_re], axis=2)
    depth = lr.shape[0]
    nm = S5_WIDTH // LANES
    slots = LANES // S5_GROUP
    place = jnp.asarray(_s5_place())
    bb = bbar.reshape(depth, N_DIR, 2, nm, slots, S5_STATE, S5_GROUP)
    wb = jnp.einsum('ldrmkpc,krpx->lmdkcx', bb, place).reshape(depth, nm, 2 * LANES, S5_PER * 2 * LANES)
    cc = jnp.stack([p["s5_c_re"].astype(F32), -p["s5_c_im"].astype(F32)], axis=2)
    cc = cc.reshape(depth, N_DIR, 2, nm, slots, S5_GROUP, S5_STATE)
    wc = jnp.einsum('ldrmkcp,krpx->lmdkcx', cc, place).reshape(depth, nm, 2 * LANES, S5_PER * 2 * LANES)
    wc = wc.transpose(0, 1, 3, 2)

    def tile_rows(a):
        a = a.reshape(depth, N_DIR, S5_NBLK, LANES).transpose(0, 2, 1, 3)
        return jnp.repeat(a, SUBLANES // N_DIR, axis=2)

    return wb.astype(BF16), wc.astype(BF16), tile_rows(abar_r), tile_rows(abar_i)


def _odd8(n):
    p = -(-n // SUBLANES)
    if p % 2 == 0:
        p += 1
    return p * SUBLANES


class _FftPlan:
    def __init__(self, seq):
        self.seq = seq
        r = 1
        while r * r < seq:
            r *= 2
        self.r = r
        self.nq = seq // r
        self.q = 2 * seq // r
        self.k1n = self.q // 2 + 1
        self.kp = -(-self.k1n // SUBLANES) * SUBLANES
        self.zpitch = _odd8(r)
        self.apitch = _odd8(2 * self.kp)
        self.cpitch = _odd8(2 * r)
        self.unroll = min(64, r)
        self.unroll2 = next(u for u in (65, 13, 5, 4, 3, 2, 1) if self.k1n % u == 0)
        p = 2 * seq
        n1 = np.arange(self.nq)[None, :]
        k1 = np.arange(self.k1n)[:, None]
        ang = 2.0 * np.pi * n1 * k1 / self.q
        f1 = np.zeros((2 * self.kp, self.nq))
        f1[:self.k1n] = np.cos(ang)
        f1[self.kp:self.kp + self.k1n] = -np.sin(ang)
        self.f1 = f1
        w = np.full((self.k1n,), 2.0)
        w[0] = 1.0
        w[-1] = 1.0
        g1 = np.zeros((self.nq, 2 * self.kp))
        g1[:, :self.k1n] = (np.cos(ang) * w[:, None] / p).T
        g1[:, self.kp:self.kp + self.k1n] = (-np.sin(ang) * w[:, None] / p).T
        self.g1 = g1
        kk = np.arange(self.k1n)[:, None, None]
        k2 = np.arange(r)[None, :, None]
        n2 = np.arange(r)[None, None, :]
        ph = 2.0 * np.pi * (n2 * k2 / r + n2 * kk / p)
        tr, ti = np.cos(ph), -np.sin(ph)
        self.m2 = np.concatenate([np.concatenate([tr, -ti], axis=2),
                                  np.concatenate([ti, tr], axis=2)], axis=1)
        ur, ui = np.transpose(tr, (0, 2, 1)), -np.transpose(ti, (0, 2, 1))
        self.m2i = np.concatenate([np.concatenate([ur, -ui], axis=2),
                                   np.concatenate([ui, ur], axis=2)], axis=1)


def _slab_load(scr, rows):
    return jnp.concatenate([scr[k, rows, :] for k in range(scr.shape[0])], axis=1)


def _slab_store(scr, rows, val):
    for k in range(scr.shape[0]):
        scr[k, rows, :] = val[:, k * LANES:(k + 1) * LANES]


def _fft_forward(plan, src_ref, f1_ref, m2_ref, zp, as_, emit):
    r, nq, kp = plan.r, plan.nq, plan.kp
    for n1 in range(nq):
        _slab_store(zp, slice(n1 * plan.zpitch, n1 * plan.zpitch + r), src_ref[n1 * r:(n1 + 1) * r, :].astype(F32))

    def stage1(n2, c):
        slab = _slab_load(zp, pl.ds(n2, nq, stride=plan.zpitch))
        a = _dot(f1_ref[...].astype(BF16), slab.astype(BF16))
        _slab_store(as_, pl.ds(pl.multiple_of(n2 * plan.apitch, SUBLANES), 2 * kp), a)
        return c

    lax.fori_loop(0, r, stage1, 0, unroll=plan.unroll)

    def stage2(k1, c):
        sr = _slab_load(as_, pl.ds(k1, r, stride=plan.apitch))
        si = _slab_load(as_, pl.ds(kp + k1, r, stride=plan.apitch))
        s = jnp.concatenate([sr, si], axis=0).astype(BF16)
        emit(k1, _dot(m2_ref[k1].astype(BF16), s))
        return c

    lax.fori_loop(0, plan.k1n, stage2, 0, unroll=plan.unroll2)


def _hy_spec_kernel(kf_ref, kb_ref, f1_ref, m2_ref, o_ref, zp, as_, *, plan):
    r = plan.r

    def emit_f(k1, x):
        o_ref[k1] = x

    def emit_b(k1, x):
        sign = jnp.where(lax.broadcasted_iota(jnp.int32, x.shape, 0) < r, 1.0, -1.0)
        o_ref[k1] = o_ref[k1] + sign * x

    _fft_forward(plan, kf_ref, f1_ref, m2_ref, zp, as_, emit_f)
    _fft_forward(plan, kb_ref, f1_ref, m2_ref, zp, as_, emit_b)


def _hy_conv_kernel(z_ref, spec_ref, f1_ref, m2_ref, m2i_ref, g1_ref, o_ref, zp, as_, cs, *, plan):
    r, nq, kp, k1n = plan.r, plan.nq, plan.kp, plan.k1n
    if kp > k1n:
        pad = slice(k1n * plan.cpitch, kp * plan.cpitch)
        for k in range(cs.shape[0]):
            cs[k, pad, :] = jnp.zeros(((kp - k1n) * plan.cpitch, LANES), F32)

    def emit(k1, x):
        kf = spec_ref[k1]
        xr, xi = x[:r], x[r:]
        kr, ki = kf[:r], kf[r:]
        prod = jnp.concatenate([xr * kr - xi * ki, xr * ki + xi * kr], axis=0).astype(BF16)
        c = _dot(m2i_ref[k1].astype(BF16), prod)
        _slab_store(cs, pl.ds(pl.multiple_of(k1 * plan.cpitch, SUBLANES), 2 * r), c)

    _fft_forward(plan, z_ref, f1_ref, m2_ref, zp, as_, emit)

    def stage3(n2, c):
        cr = _slab_load(cs, pl.ds(n2, kp, stride=plan.cpitch))
        ci = _slab_load(cs, pl.ds(r + n2, kp, stride=plan.cpitch))
        y = _dot(g1_ref[...].astype(BF16), jnp.concatenate([cr, ci], axis=0).astype(BF16))
        _slab_store(zp, pl.ds(n2, nq, stride=plan.zpitch), y)
        return c

    lax.fori_loop(0, r, stage3, 0, unroll=plan.unroll)
    for n1 in range(nq):
        o_ref[n1 * r:(n1 + 1) * r, :] = _slab_load(
            zp, slice(n1 * plan.zpitch, n1 * plan.zpitch + r)).astype(o_ref.dtype)


def _hy_filter_kernel(feat_ref, w1_ref, b1_ref, f1_ref, w2_ref, b2_ref, f2_ref, w3_ref, dl_ref, o_ref):
    feats = feat_ref[...]
    half = feats.shape[0] // 2
    cols = o_ref.shape[1]
    both = jnp.concatenate([feats[:half], feats[half:]], axis=1)
    hid = jnp.sin(f1_ref[...] * (_dot(both, w1_ref[...], HIGHEST) + b1_ref[...]))
    hid = jnp.sin(f2_ref[...] * (_dot(hid, w2_ref[...], HIGHEST) + b2_ref[...]))
    k = _dot(hid, w3_ref[...], HIGHEST)
    o_ref[:half, :] = k[:, :cols] * jnp.exp(-(feats[:half, 0:1] * dl_ref[...]))
    o_ref[half:, :] = k[:, cols:] * jnp.exp(-(feats[half:, 0:1] * dl_ref[...]))


def _hyena_features(seq):
    pos = np.arange(seq, dtype=np.float64)
    t = pos / max(seq - 1, 1)
    w = (2.0 * math.pi / seq) * pos
    bands = np.linspace(1e-4, HY_BANDS - 1, HY_BANDS, dtype=np.float64)
    ang = w[:, None] * bands
    feats = np.concatenate([t[:, None], np.cos(ang), -np.sin(ang)], axis=-1).astype(np.float32)
    out = np.zeros((seq, LANES), np.float32)
    out[:, :feats.shape[1]] = feats
    return out


def _hyena_deltas():
    max_decay = math.log(HY_DECAY_TARGET) / HY_FAST_DECAY
    min_decay = math.log(HY_DECAY_TARGET) / HY_SLOW_DECAY
    deltas = np.abs(np.linspace(min_decay, max_decay, HY_WIDTH, dtype=np.float64))
    return np.tile(deltas, 2)[None, :].astype(np.float32)


def _pad_to(x, rows, cols):
    x = x.astype(F32)
    return jnp.pad(x, ((0, 0), (0, rows - x.shape[1]), (0, cols - x.shape[2])))


def _block_diag2(a):
    a = a.astype(F32)
    z = jnp.zeros_like(a)
    return jnp.concatenate([jnp.concatenate([a, z], axis=2), jnp.concatenate([z, a], axis=2)], axis=1)


def _hyena_filter_weights(p):
    hidden = p["hy_filt_w1"].shape[2]
    assert 2 * hidden == LANES
    vec = lambda a: jnp.tile(a.astype(F32)[:, None, :], (1, 1, 2))
    return (_block_diag2(_pad_to(p["hy_filt_w1"], LANES, hidden)), vec(p["hy_filt_b1"]),
            vec(p["hy_filt_freq1"]), _block_diag2(p["hy_filt_w2"]), vec(p["hy_filt_b2"]),
            vec(p["hy_filt_freq2"]), _block_diag2(p["hy_filt_w3"]))


def _once(a):
    return pl.BlockSpec(a.shape, lambda *_: (0,) * a.ndim, pipeline_mode=pl.Buffered(1))


def _hyena_scratch(plan, c):
    nh = c // LANES
    return [pltpu.VMEM((nh, plan.nq * plan.zpitch, LANES), F32),
            pltpu.VMEM((nh, plan.r * plan.apitch, LANES), F32),
            pltpu.VMEM((nh, plan.kp * plan.cpitch, LANES), F32)]


def _hyena_spectra(seq, fw, plan, tf):
    depth = fw[0].shape[0]
    feats = jnp.asarray(_hyena_features(seq))
    deltas = jnp.asarray(_hyena_deltas())
    per_layer = lambda a: pl.BlockSpec((None,) + a.shape[1:], lambda l, i: (l,) + (0,) * (a.ndim - 1))
    filt = pl.pallas_call(
        _hy_filter_kernel,
        grid=(depth, seq // tf),
        in_specs=[pl.BlockSpec((tf, LANES), lambda l, i: (i, 0))] + [per_layer(a) for a in fw]
                 + [_full(deltas.shape)],
        out_specs=pl.BlockSpec((None, tf, 2 * HY_WIDTH), lambda l, i: (l, i, 0)),
        out_shape=jax.ShapeDtypeStruct((depth, seq, 2 * HY_WIDTH), F32),
        compiler_params=_params(("parallel", "parallel")),
        name="hy_filter",
    )(feats, *fw, deltas)
    f1 = jnp.asarray(plan.f1, F32)
    m2 = jnp.asarray(plan.m2, F32)
    c = HY_WIDTH
    return pl.pallas_call(
        functools.partial(_hy_spec_kernel, plan=plan),
        grid=(depth,),
        in_specs=[pl.BlockSpec((None, seq, c), lambda l: (l, 0, 0)),
                  pl.BlockSpec((None, seq, c), lambda l: (l, 0, 1)), _once(f1), _once(m2)],
        out_specs=pl.BlockSpec((None, plan.k1n, 2 * plan.r, c), lambda l: (l, 0, 0, 0)),
        out_shape=jax.ShapeDtypeStruct((depth, plan.k1n, 2 * plan.r, c), F32),
        scratch_shapes=_hyena_scratch(plan, c)[:2],
        compiler_params=_params(("arbitrary",)),
        name="hy_spec",
    )(filt, filt, f1, m2)


def _hyena_conv(z3, spec, l, plan):
    nb, seq, c = z3.shape
    f1 = jnp.asarray(plan.f1, F32)
    m2 = jnp.asarray(plan.m2, F32)
    m2i = jnp.asarray(plan.m2i, F32)
    g1 = jnp.asarray(plan.g1, F32)
    return pl.pallas_call(
        functools.partial(_hy_conv_kernel, plan=plan),
        grid=(nb,),
        in_specs=[pl.BlockSpec((None, seq, c), lambda b: (b, 0, 0)),
                  _resident(spec, l), _once(f1), _once(m2), _once(m2i), _once(g1)],
        out_specs=pl.BlockSpec((None, seq, c), lambda b: (b, 0, 0)),
        out_shape=jax.ShapeDtypeStruct((nb, seq, c), STREAM),
        scratch_shapes=_hyena_scratch(plan, c),
        compiler_params=_params(("parallel",)),
        name="hy_conv",
    )(z3, spec, f1, m2, m2i, g1)


def _mix_kernel(h_ref, hf_ref, hb_ref, ga_ref, yf_ref, yb_ref, ub_ref, yc_ref, z_ref, x0_ref,
                d_ref, gw_ref, gb_ref, hbias_ref, mg_ref, wo_ref, o_ref):
    f32 = lambda ref: ref[...].astype(F32)
    ya = (f32(hf_ref) + f32(hb_ref)) * _gelu(f32(ga_ref))
    yb = _gelu(f32(ub_ref) * d_ref[...] + f32(yf_ref) + f32(yb_ref))
    yb = yb * _sigmoid(_dot(yb.astype(BF16), gw_ref[...].astype(BF16)) + gb_ref[...])
    yc = (f32(yc_ref) + f32(z_ref) * hbias_ref[...]) * f32(x0_ref)
    a1, a2 = RG_WIDTH, RG_WIDTH + S5_WIDTH
    na = (_rms_nogain(ya) * mg_ref[:, :a1]).astype(BF16)
    nb = (_rms_nogain(yb) * mg_ref[:, a1:a2]).astype(BF16)
    nc = (_rms_nogain(yc) * mg_ref[:, a2:]).astype(BF16)
    wo = lambda lo, hi: wo_ref[lo:hi, :].astype(BF16)
    out = _dot(na, wo(0, a1)) + _dot(nb, wo(a1, a2)) + _dot(nc, wo(a2, wo_ref.shape[0]))
    o_ref[...] = h_ref[...] + out


def _mix(rows, consts, l, tm):
    nb, seq, dm = rows[0].shape
    return pl.pallas_call(
        _mix_kernel,
        grid=(nb * seq // tm,),
        in_specs=[_row_spec(seq, tm, a.shape[2]) for a in rows] + [_layer_spec(a, l) for a in consts[:-1]]
                 + [_resident(consts[-1], l)],
        out_specs=_row_spec(seq, tm, dm),
        out_shape=jax.ShapeDtypeStruct((nb, seq, dm), F32),
        compiler_params=_params(("parallel",)),
        name="mix",
    )(*rows, *consts)


FFN_TILE = 256


def _ffn_kernel(h_ref, hp_ref, hn_ref, g_ref, wu_ref, cw_ref, cb_ref, wd_ref, fg_ref, o_ref,
                x_scr, u_scr, gated_scr, y_scr, *, tiles_per_seq, tm, final):
    i = pl.program_id(0)
    first = (i % tiles_per_seq) == 0
    last = (i % tiles_per_seq) == tiles_per_seq - 1
    nslab = h_ref.shape[1] // LANES
    ph = tm // SUBLANES
    _fill_normed_phase_major(x_scr, h_ref[...], hp_ref[...], hn_ref[...], g_ref[...], first, last, tm)
    x = jnp.concatenate([x_scr[c] for c in range(nslab)], axis=1).astype(BF16)
    for k in range(D_FF // FFN_TILE):
        halves = []
        for part in range(2):
            lo = part * D_FF + k * FFN_TILE
            slot = 2 * k + part
            u_scr[slot] = _dot(x, wu_ref[:, lo:lo + FFN_TILE])
            halves.append(_dwconv(u_scr.at[slot], slice(0, FFN_TILE), cw_ref, cb_ref,
                                  slice(lo, lo + FFN_TILE), tm))
        gated_scr[:, k * FFN_TILE:(k + 1) * FFN_TILE] = (_gelu(halves[0]) * halves[1]).astype(BF16)
    y = _dot(gated_scr[...], wd_ref[...])
    for c in range(nslab):
        y_scr[c] = y[:, c * LANES:(c + 1) * LANES]
    for s in range(SUBLANES):
        for c in range(nslab):
            rows = slice(s * ph, (s + 1) * ph)
            lanes = slice(c * LANES, (c + 1) * LANES)
            o_ref[rows, lanes] = h_ref[rows, lanes] + y_scr[c, pl.ds(s, ph, stride=SUBLANES), :]
    if final:
        out = o_ref[...]
        o_ref[...] = out * lax.rsqrt(jnp.mean(out * out, axis=-1, keepdims=True) + RMS_EPS) * fg_ref[...]


def _resident(arr, *lead):
    rest = arr.shape[len(lead):]
    zeros = (0,) * len(rest)
    return pl.BlockSpec((None,) * len(lead) + rest, lambda *_: tuple(lead) + zeros,
                        pipeline_mode=pl.Buffered(1))


def _ffn(h, g, w_up, conv_w, conv_b, w_down, final_g, l, tm, final):
    nb, seq, d = h.shape
    prev, nxt = _halo_specs(seq, tm, d)
    return pl.pallas_call(
        functools.partial(_ffn_kernel, tiles_per_seq=seq // tm, tm=tm, final=final),
        grid=(nb * seq // tm,),
        in_specs=[_row_spec(seq, tm, d), prev, nxt,
                  _layer_spec(g, l), _resident(w_up, l), _layer_spec(conv_w, l), _layer_spec(conv_b, l),
                  _resident(w_down, l), _full(final_g.shape)],
        out_specs=_row_spec(seq, tm, d),
        out_shape=jax.ShapeDtypeStruct((nb, seq, d), F32),
        scratch_shapes=[pltpu.VMEM((d // LANES, tm + 2 * HALO, LANES), F32),
                        pltpu.VMEM((2 * (D_FF // FFN_TILE), tm + 2 * HALO, FFN_TILE), F32),
                        pltpu.VMEM((tm, D_FF), BF16),
                        pltpu.VMEM((d // LANES, tm, LANES), F32)],
        compiler_params=_params(("parallel",)),
        name="ffn_final" if final else "ffn",
    )(h, h, h, g, w_up, conv_w, conv_b, w_down, final_g)


def _row_vec(a):
    return a.astype(F32)[:, None, :]


def _trunk(x, p, *, tm, tm_wide, t_scan, tf):
    seq = x.shape[1]
    depth = p["w_in"].shape[0]
    plan = _FftPlan(seq)

    proj_c = (_row_vec(p["norm1_g"]), p["w_in"].astype(F32), p["rg_conv_w"].astype(F32),
              _row_vec(p["rg_conv_b"]), p["hy_conv_w"].astype(F32), _row_vec(p["hy_conv_b"]))
    rg_w, rg_bias, rg_sp = _rglru_weights(p)
    s5_w = _s5_weights(p)
    hy_spec = _hyena_spectra(seq, _hyena_filter_weights(p), plan, tf)
    mix_c = (_row_vec(p["s5_d"]), p["s5_glu_w"].astype(F32), _row_vec(p["s5_glu_b"]),
             _row_vec(p["hy_bias"]), _row_vec(p["mix_norm_g"]), p["w_out"].astype(F32))
    ffn_c = (_row_vec(p["norm2_g"]), p["w_up"].astype(BF16), p["ffn_conv_w"].astype(F32),
             _row_vec(p["ffn_conv_b"]), p["w_down"].astype(BF16), p["final_norm_g"].astype(F32)[None])

    h = x.astype(F32)
    for l in range(depth):
        urg, ga, ub, x0, z = _proj(h, *proj_c, l, tm_wide)
        hf, hb, yf, yb = _scans(urg, ub, rg_w, rg_bias, rg_sp, *s5_w, l, t_scan)
        yc = _hyena_conv(z, hy_spec, l, plan)
        h = _mix([h, hf, hb, ga, yf, yb, ub, yc, z, x0], mix_c, l, tm_wide)
        h = _ffn(h, *ffn_c, l, tm, final=(l == depth - 1))
    return h.astype(x.dtype)


def kernel(x, norm1_g, w_in, rg_conv_w, rg_conv_b, rg_wa, rg_ba, rg_wx, rg_bx, rg_lambda, s5_a_re, s5_a_im, s5_log_dt, s5_b_re, s5_b_im, s5_c_re, s5_c_im, s5_d, s5_glu_w, s5_glu_b, hy_conv_w, hy_conv_b, hy_filt_w1, hy_filt_b1, hy_filt_freq1, hy_filt_w2, hy_filt_b2, hy_filt_freq2, hy_filt_w3, hy_bias, mix_norm_g, w_out, norm2_g, w_up, ffn_conv_w, ffn_conv_b, w_down, final_norm_g):
    p = dict(norm1_g=norm1_g, w_in=w_in, rg_conv_w=rg_conv_w, rg_conv_b=rg_conv_b, rg_wa=rg_wa, rg_ba=rg_ba,
             rg_wx=rg_wx, rg_bx=rg_bx, rg_lambda=rg_lambda, s5_a_re=s5_a_re, s5_a_im=s5_a_im,
             s5_log_dt=s5_log_dt, s5_b_re=s5_b_re, s5_b_im=s5_b_im, s5_c_re=s5_c_re, s5_c_im=s5_c_im,
             s5_d=s5_d, s5_glu_w=s5_glu_w, s5_glu_b=s5_glu_b, hy_conv_w=hy_conv_w, hy_conv_b=hy_conv_b,
             hy_filt_w1=hy_filt_w1, hy_filt_b1=hy_filt_b1, hy_filt_freq1=hy_filt_freq1, hy_filt_w2=hy_filt_w2,
             hy_filt_b2=hy_filt_b2, hy_filt_freq2=hy_filt_freq2, hy_filt_w3=hy_filt_w3, hy_bias=hy_bias,
             mix_norm_g=mix_norm_g, w_out=w_out, norm2_g=norm2_g, w_up=w_up, ffn_conv_w=ffn_conv_w,
             ffn_conv_b=ffn_conv_b, w_down=w_down, final_norm_g=final_norm_g)
    return _trunk(x, p, tm=512, tm_wide=1024, t_scan=128, tf=512)
```

```python
import functools
import math

import numpy as np
import jax
import jax.numpy as jnp
from jax import lax
from jax.experimental import pallas as pl
from jax.experimental.pallas import tpu as pltpu

F32 = jnp.float32
BF16 = jnp.bfloat16
STREAM = jnp.bfloat16

RMS_EPS = 1e-6
RG_WIDTH = 384
RG_HEADS = 6
RG_C = 8.0
S5_WIDTH = 384
S5_GROUP = 16
S5_GROUPS = 24
S5_STATE = 64
S5_NSTATE = S5_GROUPS * S5_STATE
HY_WIDTH = 256
HY_BANDS = 16
HY_FAST_DECAY = 0.3
HY_SLOW_DECAY = 1.5
HY_DECAY_TARGET = 1e-2
D_FF = 2816
N_DIR = 2

LANES = 128
SUBLANES = 8
HALO = SUBLANES
VMEM_LIMIT = 56 * 1024 * 1024

HIGHEST = lax.Precision.HIGHEST


def _dot(a, b, precision=None):
    return jnp.dot(a, b, preferred_element_type=F32, precision=precision)


def _gelu(x):
    c = math.sqrt(2.0 / math.pi)
    return 0.5 * x * (1.0 + jnp.tanh(c * (x + 0.044715 * (x * x * x))))


def _sigmoid(x):
    return 0.5 * jnp.tanh(0.5 * x) + 0.5


def _rms_nogain(x):
    return x * lax.rsqrt(jnp.mean(x * x, axis=-1, keepdims=True) + RMS_EPS)


def _params(sem):
    return pltpu.CompilerParams(dimension_semantics=sem, vmem_limit_bytes=VMEM_LIMIT)


def _full(shape):
    nd = len(shape)
    return pl.BlockSpec(shape, lambda *_: (0,) * nd)


def _layer_spec(arr, *lead):
    rest = arr.shape[len(lead):]
    zeros = (0,) * len(rest)
    return pl.BlockSpec((None,) * len(lead) + rest, lambda *_: tuple(lead) + zeros)


def _row_spec(seq, tm, c):
    tps = seq // tm
    return pl.BlockSpec((None, tm, c), lambda i: (i // tps, i % tps, 0))


def _halo_specs(seq, tm, c):
    tps = seq // tm
    bpt = tm // HALO
    prev = pl.BlockSpec((None, HALO, c), lambda i: (i // tps, jnp.maximum((i % tps) * bpt - 1, 0), 0))
    nxt = pl.BlockSpec((None, HALO, c),
                       lambda i: (i // tps, jnp.minimum((i % tps + 1) * bpt, seq // HALO - 1), 0))
    return prev, nxt


def _fill_normed_phase_major(x_scr, h, hp, hn, g, first, last, tm):
    ph = tm // SUBLANES

    def norm(v):
        return v * lax.rsqrt(jnp.mean(v * v, axis=-1, keepdims=True) + RMS_EPS) * g

    xn = norm(h)
    xp = norm(jnp.where(first, 0.0, hp))
    xq = norm(jnp.where(last, 0.0, hn))
    for c in range(x_scr.shape[0]):
        lanes = slice(c * LANES, (c + 1) * LANES)
        for s in range(SUBLANES):
            x_scr[c, pl.ds(s, ph, stride=SUBLANES), :] = xn[s * ph:(s + 1) * ph, lanes]
        x_scr[c, tm:tm + HALO, :] = xp[:, lanes]
        x_scr[c, tm + HALO:tm + 2 * HALO, :] = xq[:, lanes]


def _time_shifted(u_ref, cols, d, tm):
    width = cols.stop - cols.start
    sub = lax.broadcasted_iota(jnp.int32, (SUBLANES, width), 0)
    tile = lambda j: u_ref[j * SUBLANES:(j + 1) * SUBLANES, cols]
    prev_halo = u_ref[tm:tm + HALO, cols]
    next_halo = u_ref[tm + HALO:tm + 2 * HALO, cols]
    ph = tm // SUBLANES
    if d == 0:
        return u_ref[0:tm, cols]
    if d < 0:
        assert d == -1
        head = jnp.where(sub == 0, pltpu.roll(prev_halo, 1, 0), pltpu.roll(tile(ph - 1), 1, 0))
        return jnp.concatenate([head, u_ref[0:tm - SUBLANES, cols]], axis=0)
    tails = [jnp.where(sub == SUBLANES - 1, pltpu.roll(next_halo, SUBLANES - 1 - jj, 0),
                       pltpu.roll(tile(jj), SUBLANES - 1, 0)) for jj in range(d)]
    return jnp.concatenate([u_ref[d * SUBLANES:tm, cols]] + tails, axis=0)


def _dwconv(u_ref, cols, w_ref, b_ref, wcols, tm):
    taps = w_ref.shape[0]
    left = (taps - 1) // 2
    acc = b_ref[:, wcols]
    for j in range(taps):
        acc = acc + _time_shifted(u_ref, cols, j - left, tm) * w_ref[j:j + 1, wcols]
    return acc


def _proj_kernel(h_ref, hp_ref, hn_ref, g_ref, w_ref, rcw_ref, rcb_ref, hcw_ref, hcb_ref,
                 urg_ref, ga_ref, ub_ref, x0_ref, z_ref, x_scr, u_scr, w_scr, *, tiles_per_seq, tm):
    i = pl.program_id(0)
    first = (i % tiles_per_seq) == 0
    last = (i % tiles_per_seq) == tiles_per_seq - 1
    g = g_ref[...]

    @pl.when(i == 0)
    def _():
        w_scr[...] = w_ref[...].astype(BF16)

    def norm(v):
        return v * lax.rsqrt(jnp.mean(v * v, axis=-1, keepdims=True) + RMS_EPS) * g

    x_scr[0:HALO, :] = norm(jnp.where(first, 0.0, hp_ref[...]))
    x_scr[HALO:HALO + tm, :] = norm(h_ref[...])
    x_scr[HALO + tm:2 * HALO + tm, :] = norm(jnp.where(last, 0.0, hn_ref[...]))
    a0, a1, a2, a3, a4 = 0, RG_WIDTH, 2 * RG_WIDTH, 2 * RG_WIDTH + S5_WIDTH, w_ref.shape[1]
    x = x_scr[...].astype(BF16)
    for lo, hi in ((a0, a1), (a3, a4), (a1, a3)):
        u_scr[:, lo:hi] = _dot(x, w_scr[:, lo:hi])
    acc = rcb_ref[...]
    for j in range(rcw_ref.shape[0]):
        acc = acc + u_scr[pl.ds(HALO + j - 1, tm), a0:a1] * rcw_ref[j:j + 1, :]
    urg_ref[...] = acc
    ga_ref[...] = u_scr[pl.ds(HALO, tm), a1:a2].astype(ga_ref.dtype)
    ub_ref[...] = u_scr[pl.ds(HALO, tm), a2:a3].astype(ub_ref.dtype)
    q = []
    for part in range(3):
        lo = part * HY_WIDTH
        acc = hcb_ref[:, lo:lo + HY_WIDTH]
        for j in range(hcw_ref.shape[0]):
            acc = acc + (u_scr[pl.ds(HALO + j - 1, tm), a3 + lo:a3 + lo + HY_WIDTH]
                         * hcw_ref[j:j + 1, lo:lo + HY_WIDTH])
        q.append(acc)
    x0_ref[...] = q[0].astype(x0_ref.dtype)
    z_ref[...] = (q[2] * q[1]).astype(z_ref.dtype)


def _proj(h, g, w_in, rcw, rcb, hcw, hcb, l, tm):
    nb, seq, d = h.shape
    cols = w_in.shape[-1]
    prev, nxt = _halo_specs(seq, tm, d)
    row = lambda c: _row_spec(seq, tm, c)
    outs = [jax.ShapeDtypeStruct((nb, seq, c), dt) for c, dt in
            ((RG_WIDTH, F32), (RG_WIDTH, STREAM), (S5_WIDTH, STREAM), (HY_WIDTH, STREAM), (HY_WIDTH, STREAM))]
    consts = [g, w_in, rcw, rcb, hcw, hcb]
    return pl.pallas_call(
        functools.partial(_proj_kernel, tiles_per_seq=seq // tm, tm=tm),
        grid=(nb * seq // tm,),
        in_specs=[row(d), prev, nxt] + [_layer_spec(g, l), _resident(w_in, l)]
                 + [_layer_spec(a, l) for a in consts[2:]],
        out_specs=[row(RG_WIDTH), row(RG_WIDTH), row(S5_WIDTH), row(HY_WIDTH), row(HY_WIDTH)],
        out_shape=outs,
        scratch_shapes=[pltpu.VMEM((tm + 2 * HALO, d), F32), pltpu.VMEM((tm + 2 * HALO, cols), F32),
                        pltpu.VMEM((d, cols), BF16)],
        compiler_params=_params(("arbitrary",)),
        name="proj",
    )(h, h, h, *consts)


RG_STAGES = 4


def _lo_mask():
    return lax.broadcasted_iota(jnp.int32, (SUBLANES, LANES), 0) < (SUBLANES // 2)


def _rglru_phases(uf_ref, ub_ref, w_ref, bias_ref, sp_ref, of_ref, ob_ref,
                  af, ab, bf, bb, carry_scr, *, t, nb):
    nm = RG_WIDTH // LANES
    tq = t // RG_STAGES

    def stage(q):
        for dr, (u_ref, a_scr, b_scr) in enumerate(((uf_ref, af, bf), (ub_ref, ab, bb))):
            t0 = q * tq if dr == 0 else t - (q + 1) * tq
            u = u_ref[:, t0:t0 + tq, :].reshape(nb * tq, RG_WIDTH)
            gates = _dot(u.astype(BF16), w_ref[dr]) + bias_ref[dr]
            r = _sigmoid(gates[:, :RG_WIDTH])
            gi = _sigmoid(gates[:, RG_WIDTH:])
            log_a = (-RG_C) * r * sp_ref[dr]
            a = jnp.exp(log_a)
            m2 = -jnp.tanh(log_a) * (a * a + 1.0)
            mult = jnp.where(m2 > 0.0, m2 * lax.rsqrt(m2), 0.0)
            bin_ = mult * (gi * u)
            for b in range(nb):
                for m in range(nm):
                    rows = pl.ds(t0 * SUBLANES + dr * nb + b, tq, stride=SUBLANES)
                    a_scr[m, rows, :] = a[b * tq:(b + 1) * tq, m * LANES:(m + 1) * LANES]
                    b_scr[m, rows, :] = bin_[b * tq:(b + 1) * tq, m * LANES:(m + 1) * LANES]

    def scan():
        lo = _lo_mask()
        carry = [carry_scr[m] for m in range(nm)]
        for j in range(t):
            row = j * SUBLANES
            mrow = (t - 1 - j) * SUBLANES
            for m in range(nm):
                at = jnp.where(lo, af[m, row:row + SUBLANES, :], ab[m, mrow:mrow + SUBLANES, :])
                bt = jnp.where(lo, bf[m, row:row + SUBLANES, :], bb[m, mrow:mrow + SUBLANES, :])
                h = at * carry[m] + bt
                bf[m, row:row + SUBLANES, :] = h
                bb[m, mrow:mrow + SUBLANES, :] = h
                carry[m] = h
        for m in range(nm):
            carry_scr[m] = carry[m]
        for b in range(nb):
            for m in range(nm):
                of_ref[b, :, m * LANES:(m + 1) * LANES] = bf[m, pl.ds(b, t, stride=SUBLANES), :].astype(STREAM)
                ob_ref[b, :, m * LANES:(m + 1) * LANES] = (
                    bb[m, pl.ds(nb + b, t, stride=SUBLANES), :].astype(STREAM))

    return [functools.partial(stage, q) for q in range(RG_STAGES)] + [scan]


def _rglru_weights(p):
    hd = RG_WIDTH // RG_HEADS
    place = np.zeros((2, RG_HEADS, hd, 2 * RG_WIDTH), np.float32)
    for q in range(2):
        for h in range(RG_HEADS):
            place[q, h, np.arange(hd), q * RG_WIDTH + h * hd + np.arange(hd)] = 1.0
    both = jnp.stack([p["rg_wa"], p["rg_wx"]], axis=2).astype(F32)
    depth = both.shape[0]
    w = jnp.einsum('ldqhij,qhjc->ldhic', both, jnp.asarray(place))
    w = w.reshape(depth, N_DIR, RG_WIDTH, 2 * RG_WIDTH).astype(BF16)
    bias = jnp.concatenate([p["rg_ba"], p["rg_bx"]], axis=-1).astype(F32)[:, :, None, :]
    x = -p["rg_lambda"].astype(F32)
    sp = (jnp.maximum(x, 0.0) + jnp.log1p(jnp.exp(-jnp.abs(x))))[:, :, None, :]
    return w, bias, sp


S5_NBLK = S5_NSTATE // LANES
S5_GRP = 4
S5_PER = S5_NBLK // (S5_WIDTH // LANES)


def _reverse_tiles(src, dst, nm, ntile):
    for j in range(ntile):
        s = (ntile - 1 - j) * SUBLANES
        for m in range(nm):
            dst[m, j * SUBLANES:(j + 1) * SUBLANES, :] = src[m, s:s + SUBLANES, :]


def _s5_phases(uf_ref, ub_ref, wb_ref, wc_ref, ar_ref, ai_ref, yf_ref, yb_ref,
               uf8, ub8, ubr, bu, ym, ymr, carry_scr, *, t, nb):
    nm = S5_WIDTH // LANES
    assert S5_GRP == S5_PER

    def load():
        for b in range(nb):
            for m in range(nm):
                lanes = slice(m * LANES, (m + 1) * LANES)
                uf8[m, pl.ds(b, t, stride=SUBLANES), :] = uf_ref[b, :, lanes].astype(F32)
                ub8[m, pl.ds(nb + b, t, stride=SUBLANES), :] = ub_ref[b, :, lanes].astype(F32)
        _reverse_tiles(ub8, ubr, nm, t)

    def bproj(m):
        lhs = jnp.concatenate([uf8[m], ubr[m]], axis=1).astype(BF16)
        res = _dot(lhs, wb_ref[m])
        for q in range(2 * S5_PER):
            bu[2 * S5_PER * m + q] = res[:, q * LANES:(q + 1) * LANES]

    def scan(g):
        blocks = list(range(g * S5_GRP, (g + 1) * S5_GRP))
        ars = [ar_ref[n] for n in blocks]
        ais = [ai_ref[n] for n in blocks]
        carry = []
        for n in blocks:
            carry += [carry_scr[2 * n], carry_scr[2 * n + 1]]
        for j in range(t):
            rows = slice(j * SUBLANES, (j + 1) * SUBLANES)
            for k, n in enumerate(blocks):
                hr, hi = carry[2 * k], carry[2 * k + 1]
                nr = ars[k] * hr - ais[k] * hi + bu[2 * n, rows, :]
                ni = ars[k] * hi + ais[k] * hr + bu[2 * n + 1, rows, :]
                bu[2 * n, rows, :] = nr
                bu[2 * n + 1, rows, :] = ni
                carry[2 * k], carry[2 * k + 1] = nr, ni
        for k, n in enumerate(blocks):
            carry_scr[2 * n] = carry[2 * k]
            carry_scr[2 * n + 1] = carry[2 * k + 1]

    def cproj(m):
        rows = lax.broadcasted_iota(jnp.int32, (SUBLANES * t, LANES), 0)
        fwd_row = (rows & (SUBLANES - 1)) < nb
        hcat = jnp.concatenate([bu[2 * S5_PER * m + q] for q in range(2 * S5_PER)], axis=1).astype(BF16)
        acc = _dot(hcat, wc_ref[m])
        ym[m] = jnp.where(fwd_row, acc[:, :LANES], acc[:, LANES:])

    def store():
        _reverse_tiles(ym, ymr, nm, t)
        for b in range(nb):
            for m in range(nm):
                lanes = slice(m * LANES, (m + 1) * LANES)
                yf_ref[b, :, lanes] = ym[m, pl.ds(b, t, stride=SUBLANES), :].astype(STREAM)
                yb_ref[b, :, lanes] = ymr[m, pl.ds(nb + b, t, stride=SUBLANES), :].astype(STREAM)

    part = functools.partial
    return (load, [part(bproj, m) for m in range(nm)], [part(scan, g) for g in range(nm)],
            [part(cproj, m) for m in range(nm)], store)


def _scans_kernel(*refs, t, nb):
    rg_in, s5_in = refs[0:5], refs[5:11]
    rg_out, s5_out = refs[11:13], refs[13:15]
    rg_scr, s5_scr = refs[15:20], refs[20:27]

    @pl.when(pl.program_id(0) == 0)
    def _():
        for scr in (*rg_scr, s5_scr[0], s5_scr[1], s5_scr[6]):
            scr[...] = jnp.zeros_like(scr)

    rg = _rglru_phases(*rg_in, *rg_out, *rg_scr, t=t, nb=nb)
    load, bproj, scan, cproj, store = _s5_phases(*s5_in, *s5_out, *s5_scr, t=t, nb=nb)
    order = [load, bproj[0], rg[0], bproj[1], rg[1], scan[0], bproj[2], rg[2], scan[1], cproj[0],
             rg[3], scan[2], cproj[1], rg[4], cproj[2], store]
    assert len(rg) == 5 and len(bproj) == 3
    for phase in order:
        phase()


def _scans(urg, ub, rg_w, rg_bias, rg_sp, wb, wc, ar, ai, l, t):
    nb, seq, c = urg.shape
    assert 2 * nb == SUBLANES and ub.shape == urg.shape
    nchunk = seq // t
    fwd = pl.BlockSpec((nb, t, c), lambda i: (0, i, 0))
    bwd = pl.BlockSpec((nb, t, c), lambda i: (0, nchunk - 1 - i, 0))
    nm = c // LANES
    rows = SUBLANES * t
    out = jax.ShapeDtypeStruct((nb, seq, c), STREAM)
    slab = pltpu.VMEM((nm, rows, LANES), F32)
    rg_scr = [slab, slab, slab, slab, pltpu.VMEM((nm, SUBLANES, LANES), F32)]
    s5_scr = [slab, slab, slab, pltpu.VMEM((2 * S5_NBLK, rows, LANES), F32), slab, slab,
              pltpu.VMEM((2 * S5_NBLK, SUBLANES, LANES), F32)]
    rg_c = [rg_w, rg_bias, rg_sp]
    s5_c = [wb, wc, ar, ai]
    return pl.pallas_call(
        functools.partial(_scans_kernel, t=t, nb=nb),
        grid=(nchunk,),
        in_specs=[fwd, bwd] + [_layer_spec(a, l) for a in rg_c]
                 + [fwd, bwd] + [_layer_spec(a, l) for a in s5_c],
        out_specs=[fwd, bwd, fwd, bwd],
        out_shape=[out, out, out, out],
        scratch_shapes=rg_scr + s5_scr,
        compiler_params=_params(("arbitrary",)),
        name="scans",
    )(urg, urg, *rg_c, ub, ub, *s5_c)


def _s5_place():
    slots = LANES // S5_GROUP
    half = LANES // S5_STATE
    place = np.zeros((slots, 2, S5_STATE, S5_PER * 2 * LANES), np.float32)
    for k in range(slots):
        for r in range(2):
            col = (k // half) * 2 * LANES + r * LANES + (k % half) * S5_STATE
            place[k, r, np.arange(S5_STATE), col + np.arange(S5_STATE)] = 1.0
    return place


def _s5_weights(p):
    lr = p["s5_a_re"].astype(F32)
    li = p["s5_a_im"].astype(F32)
    dt = jnp.exp(p["s5_log_dt"].astype(F32))[..., None]
    mag = jnp.exp(lr * dt)
    abar_r = mag * jnp.cos(li * dt)
    abar_i = mag * jnp.sin(li * dt)
    den = lr * lr + li * li
    nr = abar_r - 1.0
    ni = abar_i
    coef_r = ((nr * lr + ni * li) / den)[..., None]
    coef_i = ((ni * lr - nr * li) / den)[..., None]
    b_re = p["s5_b_re"].astype(F32)
    b_im = p["s5_b_im"].astype(F32)
    bbar = jnp.stack([coef_r * b_re - coef_i * b_im, coef_r * b_im + coef_i * b_re], axis=2)
    depth = lr.shape[0]
    nm = S5_WIDTH // LANES
    slots = LANES // S5_GROUP
    place = jnp.asarray(_s5_place())
    bb = bbar.reshape(depth, N_DIR, 2, nm, slots, S5_STATE, S5_GROUP)
    wb = jnp.einsum('ldrmkpc,krpx->lmdkcx', bb, place).reshape(depth, nm, 2 * LANES, S5_PER * 2 * LANES)
    cc = jnp.stack([p["s5_c_re"].astype(F32), -p["s5_c_im"].astype(F32)], axis=2)
    cc = cc.reshape(depth, N_DIR, 2, nm, slots, S5_GROUP, S5_STATE)
    wc = jnp.einsum('ldrmkcp,krpx->lmdkcx', cc, place).reshape(depth, nm, 2 * LANES, S5_PER * 2 * LANES)
    wc = wc.transpose(0, 1, 3, 2)

    def tile_rows(a):
        a = a.reshape(depth, N_DIR, S5_NBLK, LANES).transpose(0, 2, 1, 3)
        return jnp.repeat(a, SUBLANES // N_DIR, axis=2)

    return wb.astype(BF16), wc.astype(BF16), tile_rows(abar_r), tile_rows(abar_i)


def _odd8(n):
    p = -(-n // SUBLANES)
    if p % 2 == 0:
        p += 1
    return p * SUBLANES


class _FftPlan:
    def __init__(self, seq):
        self.seq = seq
        r = 1
        while r * r < seq:
            r *= 2
        self.r = r
        self.nq = seq // r
        self.q = 2 * seq // r
        self.k1n = self.q // 2 + 1
        self.kp = -(-self.k1n // SUBLANES) * SUBLANES
        self.zpitch = _odd8(r)
        self.apitch = _odd8(2 * self.kp)
        self.cpitch = _odd8(2 * r)
        self.unroll = min(64, r)
        self.unroll2 = next(u for u in (65, 13, 5, 4, 3, 2, 1) if self.k1n % u == 0)
        p = 2 * seq
        n1 = np.arange(self.nq)[None, :]
        k1 = np.arange(self.k1n)[:, None]
        ang = 2.0 * np.pi * n1 * k1 / self.q
        f1 = np.zeros((2 * self.kp, self.nq))
        f1[:self.k1n] = np.cos(ang)
        f1[self.kp:self.kp + self.k1n] = -np.sin(ang)
        self.f1 = f1
        w = np.full((self.k1n,), 2.0)
        w[0] = 1.0
        w[-1] = 1.0
        g1 = np.zeros((self.nq, 2 * self.kp))
        g1[:, :self.k1n] = (np.cos(ang) * w[:, None] / p).T
        g1[:, self.kp:self.kp + self.k1n] = (-np.sin(ang) * w[:, None] / p).T
        self.g1 = g1
        kk = np.arange(self.k1n)[:, None, None]
        k2 = np.arange(r)[None, :, None]
        n2 = np.arange(r)[None, None, :]
        ph = 2.0 * np.pi * (n2 * k2 / r + n2 * kk / p)
        tr, ti = np.cos(ph), -np.sin(ph)
        self.m2 = np.concatenate([np.concatenate([tr, -ti], axis=2),
                                  np.concatenate([ti, tr], axis=2)], axis=1)
        ur, ui = np.transpose(tr, (0, 2, 1)), -np.transpose(ti, (0, 2, 1))
        self.m2i = np.concatenate([np.concatenate([ur, -ui], axis=2),
                                   np.concatenate([ui, ur], axis=2)], axis=1)


def _slab_load(scr, rows):
    return jnp.concatenate([scr[k, rows, :] for k in range(scr.shape[0])], axis=1)


def _slab_store(scr, rows, val):
    for k in range(scr.shape[0]):
        scr[k, rows, :] = val[:, k * LANES:(k + 1) * LANES]


def _fft_forward(plan, src_ref, f1_ref, m2_ref, zp, as_, emit):
    r, nq, kp = plan.r, plan.nq, plan.kp
    for n1 in range(nq):
        _slab_store(zp, slice(n1 * plan.zpitch, n1 * plan.zpitch + r), src_ref[n1 * r:(n1 + 1) * r, :].astype(F32))

    def stage1(n2, c):
        slab = _slab_load(zp, pl.ds(n2, nq, stride=plan.zpitch))
        a = _dot(f1_ref[...].astype(BF16), slab.astype(BF16))
        _slab_store(as_, pl.ds(pl.multiple_of(n2 * plan.apitch, SUBLANES), 2 * kp), a)
        return c

    lax.fori_loop(0, r, stage1, 0, unroll=plan.unroll)

    def stage2(k1, c):
        sr = _slab_load(as_, pl.ds(k1, r, stride=plan.apitch))
        si = _slab_load(as_, pl.ds(kp + k1, r, stride=plan.apitch))
        s = jnp.concatenate([sr, si], axis=0).astype(BF16)
        emit(k1, _dot(m2_ref[k1].astype(BF16), s))
        return c

    lax.fori_loop(0, plan.k1n, stage2, 0, unroll=plan.unroll2)


def _hy_spec_kernel(kf_ref, kb_ref, f1_ref, m2_ref, o_ref, zp, as_, *, plan):
    r = plan.r

    def emit_f(k1, x):
        o_ref[k1] = x

    def emit_b(k1, x):
        sign = jnp.where(lax.broadcasted_iota(jnp.int32, x.shape, 0) < r, 1.0, -1.0)
        o_ref[k1] = o_ref[k1] + sign * x

    _fft_forward(plan, kf_ref, f1_ref, m2_ref, zp, as_, emit_f)
    _fft_forward(plan, kb_ref, f1_ref, m2_ref, zp, as_, emit_b)


def _hy_conv_kernel(z_ref, spec_ref, f1_ref, m2_ref, m2i_ref, g1_ref, o_ref, zp, as_, cs, *, plan):
    r, nq, kp, k1n = plan.r, plan.nq, plan.kp, plan.k1n
    if kp > k1n:
        pad = slice(k1n * plan.cpitch, kp * plan.cpitch)
        for k in range(cs.shape[0]):
            cs[k, pad, :] = jnp.zeros(((kp - k1n) * plan.cpitch, LANES), F32)

    def emit(k1, x):
        kf = spec_ref[k1]
        xr, xi = x[:r], x[r:]
        kr, ki = kf[:r], kf[r:]
        prod = jnp.concatenate([xr * kr - xi * ki, xr * ki + xi * kr], axis=0).astype(BF16)
        c = _dot(m2i_ref[k1].astype(BF16), prod)
        _slab_store(cs, pl.ds(pl.multiple_of(k1 * plan.cpitch, SUBLANES), 2 * r), c)

    _fft_forward(plan, z_ref, f1_ref, m2_ref, zp, as_, emit)

    def stage3(n2, c):
        cr = _slab_load(cs, pl.ds(n2, kp, stride=plan.cpitch))
        ci = _slab_load(cs, pl.ds(r + n2, kp, stride=plan.cpitch))
        y = _dot(g1_ref[...].astype(BF16), jnp.concatenate([cr, ci], axis=0).astype(BF16))
        _slab_store(zp, pl.ds(n2, nq, stride=plan.zpitch), y)
        return c

    lax.fori_loop(0, r, stage3, 0, unroll=plan.unroll)
    for n1 in range(nq):
        o_ref[n1 * r:(n1 + 1) * r, :] = _slab_load(
            zp, slice(n1 * plan.zpitch, n1 * plan.zpitch + r)).astype(o_ref.dtype)


def _hy_filter_kernel(feat_ref, w1_ref, b1_ref, f1_ref, w2_ref, b2_ref, f2_ref, w3_ref, dl_ref, o_ref):
    feats = feat_ref[...]
    half = feats.shape[0] // 2
    cols = o_ref.shape[1]
    both = jnp.concatenate([feats[:half], feats[half:]], axis=1)
    hid = jnp.sin(f1_ref[...] * (_dot(both, w1_ref[...], HIGHEST) + b1_ref[...]))
    hid = jnp.sin(f2_ref[...] * (_dot(hid, w2_ref[...], HIGHEST) + b2_ref[...]))
    k = _dot(hid, w3_ref[...], HIGHEST)
    o_ref[:half, :] = k[:, :cols] * jnp.exp(-(feats[:half, 0:1] * dl_ref[...]))
    o_ref[half:, :] = k[:, cols:] * jnp.exp(-(feats[half:, 0:1] * dl_ref[...]))


def _hyena_features(seq):
    pos = np.arange(seq, dtype=np.float64)
    t = pos / max(seq - 1, 1)
    w = (2.0 * math.pi / seq) * pos
    bands = np.linspace(1e-4, HY_BANDS - 1, HY_BANDS, dtype=np.float64)
    ang = w[:, None] * bands
    feats = np.concatenate([t[:, None], np.cos(ang), -np.sin(ang)], axis=-1).astype(np.float32)
    out = np.zeros((seq, LANES), np.float32)
    out[:, :feats.shape[1]] = feats
    return out


def _hyena_deltas():
    max_decay = math.log(HY_DECAY_TARGET) / HY_FAST_DECAY
    min_decay = math.log(HY_DECAY_TARGET) / HY_SLOW_DECAY
    deltas = np.abs(np.linspace(min_decay, max_decay, HY_WIDTH, dtype=np.float64))
    return np.tile(deltas, 2)[None, :].astype(np.float32)


def _pad_to(x, rows, cols):
    x = x.astype(F32)
    return jnp.pad(x, ((0, 0), (0, rows - x.shape[1]), (0, cols - x.shape[2])))


def _block_diag2(a):
    a = a.astype(F32)
    z = jnp.zeros_like(a)
    return jnp.concatenate([jnp.concatenate([a, z], axis=2), jnp.concatenate([z, a], axis=2)], axis=1)


def _hyena_filter_weights(p):
    hidden = p["hy_filt_w1"].shape[2]
    assert 2 * hidden == LANES
    vec = lambda a: jnp.tile(a.astype(F32)[:, None, :], (1, 1, 2))
    return (_block_diag2(_pad_to(p["hy_filt_w1"], LANES, hidden)), vec(p["hy_filt_b1"]),
            vec(p["hy_filt_freq1"]), _block_diag2(p["hy_filt_w2"]), vec(p["hy_filt_b2"]),
            vec(p["hy_filt_freq2"]), _block_diag2(p["hy_filt_w3"]))


def _once(a):
    return pl.BlockSpec(a.shape, lambda *_: (0,) * a.ndim, pipeline_mode=pl.Buffered(1))


def _hyena_scratch(plan, c):
    nh = c // LANES
    return [pltpu.VMEM((nh, plan.nq * plan.zpitch, LANES), F32),
            pltpu.VMEM((nh, plan.r * plan.apitch, LANES), F32),
            pltpu.VMEM((nh, plan.kp * plan.cpitch, LANES), F32)]


def _hyena_spectra(seq, fw, plan, tf):
    depth = fw[0].shape[0]
    feats = jnp.asarray(_hyena_features(seq))
    deltas = jnp.asarray(_hyena_deltas())
    per_layer = lambda a: pl.BlockSpec((None,) + a.shape[1:], lambda l, i: (l,) + (0,) * (a.ndim - 1))
    filt = pl.pallas_call(
        _hy_filter_kernel,
        grid=(depth, seq // tf),
        in_specs=[pl.BlockSpec((tf, LANES), lambda l, i: (i, 0))] + [per_layer(a) for a in fw]
                 + [_full(deltas.shape)],
        out_specs=pl.BlockSpec((None, tf, 2 * HY_WIDTH), lambda l, i: (l, i, 0)),
        out_shape=jax.ShapeDtypeStruct((depth, seq, 2 * HY_WIDTH), F32),
        compiler_params=_params(("parallel", "parallel")),
        name="hy_filter",
    )(feats, *fw, deltas)
    f1 = jnp.asarray(plan.f1, F32)
    m2 = jnp.asarray(plan.m2, F32)
    c = HY_WIDTH
    return pl.pallas_call(
        functools.partial(_hy_spec_kernel, plan=plan),
        grid=(depth,),
        in_specs=[pl.BlockSpec((None, seq, c), lambda l: (l, 0, 0)),
                  pl.BlockSpec((None, seq, c), lambda l: (l, 0, 1)), _once(f1), _once(m2)],
        out_specs=pl.BlockSpec((None, plan.k1n, 2 * plan.r, c), lambda l: (l, 0, 0, 0)),
        out_shape=jax.ShapeDtypeStruct((depth, plan.k1n, 2 * plan.r, c), F32),
        scratch_shapes=_hyena_scratch(plan, c)[:2],
        compiler_params=_params(("arbitrary",)),
        name="hy_spec",
    )(filt, filt, f1, m2)


def _hyena_conv(z3, spec, l, plan):
    nb, seq, c = z3.shape
    f1 = jnp.asarray(plan.f1, F32)
    m2 = jnp.asarray(plan.m2, F32)
    m2i = jnp.asarray(plan.m2i, F32)
    g1 = jnp.asarray(plan.g1, F32)
    return pl.pallas_call(
        functools.partial(_hy_conv_kernel, plan=plan),
        grid=(nb,),
        in_specs=[pl.BlockSpec((None, seq, c), lambda b: (b, 0, 0)),
                  _resident(spec, l), _once(f1), _once(m2), _once(m2i), _once(g1)],
        out_specs=pl.BlockSpec((None, seq, c), lambda b: (b, 0, 0)),
        out_shape=jax.ShapeDtypeStruct((nb, seq, c), STREAM),
        scratch_shapes=_hyena_scratch(plan, c),
        compiler_params=_params(("parallel",)),
        name="hy_conv",
    )(z3, spec, f1, m2, m2i, g1)


def _mix_kernel(h_ref, hf_ref, hb_ref, ga_ref, yf_ref, yb_ref, ub_ref, yc_ref, z_ref, x0_ref,
                d_ref, gw_ref, gb_ref, hbias_ref, mg_ref, wo_ref, o_ref):
    f32 = lambda ref: ref[...].astype(F32)
    ya = (f32(hf_ref) + f32(hb_ref)) * _gelu(f32(ga_ref))
    yb = _gelu(f32(ub_ref) * d_ref[...] + f32(yf_ref) + f32(yb_ref))
    yb = yb * _sigmoid(_dot(yb.astype(BF16), gw_ref[...].astype(BF16)) + gb_ref[...])
    yc = (f32(yc_ref) + f32(z_ref) * hbias_ref[...]) * f32(x0_ref)
    a1, a2 = RG_WIDTH, RG_WIDTH + S5_WIDTH
    na = (_rms_nogain(ya) * mg_ref[:, :a1]).astype(BF16)
    nb = (_rms_nogain(yb) * mg_ref[:, a1:a2]).astype(BF16)
    nc = (_rms_nogain(yc) * mg_ref[:, a2:]).astype(BF16)
    wo = lambda lo, hi: wo_ref[lo:hi, :].astype(BF16)
    out = _dot(na, wo(0, a1)) + _dot(nb, wo(a1, a2)) + _dot(nc, wo(a2, wo_ref.shape[0]))
    o_ref[...] = h_ref[...] + out


def _mix(rows, consts, l, tm):
    nb, seq, dm = rows[0].shape
    return pl.pallas_call(
        _mix_kernel,
        grid=(nb * seq // tm,),
        in_specs=[_row_spec(seq, tm, a.shape[2]) for a in rows] + [_layer_spec(a, l) for a in consts[:-1]]
                 + [_resident(consts[-1], l)],
        out_specs=_row_spec(seq, tm, dm),
        out_shape=jax.ShapeDtypeStruct((nb, seq, dm), F32),
        compiler_params=_params(("parallel",)),
        name="mix",
    )(*rows, *consts)


FFN_TILE = 256


def _ffn_kernel(h_ref, hp_ref, hn_ref, g_ref, wu_ref, cw_ref, cb_ref, wd_ref, fg_ref, o_ref,
                x_scr, u_scr, gated_scr, y_scr, *, tiles_per_seq, tm, final):
    i = pl.program_id(0)
    first = (i % tiles_per_seq) == 0
    last = (i % tiles_per_seq) == tiles_per_seq - 1
    nslab = h_ref.shape[1] // LANES
    ph = tm // SUBLANES
    _fill_normed_phase_major(x_scr, h_ref[...], hp_ref[...], hn_ref[...], g_ref[...], first, last, tm)
    x = jnp.concatenate([x_scr[c] for c in range(nslab)], axis=1).astype(BF16)
    for k in range(D_FF // FFN_TILE):
        halves = []
        for part in range(2):
            lo = part * D_FF + k * FFN_TILE
            slot = 2 * k + part
            u_scr[slot] = _dot(x, wu_ref[:, lo:lo + FFN_TILE])
            halves.append(_dwconv(u_scr.at[slot], slice(0, FFN_TILE), cw_ref, cb_ref,
                                  slice(lo, lo + FFN_TILE), tm))
        gated_scr[:, k * FFN_TILE:(k + 1) * FFN_TILE] = (_gelu(halves[0]) * halves[1]).astype(BF16)
    y = _dot(gated_scr[...], wd_ref[...])
    for c in range(nslab):
        y_scr[c] = y[:, c * LANES:(c + 1) * LANES]
    for s in range(SUBLANES):
        for c in range(nslab):
            rows = slice(s * ph, (s + 1) * ph)
            lanes = slice(c * LANES, (c + 1) * LANES)
            o_ref[rows, lanes] = h_ref[rows, lanes] + y_scr[c, pl.ds(s, ph, stride=SUBLANES), :]
    if final:
        out = o_ref[...]
        o_ref[...] = out * lax.rsqrt(jnp.mean(out * out, axis=-1, keepdims=True) + RMS_EPS) * fg_ref[...]


def _resident(arr, *lead):
    rest = arr.shape[len(lead):]
    zeros = (0,) * len(rest)
    return pl.BlockSpec((None,) * len(lead) + rest, lambda *_: tuple(lead) + zeros,
                        pipeline_mode=pl.Buffered(1))


def _ffn(h, g, w_up, conv_w, conv_b, w_down, final_g, l, tm, final):
    nb, seq, d = h.shape
    prev, nxt = _halo_specs(seq, tm, d)
    return pl.pallas_call(
        functools.partial(_ffn_kernel, tiles_per_seq=seq // tm, tm=tm, final=final),
        grid=(nb * seq // tm,),
        in_specs=[_row_spec(seq, tm, d), prev, nxt,
                  _layer_spec(g, l), _resident(w_up, l), _layer_spec(conv_w, l), _layer_spec(conv_b, l),
                  _resident(w_down, l), _full(final_g.shape)],
        out_specs=_row_spec(seq, tm, d),
        out_shape=jax.ShapeDtypeStruct((nb, seq, d), F32),
        scratch_shapes=[pltpu.VMEM((d // LANES, tm + 2 * HALO, LANES), F32),
                        pltpu.VMEM((2 * (D_FF // FFN_TILE), tm + 2 * HALO, FFN_TILE), F32),
                        pltpu.VMEM((tm, D_FF), BF16),
                        pltpu.VMEM((d // LANES, tm, LANES), F32)],
        compiler_params=_params(("parallel",)),
        name="ffn_final" if final else "ffn",
    )(h, h, h, g, w_up, conv_w, conv_b, w_down, final_g)


def _row_vec(a):
    return a.astype(F32)[:, None, :]


def _trunk(x, p, *, tm, tm_wide, t_scan, tf):
    seq = x.shape[1]
    depth = p["w_in"].shape[0]
    plan = _FftPlan(seq)

    proj_c = (_row_vec(p["norm1_g"]), p["w_in"].astype(F32), p["rg_conv_w"].astype(F32),
              _row_vec(p["rg_conv_b"]), p["hy_conv_w"].astype(F32), _row_vec(p["hy_conv_b"]))
    rg_w, rg_bias, rg_sp = _rglru_weights(p)
    s5_w = _s5_weights(p)
    hy_spec = _hyena_spectra(seq, _hyena_filter_weights(p), plan, tf)
    mix_c = (_row_vec(p["s5_d"]), p["s5_glu_w"].astype(F32), _row_vec(p["s5_glu_b"]),
             _row_vec(p["hy_bias"]), _row_vec(p["mix_norm_g"]), p["w_out"].astype(F32))
    ffn_c = (_row_vec(p["norm2_g"]), p["w_up"].astype(BF16), p["ffn_conv_w"].astype(F32),
             _row_vec(p["ffn_conv_b"]), p["w_down"].astype(BF16), p["final_norm_g"].astype(F32)[None])

    h = x.astype(F32)
    for l in range(depth):
        urg, ga, ub, x0, z = _proj(h, *proj_c, l, tm_wide)
        hf, hb, yf, yb = _scans(urg, ub, rg_w, rg_bias, rg_sp, *s5_w, l, t_scan)
        yc = _hyena_conv(z, hy_spec, l, plan)
        h = _mix([h, hf, hb, ga, yf, yb, ub, yc, z, x0], mix_c, l, tm_wide)
        h = _ffn(h, *ffn_c, l, tm, final=(l == depth - 1))
    return h.astype(x.dtype)


def kernel(x, norm1_g, w_in, rg_conv_w, rg_conv_b, rg_wa, rg_ba, rg_wx, rg_bx, rg_lambda, s5_a_re, s5_a_im, s5_log_dt, s5_b_re, s5_b_im, s5_c_re, s5_c_im, s5_d, s5_glu_w, s5_glu_b, hy_conv_w, hy_conv_b, hy_filt_w1, hy_filt_b1, hy_filt_freq1, hy_filt_w2, hy_filt_b2, hy_filt_freq2, hy_filt_w3, hy_bias, mix_norm_g, w_out, norm2_g, w_up, ffn_conv_w, ffn_conv_b, w_down, final_norm_g):
    p = dict(norm1_g=norm1_g, w_in=w_in, rg_conv_w=rg_conv_w, rg_conv_b=rg_conv_b, rg_wa=rg_wa, rg_ba=rg_ba,
             rg_wx=rg_wx, rg_bx=rg_bx, rg_lambda=rg_lambda, s5_a_re=s5_a_re, s5_a_im=s5_a_im,
             s5_log_dt=s5_log_dt, s5_b_re=s5_b_re, s5_b_im=s5_b_im, s5_c_re=s5_c_re, s5_c_im=s5_c_im,
             s5_d=s5_d, s5_glu_w=s5_glu_w, s5_glu_b=s5_glu_b, hy_conv_w=hy_conv_w, hy_conv_b=hy_conv_b,
             hy_filt_w1=hy_filt_w1, hy_filt_b1=hy_filt_b1, hy_filt_freq1=hy_filt_freq1, hy_filt_w2=hy_filt_w2,
             hy_filt_b2=hy_filt_b2, hy_filt_freq2=hy_filt_freq2, hy_filt_w3=hy_filt_w3, hy_bias=hy_bias,
             mix_norm_g=mix_norm_g, w_out=w_out, norm2_g=norm2_g, w_up=w_up, ffn_conv_w=ffn_conv_w,
             ffn_conv_b=ffn_conv_b, w_down=w_down, final_norm_g=final_norm_g)
    return _trunk(x, p, tm=512, tm_wide=1024, t_scan=128, tf=1024)
```
